```python
import math
import jax
import jax.numpy as jnp
from jax import lax
import numpy as np

D_MODEL = 1024
BATCH = 16
SEQ = 256
DEPTH = 4
DEC_BATCH = 4
DEC_SEQ = 1024
PAST_LEN = 256

GRID_W = 64
GROUP_W = 512
D_MIX = 3 * GROUP_W
MLA_HEADS = 4
MLA_NOPE = 128
MLA_ROPE = 64
MLA_V = 128
MLA_Q_RANK = 384
MLA_KV_RANK = 256
DIFF_HEADS = 4
DIFF_D = 64
ML_HEADS = 4
ML_DK = 128
ML_DV = 128
ML_CHUNK = 64
CONV_K = 3
FORGET_BIAS = 3.0
ROPE_THETA = 10000.0
NORM_EPS = 1e-6
ATTN_BLOCK = 128
N_ML_GATES = 2 * 2 * ML_HEADS
IN_SIZES = (MLA_Q_RANK, MLA_KV_RANK, MLA_ROPE, GROUP_W,
            GROUP_W, GROUP_W, GROUP_W, GROUP_W,
            GROUP_W, GROUP_W, GROUP_W, GROUP_W, GROUP_W,
            N_ML_GATES)
N_IN = int(sum(IN_SIZES))
IN_SPLITS = tuple(int(s) for s in np.cumsum(IN_SIZES)[:-1])
MLA_SCALE = (MLA_NOPE + MLA_ROPE) ** -0.5
DIFF_SCALE = DIFF_D ** -0.5

kernel_name = 'hybrid_mla_diff_mlstm_dit_step'


def rmsnorm(x, g):
    xf = x.astype(jnp.float32)
    y = xf * lax.rsqrt(jnp.mean(xf * xf, axis=-1, keepdims=True) + NORM_EPS)
    return y.astype(x.dtype) * g.astype(x.dtype)


def axial_rope_tables(n_rows, rot_dim):
    n_freq = rot_dim // 4
    inv = ROPE_THETA ** (-jnp.arange(n_freq, dtype=jnp.float32) / n_freq)
    row = jnp.repeat(jnp.arange(n_rows, dtype=jnp.float32), GRID_W)
    col = jnp.tile(jnp.arange(GRID_W, dtype=jnp.float32), n_rows)
    ang = jnp.concatenate([row[:, None] * inv, col[:, None] * inv], axis=-1)
    return jnp.cos(ang), jnp.sin(ang)


def apply_rope(x, cos, sin):
    half = x.shape[-1] // 2
    c = cos[:, None, :].astype(x.dtype)
    s = sin[:, None, :].astype(x.dtype)
    x1, x2 = x[..., :half], x[..., half:]
    return jnp.concatenate([x1 * c - x2 * s, x1 * s + x2 * c], axis=-1)


def rope_sub(x, cos, sin):
    B, T, H, E = x.shape
    return apply_rope(x.reshape(B, T, 2 * H, DIFF_D), cos, sin).reshape(B, T, H, E)


def attention(q, k, v, scale):
    B, Tq, H, dq = q.shape
    nb = Tq // ATTN_BLOCK
    qb = jnp.moveaxis(q.reshape(B, nb, ATTN_BLOCK, H, dq), 1, 0)

    def one_block(q_blk):
        s = jnp.einsum('bqhd,bkhd->bhqk', q_blk, k).astype(jnp.float32) * scale
        p = jax.nn.softmax(s, axis=-1).astype(v.dtype)
        return jnp.einsum('bhqk,bkhd->bqhd', p, v)

    o = lax.map(one_block, qb)
    return jnp.moveaxis(o, 0, 1).reshape(B, Tq, H, v.shape[-1])


def short_conv(u, w):
    ch = u.shape[-1]
    return lax.conv_general_dilated(
        u, w[:, None, :].astype(u.dtype), window_strides=(1,),
        padding=[(CONV_K // 2, CONV_K // 2)],
        dimension_numbers=('NWC', 'WIO', 'NWC'), feature_group_count=ch)


def mlstm_scan(q, k, v, ig, lf, C0, n0, m0):
    B, H, T, DK = q.shape
    nc = T // ML_CHUNK

    def to_chunks(a):
        return jnp.moveaxis(a.reshape(B, H, nc, ML_CHUNK, *a.shape[3:]), 2, 0)

    bcum = jnp.cumsum(lf.reshape(B, H, nc, ML_CHUNK), axis=-1)
    xs = (to_chunks(q), to_chunks(k), to_chunks(v), to_chunks(ig), jnp.moveaxis(bcum, 2, 0))
    mask = jnp.tril(jnp.ones((ML_CHUNK, ML_CHUNK), dtype=bool))

    def step(carry, inp):
        C, n, m = carry
        qc, kc, vc, igc, bc = inp
        d = bc[..., :, None] - bc[..., None, :] + igc[..., None, :]
        d = jnp.where(mask, d, -jnp.inf)
        g = bc + m[..., None]
        m_t = jnp.maximum(g, jnp.max(d, axis=-1))
        w_intra = jnp.exp(d - m_t[..., None])
        w_inter = jnp.exp(g - m_t)
        sw = jnp.einsum('bhtd,bhsd->bhts', qc, kc) * w_intra
        num = jnp.einsum('bhts,bhsv->bhtv', sw, vc) + w_inter[..., None] * jnp.einsum('bhtd,bhdv->bhtv', qc, C)
        den = jnp.sum(sw, axis=-1) + w_inter * jnp.einsum('bhtd,bhd->bht', qc, n)
        h = num / jnp.maximum(jnp.abs(den), jnp.exp(-m_t))[..., None]
        b_last = bc[..., -1]
        m_new = m_t[..., -1]
        a_prev = jnp.exp(b_last + m - m_new)
        w_s = jnp.exp(b_last[..., None] - bc + igc - m_new[..., None])
        kw = kc * w_s[..., None]
        C_new = a_prev[..., None, None] * C + jnp.einsum('bhsd,bhsv->bhdv', kw, vc)
        n_new = a_prev[..., None] * n + jnp.sum(kw, axis=2)
        return (C_new, n_new, m_new), h

    (C, n, m), h = lax.scan(step, (C0, n0, m0), xs)
    h = jnp.moveaxis(h, 0, 2).reshape(B, H, T, v.shape[-1])
    return h, C, n, m


def mla_expand(ckv_n, krope, w_ukv):
    B, T, _ = ckv_n.shape
    kv = (ckv_n @ w_ukv).reshape(B, T, MLA_HEADS, MLA_NOPE + MLA_V)
    k_rope = jnp.broadcast_to(krope[:, :, None, :].astype(kv.dtype), (B, T, MLA_HEADS, MLA_ROPE))
    return jnp.concatenate([kv[..., :MLA_NOPE], k_rope], axis=-1), kv[..., MLA_NOPE:]


def mixer_layer(x, mod, l, W, rope, ctx):
    B, T, _ = x.shape
    shift, scale, gate = jnp.split(mod, 3, axis=-1)
    h = rmsnorm(x, W['g_norm'][l]) * (1 + scale) + shift
    proj = h @ W['W_in'][l]
    (cq, ckv, krope, z_a, dq, dk, dv, z_b, mq, mk, mv, mo, z_c, mg) = jnp.split(proj, IN_SPLITS, axis=-1)

    q_a = (rmsnorm(cq, W['mla_q_norm'][l]) @ W['W_uq'][l]).reshape(B, T, MLA_HEADS, MLA_NOPE + MLA_ROPE)
    ckv_n = rmsnorm(ckv, W['mla_kv_norm'][l])
    if rope is not None:
        cos_a, sin_a = rope[0]
        q_a = jnp.concatenate([q_a[..., :MLA_NOPE], apply_rope(q_a[..., MLA_NOPE:], cos_a, sin_a)], axis=-1)
        krope_use = apply_rope(krope[:, :, None, :], cos_a, sin_a)[:, :, 0, :]
    else:
        krope_use = krope
    k_a, v_a = mla_expand(ckv_n, krope_use, W['W_ukv'][l])
    if ctx is not None:
        k_c, v_c = mla_expand(ctx['ckv'], ctx['krope'], W['W_ukv'][l])
        k_a = jnp.concatenate([k_a, k_c.astype(k_a.dtype)], axis=1)
        v_a = jnp.concatenate([v_a, v_c.astype(v_a.dtype)], axis=1)
    o_a = attention(q_a, k_a, v_a, MLA_SCALE)

    q_b = dq.reshape(B, T, DIFF_HEADS, 2 * DIFF_D)
    k_b = dk.reshape(B, T, DIFF_HEADS, 2 * DIFF_D)
    v_b = dv.reshape(B, T, DIFF_HEADS, 2 * DIFF_D)
    if rope is not None:
        cos_b, sin_b = rope[1]
        q_b = rope_sub(q_b, cos_b, sin_b)
        keys_b = rope_sub(k_b, cos_b, sin_b)
    else:
        keys_b = k_b
    vals_b = v_b
    if ctx is not None:
        keys_b = jnp.concatenate([keys_b, ctx['dk'].astype(keys_b.dtype)], axis=1)
        vals_b = jnp.concatenate([vals_b, ctx['dv'].astype(vals_b.dtype)], axis=1)
    lam_init = 0.8 - 0.6 * math.exp(-0.3 * l)
    lp = W['diff_lambda'][l].astype(jnp.float32)
    lam = jnp.exp(jnp.sum(lp[0] * lp[1])) - jnp.exp(jnp.sum(lp[2] * lp[3])) + lam_init
    o1 = attention(q_b[..., :DIFF_D], keys_b[..., :DIFF_D], vals_b, DIFF_SCALE)
    o2 = attention(q_b[..., DIFF_D:], keys_b[..., DIFF_D:], vals_b, DIFF_SCALE)
    o_b = rmsnorm(o1 - lam.astype(o1.dtype) * o2, W['diff_norm'][l]) * (1.0 - lam_init)

    qk = jax.nn.silu(short_conv(jnp.concatenate([mq, mk], axis=-1), W['ml_conv'][l]))
    mq_c, mk_c = qk[..., :GROUP_W], qk[..., GROUP_W:]
    f32 = jnp.float32
    qh = (mq_c * (ML_DK ** -0.5)).reshape(B, T, ML_HEADS, ML_DK).transpose(0, 2, 1, 3).astype(f32)
    kh = mk_c.reshape(B, T, ML_HEADS, ML_DK).transpose(0, 2, 1, 3).astype(f32)
    vh = mv.reshape(B, T, ML_HEADS, ML_DV).transpose(0, 2, 1, 3).astype(f32)
    gts = (mg + W['ml_gate_b'][l].reshape(N_ML_GATES).astype(mg.dtype)).astype(f32)
    gts = jnp.moveaxis(gts.reshape(B, T, 2, 2, ML_HEADS), 1, -1)
    ig_f, lf_f = gts[:, 0, 0], jax.nn.log_sigmoid(gts[:, 0, 1])
    ig_b, lf_b = gts[:, 1, 0], jax.nn.log_sigmoid(gts[:, 1, 1])
    if ctx is None:
        zC = jnp.zeros((B, ML_HEADS, ML_DK, ML_DV), f32)
        zn = jnp.zeros((B, ML_HEADS, ML_DK), f32)
        zm = jnp.zeros((B, ML_HEADS), f32)
        init_f, init_b = (zC, zn, zm), (zC, zn, zm)
    else:
        Cc, nc_, mc = ctx['C'].astype(f32), ctx['n'].astype(f32), ctx['m'].astype(f32)
        init_f = (Cc[:, 0], nc_[:, 0], mc[:, 0])
        init_b = (Cc[:, 1], nc_[:, 1], mc[:, 1])
    h_f, C_f, n_f, m_f = mlstm_scan(qh, kh, vh, ig_f, lf_f, *init_f)
    flip = lambda a: jnp.flip(a, axis=2)
    h_b, C_b, n_b, m_b = mlstm_scan(flip(qh), flip(kh), flip(vh), flip(ig_b), flip(lf_b), *init_b)
    h_c = (h_f + flip(h_b)).transpose(0, 2, 1, 3).astype(x.dtype)
    h_c = jax.nn.sigmoid(mo).reshape(B, T, ML_HEADS, ML_DV) * h_c
    o_c = rmsnorm(h_c, W['ml_norm'][l])

    y = jnp.concatenate([o_a.reshape(B, T, GROUP_W) * jax.nn.silu(z_a),
                         o_b.reshape(B, T, GROUP_W) * jax.nn.silu(z_b),
                         o_c.reshape(B, T, GROUP_W) * jax.nn.silu(z_c)], axis=-1) @ W['W_out'][l]
    x_new = x + gate * y
    side = (ckv_n, krope, k_b, v_b,
            jnp.stack([C_f, C_b], axis=1), jnp.stack([n_f, n_b], axis=1), jnp.stack([m_f, m_b], axis=1))
    return x_new, side


def setup_inputs(seed: int = 0) -> dict:
    key = jax.random.key(seed)
    ks = jax.random.split(key, 32)
    nrm = lambda k, shape, s: jax.random.normal(k, shape, jnp.float32) * s
    gain = lambda k, shape: 1.0 + 0.02 * jax.random.normal(k, shape, jnp.float32)
    gate_b = nrm(ks[21], (DEPTH, 2, 2, ML_HEADS), 0.1) + jnp.array([0.0, FORGET_BIAS], jnp.float32)[None, None, :, None]
    return {
        'x_prompt': nrm(ks[0], (BATCH, SEQ, D_MODEL), 1.0),
        'x_sample': nrm(ks[1], (DEC_BATCH, DEC_SEQ, D_MODEL), 1.0),
        'cache_mla_ckv': nrm(ks[2], (DEC_BATCH, DEPTH, PAST_LEN, MLA_KV_RANK), 1.0),
        'cache_mla_krope': nrm(ks[3], (DEC_BATCH, DEPTH, PAST_LEN, MLA_ROPE), 1.0),
        'cache_diff_k': nrm(ks[4], (DEC_BATCH, DEPTH, PAST_LEN, DIFF_HEADS, 2 * DIFF_D), 1.0),
        'cache_diff_v': nrm(ks[5], (DEC_BATCH, DEPTH, PAST_LEN, DIFF_HEADS, 2 * DIFF_D), 1.0),
        'state_mlstm_C': nrm(ks[6], (DEC_BATCH, DEPTH, 2, ML_HEADS, ML_DK, ML_DV), 1.0),
        'state_mlstm_n': nrm(ks[7], (DEC_BATCH, DEPTH, 2, ML_HEADS, ML_DK), 1.0),
        'state_mlstm_m': nrm(ks[8], (DEC_BATCH, DEPTH, 2, ML_HEADS), 1.0),
        'c': nrm(ks[9], (DEC_BATCH, D_MODEL), 1.0),
        'c_ctx': nrm(ks[10], (D_MODEL,), 1.0),
        'g_norm': gain(ks[11], (DEPTH, D_MODEL)),
        'W_mod': nrm(ks[12], (DEPTH, D_MODEL, 3 * D_MODEL), 0.5 * D_MODEL ** -0.5),
        'b_mod': nrm(ks[13], (DEPTH, 3 * D_MODEL), 0.02),
        'W_in': nrm(ks[14], (DEPTH, D_MODEL, N_IN), D_MODEL ** -0.5),
        'mla_q_norm': gain(ks[15], (DEPTH, MLA_Q_RANK)),
        'W_uq': nrm(ks[16], (DEPTH, MLA_Q_RANK, MLA_HEADS * (MLA_NOPE + MLA_ROPE)), MLA_Q_RANK ** -0.5),
        'mla_kv_norm': gain(ks[17], (DEPTH, MLA_KV_RANK)),
        'W_ukv': nrm(ks[18], (DEPTH, MLA_KV_RANK, MLA_HEADS * (MLA_NOPE + MLA_V)), MLA_KV_RANK ** -0.5),
        'diff_lambda': nrm(ks[19], (DEPTH, 4, DIFF_D), 0.1),
        'diff_norm': gain(ks[20], (DEPTH, 2 * DIFF_D)),
        'ml_conv': nrm(ks[22], (DEPTH, CONV_K, 2 * GROUP_W), CONV_K ** -0.5),
        'ml_gate_b': gate_b,
        'ml_norm': gain(ks[23], (DEPTH, ML_HEADS, ML_DV)),
        'W_out': nrm(ks[24], (DEPTH, D_MIX, D_MODEL), D_MIX ** -0.5),
        'g_final': gain(ks[25], (D_MODEL,)),
    }


def reference(x_prompt, x_sample, cache_mla_ckv, cache_mla_krope, cache_diff_k, cache_diff_v,
              state_mlstm_C, state_mlstm_n, state_mlstm_m, c, c_ctx, g_norm, W_mod, b_mod, W_in,
              mla_q_norm, W_uq, mla_kv_norm, W_ukv, diff_lambda, diff_norm, ml_conv, ml_gate_b,
              ml_norm, W_out, g_final):
    W = {'g_norm': g_norm, 'W_in': W_in, 'mla_q_norm': mla_q_norm, 'W_uq': W_uq,
         'mla_kv_norm': mla_kv_norm, 'W_ukv': W_ukv, 'diff_lambda': diff_lambda, 'diff_norm': diff_norm,
         'ml_conv': ml_conv, 'ml_gate_b': ml_gate_b, 'ml_norm': ml_norm, 'W_out': W_out}

    x = x_prompt
    sides = []
    for l in range(DEPTH):
        mod = (jax.nn.silu(c_ctx) @ W_mod[l] + b_mod[l])[None, None, :]
        x, side = mixer_layer(x, mod, l, W, None, None)
        sides.append(side)
    y_prompt = rmsnorm(x, g_final)
    new_mla_ckv = jnp.stack([s[0] for s in sides], axis=1)
    new_mla_krope = jnp.stack([s[1] for s in sides], axis=1)
    new_diff_k = jnp.stack([s[2] for s in sides], axis=1)
    new_diff_v = jnp.stack([s[3] for s in sides], axis=1)
    new_mlstm_C = jnp.stack([s[4] for s in sides], axis=1)
    new_mlstm_n = jnp.stack([s[5] for s in sides], axis=1)
    new_mlstm_m = jnp.stack([s[6] for s in sides], axis=1)

    n_rows = x_sample.shape[1] // GRID_W
    rope = (axial_rope_tables(n_rows, MLA_ROPE), axial_rope_tables(n_rows, DIFF_D))
    x = x_sample
    for l in range(DEPTH):
        mod = (jax.nn.silu(c) @ W_mod[l] + b_mod[l])[:, None, :]
        ctx = {'ckv': cache_mla_ckv[:, l], 'krope': cache_mla_krope[:, l],
               'dk': cache_diff_k[:, l], 'dv': cache_diff_v[:, l],
               'C': state_mlstm_C[:, l], 'n': state_mlstm_n[:, l], 'm': state_mlstm_m[:, l]}
        x, _ = mixer_layer(x, mod, l, W, rope, ctx)
    y_sample = rmsnorm(x, g_final)

    return (y_prompt, y_sample, new_mla_ckv, new_mla_krope, new_diff_k, new_diff_v,
            new_mlstm_C, new_mlstm_n, new_mlstm_m)
```

```python
import functools
import math

import jax
import jax.numpy as jnp
import numpy as np
from jax import lax
from jax.experimental import pallas as pl
from jax.experimental.pallas import tpu as pltpu

F32 = jnp.float32
BF16 = jnp.bfloat16

D_MODEL = 1024
DEPTH = 4
PAST_LEN = 256
GRID_W = 64
GROUP_W = 512
MLA_HEADS = 4
MLA_NOPE = 128
MLA_ROPE = 64
MLA_Q_RANK = 384
MLA_KV_RANK = 256
DIFF_HEADS = 4
DIFF_D = 64
ML_HEADS = 4
ML_DK = 128
N_ML_GATES = 16
ROPE_THETA = 10000.0
NORM_EPS = 1e-6
MLA_SCALE = (MLA_NOPE + MLA_ROPE) ** -0.5
DIFF_SCALE = DIFF_D ** -0.5

LANES = 128
HEAD_W = 128
ML_CHUNK = 256
PROJ_TM = 512
ATTN_TQ = 256
MAIN_W = 10 * GROUP_W
TAIL_W = 768
N_IN_PAD = MAIN_W + TAIL_W
VMEM_LIMIT = 56 * 1024 * 1024

C_DQ, C_DK, C_DV, C_ZB, C_MQ, C_MK, C_MV, C_MO, C_ZC, C_ZA = range(10)

_IN_SIZES = (MLA_Q_RANK, MLA_KV_RANK, MLA_ROPE, GROUP_W, GROUP_W, GROUP_W, GROUP_W, GROUP_W,
             GROUP_W, GROUP_W, GROUP_W, GROUP_W, GROUP_W, N_ML_GATES)
_IN_OFF = np.concatenate([[0], np.cumsum(_IN_SIZES)])
(_O_CQ, _O_CKV, _O_KR, _O_ZA, _O_DQ, _O_DK, _O_DV, _O_ZB, _O_MQ, _O_MK, _O_MV, _O_MO, _O_ZC,
 _O_MG) = [int(v) for v in _IN_OFF[:-1]]
GATE_LANE0 = MLA_ROPE


def _rms(x, g):
    ms = jnp.mean(x * x, axis=-1, keepdims=True)
    return x * lax.rsqrt(ms + NORM_EPS) * g


def _silu(x):
    return x * jax.nn.sigmoid(x)


def _log_sigmoid(x):
    return jnp.minimum(x, 0.0) - jnp.log1p(jnp.exp(-jnp.abs(x)))


def _swap32(x):
    lane = lax.broadcasted_iota(jnp.int32, x.shape, 1)
    fwd = pltpu.roll(x, LANES - 32, 1)
    bwd = pltpu.roll(x, 32, 1)
    return jnp.where((lane % 64) < 32, fwd, bwd)


def _rope_tile(x, cos, sin):
    return x * cos + _swap32(x) * sin


def _mod_kernel(c_ref, w_ref, b_ref, o_ref):
    a = _silu(c_ref[...]).astype(BF16)
    o_ref[...] = jnp.dot(a, w_ref[...].astype(BF16), preferred_element_type=F32) + b_ref[...]


def _mod_call(cc, W_mod, b_mod):
    tn = 1024
    return pl.pallas_call(
        _mod_kernel,
        grid=(DEPTH, 3 * D_MODEL // tn),
        in_specs=[pl.BlockSpec((8, D_MODEL), lambda l, j: (0, 0)),
                  pl.BlockSpec((None, D_MODEL, tn), lambda l, j: (l, 0, j)),
                  pl.BlockSpec((None, 1, tn), lambda l, j: (l, 0, j))],
        out_specs=pl.BlockSpec((None, 8, tn), lambda l, j: (l, 0, j)),
        out_shape=jax.ShapeDtypeStruct((DEPTH, 8, 3 * D_MODEL), F32),
        compiler_params=pltpu.CompilerParams(dimension_semantics=("arbitrary", "arbitrary"),
                                             vmem_limit_bytes=VMEM_LIMIT),
        name="mod",
    )(cc, W_mod, b_mod.reshape(DEPTH, 1, 3 * D_MODEL))


def _proj_kernel(has_side, l_ref, x_ref, mod_ref, g_ref, w_ref, main_ref, tail_ref, *side_refs):
    del l_ref
    mod = mod_ref[...]
    y = _rms(x_ref[...], g_ref[...])
    h = (y * (1.0 + mod[:, D_MODEL:2 * D_MODEL]) + mod[:, :D_MODEL]).astype(BF16)
    for s in range(MAIN_W // GROUP_W):
        cols = slice(s * GROUP_W, (s + 1) * GROUP_W)
        acc = jnp.dot(h, w_ref[:, cols], preferred_element_type=F32)
        main_ref[:, cols] = acc.astype(BF16)
        if has_side and s == C_DK:
            side_refs[0][...] = acc
        if has_side and s == C_DV:
            side_refs[1][...] = acc
    tail_ref[...] = jnp.dot(h, w_ref[:, MAIN_W:], preferred_element_type=F32)


def _proj_call(l_arr, x2, mod_all, g_norm3, w_in_r, T, has_ctx):
    n_tok = x2.shape[0]
    tm = PROJ_TM
    has_side = not has_ctx
    if has_ctx:
        mod_map = lambda i, l: (l[0], 1 + (i * tm) // T, 0, 0)
    else:
        mod_map = lambda i, l: (l[0], 0, 0, 0)
    out_shape = [jax.ShapeDtypeStruct((n_tok, MAIN_W), BF16), jax.ShapeDtypeStruct((n_tok, TAIL_W), F32)]
    out_specs = [pl.BlockSpec((tm, MAIN_W), lambda i, l: (i, 0)), pl.BlockSpec((tm, TAIL_W), lambda i, l: (i, 0))]
    if has_side:
        out_shape += [jax.ShapeDtypeStruct((n_tok, GROUP_W), F32)] * 2
        out_specs += [pl.BlockSpec((tm, GROUP_W), lambda i, l: (i, 0))] * 2
    grid_spec = pltpu.PrefetchScalarGridSpec(
        num_scalar_prefetch=1,
        grid=(n_tok // tm,),
        in_specs=[pl.BlockSpec((tm, D_MODEL), lambda i, l: (i, 0)),
                  pl.BlockSpec((None, None, 1, 3 * D_MODEL), mod_map),
                  pl.BlockSpec((None, 1, D_MODEL), lambda i, l: (l[0], 0, 0)),
                  pl.BlockSpec((None, D_MODEL, N_IN_PAD), lambda i, l: (l[0], 0, 0),
                               pipeline_mode=pl.Buffered(1))],
        out_specs=out_specs,
    )
    return pl.pallas_call(
        functools.partial(_proj_kernel, has_side),
        grid_spec=grid_spec,
        out_shape=out_shape,
        compiler_params=pltpu.CompilerParams(dimension_semantics=("arbitrary",),
                                             vmem_limit_bytes=VMEM_LIMIT),
        name="proj_lat" if has_ctx else "proj_ctx",
    )(l_arr, x2, mod_all, g_norm3, w_in_r)


def _mlstm_kernel(has_ctx, T, *refs):
    nc = T // ML_CHUNK
    L = ML_CHUNK
    use_inter = has_ctx or nc > 1
    it = iter(refs)
    l_ref = next(it)
    mq_ref, mk_ref, mv_ref, mo_ref, zc_ref, aux_ref, convw_ref, gb_ref, mln_ref = [next(it) for _ in range(9)]
    if has_ctx:
        c0_ref, n0_ref, m0_ref = [next(it) for _ in range(3)]
    yc_ref = next(it)
    if not has_ctx:
        cout_ref, nout_ref, mout_ref = [next(it) for _ in range(3)]
    q_s, k_s, va_s, hf_s, hb_s, caug_s, m_s = [next(it) for _ in range(7)]
    del l_ref

    row = lax.broadcasted_iota(jnp.int32, (T, GROUP_W), 0)

    def conv_silu(u, w3):
        up = jnp.where(row == 0, 0.0, pltpu.roll(u, 1, 0))
        un = jnp.where(row == T - 1, 0.0, pltpu.roll(u, T - 1, 0))
        return _silu(w3[0:1, :] * up + w3[1:2, :] * u + w3[2:3, :] * un)

    convw = convw_ref[...]
    q_s[...] = (conv_silu(mq_ref[...].astype(F32), convw[:, :GROUP_W]) * (ML_DK ** -0.5)).astype(BF16)
    k_s[...] = conv_silu(mk_ref[...].astype(F32), convw[:, GROUP_W:])
    lane_t = lax.broadcasted_iota(jnp.int32, (T, HEAD_W), 1)
    ones_col = jnp.where(lane_t == 0, 1.0, 0.0).astype(BF16)
    for h in range(ML_HEADS):
        va_s[:, 2 * h * HEAD_W:(2 * h + 1) * HEAD_W] = mv_ref[:, h * HEAD_W:(h + 1) * HEAD_W]
        va_s[:, (2 * h + 1) * HEAD_W:(2 * h + 2) * HEAD_W] = ones_col

    lane_c = lax.broadcasted_iota(jnp.int32, (ML_DK, HEAD_W), 1)
    if has_ctx:
        n0 = n0_ref[...].reshape(2 * ML_HEADS, ML_DK)
        n0t = jnp.concatenate([n0, jnp.zeros((ML_DK - 2 * ML_HEADS, ML_DK), F32)], axis=0).T
        m0 = m0_ref[...]
        m_s[...] = m0
        for r in range(2 * ML_HEADS):
            caug_s[r, :, :HEAD_W] = c0_ref[r]
            caug_s[r, :, HEAD_W:] = jnp.where(lane_c == 0, n0t[:, r:r + 1], 0.0)
    else:
        caug_s[...] = jnp.zeros_like(caug_s)
        m_s[...] = jnp.zeros_like(m_s)

    ri = lax.broadcasted_iota(jnp.int32, (L, L), 0)
    ci = lax.broadcasted_iota(jnp.int32, (L, L), 1)
    tril = ri >= ci
    triu = ri <= ci

    def do_dir(d, c):
        mask = tril if d == 0 else triu
        tri_mat = jnp.where(mask, 1.0, 0.0)
        last = L - 1 if d == 0 else 0
        h_s = hf_s if d == 0 else hb_s
        rows = slice(c * L, (c + 1) * L) if isinstance(c, int) else pl.ds(pl.multiple_of(c * L, L), L)
        gcol = aux_ref[rows, :] + gb_ref[...]
        lf = _log_sigmoid(gcol)
        bc = jnp.dot(tri_mat, lf, precision=lax.Precision.HIGHEST, preferred_element_type=F32)
        g_t = gcol.T
        bc_t = bc.T
        for h in range(ML_HEADS):
            r = d * ML_HEADS + h
            ji = GATE_LANE0 + d * 2 * ML_HEADS + h
            jf = ji + ML_HEADS
            bc_col, bc_row = bc[:, jf:jf + 1], bc_t[jf:jf + 1, :]
            ig_col, ig_row = gcol[:, ji:ji + 1], g_t[ji:ji + 1, :]
            m = m_s[r:r + 1, 0:1]
            dmat = jnp.where(mask, bc_col - bc_row + ig_row, -jnp.inf)
            g = bc_col + m
            m_t = jnp.maximum(g, jnp.max(dmat, axis=-1, keepdims=True))
            w_intra = jnp.exp(dmat - m_t)
            hs = slice(h * HEAD_W, (h + 1) * HEAD_W)
            qc = q_s[rows, hs]
            kc = k_s[rows, hs]
            vac = va_s[rows, 2 * h * HEAD_W:(2 * h + 2) * HEAD_W]
            s = lax.dot_general(qc, kc.astype(BF16), (((1,), (1,)), ((), ())), preferred_element_type=F32)
            nd = jnp.dot((s * w_intra).astype(BF16), vac, preferred_element_type=F32)
            if use_inter:
                w_inter = jnp.exp(g - m_t)
                nd = nd + w_inter * jnp.dot(qc, caug_s[r].astype(BF16), preferred_element_type=F32)
            den = nd[:, HEAD_W:HEAD_W + 1]
            h_s[rows, hs] = nd[:, :HEAD_W] / jnp.maximum(jnp.abs(den), jnp.exp(-m_t))
            b_last = bc_col[last:last + 1, :]
            m_new = m_t[last:last + 1, :]
            w_s = jnp.exp(b_last - bc_col + ig_col - m_new)
            kw_t = (kc * w_s).T.astype(BF16)
            upd = jnp.dot(kw_t, vac, preferred_element_type=F32)
            if use_inter:
                upd = upd + jnp.exp(b_last + m - m_new) * caug_s[r]
            caug_s[r] = upd
            m_s[r:r + 1, :] = jnp.broadcast_to(m_new, (1, HEAD_W))

    if nc == 1:
        do_dir(0, 0)
        do_dir(1, 0)
    else:
        def body(i, carry):
            do_dir(0, i)
            do_dir(1, nc - 1 - i)
            return carry
        lax.fori_loop(0, nc, body, 0)

    mln = mln_ref[...]
    for h in range(ML_HEADS):
        hs = slice(h * HEAD_W, (h + 1) * HEAD_W)
        hc = jax.nn.sigmoid(mo_ref[:, hs].astype(F32)) * (hf_s[:, hs] + hb_s[:, hs])
        yc_ref[:, hs] = (_rms(hc, mln[:, hs]) * _silu(zc_ref[:, hs].astype(F32))).astype(BF16)

    if not has_ctx:
        for r in range(2 * ML_HEADS):
            cout_ref[r] = caug_s[r, :, :HEAD_W]
            nout_ref[r:r + 1, :] = caug_s[r, :, HEAD_W:].T[0:1, :]
        mout_ref[...] = m_s[...]


def _mlstm_call(l_arr, main, tail, ml_conv, gate_row, ml_norm2, T, has_ctx, ctx_state=None):
    n_tok = main.shape[0]
    B = n_tok // T
    main3 = main.reshape(B, T, MAIN_W)
    tail3 = tail.reshape(B, T, TAIL_W)

    def col(c):
        return pl.BlockSpec((None, T, GROUP_W), lambda b, l, c=c: (b, 0, c))

    in_specs = [col(C_MQ), col(C_MK), col(C_MV), col(C_MO), col(C_ZC),
                pl.BlockSpec((None, T, LANES), lambda b, l: (b, 0, 2)),
                pl.BlockSpec((None, 3, 2 * GROUP_W), lambda b, l: (l[0], 0, 0)),
                pl.BlockSpec((None, 1, LANES), lambda b, l: (l[0], 0, 0)),
                pl.BlockSpec((None, 1, GROUP_W), lambda b, l: (l[0], 0, 0))]
    args = [main3, main3, main3, main3, main3, tail3, ml_conv, gate_row, ml_norm2]
    out_shape = [jax.ShapeDtypeStruct((B, T, GROUP_W), BF16)]
    out_specs = [pl.BlockSpec((None, T, GROUP_W), lambda b, l: (b, 0, 0))]
    if has_ctx:
        c0, n0, m0 = ctx_state
        in_specs += [pl.BlockSpec((None, None, 2 * ML_HEADS, ML_DK, HEAD_W), lambda b, l: (b, l[0], 0, 0, 0)),
                     pl.BlockSpec((None, None, 2 * ML_HEADS, ML_DK), lambda b, l: (b, l[0], 0, 0)),
                     pl.BlockSpec((None, None, 2 * ML_HEADS, LANES), lambda b, l: (b, l[0], 0, 0))]
        args += [c0, n0, m0]
    else:
        out_shape += [jax.ShapeDtypeStruct((B, 2 * ML_HEADS, ML_DK, HEAD_W), F32),
                      jax.ShapeDtypeStruct((B, 2 * ML_HEADS, ML_DK), F32),
                      jax.ShapeDtypeStruct((B, 2 * ML_HEADS, LANES), F32)]
        out_specs += [pl.BlockSpec((None, 2 * ML_HEADS, ML_DK, HEAD_W), lambda b, l: (b, 0, 0, 0)),
                      pl.BlockSpec((None, 2 * ML_HEADS, ML_DK), lambda b, l: (b, 0, 0)),
                      pl.BlockSpec((None, 2 * ML_HEADS, LANES), lambda b, l: (b, 0, 0))]
    grid_spec = pltpu.PrefetchScalarGridSpec(
        num_scalar_prefetch=1, grid=(B,), in_specs=in_specs, out_specs=out_specs,
        scratch_shapes=[pltpu.VMEM((T, GROUP_W), BF16),
                        pltpu.VMEM((T, GROUP_W), F32),
                        pltpu.VMEM((T, 2 * GROUP_W), BF16),
                        pltpu.VMEM((T, GROUP_W), F32),
                        pltpu.VMEM((T, GROUP_W), F32),
                        pltpu.VMEM((2 * ML_HEADS, ML_DK, 2 * HEAD_W), F32),
                        pltpu.VMEM((2 * ML_HEADS, LANES), F32)])
    return pl.pallas_call(
        functools.partial(_mlstm_kernel, has_ctx, T),
        grid_spec=grid_spec,
        out_shape=out_shape,
        compiler_params=pltpu.CompilerParams(dimension_semantics=("arbitrary",),
                                             vmem_limit_bytes=VMEM_LIMIT),
        name="mlstm_lat" if has_ctx else "mlstm_ctx",
    )(l_arr, *args)


def _attn_kernel(has_ctx, is_last, T, *refs):
    Tk = T + (PAST_LEN if has_ctx else 0)
    it = iter(refs)
    l_ref = next(it)
    (cq_ref, ckv_ref, aux_ref, dq_ref, dk_ref, dv_ref, zb_ref, za_ref, yc_ref, x_ref, mod_ref,
     wuq_ref, wukv_ref, wout_ref, gq_ref, gkv_ref, gdn_ref, lam_ref) = [next(it) for _ in range(18)]
    if is_last:
        gfin_ref = next(it)
    if has_ctx:
        cosq_ref, sinq_ref, cosk_ref, sink_ref, cckv_ref, ckr_ref, cdk_ref, cdv_ref = [next(it) for _ in range(8)]
    xo_ref = next(it)
    if is_last:
        yfin_ref = next(it)
    if not has_ctx:
        ckvn_ref, kro_ref = [next(it) for _ in range(2)]
    ka_s, va_s, kb_s, vb_s = [next(it) for _ in range(4)]

    qi = pl.program_id(1)

    @pl.when(qi == 0)
    def _build_keys():
        wukv = wukv_ref[...]
        ckv_n = _rms(ckv_ref[...], gkv_ref[...])
        aux = aux_ref[...]
        if not has_ctx:
            ckvn_ref[...] = ckv_n
            kro_ref[...] = aux[:, :MLA_ROPE]
        kv = jnp.dot(ckv_n.astype(BF16), wukv, preferred_element_type=F32)
        lane = lax.broadcasted_iota(jnp.int32, aux.shape, 1)
        kr = _rope_tile(aux, cosk_ref[...], sink_ref[...]) if has_ctx else aux
        kr = jnp.where(lane < MLA_ROPE, kr, 0.0).astype(BF16)
        for h in range(MLA_HEADS):
            ka_s[0:T, 2 * h * HEAD_W:(2 * h + 1) * HEAD_W] = kv[:, h * HEAD_W:(h + 1) * HEAD_W].astype(BF16)
            ka_s[0:T, (2 * h + 1) * HEAD_W:(2 * h + 2) * HEAD_W] = kr
        va_s[0:T, :] = kv[:, GROUP_W:].astype(BF16)
        vb_s[0:T, :] = dv_ref[...]
        if has_ctx:
            for h in range(DIFF_HEADS):
                hs = slice(h * HEAD_W, (h + 1) * HEAD_W)
                kb_s[0:T, hs] = _rope_tile(dk_ref[:, hs].astype(F32), cosk_ref[...], sink_ref[...]).astype(BF16)
            kvc = jnp.dot(cckv_ref[...].astype(BF16), wukv, preferred_element_type=F32)
            ckr = ckr_ref[...].astype(BF16)
            for h in range(MLA_HEADS):
                ka_s[T:Tk, 2 * h * HEAD_W:(2 * h + 1) * HEAD_W] = kvc[:, h * HEAD_W:(h + 1) * HEAD_W].astype(BF16)
                ka_s[T:Tk, (2 * h + 1) * HEAD_W:(2 * h + 2) * HEAD_W] = ckr
            va_s[T:Tk, :] = kvc[:, GROUP_W:].astype(BF16)
            kb_s[T:Tk, :] = cdk_ref[...].astype(BF16)
            vb_s[T:Tk, :] = cdv_ref[...].astype(BF16)
        else:
            kb_s[0:T, :] = dk_ref[...]

    def softmax_pv(s, v):
        e = jnp.exp(s - jnp.max(s, axis=-1, keepdims=True))
        o = jnp.dot(e.astype(BF16), v, preferred_element_type=F32)
        return o / jnp.sum(e, axis=-1, keepdims=True)

    nt = (((1,), (1,)), ((), ()))
    tq = cq_ref.shape[0]

    qa = jnp.dot(_rms(cq_ref[...], gq_ref[...]).astype(BF16), wuq_ref[...], preferred_element_type=F32)
    y = jnp.zeros((tq, D_MODEL), F32)
    for h in range(MLA_HEADS):
        q_nope = qa[:, 2 * h * HEAD_W:(2 * h + 1) * HEAD_W]
        q_rope = qa[:, (2 * h + 1) * HEAD_W:(2 * h + 2) * HEAD_W]
        if has_ctx:
            q_rope = _rope_tile(q_rope, cosq_ref[...], sinq_ref[...])
        q_h = (jnp.concatenate([q_nope, q_rope], axis=-1) * MLA_SCALE).astype(BF16)
        s = lax.dot_general(q_h, ka_s[:, 2 * h * HEAD_W:(2 * h + 2) * HEAD_W], nt, preferred_element_type=F32)
        hs = slice(h * HEAD_W, (h + 1) * HEAD_W)
        o = softmax_pv(s, va_s[:, hs])
        ya = (o * _silu(za_ref[:, hs].astype(F32))).astype(BF16)
        y = y + jnp.dot(ya, wout_ref[h * HEAD_W:(h + 1) * HEAD_W, :], preferred_element_type=F32)

    lp = lam_ref[...]
    lf = l_ref[0].astype(F32)
    lam_init = 0.8 - 0.6 * jnp.exp(jnp.full((1, 1), -0.3, F32) * lf)
    lam = (jnp.exp(jnp.sum(lp[0:1, :] * lp[1:2, :], axis=-1, keepdims=True))
           - jnp.exp(jnp.sum(lp[2:3, :] * lp[3:4, :], axis=-1, keepdims=True)) + lam_init)
    lane_q = lax.broadcasted_iota(jnp.int32, (tq, HEAD_W), 1)
    for h in range(DIFF_HEADS):
        hs = slice(h * HEAD_W, (h + 1) * HEAD_W)
        q_h = dq_ref[:, hs].astype(F32)
        if has_ctx:
            q_h = _rope_tile(q_h, cosq_ref[...], sinq_ref[...])
        q_h = q_h * DIFF_SCALE
        q1 = jnp.where(lane_q < DIFF_D, q_h, 0.0).astype(BF16)
        q2 = jnp.where(lane_q >= DIFF_D, q_h, 0.0).astype(BF16)
        k_h = kb_s[:, hs]
        v_h = vb_s[:, hs]
        o1 = softmax_pv(lax.dot_general(q1, k_h, nt, preferred_element_type=F32), v_h)
        o2 = softmax_pv(lax.dot_general(q2, k_h, nt, preferred_element_type=F32), v_h)
        o = _rms(o1 - lam * o2, gdn_ref[...]) * (1.0 - lam_init)
        yb = (o * _silu(zb_ref[:, hs].astype(F32))).astype(BF16)
        y = y + jnp.dot(yb, wout_ref[GROUP_W + h * HEAD_W:GROUP_W + (h + 1) * HEAD_W, :],
                        preferred_element_type=F32)

    y = y + jnp.dot(yc_ref[...], wout_ref[2 * GROUP_W:, :], preferred_element_type=F32)
    x_new = x_ref[...] + mod_ref[:, 2 * D_MODEL:] * y
    xo_ref[...] = x_new
    if is_last:
        yfin_ref[...] = _rms(x_new, gfin_ref[...])


def _attn_call(l_arr, main, tail, yc, x2, mod_all, w, T, has_ctx, is_last, rope=None, ctx=None):
    n_tok = main.shape[0]
    B = n_tok // T
    tq = ATTN_TQ
    nq = T // tq
    Tk = T + (PAST_LEN if has_ctx else 0)
    main3 = main.reshape(B, T, MAIN_W)
    tail3 = tail.reshape(B, T, TAIL_W)
    x3 = x2.reshape(B, T, D_MODEL)

    def tile(c, width):
        return pl.BlockSpec((None, tq, width), lambda b, q, l, c=c: (b, q, c))

    def full(c, width):
        return pl.BlockSpec((None, T, width), lambda b, q, l, c=c: (b, 0, c))

    def wspec(shape):
        return pl.BlockSpec((None,) + shape, lambda b, q, l: (l[0],) + (0,) * len(shape))

    if has_ctx:
        mod_map = lambda b, q, l: (l[0], 1 + b, 0, 0)
    else:
        mod_map = lambda b, q, l: (l[0], 0, 0, 0)

    in_specs = [tile(1, MLA_Q_RANK),
                full(0, MLA_KV_RANK),
                full(2, LANES),
                tile(C_DQ, GROUP_W), full(C_DK, GROUP_W), full(C_DV, GROUP_W),
                tile(C_ZB, GROUP_W), tile(C_ZA, GROUP_W),
                pl.BlockSpec((None, tq, GROUP_W), lambda b, q, l: (b, q, 0)),
                pl.BlockSpec((None, tq, D_MODEL), lambda b, q, l: (b, q, 0)),
                pl.BlockSpec((None, None, 1, 3 * D_MODEL), mod_map),
                wspec((MLA_Q_RANK, 2 * GROUP_W)), wspec((MLA_KV_RANK, 2 * GROUP_W)),
                wspec((3 * GROUP_W, D_MODEL)),
                wspec((1, MLA_Q_RANK)), wspec((1, MLA_KV_RANK)), wspec((1, HEAD_W)),
                wspec((4, DIFF_D))]
    args = [tail3, tail3, tail3, main3, main3, main3, main3, main3, yc, x3, mod_all,
            w['wuq'], w['wukv'], w['wout'], w['gq'], w['gkv'], w['gdn'], w['lam']]
    if is_last:
        in_specs.append(pl.BlockSpec((1, D_MODEL), lambda b, q, l: (0, 0)))
        args.append(w['gfin'])
    if has_ctx:
        cos_t, sin_t = rope
        in_specs += [pl.BlockSpec((tq, LANES), lambda b, q, l: (q, 0)),
                     pl.BlockSpec((tq, LANES), lambda b, q, l: (q, 0)),
                     pl.BlockSpec((T, LANES), lambda b, q, l: (0, 0)),
                     pl.BlockSpec((T, LANES), lambda b, q, l: (0, 0)),
                     pl.BlockSpec((None, None, PAST_LEN, MLA_KV_RANK), lambda b, q, l: (b, l[0], 0, 0)),
                     pl.BlockSpec((None, None, PAST_LEN, LANES), lambda b, q, l: (b, l[0], 0, 0)),
                     pl.BlockSpec((None, None, PAST_LEN, GROUP_W), lambda b, q, l: (b, l[0], 0, 0)),
                     pl.BlockSpec((None, None, PAST_LEN, GROUP_W), lambda b, q, l: (b, l[0], 0, 0))]
        args += [cos_t, sin_t, cos_t, sin_t, ctx['ckv'], ctx['krope'], ctx['dk'], ctx['dv']]
    out_shape = [jax.ShapeDtypeStruct((B, T, D_MODEL), F32)]
    out_specs = [pl.BlockSpec((None, tq, D_MODEL), lambda b, q, l: (b, q, 0))]
    if is_last:
        out_shape.append(jax.ShapeDtypeStruct((B, T, D_MODEL), F32))
        out_specs.append(pl.BlockSpec((None, tq, D_MODEL), lambda b, q, l: (b, q, 0)))
    if not has_ctx:
        out_shape += [jax.ShapeDtypeStruct((B, T, MLA_KV_RANK), F32), jax.ShapeDtypeStruct((B, T, MLA_ROPE), F32)]
        out_specs += [pl.BlockSpec((None, T, MLA_KV_RANK), lambda b, q, l: (b, 0, 0)),
                      pl.BlockSpec((None, T, MLA_ROPE), lambda b, q, l: (b, 0, 0))]
    grid_spec = pltpu.PrefetchScalarGridSpec(
        num_scalar_prefetch=1, grid=(B, nq), in_specs=in_specs, out_specs=out_specs,
        scratch_shapes=[pltpu.VMEM((Tk, 2 * GROUP_W), BF16),
                        pltpu.VMEM((Tk, GROUP_W), BF16),
                        pltpu.VMEM((Tk, GROUP_W), BF16),
                        pltpu.VMEM((Tk, GROUP_W), BF16)])
    return pl.pallas_call(
        functools.partial(_attn_kernel, has_ctx, is_last, T),
        grid_spec=grid_spec,
        out_shape=out_shape,
        compiler_params=pltpu.CompilerParams(dimension_semantics=("arbitrary", "arbitrary"),
                                             vmem_limit_bytes=VMEM_LIMIT),
        name="attn_lat" if has_ctx else "attn_ctx",
    )(l_arr, *args)


def _rope_tables(n_tok):
    n_freq = MLA_ROPE // 4
    inv = ROPE_THETA ** (-jnp.arange(n_freq, dtype=F32) / n_freq)
    n_rows = n_tok // GRID_W
    rowp = jnp.repeat(jnp.arange(n_rows, dtype=F32), GRID_W)
    colp = jnp.tile(jnp.arange(GRID_W, dtype=F32), n_rows)
    ang = jnp.concatenate([rowp[:, None] * inv, colp[:, None] * inv], axis=-1)
    cos, sin = jnp.cos(ang), jnp.sin(ang)
    cos64 = jnp.concatenate([cos, cos], axis=-1)
    sin64 = jnp.concatenate([-sin, sin], axis=-1)
    return jnp.concatenate([cos64, cos64], axis=-1), jnp.concatenate([sin64, sin64], axis=-1)


def _prep_weights(W_in, W_uq, W_ukv, W_out):
    seg = lambda o, n: W_in[:, :, o:o + n]
    pad = jnp.zeros((DEPTH, D_MODEL, LANES - MLA_ROPE - N_ML_GATES), W_in.dtype)
    w_in_r = jnp.concatenate(
        [seg(_O_DQ, GROUP_W), seg(_O_DK, GROUP_W), seg(_O_DV, GROUP_W), seg(_O_ZB, GROUP_W),
         seg(_O_MQ, GROUP_W), seg(_O_MK, GROUP_W), seg(_O_MV, GROUP_W), seg(_O_MO, GROUP_W),
         seg(_O_ZC, GROUP_W), seg(_O_ZA, GROUP_W),
         seg(_O_CKV, MLA_KV_RANK), seg(_O_KR, MLA_ROPE), seg(_O_MG, N_ML_GATES), pad,
         seg(_O_CQ, MLA_Q_RANK)], axis=-1).astype(BF16)
    wq = W_uq.reshape(DEPTH, MLA_Q_RANK, MLA_HEADS, MLA_NOPE + MLA_ROPE)
    wq = jnp.pad(wq, ((0, 0), (0, 0), (0, 0), (0, 2 * HEAD_W - MLA_NOPE - MLA_ROPE)))
    wuq_r = wq.reshape(DEPTH, MLA_Q_RANK, MLA_HEADS * 2 * HEAD_W).astype(BF16)
    wkv = W_ukv.reshape(DEPTH, MLA_KV_RANK, MLA_HEADS, 2, HEAD_W)
    wukv_r = jnp.swapaxes(wkv, 2, 3).reshape(DEPTH, MLA_KV_RANK, 2 * GROUP_W).astype(BF16)
    return w_in_r, wuq_r, wukv_r, W_out.astype(BF16)


def kernel(x_prompt, x_sample, cache_mla_ckv, cache_mla_krope, cache_diff_k, cache_diff_v, state_mlstm_C, state_mlstm_n, state_mlstm_m, c, c_ctx, g_norm, W_mod, b_mod, W_in, mla_q_norm, W_uq, mla_kv_norm, W_ukv, diff_lambda, diff_norm, ml_conv, ml_gate_b, ml_norm, W_out, g_final):
    Bc, Tc, _ = x_prompt.shape
    Bs, Ts, _ = x_sample.shape

    w_in_r, wuq_r, wukv_r, wout_r = _prep_weights(W_in, W_uq, W_ukv, W_out)
    w = {'wuq': wuq_r, 'wukv': wukv_r, 'wout': wout_r,
         'gq': mla_q_norm.reshape(DEPTH, 1, MLA_Q_RANK), 'gkv': mla_kv_norm.reshape(DEPTH, 1, MLA_KV_RANK),
         'gdn': diff_norm.reshape(DEPTH, 1, 2 * DIFF_D), 'lam': diff_lambda,
         'gfin': g_final.reshape(1, D_MODEL)}
    g_norm3 = g_norm.reshape(DEPTH, 1, D_MODEL)
    ml_norm2 = ml_norm.reshape(DEPTH, 1, GROUP_W)
    gate_row = jnp.pad(ml_gate_b.reshape(DEPTH, 1, N_ML_GATES),
                       ((0, 0), (0, 0), (GATE_LANE0, LANES - GATE_LANE0 - N_ML_GATES)))

    cc = jnp.concatenate([c_ctx[None, :], c, jnp.zeros((8 - 1 - Bs, D_MODEL), F32)], axis=0)
    mod_all = _mod_call(cc, W_mod, b_mod).reshape(DEPTH, 8, 1, 3 * D_MODEL)

    x2 = x_prompt.reshape(Bc * Tc, D_MODEL)
    sides = []
    y_prompt = None
    for l in range(DEPTH):
        l_arr = jnp.full((1,), l, jnp.int32)
        main, tail, dk32, dv32 = _proj_call(l_arr, x2, mod_all, g_norm3, w_in_r, Tc, False)
        yc, c_fin, n_fin, m_fin = _mlstm_call(l_arr, main, tail, ml_conv, gate_row, ml_norm2, Tc, False)
        outs = _attn_call(l_arr, main, tail, yc, x2, mod_all, w, Tc, False, l == DEPTH - 1)
        if l == DEPTH - 1:
            x3, y_prompt, ckvn, kro = outs
        else:
            x3, ckvn, kro = outs
        x2 = x3.reshape(Bc * Tc, D_MODEL)
        sides.append((ckvn, kro, dk32.reshape(Bc, Tc, DIFF_HEADS, 2 * DIFF_D),
                      dv32.reshape(Bc, Tc, DIFF_HEADS, 2 * DIFF_D),
                      c_fin.reshape(Bc, 2, ML_HEADS, ML_DK, HEAD_W), n_fin.reshape(Bc, 2, ML_HEADS, ML_DK),
                      m_fin[:, :, 0].reshape(Bc, 2, ML_HEADS)))
    stacked = [jnp.stack([s[i] for s in sides], axis=1) for i in range(7)]

    rope = _rope_tables(Ts)
    ctx = {'ckv': cache_mla_ckv,
           'krope': jnp.pad(cache_mla_krope, ((0, 0), (0, 0), (0, 0), (0, LANES - MLA_ROPE))),
           'dk': cache_diff_k.reshape(Bs, DEPTH, PAST_LEN, GROUP_W),
           'dv': cache_diff_v.reshape(Bs, DEPTH, PAST_LEN, GROUP_W)}
    ctx_state = (state_mlstm_C.reshape(Bs, DEPTH, 2 * ML_HEADS, ML_DK, HEAD_W),
                 state_mlstm_n.reshape(Bs, DEPTH, 2 * ML_HEADS, ML_DK),
                 jnp.broadcast_to(state_mlstm_m.reshape(Bs, DEPTH, 2 * ML_HEADS, 1),
                                  (Bs, DEPTH, 2 * ML_HEADS, LANES)))
    x2 = x_sample.reshape(Bs * Ts, D_MODEL)
    y_sample = None
    for l in range(DEPTH):
        l_arr = jnp.full((1,), l, jnp.int32)
        main, tail = _proj_call(l_arr, x2, mod_all, g_norm3, w_in_r, Ts, True)
        (yc,) = _mlstm_call(l_arr, main, tail, ml_conv, gate_row, ml_norm2, Ts, True, ctx_state)
        outs = _attn_call(l_arr, main, tail, yc, x2, mod_all, w, Ts, True, l == DEPTH - 1, rope, ctx)
        if l == DEPTH - 1:
            x3, y_sample = outs
        else:
            (x3,) = outs
        x2 = x3.reshape(Bs * Ts, D_MODEL)

    return (y_prompt, y_sample, *stacked)
```

```python
import functools
import math

import jax
import jax.numpy as jnp
import numpy as np
from jax import lax
from jax.experimental import pallas as pl
from jax.experimental.pallas import tpu as pltpu

F32 = jnp.float32
BF16 = jnp.bfloat16

D_MODEL = 1024
DEPTH = 4
PAST_LEN = 256
GRID_W = 64
GROUP_W = 512
MLA_HEADS = 4
MLA_NOPE = 128
MLA_ROPE = 64
MLA_Q_RANK = 384
MLA_KV_RANK = 256
DIFF_HEADS = 4
DIFF_D = 64
ML_HEADS = 4
ML_DK = 128
N_ML_GATES = 16
ROPE_THETA = 10000.0
NORM_EPS = 1e-6
MLA_SCALE = (MLA_NOPE + MLA_ROPE) ** -0.5
DIFF_SCALE = DIFF_D ** -0.5

LANES = 128
HEAD_W = 128
ML_CHUNK = 256
PROJ_TM = 512
ATTN_TQ = 256
MAIN_W = 10 * GROUP_W
TAIL_W = 768
N_IN_PAD = MAIN_W + TAIL_W
VMEM_LIMIT = 56 * 1024 * 1024

C_ZA, C_DQ, C_DK, C_DV, C_ZB, C_MQ, C_MK, C_MV, C_MO, C_ZC = range(10)

_IN_SIZES = (MLA_Q_RANK, MLA_KV_RANK, MLA_ROPE, GROUP_W, GROUP_W, GROUP_W, GROUP_W, GROUP_W,
             GROUP_W, GROUP_W, GROUP_W, GROUP_W, GROUP_W, N_ML_GATES)
_IN_OFF = np.concatenate([[0], np.cumsum(_IN_SIZES)])
(_O_CQ, _O_CKV, _O_KR, _O_ZA, _O_DQ, _O_DK, _O_DV, _O_ZB, _O_MQ, _O_MK, _O_MV, _O_MO, _O_ZC,
 _O_MG) = [int(v) for v in _IN_OFF[:-1]]
GATE_LANE0 = MLA_ROPE


def _rms(x, g):
    ms = jnp.mean(x * x, axis=-1, keepdims=True)
    return x * lax.rsqrt(ms + NORM_EPS) * g


def _silu(x):
    return x * jax.nn.sigmoid(x)


def _log_sigmoid(x):
    return jnp.minimum(x, 0.0) - jnp.log1p(jnp.exp(-jnp.abs(x)))


def _swap32(x):
    lane = lax.broadcasted_iota(jnp.int32, x.shape, 1)
    fwd = pltpu.roll(x, LANES - 32, 1)
    bwd = pltpu.roll(x, 32, 1)
    return jnp.where((lane % 64) < 32, fwd, bwd)


def _rope_tile(x, cos, sin):
    return x * cos + _swap32(x) * sin


def _mod_kernel(c_ref, w_ref, b_ref, o_ref):
    a = _silu(c_ref[...]).astype(BF16)
    o_ref[...] = jnp.dot(a, w_ref[...].astype(BF16), preferred_element_type=F32) + b_ref[...]


def _mod_call(cc, W_mod, b_mod):
    tn = 1024
    return pl.pallas_call(
        _mod_kernel,
        grid=(DEPTH, 3 * D_MODEL // tn),
        in_specs=[pl.BlockSpec((8, D_MODEL), lambda l, j: (0, 0)),
                  pl.BlockSpec((None, D_MODEL, tn), lambda l, j: (l, 0, j)),
                  pl.BlockSpec((None, 1, tn), lambda l, j: (l, 0, j))],
        out_specs=pl.BlockSpec((None, 8, tn), lambda l, j: (l, 0, j)),
        out_shape=jax.ShapeDtypeStruct((DEPTH, 8, 3 * D_MODEL), F32),
        compiler_params=pltpu.CompilerParams(dimension_semantics=("arbitrary", "arbitrary"),
                                             vmem_limit_bytes=VMEM_LIMIT),
        name="mod",
    )(cc, W_mod, b_mod.reshape(DEPTH, 1, 3 * D_MODEL))


def _proj_kernel(has_side, T, *refs):
    if has_side:
        (l_ref, x_ref, mod_ref, g_ref, w_ref, wt_ref, dk_in, dv_in,
         main_ref, tail_ref, dk_ref, dv_ref) = refs
        del dk_in, dv_in
    else:
        l_ref, x_ref, mod_ref, g_ref, w_ref, wt_ref, main_ref, tail_ref = refs
    del l_ref
    tm = x_ref.shape[0]
    mod = mod_ref[...]
    y = _rms(x_ref[...], g_ref[...])
    h = (y * (1.0 + mod[:, D_MODEL:2 * D_MODEL]) + mod[:, :D_MODEL]).astype(BF16)
    for s in range(MAIN_W // GROUP_W):
        cols = slice(s * GROUP_W, (s + 1) * GROUP_W)
        acc = jnp.dot(h, w_ref[:, cols], preferred_element_type=F32)
        main_ref[:, cols] = acc.astype(BF16)
        if has_side and s in (C_DK, C_DV):
            side_ref = dk_ref if s == C_DK else dv_ref
            for b in range(tm // T):
                for hd in range(DIFF_HEADS):
                    side_ref[b, pl.ds(hd, T, stride=DIFF_HEADS), :] = (
                        acc[b * T:(b + 1) * T, hd * HEAD_W:(hd + 1) * HEAD_W])
    tail_ref[...] = jnp.dot(h, wt_ref[...], preferred_element_type=F32)


def _proj_call(l_arr, x2, mod_all, g_norm3, w_mid, w_tail, T, has_ctx, side_bufs=None):
    n_tok = x2.shape[0]
    tm = PROJ_TM
    has_side = not has_ctx
    if has_ctx:
        mod_map = lambda i, l: (l[0], 1 + (i * tm) // T, 0, 0)
    else:
        mod_map = lambda i, l: (l[0], 0, 0, 0)
    in_specs = [pl.BlockSpec((tm, D_MODEL), lambda i, l: (i, 0)),
                pl.BlockSpec((None, None, 1, 3 * D_MODEL), mod_map),
                pl.BlockSpec((None, 1, D_MODEL), lambda i, l: (l[0], 0, 0)),
                pl.BlockSpec((None, D_MODEL, MAIN_W), lambda i, l: (l[0], 0, 0), pipeline_mode=pl.Buffered(1)),
                pl.BlockSpec((None, D_MODEL, TAIL_W), lambda i, l: (l[0], 0, 0), pipeline_mode=pl.Buffered(1))]
    args = [x2, mod_all, g_norm3, w_mid, w_tail]
    out_shape = [jax.ShapeDtypeStruct((n_tok, MAIN_W), BF16), jax.ShapeDtypeStruct((n_tok, TAIL_W), F32)]
    out_specs = [pl.BlockSpec((tm, MAIN_W), lambda i, l: (i, 0)), pl.BlockSpec((tm, TAIL_W), lambda i, l: (i, 0))]
    aliases = {}
    if has_side:
        bt = tm // T
        in_specs += [pl.BlockSpec(memory_space=pl.ANY)] * 2
        args += list(side_bufs)
        out_shape += [jax.ShapeDtypeStruct(b.shape, b.dtype) for b in side_bufs]
        out_specs += [pl.BlockSpec((bt, None, T * DIFF_HEADS, HEAD_W), lambda i, l: (i, l[0], 0, 0))] * 2
        aliases = {6: 2, 7: 3}
    grid_spec = pltpu.PrefetchScalarGridSpec(
        num_scalar_prefetch=1, grid=(n_tok // tm,), in_specs=in_specs, out_specs=out_specs)
    return pl.pallas_call(
        functools.partial(_proj_kernel, has_side, T),
        grid_spec=grid_spec,
        out_shape=out_shape,
        input_output_aliases=aliases,
        compiler_params=pltpu.CompilerParams(dimension_semantics=("arbitrary",),
                                             vmem_limit_bytes=VMEM_LIMIT),
        name="proj_lat" if has_ctx else "proj_ctx",
    )(l_arr, *args)


def _mlstm_kernel(has_ctx, T, *refs):
    nc = T // ML_CHUNK
    L = ML_CHUNK
    use_inter = has_ctx or nc > 1
    it = iter(refs)
    l_ref = next(it)
    mq_ref, mk_ref, mv_ref, mo_ref, zc_ref, aux_ref, convw_ref, gb_ref, mln_ref = [next(it) for _ in range(9)]
    if has_ctx:
        c0_ref, n0_ref, m0_ref = [next(it) for _ in range(3)]
    else:
        for _ in range(3):
            next(it)
    yc_ref = next(it)
    if not has_ctx:
        cout_ref, nout_ref, mout_ref = [next(it) for _ in range(3)]
    q_s, k_s, va_s, hf_s, hb_s, caug_s, m_s = [next(it) for _ in range(7)]
    del l_ref

    row = lax.broadcasted_iota(jnp.int32, (T, GROUP_W), 0)

    def conv_silu(u, w3):
        up = jnp.where(row == 0, 0.0, pltpu.roll(u, 1, 0))
        un = jnp.where(row == T - 1, 0.0, pltpu.roll(u, T - 1, 0))
        return _silu(w3[0:1, :] * up + w3[1:2, :] * u + w3[2:3, :] * un)

    convw = convw_ref[...]
    q_s[...] = (conv_silu(mq_ref[...].astype(F32), convw[:, :GROUP_W]) * (ML_DK ** -0.5)).astype(BF16)
    k_s[...] = conv_silu(mk_ref[...].astype(F32), convw[:, GROUP_W:])
    lane_t = lax.broadcasted_iota(jnp.int32, (T, HEAD_W), 1)
    ones_col = jnp.where(lane_t == 0, 1.0, 0.0).astype(BF16)
    for h in range(ML_HEADS):
        va_s[:, 2 * h * HEAD_W:(2 * h + 1) * HEAD_W] = mv_ref[:, h * HEAD_W:(h + 1) * HEAD_W]
        va_s[:, (2 * h + 1) * HEAD_W:(2 * h + 2) * HEAD_W] = ones_col

    lane_c = lax.broadcasted_iota(jnp.int32, (ML_DK, HEAD_W), 1)
    if has_ctx:
        n0 = n0_ref[...].reshape(2 * ML_HEADS, ML_DK)
        n0t = jnp.concatenate([n0, jnp.zeros((ML_DK - 2 * ML_HEADS, ML_DK), F32)], axis=0).T
        m0 = m0_ref[...]
        m_s[...] = m0
        for r in range(2 * ML_HEADS):
            caug_s[r, :, :HEAD_W] = c0_ref[r]
            caug_s[r, :, HEAD_W:] = jnp.where(lane_c == 0, n0t[:, r:r + 1], 0.0)
    else:
        caug_s[...] = jnp.zeros_like(caug_s)
        m_s[...] = jnp.zeros_like(m_s)

    ri = lax.broadcasted_iota(jnp.int32, (L, L), 0)
    ci = lax.broadcasted_iota(jnp.int32, (L, L), 1)
    tril = ri >= ci
    triu = ri <= ci

    def do_dir(d, c):
        mask = tril if d == 0 else triu
        tri_mat = jnp.where(mask, 1.0, 0.0)
        last = L - 1 if d == 0 else 0
        h_s = hf_s if d == 0 else hb_s
        rows = slice(c * L, (c + 1) * L) if isinstance(c, int) else pl.ds(pl.multiple_of(c * L, L), L)
        gcol = aux_ref[rows, :] + gb_ref[...]
        lf = _log_sigmoid(gcol)
        bc = jnp.dot(tri_mat, lf, precision=lax.Precision.HIGHEST, preferred_element_type=F32)
        g_t = gcol.T
        bc_t = bc.T
        for h in range(ML_HEADS):
            r = d * ML_HEADS + h
            ji = GATE_LANE0 + d * 2 * ML_HEADS + h
            jf = ji + ML_HEADS
            bc_col, bc_row = bc[:, jf:jf + 1], bc_t[jf:jf + 1, :]
            ig_col, ig_row = gcol[:, ji:ji + 1], g_t[ji:ji + 1, :]
            m = m_s[r:r + 1, 0:1]
            dmat = jnp.where(mask, bc_col - bc_row + ig_row, -jnp.inf)
            g = bc_col + m
            m_t = jnp.maximum(g, jnp.max(dmat, axis=-1, keepdims=True))
            w_intra = jnp.exp(dmat - m_t)
            hs = slice(h * HEAD_W, (h + 1) * HEAD_W)
            qc = q_s[rows, hs]
            kc = k_s[rows, hs]
            vac = va_s[rows, 2 * h * HEAD_W:(2 * h + 2) * HEAD_W]
            s = lax.dot_general(qc, kc.astype(BF16), (((1,), (1,)), ((), ())), preferred_element_type=F32)
            nd = jnp.dot((s * w_intra).astype(BF16), vac, preferred_element_type=F32)
            if use_inter:
                w_inter = jnp.exp(g - m_t)
                nd = nd + w_inter * jnp.dot(qc, caug_s[r].astype(BF16), preferred_element_type=F32)
            den = nd[:, HEAD_W:HEAD_W + 1]
            h_s[rows, hs] = nd[:, :HEAD_W] / jnp.maximum(jnp.abs(den), jnp.exp(-m_t))
            b_last = bc_col[last:last + 1, :]
            m_new = m_t[last:last + 1, :]
            w_s = jnp.exp(b_last - bc_col + ig_col - m_new)
            kw_t = (kc * w_s).T.astype(BF16)
            upd = jnp.dot(kw_t, vac, preferred_element_type=F32)
            if use_inter:
                upd = upd + jnp.exp(b_last + m - m_new) * caug_s[r]
            caug_s[r] = upd
            m_s[r:r + 1, :] = jnp.broadcast_to(m_new, (1, HEAD_W))

    if nc == 1:
        do_dir(0, 0)
        do_dir(1, 0)
    else:
        def body(i, carry):
            do_dir(0, i)
            do_dir(1, nc - 1 - i)
            return carry
        lax.fori_loop(0, nc, body, 0)

    mln = mln_ref[...]
    for h in range(ML_HEADS):
        hs = slice(h * HEAD_W, (h + 1) * HEAD_W)
        hc = jax.nn.sigmoid(mo_ref[:, hs].astype(F32)) * (hf_s[:, hs] + hb_s[:, hs])
        yc_ref[:, hs] = (_rms(hc, mln[:, hs]) * _silu(zc_ref[:, hs].astype(F32))).astype(BF16)

    if not has_ctx:
        for r in range(2 * ML_HEADS):
            cout_ref[r] = caug_s[r, :, :HEAD_W]
            nout_ref[r:r + 1, :] = caug_s[r, :, HEAD_W:].T[0:1, :]
        mout_ref[...] = m_s[...]


def _mlstm_call(l_arr, main, tail, ml_conv, gate_row, ml_norm2, T, has_ctx, ctx_state=None, side_bufs=None):
    n_tok = main.shape[0]
    B = n_tok // T
    main3 = main.reshape(B, T, MAIN_W)
    tail3 = tail.reshape(B, T, TAIL_W)

    def col(c):
        return pl.BlockSpec((None, T, GROUP_W), lambda b, l, c=c: (b, 0, c))

    in_specs = [col(C_MQ), col(C_MK), col(C_MV), col(C_MO), col(C_ZC),
                pl.BlockSpec((None, T, LANES), lambda b, l: (b, 0, 2)),
                pl.BlockSpec((None, 3, 2 * GROUP_W), lambda b, l: (l[0], 0, 0)),
                pl.BlockSpec((None, 1, LANES), lambda b, l: (l[0], 0, 0)),
                pl.BlockSpec((None, 1, GROUP_W), lambda b, l: (l[0], 0, 0))]
    args = [main3, main3, main3, main3, main3, tail3, ml_conv, gate_row, ml_norm2]
    out_shape = [jax.ShapeDtypeStruct((B, T, GROUP_W), BF16)]
    out_specs = [pl.BlockSpec((None, T, GROUP_W), lambda b, l: (b, 0, 0))]
    if has_ctx:
        c0, n0, m0 = ctx_state
        in_specs += [pl.BlockSpec((None, None, 2 * ML_HEADS, ML_DK, HEAD_W), lambda b, l: (b, l[0], 0, 0, 0)),
                     pl.BlockSpec((None, None, 2 * ML_HEADS, ML_DK), lambda b, l: (b, l[0], 0, 0)),
                     pl.BlockSpec((None, None, 2 * ML_HEADS, LANES), lambda b, l: (b, l[0], 0, 0))]
        args += [c0, n0, m0]
        aliases = {}
    else:
        in_specs += [pl.BlockSpec(memory_space=pl.ANY)] * 3
        args += list(side_bufs)
        out_shape += [jax.ShapeDtypeStruct(b.shape, b.dtype) for b in side_bufs]
        out_specs += [pl.BlockSpec((None, None, 2 * ML_HEADS, ML_DK, HEAD_W), lambda b, l: (b, l[0], 0, 0, 0)),
                      pl.BlockSpec((None, None, 2 * ML_HEADS, ML_DK), lambda b, l: (b, l[0], 0, 0)),
                      pl.BlockSpec((None, None, 2 * ML_HEADS, LANES), lambda b, l: (b, l[0], 0, 0))]
        aliases = {10: 1, 11: 2, 12: 3}
    grid_spec = pltpu.PrefetchScalarGridSpec(
        num_scalar_prefetch=1, grid=(B,), in_specs=in_specs, out_specs=out_specs,
        scratch_shapes=[pltpu.VMEM((T, GROUP_W), BF16),
                        pltpu.VMEM((T, GROUP_W), F32),
                        pltpu.VMEM((T, 2 * GROUP_W), BF16),
                        pltpu.VMEM((T, GROUP_W), F32),
                        pltpu.VMEM((T, GROUP_W), F32),
                        pltpu.VMEM((2 * ML_HEADS, ML_DK, 2 * HEAD_W), F32),
                        pltpu.VMEM((2 * ML_HEADS, LANES), F32)])
    return pl.pallas_call(
        functools.partial(_mlstm_kernel, has_ctx, T),
        grid_spec=grid_spec,
        out_shape=out_shape,
        input_output_aliases=aliases,
        compiler_params=pltpu.CompilerParams(dimension_semantics=("arbitrary",),
                                             vmem_limit_bytes=VMEM_LIMIT),
        name="mlstm_lat" if has_ctx else "mlstm_ctx",
    )(l_arr, *args)


def _attn_kernel(has_ctx, is_last, T, *refs):
    Tk = T + (PAST_LEN if has_ctx else 0)
    it = iter(refs)
    l_ref = next(it)
    (cq_ref, ckv_ref, aux_ref, dq_ref, dk_ref, dv_ref, zb_ref, za_ref, yc_ref, x_ref, mod_ref,
     wuq_ref, wukv_ref, wout_ref, gq_ref, gkv_ref, gdn_ref, lam_ref) = [next(it) for _ in range(18)]
    if is_last:
        gfin_ref = next(it)
    if has_ctx:
        cosq_ref, sinq_ref, cosk_ref, sink_ref, cckv_ref, ckr_ref, cdk_ref, cdv_ref = [next(it) for _ in range(8)]
    else:
        for _ in range(2):
            next(it)
    xo_ref = next(it)
    if is_last:
        yfin_ref = next(it)
    if not has_ctx:
        ckvn_ref, kro_ref = [next(it) for _ in range(2)]
    ka_s, va_s, kb_s, vb_s = [next(it) for _ in range(4)]

    qi = pl.program_id(1)

    @pl.when(qi == 0)
    def _build_keys():
        wukv = wukv_ref[...]
        ckv_n = _rms(ckv_ref[...], gkv_ref[...])
        aux = aux_ref[...]
        if not has_ctx:
            ckvn_ref[...] = ckv_n
            kro_ref[...] = aux[:, :MLA_ROPE]
        kv = jnp.dot(ckv_n.astype(BF16), wukv, preferred_element_type=F32)
        lane = lax.broadcasted_iota(jnp.int32, aux.shape, 1)
        kr = _rope_tile(aux, cosk_ref[...], sink_ref[...]) if has_ctx else aux
        kr = jnp.where(lane < MLA_ROPE, kr, 0.0).astype(BF16)
        for h in range(MLA_HEADS):
            ka_s[0:T, 2 * h * HEAD_W:(2 * h + 1) * HEAD_W] = kv[:, h * HEAD_W:(h + 1) * HEAD_W].astype(BF16)
            ka_s[0:T, (2 * h + 1) * HEAD_W:(2 * h + 2) * HEAD_W] = kr
        va_s[0:T, :] = kv[:, GROUP_W:].astype(BF16)
        vb_s[0:T, :] = dv_ref[...]
        if has_ctx:
            for h in range(DIFF_HEADS):
                hs = slice(h * HEAD_W, (h + 1) * HEAD_W)
                kb_s[0:T, hs] = _rope_tile(dk_ref[:, hs].astype(F32), cosk_ref[...], sink_ref[...]).astype(BF16)
            kvc = jnp.dot(cckv_ref[...].astype(BF16), wukv, preferred_element_type=F32)
            ckr = ckr_ref[...].astype(BF16)
            for h in range(MLA_HEADS):
                ka_s[T:Tk, 2 * h * HEAD_W:(2 * h + 1) * HEAD_W] = kvc[:, h * HEAD_W:(h + 1) * HEAD_W].astype(BF16)
                ka_s[T:Tk, (2 * h + 1) * HEAD_W:(2 * h + 2) * HEAD_W] = ckr
            va_s[T:Tk, :] = kvc[:, GROUP_W:].astype(BF16)
            for h in range(DIFF_HEADS):
                hs = slice(h * HEAD_W, (h + 1) * HEAD_W)
                kb_s[T:Tk, hs] = cdk_ref[pl.ds(h, PAST_LEN, stride=DIFF_HEADS), :].astype(BF16)
                vb_s[T:Tk, hs] = cdv_ref[pl.ds(h, PAST_LEN, stride=DIFF_HEADS), :].astype(BF16)
        else:
            kb_s[0:T, :] = dk_ref[...]

    def softmax_pv(s, v):
        e = jnp.exp(s - jnp.max(s, axis=-1, keepdims=True))
        o = jnp.dot(e.astype(BF16), v, preferred_element_type=F32)
        return o / jnp.sum(e, axis=-1, keepdims=True)

    nt = (((1,), (1,)), ((), ()))
    tq = cq_ref.shape[0]

    qa = jnp.dot(_rms(cq_ref[...], gq_ref[...]).astype(BF16), wuq_ref[...], preferred_element_type=F32)
    y = jnp.zeros((tq, D_MODEL), F32)
    for h in range(MLA_HEADS):
        q_nope = qa[:, 2 * h * HEAD_W:(2 * h + 1) * HEAD_W]
        q_rope = qa[:, (2 * h + 1) * HEAD_W:(2 * h + 2) * HEAD_W]
        if has_ctx:
            q_rope = _rope_tile(q_rope, cosq_ref[...], sinq_ref[...])
        q_h = (jnp.concatenate([q_nope, q_rope], axis=-1) * MLA_SCALE).astype(BF16)
        s = lax.dot_general(q_h, ka_s[:, 2 * h * HEAD_W:(2 * h + 2) * HEAD_W], nt, preferred_element_type=F32)
        hs = slice(h * HEAD_W, (h + 1) * HEAD_W)
        o = softmax_pv(s, va_s[:, hs])
        ya = (o * _silu(za_ref[:, hs].astype(F32))).astype(BF16)
        y = y + jnp.dot(ya, wout_ref[h * HEAD_W:(h + 1) * HEAD_W, :], preferred_element_type=F32)

    lp = lam_ref[...]
    lf = l_ref[0].astype(F32)
    lam_init = 0.8 - 0.6 * jnp.exp(jnp.full((1, 1), -0.3, F32) * lf)
    lam = (jnp.exp(jnp.sum(lp[0:1, :] * lp[1:2, :], axis=-1, keepdims=True))
           - jnp.exp(jnp.sum(lp[2:3, :] * lp[3:4, :], axis=-1, keepdims=True)) + lam_init)
    lane_q = lax.broadcasted_iota(jnp.int32, (tq, HEAD_W), 1)
    for h in range(DIFF_HEADS):
        hs = slice(h * HEAD_W, (h + 1) * HEAD_W)
        q_h = dq_ref[:, hs].astype(F32)
        if has_ctx:
            q_h = _rope_tile(q_h, cosq_ref[...], sinq_ref[...])
        q_h = q_h * DIFF_SCALE
        q1 = jnp.where(lane_q < DIFF_D, q_h, 0.0).astype(BF16)
        q2 = jnp.where(lane_q >= DIFF_D, q_h, 0.0).astype(BF16)
        k_h = kb_s[:, hs]
        v_h = vb_s[:, hs]
        o1 = softmax_pv(lax.dot_general(q1, k_h, nt, preferred_element_type=F32), v_h)
        o2 = softmax_pv(lax.dot_general(q2, k_h, nt, preferred_element_type=F32), v_h)
        o = _rms(o1 - lam * o2, gdn_ref[...]) * (1.0 - lam_init)
        yb = (o * _silu(zb_ref[:, hs].astype(F32))).astype(BF16)
        y = y + jnp.dot(yb, wout_ref[GROUP_W + h * HEAD_W:GROUP_W + (h + 1) * HEAD_W, :],
                        preferred_element_type=F32)

    y = y + jnp.dot(yc_ref[...], wout_ref[2 * GROUP_W:, :], preferred_element_type=F32)
    x_new = x_ref[...] + mod_ref[:, 2 * D_MODEL:] * y
    xo_ref[...] = x_new
    if is_last:
        yfin_ref[...] = _rms(x_new, gfin_ref[...])


def _attn_call(l_arr, main, tail, yc, x2, mod_all, w, T, has_ctx, is_last, rope=None, ctx=None, side_bufs=None):
    n_tok = main.shape[0]
    B = n_tok // T
    tq = ATTN_TQ
    nq = T // tq
    Tk = T + (PAST_LEN if has_ctx else 0)
    main3 = main.reshape(B, T, MAIN_W)
    tail3 = tail.reshape(B, T, TAIL_W)
    x3 = x2.reshape(B, T, D_MODEL)

    def tile(c, width):
        return pl.BlockSpec((None, tq, width), lambda b, q, l, c=c: (b, q, c))

    def full(c, width):
        return pl.BlockSpec((None, T, width), lambda b, q, l, c=c: (b, 0, c))

    def wspec(shape):
        return pl.BlockSpec((None,) + shape, lambda b, q, l: (l[0],) + (0,) * len(shape))

    if has_ctx:
        mod_map = lambda b, q, l: (l[0], 1 + b, 0, 0)
    else:
        mod_map = lambda b, q, l: (l[0], 0, 0, 0)

    in_specs = [tile(1, MLA_Q_RANK),
                full(0, MLA_KV_RANK),
                full(2, LANES),
                tile(C_DQ, GROUP_W), full(C_DK, GROUP_W), full(C_DV, GROUP_W),
                tile(C_ZB, GROUP_W), tile(C_ZA, GROUP_W),
                pl.BlockSpec((None, tq, GROUP_W), lambda b, q, l: (b, q, 0)),
                pl.BlockSpec((None, tq, D_MODEL), lambda b, q, l: (b, q, 0)),
                pl.BlockSpec((None, None, 1, 3 * D_MODEL), mod_map),
                wspec((MLA_Q_RANK, 2 * GROUP_W)), wspec((MLA_KV_RANK, 2 * GROUP_W)),
                wspec((3 * GROUP_W, D_MODEL)),
                wspec((1, MLA_Q_RANK)), wspec((1, MLA_KV_RANK)), wspec((1, HEAD_W)),
                wspec((4, DIFF_D))]
    args = [tail3, tail3, tail3, main3, main3, main3, main3, main3, yc, x3, mod_all,
            w['wuq'], w['wukv'], w['wout'], w['gq'], w['gkv'], w['gdn'], w['lam']]
    if is_last:
        in_specs.append(pl.BlockSpec((1, D_MODEL), lambda b, q, l: (0, 0)))
        args.append(w['gfin'])
    if has_ctx:
        cos_t, sin_t = rope
        in_specs += [pl.BlockSpec((tq, LANES), lambda b, q, l: (q, 0)),
                     pl.BlockSpec((tq, LANES), lambda b, q, l: (q, 0)),
                     pl.BlockSpec((T, LANES), lambda b, q, l: (0, 0)),
                     pl.BlockSpec((T, LANES), lambda b, q, l: (0, 0)),
                     pl.BlockSpec((None, None, PAST_LEN, MLA_KV_RANK), lambda b, q, l: (b, l[0], 0, 0)),
                     pl.BlockSpec((None, None, PAST_LEN, LANES), lambda b, q, l: (b, l[0], 0, 0)),
                     pl.BlockSpec((None, None, PAST_LEN * DIFF_HEADS, HEAD_W), lambda b, q, l: (b, l[0], 0, 0)),
                     pl.BlockSpec((None, None, PAST_LEN * DIFF_HEADS, HEAD_W), lambda b, q, l: (b, l[0], 0, 0))]
        args += [cos_t, sin_t, cos_t, sin_t, ctx['ckv'], ctx['krope'], ctx['dk'], ctx['dv']]
    out_shape = [jax.ShapeDtypeStruct((B, T, D_MODEL), F32)]
    out_specs = [pl.BlockSpec((None, tq, D_MODEL), lambda b, q, l: (b, q, 0))]
    if is_last:
        out_shape.append(jax.ShapeDtypeStruct((B, T, D_MODEL), F32))
        out_specs.append(pl.BlockSpec((None, tq, D_MODEL), lambda b, q, l: (b, q, 0)))
    aliases = {}
    if not has_ctx:
        n_in = 1 + len(args)
        in_specs += [pl.BlockSpec(memory_space=pl.ANY)] * 2
        args += list(side_bufs)
        aliases = {n_in: len(out_shape), n_in + 1: len(out_shape) + 1}
        out_shape += [jax.ShapeDtypeStruct(b.shape, b.dtype) for b in side_bufs]
        out_specs += [pl.BlockSpec((None, None, T, MLA_KV_RANK), lambda b, q, l: (b, l[0], 0, 0)),
                      pl.BlockSpec((None, None, T, MLA_ROPE), lambda b, q, l: (b, l[0], 0, 0))]
    grid_spec = pltpu.PrefetchScalarGridSpec(
        num_scalar_prefetch=1, grid=(B, nq), in_specs=in_specs, out_specs=out_specs,
        scratch_shapes=[pltpu.VMEM((Tk, 2 * GROUP_W), BF16),
                        pltpu.VMEM((Tk, GROUP_W), BF16),
                        pltpu.VMEM((Tk, GROUP_W), BF16),
                        pltpu.VMEM((Tk, GROUP_W), BF16)])
    return pl.pallas_call(
        functools.partial(_attn_kernel, has_ctx, is_last, T),
        grid_spec=grid_spec,
        out_shape=out_shape,
        input_output_aliases=aliases,
        compiler_params=pltpu.CompilerParams(dimension_semantics=("arbitrary", "arbitrary"),
                                             vmem_limit_bytes=VMEM_LIMIT),
        name="attn_lat" if has_ctx else "attn_ctx",
    )(l_arr, *args)


def _rope_tables(n_tok):
    n_freq = MLA_ROPE // 4
    inv = ROPE_THETA ** (-jnp.arange(n_freq, dtype=F32) / n_freq)
    n_rows = n_tok // GRID_W
    rowp = jnp.repeat(jnp.arange(n_rows, dtype=F32), GRID_W)
    colp = jnp.tile(jnp.arange(GRID_W, dtype=F32), n_rows)
    ang = jnp.concatenate([rowp[:, None] * inv, colp[:, None] * inv], axis=-1)
    cos, sin = jnp.cos(ang), jnp.sin(ang)
    cos64 = jnp.concatenate([cos, cos], axis=-1)
    sin64 = jnp.concatenate([-sin, sin], axis=-1)
    return jnp.concatenate([cos64, cos64], axis=-1), jnp.concatenate([sin64, sin64], axis=-1)


def _prep_weights(W_in, W_uq, W_ukv, W_out):
    seg = lambda o, n: W_in[:, :, o:o + n]
    w_mid = seg(_O_ZA, MAIN_W).astype(BF16)
    pad = jnp.zeros((DEPTH, D_MODEL, LANES - MLA_ROPE - N_ML_GATES), W_in.dtype)
    w_tail = jnp.concatenate([seg(_O_CKV, MLA_KV_RANK), seg(_O_KR, MLA_ROPE), seg(_O_MG, N_ML_GATES), pad,
                              seg(_O_CQ, MLA_Q_RANK)], axis=-1).astype(BF16)
    wq = W_uq.reshape(DEPTH, MLA_Q_RANK, MLA_HEADS, MLA_NOPE + MLA_ROPE)
    wq = jnp.pad(wq, ((0, 0), (0, 0), (0, 0), (0, 2 * HEAD_W - MLA_NOPE - MLA_ROPE)))
    wuq_r = wq.reshape(DEPTH, MLA_Q_RANK, MLA_HEADS * 2 * HEAD_W).astype(BF16)
    wkv = W_ukv.reshape(DEPTH, MLA_KV_RANK, MLA_HEADS, 2, HEAD_W)
    wukv_r = jnp.swapaxes(wkv, 2, 3).reshape(DEPTH, MLA_KV_RANK, 2 * GROUP_W).astype(BF16)
    return w_mid, w_tail, wuq_r, wukv_r, W_out.astype(BF16)


def kernel(x_prompt, x_sample, cache_mla_ckv, cache_mla_krope, cache_diff_k, cache_diff_v, state_mlstm_C, state_mlstm_n, state_mlstm_m, c, c_ctx, g_norm, W_mod, b_mod, W_in, mla_q_norm, W_uq, mla_kv_norm, W_ukv, diff_lambda, diff_norm, ml_conv, ml_gate_b, ml_norm, W_out, g_final):
    Bc, Tc, _ = x_prompt.shape
    Bs, Ts, _ = x_sample.shape

    w_mid, w_tail, wuq_r, wukv_r, wout_r = _prep_weights(W_in, W_uq, W_ukv, W_out)
    w = {'wuq': wuq_r, 'wukv': wukv_r, 'wout': wout_r,
         'gq': mla_q_norm.reshape(DEPTH, 1, MLA_Q_RANK), 'gkv': mla_kv_norm.reshape(DEPTH, 1, MLA_KV_RANK),
         'gdn': diff_norm.reshape(DEPTH, 1, 2 * DIFF_D), 'lam': diff_lambda,
         'gfin': g_final.reshape(1, D_MODEL)}
    g_norm3 = g_norm.reshape(DEPTH, 1, D_MODEL)
    ml_norm2 = ml_norm.reshape(DEPTH, 1, GROUP_W)
    gate_row = jnp.pad(ml_gate_b.reshape(DEPTH, 1, N_ML_GATES),
                       ((0, 0), (0, 0), (GATE_LANE0, LANES - GATE_LANE0 - N_ML_GATES)))

    cc = jnp.concatenate([c_ctx[None, :], c, jnp.zeros((8 - 1 - Bs, D_MODEL), F32)], axis=0)
    mod_all = _mod_call(cc, W_mod, b_mod).reshape(DEPTH, 8, 1, 3 * D_MODEL)

    ckvn = jnp.zeros((Bc, DEPTH, Tc, MLA_KV_RANK), F32)
    kro = jnp.zeros((Bc, DEPTH, Tc, MLA_ROPE), F32)
    dk_o = jnp.zeros((Bc, DEPTH, Tc * DIFF_HEADS, HEAD_W), F32)
    dv_o = jnp.zeros((Bc, DEPTH, Tc * DIFF_HEADS, HEAD_W), F32)
    c_o = jnp.zeros((Bc, DEPTH, 2 * ML_HEADS, ML_DK, HEAD_W), F32)
    n_o = jnp.zeros((Bc, DEPTH, 2 * ML_HEADS, ML_DK), F32)
    m_o = jnp.zeros((Bc, DEPTH, 2 * ML_HEADS, LANES), F32)
    x2 = x_prompt.reshape(Bc * Tc, D_MODEL)
    y_prompt = None
    for l in range(DEPTH):
        l_arr = jnp.full((1,), l, jnp.int32)
        main, tail, dk_o, dv_o = _proj_call(l_arr, x2, mod_all, g_norm3, w_mid, w_tail, Tc, False, (dk_o, dv_o))
        yc, c_o, n_o, m_o = _mlstm_call(l_arr, main, tail, ml_conv, gate_row, ml_norm2, Tc, False,
                                        side_bufs=(c_o, n_o, m_o))
        outs = _attn_call(l_arr, main, tail, yc, x2, mod_all, w, Tc, False, l == DEPTH - 1,
                          side_bufs=(ckvn, kro))
        if l == DEPTH - 1:
            x3, y_prompt, ckvn, kro = outs
        else:
            x3, ckvn, kro = outs
        x2 = x3.reshape(Bc * Tc, D_MODEL)
    side_outs = (ckvn, kro,
                 dk_o.reshape(Bc, DEPTH, Tc, DIFF_HEADS, 2 * DIFF_D),
                 dv_o.reshape(Bc, DEPTH, Tc, DIFF_HEADS, 2 * DIFF_D),
                 c_o.reshape(Bc, DEPTH, 2, ML_HEADS, ML_DK, HEAD_W),
                 n_o.reshape(Bc, DEPTH, 2, ML_HEADS, ML_DK),
                 m_o[:, :, :, 0].reshape(Bc, DEPTH, 2, ML_HEADS))

    rope = _rope_tables(Ts)
    ctx = {'ckv': cache_mla_ckv,
           'krope': jnp.pad(cache_mla_krope, ((0, 0), (0, 0), (0, 0), (0, LANES - MLA_ROPE))),
           'dk': cache_diff_k.reshape(Bs, DEPTH, PAST_LEN * DIFF_HEADS, HEAD_W),
           'dv': cache_diff_v.reshape(Bs, DEPTH, PAST_LEN * DIFF_HEADS, HEAD_W)}
    ctx_state = (state_mlstm_C.reshape(Bs, DEPTH, 2 * ML_HEADS, ML_DK, HEAD_W),
                 state_mlstm_n.reshape(Bs, DEPTH, 2 * ML_HEADS, ML_DK),
                 jnp.broadcast_to(state_mlstm_m.reshape(Bs, DEPTH, 2 * ML_HEADS, 1),
                                  (Bs, DEPTH, 2 * ML_HEADS, LANES)))
    x2 = x_sample.reshape(Bs * Ts, D_MODEL)
    y_sample = None
    for l in range(DEPTH):
        l_arr = jnp.full((1,), l, jnp.int32)
        main, tail = _proj_call(l_arr, x2, mod_all, g_norm3, w_mid, w_tail, Ts, True)
        (yc,) = _mlstm_call(l_arr, main, tail, ml_conv, gate_row, ml_norm2, Ts, True, ctx_state)
        outs = _attn_call(l_arr, main, tail, yc, x2, mod_all, w, Ts, True, l == DEPTH - 1, rope, ctx)
        if l == DEPTH - 1:
            x3, y_sample = outs
        else:
            (x3,) = outs
        x2 = x3.reshape(Bs * Ts, D_MODEL)

    return (y_prompt, y_sample, *side_outs)
```

```python
import functools
import math

import jax
import jax.numpy as jnp
import numpy as np
from jax import lax
from jax.experimental import pallas as pl
from jax.experimental.pallas import tpu as pltpu

F32 = jnp.float32
BF16 = jnp.bfloat16

D_MODEL = 1024
DEPTH = 4
PAST_LEN = 256
GRID_W = 64
GROUP_W = 512
MLA_HEADS = 4
MLA_NOPE = 128
MLA_ROPE = 64
MLA_Q_RANK = 384
MLA_KV_RANK = 256
DIFF_HEADS = 4
DIFF_D = 64
ML_HEADS = 4
ML_DK = 128
N_ML_GATES = 16
ROPE_THETA = 10000.0
NORM_EPS = 1e-6
MLA_SCALE = (MLA_NOPE + MLA_ROPE) ** -0.5
DIFF_SCALE = DIFF_D ** -0.5
LOG2E = math.log2(math.e)

LANES = 128
HEAD_W = 128
ML_CHUNK = 256
ML_VT_ROWS = 144
PROJ_TM = 512
ATTN_TQ = 256
MAIN_W = 10 * GROUP_W
TAIL_W = 768
N_IN_PAD = MAIN_W + TAIL_W
VMEM_LIMIT = 56 * 1024 * 1024

C_ZA, C_DQ, C_DK, C_DV, C_ZB, C_MQ, C_MK, C_MV, C_MO, C_ZC = range(10)

_IN_SIZES = (MLA_Q_RANK, MLA_KV_RANK, MLA_ROPE, GROUP_W, GROUP_W, GROUP_W, GROUP_W, GROUP_W,
             GROUP_W, GROUP_W, GROUP_W, GROUP_W, GROUP_W, N_ML_GATES)
_IN_OFF = np.concatenate([[0], np.cumsum(_IN_SIZES)])
(_O_CQ, _O_CKV, _O_KR, _O_ZA, _O_DQ, _O_DK, _O_DV, _O_ZB, _O_MQ, _O_MK, _O_MV, _O_MO, _O_ZC,
 _O_MG) = [int(v) for v in _IN_OFF[:-1]]
GATE_LANE0 = MLA_ROPE


def _rms(x, g):
    ms = jnp.mean(x * x, axis=-1, keepdims=True)
    return x * lax.rsqrt(ms + NORM_EPS) * g


def _silu(x):
    return x * jax.nn.sigmoid(x)


def _log_sigmoid(x):
    return jnp.minimum(x, 0.0) - jnp.log1p(jnp.exp(-jnp.abs(x)))


def _swap32(x):
    lane = lax.broadcasted_iota(jnp.int32, x.shape, 1)
    fwd = pltpu.roll(x, LANES - 32, 1)
    bwd = pltpu.roll(x, 32, 1)
    return jnp.where((lane % 64) < 32, fwd, bwd)


def _rope_tile(x, cos, sin):
    return x * cos + _swap32(x) * sin


def _mod_kernel(c_ref, w_ref, b_ref, o_ref):
    a = _silu(c_ref[...]).astype(BF16)
    o_ref[...] = jnp.dot(a, w_ref[...].astype(BF16), preferred_element_type=F32) + b_ref[...]


def _mod_call(cc, W_mod, b_mod):
    tn = 1024
    return pl.pallas_call(
        _mod_kernel,
        grid=(DEPTH, 3 * D_MODEL // tn),
        in_specs=[pl.BlockSpec((8, D_MODEL), lambda l, j: (0, 0)),
                  pl.BlockSpec((None, D_MODEL, tn), lambda l, j: (l, 0, j)),
                  pl.BlockSpec((None, 1, tn), lambda l, j: (l, 0, j))],
        out_specs=pl.BlockSpec((None, 8, tn), lambda l, j: (l, 0, j)),
        out_shape=jax.ShapeDtypeStruct((DEPTH, 8, 3 * D_MODEL), F32),
        compiler_params=pltpu.CompilerParams(dimension_semantics=("arbitrary", "arbitrary"),
                                             vmem_limit_bytes=VMEM_LIMIT),
        name="mod",
    )(cc, W_mod, b_mod.reshape(DEPTH, 1, 3 * D_MODEL))


def _proj_kernel(has_side, T, *refs):
    if has_side:
        (l_ref, x_ref, mod_ref, g_ref, w_ref, wt_ref, dk_in, dv_in,
         main_ref, tail_ref, dk_ref, dv_ref) = refs
        del dk_in, dv_in
    else:
        l_ref, x_ref, mod_ref, g_ref, w_ref, wt_ref, main_ref, tail_ref = refs
    del l_ref
    tm = x_ref.shape[0]
    mod = mod_ref[...]
    y = _rms(x_ref[...], g_ref[...])
    h = (y * (1.0 + mod[:, D_MODEL:2 * D_MODEL]) + mod[:, :D_MODEL]).astype(BF16)
    for s in range(MAIN_W // GROUP_W):
        cols = slice(s * GROUP_W, (s + 1) * GROUP_W)
        acc = jnp.dot(h, w_ref[:, cols], preferred_element_type=F32)
        if s in (C_ZA, C_ZB, C_ZC):
            main_ref[:, cols] = _silu(acc).astype(BF16)
        elif s == C_MO:
            main_ref[:, cols] = jax.nn.sigmoid(acc).astype(BF16)
        else:
            main_ref[:, cols] = acc.astype(BF16)
        if has_side and s in (C_DK, C_DV):
            side_ref = dk_ref if s == C_DK else dv_ref
            for b in range(tm // T):
                for hd in range(DIFF_HEADS):
                    side_ref[b, pl.ds(hd, T, stride=DIFF_HEADS), :] = (
                        acc[b * T:(b + 1) * T, hd * HEAD_W:(hd + 1) * HEAD_W])
    tail_ref[...] = jnp.dot(h, wt_ref[...], preferred_element_type=F32)


def _proj_call(l_arr, x2, mod_all, g_norm3, w_mid, w_tail, T, has_ctx, side_bufs=None):
    n_tok = x2.shape[0]
    tm = PROJ_TM
    has_side = not has_ctx
    if has_ctx:
        mod_map = lambda i, l: (l[0], 1 + (i * tm) // T, 0, 0)
    else:
        mod_map = lambda i, l: (l[0], 0, 0, 0)
    in_specs = [pl.BlockSpec((tm, D_MODEL), lambda i, l: (i, 0)),
                pl.BlockSpec((None, None, 1, 3 * D_MODEL), mod_map),
                pl.BlockSpec((None, 1, D_MODEL), lambda i, l: (l[0], 0, 0)),
                pl.BlockSpec((None, D_MODEL, MAIN_W), lambda i, l: (l[0], 0, 0), pipeline_mode=pl.Buffered(1)),
                pl.BlockSpec((None, D_MODEL, TAIL_W), lambda i, l: (l[0], 0, 0), pipeline_mode=pl.Buffered(1))]
    args = [x2, mod_all, g_norm3, w_mid, w_tail]
    out_shape = [jax.ShapeDtypeStruct((n_tok, MAIN_W), BF16), jax.ShapeDtypeStruct((n_tok, TAIL_W), F32)]
    out_specs = [pl.BlockSpec((tm, MAIN_W), lambda i, l: (i, 0)), pl.BlockSpec((tm, TAIL_W), lambda i, l: (i, 0))]
    aliases = {}
    if has_side:
        bt = tm // T
        in_specs += [pl.BlockSpec(memory_space=pl.ANY)] * 2
        args += list(side_bufs)
        out_shape += [jax.ShapeDtypeStruct(b.shape, b.dtype) for b in side_bufs]
        out_specs += [pl.BlockSpec((bt, None, T * DIFF_HEADS, HEAD_W), lambda i, l: (i, l[0], 0, 0))] * 2
        aliases = {6: 2, 7: 3}
    grid_spec = pltpu.PrefetchScalarGridSpec(
        num_scalar_prefetch=1, grid=(n_tok // tm,), in_specs=in_specs, out_specs=out_specs)
    return pl.pallas_call(
        functools.partial(_proj_kernel, has_side, T),
        grid_spec=grid_spec,
        out_shape=out_shape,
        input_output_aliases=aliases,
        compiler_params=pltpu.CompilerParams(dimension_semantics=("arbitrary",),
                                             vmem_limit_bytes=VMEM_LIMIT),
        name="proj_lat" if has_ctx else "proj_ctx",
    )(l_arr, *args)


def _mlstm_kernel(has_ctx, T, *refs):
    nc = T // ML_CHUNK
    L = ML_CHUNK
    use_inter = has_ctx or nc > 1
    it = iter(refs)
    l_ref = next(it)
    mq_ref, mk_ref, mv_ref, mo_ref, zc_ref, aux_ref, convw_ref, gb_ref, mln_ref = [next(it) for _ in range(9)]
    if has_ctx:
        c0_ref, n0_ref, m0_ref = [next(it) for _ in range(3)]
    else:
        for _ in range(3):
            next(it)
    yc_ref = next(it)
    if not has_ctx:
        cout_ref, nout_ref, mout_ref = [next(it) for _ in range(3)]
    q_s, k_s, vt_s, hf_s, hb_s, caug_s, m_s, gt_s = [next(it) for _ in range(8)]
    del l_ref

    row_b = lax.broadcasted_iota(jnp.int32, (L, LANES), 0)
    convw = convw_ref[...]
    halo = 16
    for c in range(nc):
        cs = slice(c * L, (c + 1) * L)
        for j in range(2 * GROUP_W // LANES):
            src = mq_ref if j < GROUP_W // LANES else mk_ref
            ls = slice((j % (GROUP_W // LANES)) * LANES, (j % (GROUP_W // LANES) + 1) * LANES)
            u = src[cs, ls].astype(F32)
            prev = src[c * L - halo:c * L, ls].astype(F32)[halo - 1:halo, :] if c > 0 else 0.0
            nxt = src[(c + 1) * L:(c + 1) * L + halo, ls].astype(F32)[0:1, :] if c < nc - 1 else 0.0
            up = jnp.where(row_b == 0, prev, pltpu.roll(u, 1, 0))
            un = jnp.where(row_b == L - 1, nxt, pltpu.roll(u, L - 1, 0))
            w3 = convw[:, j * LANES:(j + 1) * LANES]
            y = _silu(w3[0:1, :] * up + w3[1:2, :] * u + w3[2:3, :] * un)
            if j < GROUP_W // LANES:
                q_s[cs, ls] = (y * (ML_DK ** -0.5)).astype(BF16)
            else:
                k_s[cs, ls] = y
    row16 = lax.broadcasted_iota(jnp.int32, (ML_VT_ROWS - HEAD_W, L), 0)
    ones_rows = jnp.where(row16 == 0, 1.0, 0.0).astype(BF16)
    g_t = (aux_ref[...] + gb_ref[...]).T
    for c in range(nc):
        cs = slice(c * L, (c + 1) * L)
        gt_s[c] = g_t[GATE_LANE0:GATE_LANE0 + N_ML_GATES, cs]
        for h in range(ML_HEADS):
            vt_s[c, h, 0:HEAD_W, :] = mv_ref[cs, h * HEAD_W:(h + 1) * HEAD_W].astype(F32).T.astype(BF16)
            vt_s[c, h, HEAD_W:, :] = ones_rows

    reps = L // LANES
    if has_ctx:
        n0 = n0_ref[...]
        m0 = m0_ref[...]
        m_s[0] = jnp.concatenate([m0] * reps, axis=1)
        m_s[1] = jnp.concatenate([pltpu.roll(m0, ML_HEADS, 0)] * reps, axis=1)
        row_n = lax.broadcasted_iota(jnp.int32, (ML_VT_ROWS - HEAD_W, ML_DK), 0)
        for r in range(2 * ML_HEADS):
            caug_s[r, 0:HEAD_W, :] = c0_ref[r].T
            caug_s[r, HEAD_W:, :] = jnp.where(row_n == 0, n0[r:r + 1, :], 0.0)
    else:
        caug_s[...] = jnp.zeros_like(caug_s)
        m_s[...] = jnp.zeros_like(m_s)

    ri = lax.broadcasted_iota(jnp.int32, (L, L), 0)
    ci = lax.broadcasted_iota(jnp.int32, (L, L), 1)
    lane_r = lax.broadcasted_iota(jnp.int32, (2 * ML_HEADS, L), 1)
    nt = (((1,), (1,)), ((), ()))

    def split3(x):
        hi = x.astype(BF16).astype(F32)
        mid = (x - hi).astype(BF16).astype(F32)
        return hi, mid, (x - hi - mid).astype(BF16).astype(F32)

    def do_dir(d, c):
        fwd = d == 0
        mask = (ri <= ci) if fwd else (ri >= ci)
        tri_t = jnp.where(mask, 1.0, 0.0).astype(BF16)
        last = L - 1 if fwd else 0
        h_s = hf_s if fwd else hb_s
        rows = slice(c * L, (c + 1) * L) if isinstance(c, int) else pl.ds(pl.multiple_of(c * L, L), L)

        g8 = gt_s[c, d * 2 * ML_HEADS:(d + 1) * 2 * ML_HEADS, :]
        hi8, mid8, lo8 = split3(_log_sigmoid(g8))
        stack = jnp.concatenate([hi8, mid8, lo8, jnp.zeros_like(hi8)], axis=0).astype(BF16)
        part8 = jnp.dot(stack, tri_t, preferred_element_type=F32)
        bc8 = part8[0:8] + part8[8:16] + part8[16:24]
        bcs = pltpu.roll(bc8, ML_HEADS, 0)
        a8 = g8 - bcs
        cm8 = a8
        k = 1
        while k < L:
            if fwd:
                cm8 = jnp.maximum(cm8, jnp.where(lane_r >= k, pltpu.roll(cm8, k, 1), -jnp.inf))
            else:
                cm8 = jnp.maximum(cm8, jnp.where(lane_r < L - k, pltpu.roll(cm8, L - k, 1), -jnp.inf))
            k *= 2
        m8 = m_s[d]
        g_row = bcs + m8
        m_t = jnp.maximum(g_row, bcs + cm8)
        w_inter = jnp.exp(g_row - m_t)
        e_inv = jnp.exp(-m_t)
        c_row = bcs - m_t
        b_last = bcs[:, last:last + 1]
        m_new = m_t[:, last:last + 1]
        a_prev = jnp.exp(b_last + m8[:, 0:1] - m_new)
        w_s = jnp.exp(a8 + (b_last - m_new))
        m_s[d] = jnp.broadcast_to(m_new, (2 * ML_HEADS, L))
        a_n = jnp.concatenate([a8, jnp.zeros((LANES - 2 * ML_HEADS, L), F32)], axis=0).T

        heads = range(ML_HEADS)
        hsl = [slice(h * HEAD_W, (h + 1) * HEAD_W) for h in heads]
        qcs = [q_s[rows, hsl[h]] for h in heads]
        kcs = [k_s[rows, hsl[h]].astype(BF16) for h in heads]
        vts = [vt_s[c, h] for h in heads]
        s_ts = [lax.dot_general(kcs[h], qcs[h], nt, preferred_element_type=F32) for h in heads]
        sws = [(s_ts[h] * jnp.exp(jnp.where(mask, a_n[:, h:h + 1] + c_row[h:h + 1, :], -jnp.inf))).astype(BF16)
               for h in heads]
        nds = [jnp.dot(vts[h], sws[h], preferred_element_type=F32) for h in heads]
        if use_inter:
            inters = [lax.dot_general(caug_s[d * ML_HEADS + h].astype(BF16), qcs[h], nt,
                                      preferred_element_type=F32) for h in heads]
            nds = [nds[h] + w_inter[h:h + 1, :] * inters[h] for h in heads]
        for h in heads:
            inv = 1.0 / jnp.maximum(jnp.abs(nds[h][HEAD_W:HEAD_W + 1, :]), e_inv[h:h + 1, :])
            h_s[rows, hsl[h]] = (nds[h][0:HEAD_W, :] * inv).T
        upds = [jnp.dot((vts[h].astype(F32) * w_s[h:h + 1, :]).astype(BF16), kcs[h], preferred_element_type=F32)
                for h in heads]
        for h in heads:
            r = d * ML_HEADS + h
            caug_s[r] = upds[h] + a_prev[h:h + 1, :] * caug_s[r] if use_inter else upds[h]

    if nc == 1:
        do_dir(0, 0)
        do_dir(1, 0)
    else:
        def body(i, carry):
            do_dir(0, i)
            do_dir(1, nc - 1 - i)
            return carry
        lax.fori_loop(0, nc, body, 0)

    mln = mln_ref[...]
    for c in range(nc):
        cs = slice(c * L, (c + 1) * L)
        for h in range(ML_HEADS):
            hs = slice(h * HEAD_W, (h + 1) * HEAD_W)
            hc = mo_ref[cs, hs].astype(F32) * (hf_s[cs, hs] + hb_s[cs, hs])
            yc_ref[cs, hs] = (_rms(hc, mln[:, hs]) * zc_ref[cs, hs].astype(F32)).astype(BF16)

    if not has_ctx:
        for r in range(2 * ML_HEADS):
            cout_ref[r] = caug_s[r, 0:HEAD_W, :].T
            nout_ref[r:r + 1, :] = caug_s[r, HEAD_W:HEAD_W + 1, :]
        for d in range(2):
            mout_ref[d * ML_HEADS:(d + 1) * ML_HEADS, :] = m_s[d, 0:ML_HEADS, 0:LANES]


def _mlstm_call(l_arr, main, tail, ml_conv, gate_row, ml_norm2, T, has_ctx, ctx_state=None, side_bufs=None):
    n_tok = main.shape[0]
    B = n_tok // T
    main3 = main.reshape(B, T, MAIN_W)
    tail3 = tail.reshape(B, T, TAIL_W)

    def col(c):
        return pl.BlockSpec((None, T, GROUP_W), lambda b, l, c=c: (b, 0, c))

    in_specs = [col(C_MQ), col(C_MK), col(C_MV), col(C_MO), col(C_ZC),
                pl.BlockSpec((None, T, LANES), lambda b, l: (b, 0, 2)),
                pl.BlockSpec((None, 3, 2 * GROUP_W), lambda b, l: (l[0], 0, 0)),
                pl.BlockSpec((None, 1, LANES), lambda b, l: (l[0], 0, 0)),
                pl.BlockSpec((None, 1, GROUP_W), lambda b, l: (l[0], 0, 0))]
    args = [main3, main3, main3, main3, main3, tail3, ml_conv, gate_row, ml_norm2]
    out_shape = [jax.ShapeDtypeStruct((B, T, GROUP_W), BF16)]
    out_specs = [pl.BlockSpec((None, T, GROUP_W), lambda b, l: (b, 0, 0))]
    if has_ctx:
        c0, n0, m0 = ctx_state
        in_specs += [pl.BlockSpec((None, None, 2 * ML_HEADS, ML_DK, HEAD_W), lambda b, l: (b, l[0], 0, 0, 0)),
                     pl.BlockSpec((None, None, 2 * ML_HEADS, ML_DK), lambda b, l: (b, l[0], 0, 0)),
                     pl.BlockSpec((None, None, 2 * ML_HEADS, LANES), lambda b, l: (b, l[0], 0, 0))]
        args += [c0, n0, m0]
        aliases = {}
    else:
        in_specs += [pl.BlockSpec(memory_space=pl.ANY)] * 3
        args += list(side_bufs)
        out_shape += [jax.ShapeDtypeStruct(b.shape, b.dtype) for b in side_bufs]
        out_specs += [pl.BlockSpec((None, None, 2 * ML_HEADS, ML_DK, HEAD_W), lambda b, l: (b, l[0], 0, 0, 0)),
                      pl.BlockSpec((None, None, 2 * ML_HEADS, ML_DK), lambda b, l: (b, l[0], 0, 0)),
                      pl.BlockSpec((None, None, 2 * ML_HEADS, LANES), lambda b, l: (b, l[0], 0, 0))]
        aliases = {10: 1, 11: 2, 12: 3}
    grid_spec = pltpu.PrefetchScalarGridSpec(
        num_scalar_prefetch=1, grid=(B,), in_specs=in_specs, out_specs=out_specs,
        scratch_shapes=[pltpu.VMEM((T, GROUP_W), BF16),
                        pltpu.VMEM((T, GROUP_W), F32),
                        pltpu.VMEM((T // ML_CHUNK, ML_HEADS, ML_VT_ROWS, ML_CHUNK), BF16),
                        pltpu.VMEM((T, GROUP_W), F32),
                        pltpu.VMEM((T, GROUP_W), F32),
                        pltpu.VMEM((2 * ML_HEADS, ML_VT_ROWS, ML_DK), F32),
                        pltpu.VMEM((2, 2 * ML_HEADS, ML_CHUNK), F32),
                        pltpu.VMEM((T // ML_CHUNK, N_ML_GATES, ML_CHUNK), F32)])
    return pl.pallas_call(
        functools.partial(_mlstm_kernel, has_ctx, T),
        grid_spec=grid_spec,
        out_shape=out_shape,
        input_output_aliases=aliases,
        compiler_params=pltpu.CompilerParams(dimension_semantics=("arbitrary",),
                                             vmem_limit_bytes=VMEM_LIMIT),
        name="mlstm_lat" if has_ctx else "mlstm_ctx",
    )(l_arr, *args)


def _attn_kernel(has_ctx, is_last, T, *refs):
    Tk = T + (PAST_LEN if has_ctx else 0)
    it = iter(refs)
    l_ref = next(it)
    (cq_ref, ckv_ref, aux_ref, dq_ref, dk_ref, dv_ref, zb_ref, za_ref, yc_ref, x_ref, mod_ref,
     wuq_ref, wukv_ref, wout_ref, gq_ref, gkv_ref, gdn_ref, lam_ref) = [next(it) for _ in range(18)]
    if is_last:
        gfin_ref = next(it)
    if has_ctx:
        cosq_ref, sinq_ref, cosk_ref, sink_ref, cckv_ref, ckr_ref, cdk_ref, cdv_ref = [next(it) for _ in range(8)]
    else:
        for _ in range(2):
            next(it)
    xo_ref = next(it)
    if is_last:
        yfin_ref = next(it)
    if not has_ctx:
        ckvn_ref, kro_ref = [next(it) for _ in range(2)]
    ka_s, va_s, kb_s, vb_s, ycat_s = [next(it) for _ in range(5)]

    qi = pl.program_id(1)

    @pl.when(qi == 0)
    def _build_keys():
        wukv = wukv_ref[...]
        ckv_n = _rms(ckv_ref[...], gkv_ref[...])
        aux = aux_ref[...]
        if not has_ctx:
            ckvn_ref[...] = ckv_n
            kro_ref[...] = aux[:, :MLA_ROPE]
        kv = jnp.dot(ckv_n.astype(BF16), wukv, preferred_element_type=F32)
        lane = lax.broadcasted_iota(jnp.int32, aux.shape, 1)
        kr = _rope_tile(aux, cosk_ref[...], sink_ref[...]) if has_ctx else aux
        kr = jnp.where(lane < MLA_ROPE, kr, 0.0).astype(BF16)
        for h in range(MLA_HEADS):
            ka_s[0:T, 2 * h * HEAD_W:(2 * h + 1) * HEAD_W] = kv[:, h * HEAD_W:(h + 1) * HEAD_W].astype(BF16)
            ka_s[0:T, (2 * h + 1) * HEAD_W:(2 * h + 2) * HEAD_W] = kr
        ones_t = jnp.where(lane == 0, 1.0, 0.0).astype(BF16)
        for h in range(MLA_HEADS):
            hs = slice(h * HEAD_W, (h + 1) * HEAD_W)
            va_s[0:T, 2 * h * HEAD_W:(2 * h + 1) * HEAD_W] = kv[:, GROUP_W + h * HEAD_W:GROUP_W + (h + 1) * HEAD_W].astype(BF16)
            va_s[0:T, (2 * h + 1) * HEAD_W:(2 * h + 2) * HEAD_W] = ones_t
            vb_s[0:T, 2 * h * HEAD_W:(2 * h + 1) * HEAD_W] = dv_ref[:, hs]
            vb_s[0:T, (2 * h + 1) * HEAD_W:(2 * h + 2) * HEAD_W] = ones_t
        if has_ctx:
            for h in range(DIFF_HEADS):
                hs = slice(h * HEAD_W, (h + 1) * HEAD_W)
                kb_s[0:T, hs] = _rope_tile(dk_ref[:, hs].astype(F32), cosk_ref[...], sink_ref[...]).astype(BF16)
            kvc = jnp.dot(cckv_ref[...].astype(BF16), wukv, preferred_element_type=F32)
            ckr = ckr_ref[...].astype(BF16)
            for h in range(MLA_HEADS):
                ka_s[T:Tk, 2 * h * HEAD_W:(2 * h + 1) * HEAD_W] = kvc[:, h * HEAD_W:(h + 1) * HEAD_W].astype(BF16)
                ka_s[T:Tk, (2 * h + 1) * HEAD_W:(2 * h + 2) * HEAD_W] = ckr
            lane_c = lax.broadcasted_iota(jnp.int32, (PAST_LEN, LANES), 1)
            ones_c = jnp.where(lane_c == 0, 1.0, 0.0).astype(BF16)
            for h in range(DIFF_HEADS):
                hs = slice(h * HEAD_W, (h + 1) * HEAD_W)
                va_s[T:Tk, 2 * h * HEAD_W:(2 * h + 1) * HEAD_W] = kvc[:, GROUP_W + h * HEAD_W:GROUP_W + (h + 1) * HEAD_W].astype(BF16)
                va_s[T:Tk, (2 * h + 1) * HEAD_W:(2 * h + 2) * HEAD_W] = ones_c
                kb_s[T:Tk, hs] = cdk_ref[pl.ds(h, PAST_LEN, stride=DIFF_HEADS), :].astype(BF16)
                vb_s[T:Tk, 2 * h * HEAD_W:(2 * h + 1) * HEAD_W] = cdv_ref[pl.ds(h, PAST_LEN, stride=DIFF_HEADS), :].astype(BF16)
                vb_s[T:Tk, (2 * h + 1) * HEAD_W:(2 * h + 2) * HEAD_W] = ones_c
        else:
            kb_s[0:T, :] = dk_ref[...]

    def softmax_pv(s, v_aug):
        e = jnp.exp2(s - jnp.max(s, axis=-1, keepdims=True))
        pv = jnp.dot(e.astype(BF16), v_aug, preferred_element_type=F32)
        return pv[:, :HEAD_W] * (1.0 / pv[:, HEAD_W:HEAD_W + 1])

    nt = (((1,), (1,)), ((), ()))
    tq = cq_ref.shape[0]

    qa = jnp.dot(_rms(cq_ref[...], gq_ref[...]).astype(BF16), wuq_ref[...], preferred_element_type=F32)
    for h in range(MLA_HEADS):
        q_nope = qa[:, 2 * h * HEAD_W:(2 * h + 1) * HEAD_W]
        q_rope = qa[:, (2 * h + 1) * HEAD_W:(2 * h + 2) * HEAD_W]
        if has_ctx:
            q_rope = _rope_tile(q_rope, cosq_ref[...], sinq_ref[...])
        q_h = (jnp.concatenate([q_nope, q_rope], axis=-1) * (MLA_SCALE * LOG2E)).astype(BF16)
        s = lax.dot_general(q_h, ka_s[:, 2 * h * HEAD_W:(2 * h + 2) * HEAD_W], nt, preferred_element_type=F32)
        hs = slice(h * HEAD_W, (h + 1) * HEAD_W)
        o = softmax_pv(s, va_s[:, 2 * h * HEAD_W:(2 * h + 2) * HEAD_W])
        ycat_s[:, hs] = (o * za_ref[:, hs].astype(F32)).astype(BF16)

    lp = lam_ref[...]
    lf = l_ref[0].astype(F32)
    lam_init = 0.8 - 0.6 * jnp.exp(jnp.full((1, 1), -0.3, F32) * lf)
    lam = (jnp.exp(jnp.sum(lp[0:1, :] * lp[1:2, :], axis=-1, keepdims=True))
           - jnp.exp(jnp.sum(lp[2:3, :] * lp[3:4, :], axis=-1, keepdims=True)) + lam_init)
    lane_q = lax.broadcasted_iota(jnp.int32, (tq, HEAD_W), 1)
    for h in range(DIFF_HEADS):
        hs = slice(h * HEAD_W, (h + 1) * HEAD_W)
        q_h = dq_ref[:, hs].astype(F32)
        if has_ctx:
            q_h = _rope_tile(q_h, cosq_ref[...], sinq_ref[...])
        q_h = q_h * (DIFF_SCALE * LOG2E)
        q1 = jnp.where(lane_q < DIFF_D, q_h, 0.0).astype(BF16)
        q2 = jnp.where(lane_q >= DIFF_D, q_h, 0.0).astype(BF16)
        k_h = kb_s[:, hs]
        v_h = vb_s[:, 2 * h * HEAD_W:(2 * h + 2) * HEAD_W]
        o1 = softmax_pv(lax.dot_general(q1, k_h, nt, preferred_element_type=F32), v_h)
        o2 = softmax_pv(lax.dot_general(q2, k_h, nt, preferred_element_type=F32), v_h)
        o = _rms(o1 - lam * o2, gdn_ref[...]) * (1.0 - lam_init)
        ycat_s[:, GROUP_W + h * HEAD_W:GROUP_W + (h + 1) * HEAD_W] = (o * zb_ref[:, hs].astype(F32)).astype(BF16)

    ycat_s[:, 2 * GROUP_W:] = yc_ref[...]
    y = jnp.dot(ycat_s[...], wout_ref[...], preferred_element_type=F32)
    x_new = x_ref[...] + mod_ref[:, 2 * D_MODEL:] * y
    xo_ref[...] = x_new
    if is_last:
        yfin_ref[...] = _rms(x_new, gfin_ref[...])


def _attn_call(l_arr, main, tail, yc, x2, mod_all, w, T, has_ctx, is_last, rope=None, ctx=None, side_bufs=None):
    n_tok = main.shape[0]
    B = n_tok // T
    tq = ATTN_TQ
    nq = T // tq
    Tk = T + (PAST_LEN if has_ctx else 0)
    main3 = main.reshape(B, T, MAIN_W)
    tail3 = tail.reshape(B, T, TAIL_W)
    x3 = x2.reshape(B, T, D_MODEL)

    def tile(c, width):
        return pl.BlockSpec((None, tq, width), lambda b, q, l, c=c: (b, q, c))

    def full(c, width):
        return pl.BlockSpec((None, T, width), lambda b, q, l, c=c: (b, 0, c))

    def wspec(shape):
        return pl.BlockSpec((None,) + shape, lambda b, q, l: (l[0],) + (0,) * len(shape))

    if has_ctx:
        mod_map = lambda b, q, l: (l[0], 1 + b, 0, 0)
    else:
        mod_map = lambda b, q, l: (l[0], 0, 0, 0)

    in_specs = [tile(1, MLA_Q_RANK),
                full(0, MLA_KV_RANK),
                full(2, LANES),
                tile(C_DQ, GROUP_W), full(C_DK, GROUP_W), full(C_DV, GROUP_W),
                tile(C_ZB, GROUP_W), tile(C_ZA, GROUP_W),
                pl.BlockSpec((None, tq, GROUP_W), lambda b, q, l: (b, q, 0)),
                pl.BlockSpec((None, tq, D_MODEL), lambda b, q, l: (b, q, 0)),
                pl.BlockSpec((None, None, 1, 3 * D_MODEL), mod_map),
                wspec((MLA_Q_RANK, 2 * GROUP_W)), wspec((MLA_KV_RANK, 2 * GROUP_W)),
                wspec((3 * GROUP_W, D_MODEL)),
                wspec((1, MLA_Q_RANK)), wspec((1, MLA_KV_RANK)), wspec((1, HEAD_W)),
                wspec((4, DIFF_D))]
    args = [tail3, tail3, tail3, main3, main3, main3, main3, main3, yc, x3, mod_all,
            w['wuq'], w['wukv'], w['wout'], w['gq'], w['gkv'], w['gdn'], w['lam']]
    if is_last:
        in_specs.append(pl.BlockSpec((1, D_MODEL), lambda b, q, l: (0, 0)))
        args.append(w['gfin'])
    if has_ctx:
        cos_t, sin_t = rope
        in_specs += [pl.BlockSpec((tq, LANES), lambda b, q, l: (q, 0)),
                     pl.BlockSpec((tq, LANES), lambda b, q, l: (q, 0)),
                     pl.BlockSpec((T, LANES), lambda b, q, l: (0, 0)),
                     pl.BlockSpec((T, LANES), lambda b, q, l: (0, 0)),
                     pl.BlockSpec((None, None, PAST_LEN, MLA_KV_RANK), lambda b, q, l: (b, l[0], 0, 0)),
                     pl.BlockSpec((None, None, PAST_LEN, LANES), lambda b, q, l: (b, l[0], 0, 0)),
                     pl.BlockSpec((None, None, PAST_LEN * DIFF_HEADS, HEAD_W), lambda b, q, l: (b, l[0], 0, 0)),
                     pl.BlockSpec((None, None, PAST_LEN * DIFF_HEADS, HEAD_W), lambda b, q, l: (b, l[0], 0, 0))]
        args += [cos_t, sin_t, cos_t, sin_t, ctx['ckv'], ctx['krope'], ctx['dk'], ctx['dv']]
    out_shape = [jax.ShapeDtypeStruct((B, T, D_MODEL), F32)]
    out_specs = [pl.BlockSpec((None, tq, D_MODEL), lambda b, q, l: (b, q, 0))]
    if is_last:
        out_shape.append(jax.ShapeDtypeStruct((B, T, D_MODEL), F32))
        out_specs.append(pl.BlockSpec((None, tq, D_MODEL), lambda b, q, l: (b, q, 0)))
    aliases = {}
    if not has_ctx:
        n_in = 1 + len(args)
        in_specs += [pl.BlockSpec(memory_space=pl.ANY)] * 2
        args += list(side_bufs)
        aliases = {n_in: len(out_shape), n_in + 1: len(out_shape) + 1}
        out_shape += [jax.ShapeDtypeStruct(b.shape, b.dtype) for b in side_bufs]
        out_specs += [pl.BlockSpec((None, None, T, MLA_KV_RANK), lambda b, q, l: (b, l[0], 0, 0)),
                      pl.BlockSpec((None, None, T, MLA_ROPE), lambda b, q, l: (b, l[0], 0, 0))]
    grid_spec = pltpu.PrefetchScalarGridSpec(
        num_scalar_prefetch=1, grid=(B, nq), in_specs=in_specs, out_specs=out_specs,
        scratch_shapes=[pltpu.VMEM((Tk, 2 * GROUP_W), BF16),
                        pltpu.VMEM((Tk, 2 * GROUP_W), BF16),
                        pltpu.VMEM((Tk, GROUP_W), BF16),
                        pltpu.VMEM((Tk, 2 * GROUP_W), BF16),
                        pltpu.VMEM((tq, 3 * GROUP_W), BF16)])
    return pl.pallas_call(
        functools.partial(_attn_kernel, has_ctx, is_last, T),
        grid_spec=grid_spec,
        out_shape=out_shape,
        input_output_aliases=aliases,
        compiler_params=pltpu.CompilerParams(dimension_semantics=("arbitrary", "arbitrary"),
                                             vmem_limit_bytes=VMEM_LIMIT),
        name="attn_lat" if has_ctx else "attn_ctx",
    )(l_arr, *args)


def _rope_tables(n_tok):
    n_freq = MLA_ROPE // 4
    inv = ROPE_THETA ** (-jnp.arange(n_freq, dtype=F32) / n_freq)
    n_rows = n_tok // GRID_W
    rowp = jnp.repeat(jnp.arange(n_rows, dtype=F32), GRID_W)
    colp = jnp.tile(jnp.arange(GRID_W, dtype=F32), n_rows)
    ang = jnp.concatenate([rowp[:, None] * inv, colp[:, None] * inv], axis=-1)
    cos, sin = jnp.cos(ang), jnp.sin(ang)
    cos64 = jnp.concatenate([cos, cos], axis=-1)
    sin64 = jnp.concatenate([-sin, sin], axis=-1)
    return jnp.concatenate([cos64, cos64], axis=-1), jnp.concatenate([sin64, sin64], axis=-1)


def _prep_weights(W_in, W_uq, W_ukv, W_out):
    seg = lambda o, n: W_in[:, :, o:o + n]
    w_mid = seg(_O_ZA, MAIN_W).astype(BF16)
    pad = jnp.zeros((DEPTH, D_MODEL, LANES - MLA_ROPE - N_ML_GATES), W_in.dtype)
    w_tail = jnp.concatenate([seg(_O_CKV, MLA_KV_RANK), seg(_O_KR, MLA_ROPE), seg(_O_MG, N_ML_GATES), pad,
                              seg(_O_CQ, MLA_Q_RANK)], axis=-1).astype(BF16)
    wq = W_uq.reshape(DEPTH, MLA_Q_RANK, MLA_HEADS, MLA_NOPE + MLA_ROPE)
    wq = jnp.pad(wq, ((0, 0), (0, 0), (0, 0), (0, 2 * HEAD_W - MLA_NOPE - MLA_ROPE)))
    wuq_r = wq.reshape(DEPTH, MLA_Q_RANK, MLA_HEADS * 2 * HEAD_W).astype(BF16)
    wkv = W_ukv.reshape(DEPTH, MLA_KV_RANK, MLA_HEADS, 2, HEAD_W)
    wukv_r = jnp.swapaxes(wkv, 2, 3).reshape(DEPTH, MLA_KV_RANK, 2 * GROUP_W).astype(BF16)
    return w_mid, w_tail, wuq_r, wukv_r, W_out.astype(BF16)


def kernel(x_prompt, x_sample, cache_mla_ckv, cache_mla_krope, cache_diff_k, cache_diff_v, state_mlstm_C, state_mlstm_n, state_mlstm_m, c, c_ctx, g_norm, W_mod, b_mod, W_in, mla_q_norm, W_uq, mla_kv_norm, W_ukv, diff_lambda, diff_norm, ml_conv, ml_gate_b, ml_norm, W_out, g_final):
    Bc, Tc, _ = x_prompt.shape
    Bs, Ts, _ = x_sample.shape

    w_mid, w_tail, wuq_r, wukv_r, wout_r = _prep_weights(W_in, W_uq, W_ukv, W_out)
    w = {'wuq': wuq_r, 'wukv': wukv_r, 'wout': wout_r,
         'gq': mla_q_norm.reshape(DEPTH, 1, MLA_Q_RANK), 'gkv': mla_kv_norm.reshape(DEPTH, 1, MLA_KV_RANK),
         'gdn': diff_norm.reshape(DEPTH, 1, 2 * DIFF_D), 'lam': diff_lambda,
         'gfin': g_final.reshape(1, D_MODEL)}
    g_norm3 = g_norm.reshape(DEPTH, 1, D_MODEL)
    ml_norm2 = ml_norm.reshape(DEPTH, 1, GROUP_W)
    gate_row = jnp.pad(ml_gate_b.reshape(DEPTH, 1, N_ML_GATES),
                       ((0, 0), (0, 0), (GATE_LANE0, LANES - GATE_LANE0 - N_ML_GATES)))

    cc = jnp.concatenate([c_ctx[None, :], c, jnp.zeros((8 - 1 - Bs, D_MODEL), F32)], axis=0)
    mod_all = _mod_call(cc, W_mod, b_mod).reshape(DEPTH, 8, 1, 3 * D_MODEL)

    ckvn = jnp.zeros((Bc, DEPTH, Tc, MLA_KV_RANK), F32)
    kro = jnp.zeros((Bc, DEPTH, Tc, MLA_ROPE), F32)
    dk_o = jnp.zeros((Bc, DEPTH, Tc * DIFF_HEADS, HEAD_W), F32)
    dv_o = jnp.zeros((Bc, DEPTH, Tc * DIFF_HEADS, HEAD_W), F32)
    c_o = jnp.zeros((Bc, DEPTH, 2 * ML_HEADS, ML_DK, HEAD_W), F32)
    n_o = jnp.zeros((Bc, DEPTH, 2 * ML_HEADS, ML_DK), F32)
    m_o = jnp.zeros((Bc, DEPTH, 2 * ML_HEADS, LANES), F32)
    x2 = x_prompt.reshape(Bc * Tc, D_MODEL)
    y_prompt = None
    for l in range(DEPTH):
        l_arr = jnp.full((1,), l, jnp.int32)
        main, tail, dk_o, dv_o = _proj_call(l_arr, x2, mod_all, g_norm3, w_mid, w_tail, Tc, False, (dk_o, dv_o))
        yc, c_o, n_o, m_o = _mlstm_call(l_arr, main, tail, ml_conv, gate_row, ml_norm2, Tc, False,
                                        side_bufs=(c_o, n_o, m_o))
        outs = _attn_call(l_arr, main, tail, yc, x2, mod_all, w, Tc, False, l == DEPTH - 1,
                          side_bufs=(ckvn, kro))
        if l == DEPTH - 1:
            x3, y_prompt, ckvn, kro = outs
        else:
            x3, ckvn, kro = outs
        x2 = x3.reshape(Bc * Tc, D_MODEL)
    side_outs = (ckvn, kro,
                 dk_o.reshape(Bc, DEPTH, Tc, DIFF_HEADS, 2 * DIFF_D),
                 dv_o.reshape(Bc, DEPTH, Tc, DIFF_HEADS, 2 * DIFF_D),
                 c_o.reshape(Bc, DEPTH, 2, ML_HEADS, ML_DK, HEAD_W),
                 n_o.reshape(Bc, DEPTH, 2, ML_HEADS, ML_DK),
                 m_o[:, :, :, 0].reshape(Bc, DEPTH, 2, ML_HEADS))

    rope = _rope_tables(Ts)
    ctx = {'ckv': cache_mla_ckv,
           'krope': jnp.pad(cache_mla_krope, ((0, 0), (0, 0), (0, 0), (0, LANES - MLA_ROPE))),
           'dk': cache_diff_k.reshape(Bs, DEPTH, PAST_LEN * DIFF_HEADS, HEAD_W),
           'dv': cache_diff_v.reshape(Bs, DEPTH, PAST_LEN * DIFF_HEADS, HEAD_W)}
    ctx_state = (state_mlstm_C.reshape(Bs, DEPTH, 2 * ML_HEADS, ML_DK, HEAD_W),
                 state_mlstm_n.reshape(Bs, DEPTH, 2 * ML_HEADS, ML_DK),
                 jnp.broadcast_to(state_mlstm_m.reshape(Bs, DEPTH, 2 * ML_HEADS, 1),
                                  (Bs, DEPTH, 2 * ML_HEADS, LANES)))
    x2 = x_sample.reshape(Bs * Ts, D_MODEL)
    y_sample = None
    for l in range(DEPTH):
        l_arr = jnp.full((1,), l, jnp.int32)
        main, tail = _proj_call(l_arr, x2, mod_all, g_norm3, w_mid, w_tail, Ts, True)
        (yc,) = _mlstm_call(l_arr, main, tail, ml_conv, gate_row, ml_norm2, Ts, True, ctx_state)
        outs = _attn_call(l_arr, main, tail, yc, x2, mod_all, w, Ts, True, l == DEPTH - 1, rope, ctx)
        if l == DEPTH - 1:
            x3, y_sample = outs
        else:
            (x3,) = outs
        x2 = x3.reshape(Bs * Ts, D_MODEL)

    return (y_prompt, y_sample, *side_outs)
```

```python
import functools
import math

import jax
import jax.numpy as jnp
import numpy as np
from jax import lax
from jax.experimental import pallas as pl
from jax.experimental.pallas import tpu as pltpu

F32 = jnp.float32
BF16 = jnp.bfloat16

D_MODEL = 1024
DEPTH = 4
PAST_LEN = 256
GRID_W = 64
GROUP_W = 512
MLA_HEADS = 4
MLA_NOPE = 128
MLA_ROPE = 64
MLA_Q_RANK = 384
MLA_KV_RANK = 256
DIFF_HEADS = 4
DIFF_D = 64
ML_HEADS = 4
ML_DK = 128
N_ML_GATES = 16
ROPE_THETA = 10000.0
NORM_EPS = 1e-6
MLA_SCALE = (MLA_NOPE + MLA_ROPE) ** -0.5
DIFF_SCALE = DIFF_D ** -0.5
LOG2E = math.log2(math.e)

LANES = 128
HEAD_W = 128
ML_CHUNK = 256
ML_VT_ROWS = 144
PROJ_TM = 512
ATTN_TQ = 256
ATTN_AHEAD_CTX = 12
ATTN_AHEAD_LAT = 4
MAIN_W = 10 * GROUP_W
TAIL_W = 768
N_IN_PAD = MAIN_W + TAIL_W
VMEM_LIMIT = 56 * 1024 * 1024

C_ZA, C_DQ, C_DK, C_DV, C_ZB, C_MQ, C_MK, C_MV, C_MO, C_ZC = range(10)

_IN_SIZES = (MLA_Q_RANK, MLA_KV_RANK, MLA_ROPE, GROUP_W, GROUP_W, GROUP_W, GROUP_W, GROUP_W,
             GROUP_W, GROUP_W, GROUP_W, GROUP_W, GROUP_W, N_ML_GATES)
_IN_OFF = np.concatenate([[0], np.cumsum(_IN_SIZES)])
(_O_CQ, _O_CKV, _O_KR, _O_ZA, _O_DQ, _O_DK, _O_DV, _O_ZB, _O_MQ, _O_MK, _O_MV, _O_MO, _O_ZC,
 _O_MG) = [int(v) for v in _IN_OFF[:-1]]
GATE_LANE0 = MLA_ROPE


def _rms(x, g):
    ms = jnp.mean(x * x, axis=-1, keepdims=True)
    return x * lax.rsqrt(ms + NORM_EPS) * g


def _silu(x):
    return x * jax.nn.sigmoid(x)


def _log_sigmoid(x):
    return jnp.minimum(x, 0.0) - jnp.log1p(jnp.exp(-jnp.abs(x)))


def _swap32(x):
    lane = lax.broadcasted_iota(jnp.int32, x.shape, 1)
    fwd = pltpu.roll(x, LANES - 32, 1)
    bwd = pltpu.roll(x, 32, 1)
    return jnp.where((lane % 64) < 32, fwd, bwd)


def _rope_tile(x, cos, sin):
    return x * cos + _swap32(x) * sin


def _mod_kernel(c_ref, w_ref, b_ref, o_ref):
    a = _silu(c_ref[...]).astype(BF16)
    o_ref[...] = jnp.dot(a, w_ref[...].astype(BF16), preferred_element_type=F32) + b_ref[...]


def _mod_call(cc, W_mod, b_mod):
    tn = 1024
    return pl.pallas_call(
        _mod_kernel,
        grid=(DEPTH, 3 * D_MODEL // tn),
        in_specs=[pl.BlockSpec((8, D_MODEL), lambda l, j: (0, 0)),
                  pl.BlockSpec((None, D_MODEL, tn), lambda l, j: (l, 0, j)),
                  pl.BlockSpec((None, 1, tn), lambda l, j: (l, 0, j))],
        out_specs=pl.BlockSpec((None, 8, tn), lambda l, j: (l, 0, j)),
        out_shape=jax.ShapeDtypeStruct((DEPTH, 8, 3 * D_MODEL), F32),
        compiler_params=pltpu.CompilerParams(dimension_semantics=("arbitrary", "arbitrary"),
                                             vmem_limit_bytes=VMEM_LIMIT),
        name="mod",
    )(cc, W_mod, b_mod.reshape(DEPTH, 1, 3 * D_MODEL))


def _proj_kernel(has_side, T, *refs):
    if has_side:
        (l_ref, x_ref, mod_ref, g_ref, w_ref, wt_ref, dk_in, dv_in,
         main_ref, tail_ref, dk_ref, dv_ref) = refs
        del dk_in, dv_in
    else:
        l_ref, x_ref, mod_ref, g_ref, w_ref, wt_ref, main_ref, tail_ref = refs
    del l_ref
    tm = x_ref.shape[0]
    mod = mod_ref[...]
    y = _rms(x_ref[...], g_ref[...])
    h = (y * (1.0 + mod[:, D_MODEL:2 * D_MODEL]) + mod[:, :D_MODEL]).astype(BF16)
    for s in range(MAIN_W // GROUP_W):
        cols = slice(s * GROUP_W, (s + 1) * GROUP_W)
        acc = jnp.dot(h, w_ref[:, cols], preferred_element_type=F32)
        if s in (C_ZA, C_ZB, C_ZC):
            main_ref[:, cols] = _silu(acc).astype(BF16)
        elif s == C_MO:
            main_ref[:, cols] = jax.nn.sigmoid(acc).astype(BF16)
        else:
            main_ref[:, cols] = acc.astype(BF16)
        if has_side and s in (C_DK, C_DV):
            side_ref = dk_ref if s == C_DK else dv_ref
            for b in range(tm // T):
                for hd in range(DIFF_HEADS):
                    side_ref[b, pl.ds(hd, T, stride=DIFF_HEADS), :] = (
                        acc[b * T:(b + 1) * T, hd * HEAD_W:(hd + 1) * HEAD_W])
    tail_ref[...] = jnp.dot(h, wt_ref[...], preferred_element_type=F32)


def _proj_call(l_arr, x2, mod_all, g_norm3, w_mid, w_tail, T, has_ctx, side_bufs=None):
    n_tok = x2.shape[0]
    tm = PROJ_TM
    has_side = not has_ctx
    if has_ctx:
        mod_map = lambda i, l: (l[0], 1 + (i * tm) // T, 0, 0)
    else:
        mod_map = lambda i, l: (l[0], 0, 0, 0)
    in_specs = [pl.BlockSpec((tm, D_MODEL), lambda i, l: (i, 0)),
                pl.BlockSpec((None, None, 1, 3 * D_MODEL), mod_map),
                pl.BlockSpec((None, 1, D_MODEL), lambda i, l: (l[0], 0, 0)),
                pl.BlockSpec((None, D_MODEL, MAIN_W), lambda i, l: (l[0], 0, 0), pipeline_mode=pl.Buffered(1)),
                pl.BlockSpec((None, D_MODEL, TAIL_W), lambda i, l: (l[0], 0, 0), pipeline_mode=pl.Buffered(1))]
    args = [x2, mod_all, g_norm3, w_mid, w_tail]
    out_shape = [jax.ShapeDtypeStruct((n_tok, MAIN_W), BF16), jax.ShapeDtypeStruct((n_tok, TAIL_W), F32)]
    out_specs = [pl.BlockSpec((tm, MAIN_W), lambda i, l: (i, 0)), pl.BlockSpec((tm, TAIL_W), lambda i, l: (i, 0))]
    aliases = {}
    if has_side:
        bt = tm // T
        in_specs += [pl.BlockSpec(memory_space=pl.ANY)] * 2
        args += list(side_bufs)
        out_shape += [jax.ShapeDtypeStruct(b.shape, b.dtype) for b in side_bufs]
        out_specs += [pl.BlockSpec((bt, None, T * DIFF_HEADS, HEAD_W), lambda i, l: (i, l[0], 0, 0))] * 2
        aliases = {6: 2, 7: 3}
    grid_spec = pltpu.PrefetchScalarGridSpec(
        num_scalar_prefetch=1, grid=(n_tok // tm,), in_specs=in_specs, out_specs=out_specs)
    return pl.pallas_call(
        functools.partial(_proj_kernel, has_side, T),
        grid_spec=grid_spec,
        out_shape=out_shape,
        input_output_aliases=aliases,
        compiler_params=pltpu.CompilerParams(dimension_semantics=("arbitrary",),
                                             vmem_limit_bytes=VMEM_LIMIT),
        name="proj_lat" if has_ctx else "proj_ctx",
    )(l_arr, *args)


def _mlstm_kernel(has_ctx, T, *refs):
    nc = T // ML_CHUNK
    L = ML_CHUNK
    use_inter = has_ctx or nc > 1
    it = iter(refs)
    l_ref = next(it)
    mq_ref, mk_ref, mv_ref, mo_ref, zc_ref, aux_ref, convw_ref, gb_ref, mln_ref = [next(it) for _ in range(9)]
    if has_ctx:
        c0_ref, n0_ref, m0_ref = [next(it) for _ in range(3)]
    else:
        for _ in range(3):
            next(it)
    yc_ref = next(it)
    if not has_ctx:
        cout_ref, nout_ref, mout_ref = [next(it) for _ in range(3)]
    q_s, k_s, vt_s, hf_s, hb_s, caug_s, m_s, gt_s = [next(it) for _ in range(8)]
    del l_ref

    row_b = lax.broadcasted_iota(jnp.int32, (L, LANES), 0)
    convw = convw_ref[...]
    halo = 16
    for c in range(nc):
        cs = slice(c * L, (c + 1) * L)
        for j in range(2 * GROUP_W // LANES):
            src = mq_ref if j < GROUP_W // LANES else mk_ref
            ls = slice((j % (GROUP_W // LANES)) * LANES, (j % (GROUP_W // LANES) + 1) * LANES)
            u = src[cs, ls].astype(F32)
            prev = src[c * L - halo:c * L, ls].astype(F32)[halo - 1:halo, :] if c > 0 else 0.0
            nxt = src[(c + 1) * L:(c + 1) * L + halo, ls].astype(F32)[0:1, :] if c < nc - 1 else 0.0
            up = jnp.where(row_b == 0, prev, pltpu.roll(u, 1, 0))
            un = jnp.where(row_b == L - 1, nxt, pltpu.roll(u, L - 1, 0))
            w3 = convw[:, j * LANES:(j + 1) * LANES]
            y = _silu(w3[0:1, :] * up + w3[1:2, :] * u + w3[2:3, :] * un)
            if j < GROUP_W // LANES:
                q_s[cs, ls] = (y * (ML_DK ** -0.5)).astype(BF16)
            else:
                k_s[cs, ls] = y
    row16 = lax.broadcasted_iota(jnp.int32, (ML_VT_ROWS - HEAD_W, L), 0)
    ones_rows = jnp.where(row16 == 0, 1.0, 0.0).astype(BF16)
    g_t = (aux_ref[...] + gb_ref[...]).T
    for c in range(nc):
        cs = slice(c * L, (c + 1) * L)
        gt_s[c] = g_t[GATE_LANE0:GATE_LANE0 + N_ML_GATES, cs]
        for h in range(ML_HEADS):
            vt_s[c, h, 0:HEAD_W, :] = mv_ref[cs, h * HEAD_W:(h + 1) * HEAD_W].astype(F32).T.astype(BF16)
            vt_s[c, h, HEAD_W:, :] = ones_rows

    reps = L // LANES
    if has_ctx:
        n0 = n0_ref[...]
        m0 = m0_ref[...]
        m_s[0] = jnp.concatenate([m0] * reps, axis=1)
        m_s[1] = jnp.concatenate([pltpu.roll(m0, ML_HEADS, 0)] * reps, axis=1)
        row_n = lax.broadcasted_iota(jnp.int32, (ML_VT_ROWS - HEAD_W, ML_DK), 0)
        for r in range(2 * ML_HEADS):
            caug_s[r, 0:HEAD_W, :] = c0_ref[r].T
            caug_s[r, HEAD_W:, :] = jnp.where(row_n == 0, n0[r:r + 1, :], 0.0)
    else:
        caug_s[...] = jnp.zeros_like(caug_s)
        m_s[...] = jnp.zeros_like(m_s)

    ri = lax.broadcasted_iota(jnp.int32, (L, L), 0)
    ci = lax.broadcasted_iota(jnp.int32, (L, L), 1)
    lane_r = lax.broadcasted_iota(jnp.int32, (2 * ML_HEADS, L), 1)
    nt = (((1,), (1,)), ((), ()))

    def split3(x):
        hi = x.astype(BF16).astype(F32)
        mid = (x - hi).astype(BF16).astype(F32)
        return hi, mid, (x - hi - mid).astype(BF16).astype(F32)

    masks = ((ri <= ci), (ri >= ci))

    def gate_rows(d, c):
        fwd = d == 0
        tri_t = jnp.where(masks[d], 1.0, 0.0).astype(BF16)
        last = L - 1 if fwd else 0

        g8 = gt_s[c, d * 2 * ML_HEADS:(d + 1) * 2 * ML_HEADS, :]
        hi8, mid8, lo8 = split3(_log_sigmoid(g8))
        stack = jnp.concatenate([hi8, mid8, lo8, jnp.zeros_like(hi8)], axis=0).astype(BF16)
        part8 = jnp.dot(stack, tri_t, preferred_element_type=F32)
        bc8 = part8[0:8] + part8[8:16] + part8[16:24]
        bcs = pltpu.roll(bc8, ML_HEADS, 0)
        a8 = g8 - bcs
        cm8 = a8
        k = 1
        while k < L:
            if fwd:
                cm8 = jnp.maximum(cm8, jnp.where(lane_r >= k, pltpu.roll(cm8, k, 1), -jnp.inf))
            else:
                cm8 = jnp.maximum(cm8, jnp.where(lane_r < L - k, pltpu.roll(cm8, L - k, 1), -jnp.inf))
            k *= 2
        m8 = m_s[d]
        g_row = bcs + m8
        m_t = jnp.maximum(g_row, bcs + cm8)
        w_inter = jnp.exp(g_row - m_t)
        e_inv = jnp.exp(-m_t)
        c_row = bcs - m_t
        b_last = bcs[:, last:last + 1]
        m_new = m_t[:, last:last + 1]
        a_prev = jnp.exp(b_last + m8[:, 0:1] - m_new)
        w_s = jnp.exp(a8 + (b_last - m_new))
        m_s[d] = jnp.broadcast_to(m_new, (2 * ML_HEADS, L))
        a_n = jnp.concatenate([a8, jnp.zeros((LANES - 2 * ML_HEADS, L), F32)], axis=0).T
        return a_n, c_row, w_inter, e_inv, w_s, a_prev

    def do_pair(c_f, c_b):
        stats = (gate_rows(0, c_f), gate_rows(1, c_b))
        combos = [(d, h) for d in range(2) for h in range(ML_HEADS)]
        rows = []
        for c in (c_f, c_b):
            rows.append(slice(c * L, (c + 1) * L) if isinstance(c, int) else pl.ds(pl.multiple_of(c * L, L), L))
        chunk = (c_f, c_b)
        hsl = [slice(h * HEAD_W, (h + 1) * HEAD_W) for h in range(ML_HEADS)]
        qcs = [q_s[rows[d], hsl[h]] for d, h in combos]
        kcs = [k_s[rows[d], hsl[h]].astype(BF16) for d, h in combos]
        vts = [vt_s[chunk[d], h] for d, h in combos]
        n = range(len(combos))
        s_ts = [lax.dot_general(kcs[i], qcs[i], nt, preferred_element_type=F32) for i in n]
        sws = []
        for i, (d, h) in enumerate(combos):
            a_n, c_row = stats[d][0], stats[d][1]
            w_t = jnp.exp(jnp.where(masks[d], a_n[:, h:h + 1] + c_row[h:h + 1, :], -jnp.inf))
            sws.append((s_ts[i] * w_t).astype(BF16))
        nds = [jnp.dot(vts[i], sws[i], preferred_element_type=F32) for i in n]
        if use_inter:
            inters = [lax.dot_general(caug_s[d * ML_HEADS + h].astype(BF16), qcs[i], nt,
                                      preferred_element_type=F32) for i, (d, h) in enumerate(combos)]
            nds = [nds[i] + stats[d][2][h:h + 1, :] * inters[i] for i, (d, h) in enumerate(combos)]
        for i, (d, h) in enumerate(combos):
            inv = 1.0 / jnp.maximum(jnp.abs(nds[i][HEAD_W:HEAD_W + 1, :]), stats[d][3][h:h + 1, :])
            h_s = hf_s if d == 0 else hb_s
            h_s[rows[d], hsl[h]] = (nds[i][0:HEAD_W, :] * inv).T
        upds = [jnp.dot((vts[i].astype(F32) * stats[d][4][h:h + 1, :]).astype(BF16), kcs[i],
                        preferred_element_type=F32) for i, (d, h) in enumerate(combos)]
        for i, (d, h) in enumerate(combos):
            r = d * ML_HEADS + h
            caug_s[r] = upds[i] + stats[d][5][h:h + 1, :] * caug_s[r] if use_inter else upds[i]

    if nc == 1:
        do_pair(0, 0)
    else:
        def body(i, carry):
            do_pair(i, nc - 1 - i)
            return carry
        lax.fori_loop(0, nc, body, 0)

    mln = mln_ref[...]
    for c in range(nc):
        cs = slice(c * L, (c + 1) * L)
        for h in range(ML_HEADS):
            hs = slice(h * HEAD_W, (h + 1) * HEAD_W)
            hc = mo_ref[cs, hs].astype(F32) * (hf_s[cs, hs] + hb_s[cs, hs])
            yc_ref[cs, hs] = (_rms(hc, mln[:, hs]) * zc_ref[cs, hs].astype(F32)).astype(BF16)

    if not has_ctx:
        for r in range(2 * ML_HEADS):
            cout_ref[r] = caug_s[r, 0:HEAD_W, :].T
            nout_ref[r:r + 1, :] = caug_s[r, HEAD_W:HEAD_W + 1, :]
        for d in range(2):
            mout_ref[d * ML_HEADS:(d + 1) * ML_HEADS, :] = m_s[d, 0:ML_HEADS, 0:LANES]


def _mlstm_call(l_arr, main, tail, ml_conv, gate_row, ml_norm2, T, has_ctx, ctx_state=None, side_bufs=None):
    n_tok = main.shape[0]
    B = n_tok // T
    main3 = main.reshape(B, T, MAIN_W)
    tail3 = tail.reshape(B, T, TAIL_W)

    def col(c):
        return pl.BlockSpec((None, T, GROUP_W), lambda b, l, c=c: (b, 0, c))

    in_specs = [col(C_MQ), col(C_MK), col(C_MV), col(C_MO), col(C_ZC),
                pl.BlockSpec((None, T, LANES), lambda b, l: (b, 0, 2)),
                pl.BlockSpec((None, 3, 2 * GROUP_W), lambda b, l: (l[0], 0, 0)),
                pl.BlockSpec((None, 1, LANES), lambda b, l: (l[0], 0, 0)),
                pl.BlockSpec((None, 1, GROUP_W), lambda b, l: (l[0], 0, 0))]
    args = [main3, main3, main3, main3, main3, tail3, ml_conv, gate_row, ml_norm2]
    out_shape = [jax.ShapeDtypeStruct((B, T, GROUP_W), BF16)]
    out_specs = [pl.BlockSpec((None, T, GROUP_W), lambda b, l: (b, 0, 0))]
    if has_ctx:
        c0, n0, m0 = ctx_state
        in_specs += [pl.BlockSpec((None, None, 2 * ML_HEADS, ML_DK, HEAD_W), lambda b, l: (b, l[0], 0, 0, 0)),
                     pl.BlockSpec((None, None, 2 * ML_HEADS, ML_DK), lambda b, l: (b, l[0], 0, 0)),
                     pl.BlockSpec((None, None, 2 * ML_HEADS, LANES), lambda b, l: (b, l[0], 0, 0))]
        args += [c0, n0, m0]
        aliases = {}
    else:
        in_specs += [pl.BlockSpec(memory_space=pl.ANY)] * 3
        args += list(side_bufs)
        out_shape += [jax.ShapeDtypeStruct(b.shape, b.dtype) for b in side_bufs]
        out_specs += [pl.BlockSpec((None, None, 2 * ML_HEADS, ML_DK, HEAD_W), lambda b, l: (b, l[0], 0, 0, 0)),
                      pl.BlockSpec((None, None, 2 * ML_HEADS, ML_DK), lambda b, l: (b, l[0], 0, 0)),
                      pl.BlockSpec((None, None, 2 * ML_HEADS, LANES), lambda b, l: (b, l[0], 0, 0))]
        aliases = {10: 1, 11: 2, 12: 3}
    grid_spec = pltpu.PrefetchScalarGridSpec(
        num_scalar_prefetch=1, grid=(B,), in_specs=in_specs, out_specs=out_specs,
        scratch_shapes=[pltpu.VMEM((T, GROUP_W), BF16),
                        pltpu.VMEM((T, GROUP_W), F32),
                        pltpu.VMEM((T // ML_CHUNK, ML_HEADS, ML_VT_ROWS, ML_CHUNK), BF16),
                        pltpu.VMEM((T, GROUP_W), F32),
                        pltpu.VMEM((T, GROUP_W), F32),
                        pltpu.VMEM((2 * ML_HEADS, ML_VT_ROWS, ML_DK), F32),
                        pltpu.VMEM((2, 2 * ML_HEADS, ML_CHUNK), F32),
                        pltpu.VMEM((T // ML_CHUNK, N_ML_GATES, ML_CHUNK), F32)])
    return pl.pallas_call(
        functools.partial(_mlstm_kernel, has_ctx, T),
        grid_spec=grid_spec,
        out_shape=out_shape,
        input_output_aliases=aliases,
        compiler_params=pltpu.CompilerParams(dimension_semantics=("arbitrary",),
                                             vmem_limit_bytes=VMEM_LIMIT),
        name="mlstm_lat" if has_ctx else "mlstm_ctx",
    )(l_arr, *args)


def _attn_kernel(has_ctx, is_last, T, *refs):
    Tk = T + (PAST_LEN if has_ctx else 0)
    it = iter(refs)
    l_ref = next(it)
    (cq_ref, ckv_ref, aux_ref, dq_ref, dk_ref, dv_ref, zb_ref, za_ref, yc_ref, x_ref, mod_ref,
     wuq_ref, wukv_ref, wout_ref, gq_ref, gkv_ref, gdn_ref, lam_ref) = [next(it) for _ in range(18)]
    if is_last:
        gfin_ref = next(it)
    if has_ctx:
        cosq_ref, sinq_ref, cosk_ref, sink_ref, cckv_ref, ckr_ref, cdk_ref, cdv_ref = [next(it) for _ in range(8)]
    else:
        for _ in range(2):
            next(it)
    xo_ref = next(it)
    if is_last:
        yfin_ref = next(it)
    if not has_ctx:
        ckvn_ref, kro_ref = [next(it) for _ in range(2)]
    ka_s, va_s, kb_s, vb_s, ycat_s = [next(it) for _ in range(5)]

    qi = pl.program_id(1)

    @pl.when(qi == 0)
    def _build_keys():
        wukv = wukv_ref[...]
        ckv_n = _rms(ckv_ref[...], gkv_ref[...])
        aux = aux_ref[...]
        if not has_ctx:
            ckvn_ref[...] = ckv_n
            kro_ref[...] = aux[:, :MLA_ROPE]
        kv = jnp.dot(ckv_n.astype(BF16), wukv, preferred_element_type=F32)
        lane = lax.broadcasted_iota(jnp.int32, aux.shape, 1)
        kr = _rope_tile(aux, cosk_ref[...], sink_ref[...]) if has_ctx else aux
        kr = jnp.where(lane < MLA_ROPE, kr, 0.0).astype(BF16)
        for h in range(MLA_HEADS):
            ka_s[0:T, 2 * h * HEAD_W:(2 * h + 1) * HEAD_W] = kv[:, h * HEAD_W:(h + 1) * HEAD_W].astype(BF16)
            ka_s[0:T, (2 * h + 1) * HEAD_W:(2 * h + 2) * HEAD_W] = kr
        ones_t = jnp.where(lane == 0, 1.0, 0.0).astype(BF16)
        for h in range(MLA_HEADS):
            hs = slice(h * HEAD_W, (h + 1) * HEAD_W)
            va_s[0:T, 2 * h * HEAD_W:(2 * h + 1) * HEAD_W] = kv[:, GROUP_W + h * HEAD_W:GROUP_W + (h + 1) * HEAD_W].astype(BF16)
            va_s[0:T, (2 * h + 1) * HEAD_W:(2 * h + 2) * HEAD_W] = ones_t
            vb_s[0:T, 2 * h * HEAD_W:(2 * h + 1) * HEAD_W] = dv_ref[:, hs]
            vb_s[0:T, (2 * h + 1) * HEAD_W:(2 * h + 2) * HEAD_W] = ones_t
        if has_ctx:
            for h in range(DIFF_HEADS):
                hs = slice(h * HEAD_W, (h + 1) * HEAD_W)
                kb_s[0:T, hs] = _rope_tile(dk_ref[:, hs].astype(F32), cosk_ref[...], sink_ref[...]).astype(BF16)
            kvc = jnp.dot(cckv_ref[...].astype(BF16), wukv, preferred_element_type=F32)
            ckr = ckr_ref[...].astype(BF16)
            for h in range(MLA_HEADS):
                ka_s[T:Tk, 2 * h * HEAD_W:(2 * h + 1) * HEAD_W] = kvc[:, h * HEAD_W:(h + 1) * HEAD_W].astype(BF16)
                ka_s[T:Tk, (2 * h + 1) * HEAD_W:(2 * h + 2) * HEAD_W] = ckr
            lane_c = lax.broadcasted_iota(jnp.int32, (PAST_LEN, LANES), 1)
            ones_c = jnp.where(lane_c == 0, 1.0, 0.0).astype(BF16)
            for h in range(DIFF_HEADS):
                hs = slice(h * HEAD_W, (h + 1) * HEAD_W)
                va_s[T:Tk, 2 * h * HEAD_W:(2 * h + 1) * HEAD_W] = kvc[:, GROUP_W + h * HEAD_W:GROUP_W + (h + 1) * HEAD_W].astype(BF16)
                va_s[T:Tk, (2 * h + 1) * HEAD_W:(2 * h + 2) * HEAD_W] = ones_c
                kb_s[T:Tk, hs] = cdk_ref[pl.ds(h, PAST_LEN, stride=DIFF_HEADS), :].astype(BF16)
                vb_s[T:Tk, 2 * h * HEAD_W:(2 * h + 1) * HEAD_W] = cdv_ref[pl.ds(h, PAST_LEN, stride=DIFF_HEADS), :].astype(BF16)
                vb_s[T:Tk, (2 * h + 1) * HEAD_W:(2 * h + 2) * HEAD_W] = ones_c
        else:
            kb_s[0:T, :] = dk_ref[...]

    nt = (((1,), (1,)), ((), ()))
    tq = cq_ref.shape[0]

    maps = []
    qa = jnp.dot(_rms(cq_ref[...], gq_ref[...]).astype(BF16), wuq_ref[...], preferred_element_type=F32)
    for h in range(MLA_HEADS):
        q_nope = qa[:, 2 * h * HEAD_W:(2 * h + 1) * HEAD_W]
        q_rope = qa[:, (2 * h + 1) * HEAD_W:(2 * h + 2) * HEAD_W]
        if has_ctx:
            q_rope = _rope_tile(q_rope, cosq_ref[...], sinq_ref[...])
        q_h = (jnp.concatenate([q_nope, q_rope], axis=-1) * (MLA_SCALE * LOG2E)).astype(BF16)
        two = slice(2 * h * HEAD_W, (2 * h + 2) * HEAD_W)
        maps.append((q_h, functools.partial(lambda sl: ka_s[:, sl], two),
                     functools.partial(lambda sl: va_s[:, sl], two)))
    lane_q = lax.broadcasted_iota(jnp.int32, (tq, HEAD_W), 1)
    for h in range(DIFF_HEADS):
        hs = slice(h * HEAD_W, (h + 1) * HEAD_W)
        q_h = dq_ref[:, hs].astype(F32)
        if has_ctx:
            q_h = _rope_tile(q_h, cosq_ref[...], sinq_ref[...])
        q_h = q_h * (DIFF_SCALE * LOG2E)
        two = slice(2 * h * HEAD_W, (2 * h + 2) * HEAD_W)
        for q_m in (jnp.where(lane_q < DIFF_D, q_h, 0.0), jnp.where(lane_q >= DIFF_D, q_h, 0.0)):
            maps.append((q_m.astype(BF16), functools.partial(lambda sl: kb_s[:, sl], hs),
                         functools.partial(lambda sl: vb_s[:, sl], two)))

    def scores(i):
        return lax.dot_general(maps[i][0], maps[i][1](), nt, preferred_element_type=F32)

    def softmax_pv(i, s):
        e = jnp.exp2(s - jnp.max(s, axis=-1, keepdims=True))
        pv = jnp.dot(e.astype(BF16), maps[i][2](), preferred_element_type=F32)
        return pv[:, :HEAD_W] * (1.0 / pv[:, HEAD_W:HEAD_W + 1])

    n_maps = len(maps)
    ahead = min(ATTN_AHEAD_CTX if not has_ctx else ATTN_AHEAD_LAT, n_maps)
    pending = {i: scores(i) for i in range(ahead)}
    outs = []
    for i in range(n_maps):
        if i + ahead < n_maps:
            pending[i + ahead] = scores(i + ahead)
        outs.append(softmax_pv(i, pending.pop(i)))

    for h in range(MLA_HEADS):
        hs = slice(h * HEAD_W, (h + 1) * HEAD_W)
        ycat_s[:, hs] = (outs[h] * za_ref[:, hs].astype(F32)).astype(BF16)

    lp = lam_ref[...]
    lf = l_ref[0].astype(F32)
    lam_init = 0.8 - 0.6 * jnp.exp(jnp.full((1, 1), -0.3, F32) * lf)
    lam = (jnp.exp(jnp.sum(lp[0:1, :] * lp[1:2, :], axis=-1, keepdims=True))
           - jnp.exp(jnp.sum(lp[2:3, :] * lp[3:4, :], axis=-1, keepdims=True)) + lam_init)
    for h in range(DIFF_HEADS):
        hs = slice(h * HEAD_W, (h + 1) * HEAD_W)
        o1, o2 = outs[MLA_HEADS + 2 * h], outs[MLA_HEADS + 2 * h + 1]
        o = _rms(o1 - lam * o2, gdn_ref[...]) * (1.0 - lam_init)
        ycat_s[:, GROUP_W + h * HEAD_W:GROUP_W + (h + 1) * HEAD_W] = (o * zb_ref[:, hs].astype(F32)).astype(BF16)

    ycat_s[:, 2 * GROUP_W:] = yc_ref[...]
    y = jnp.dot(ycat_s[...], wout_ref[...], preferred_element_type=F32)
    x_new = x_ref[...] + mod_ref[:, 2 * D_MODEL:] * y
    xo_ref[...] = x_new
    if is_last:
        yfin_ref[...] = _rms(x_new, gfin_ref[...])


def _attn_call(l_arr, main, tail, yc, x2, mod_all, w, T, has_ctx, is_last, rope=None, ctx=None, side_bufs=None):
    n_tok = main.shape[0]
    B = n_tok // T
    tq = ATTN_TQ
    nq = T // tq
    Tk = T + (PAST_LEN if has_ctx else 0)
    main3 = main.reshape(B, T, MAIN_W)
    tail3 = tail.reshape(B, T, TAIL_W)
    x3 = x2.reshape(B, T, D_MODEL)

    def tile(c, width):
        return pl.BlockSpec((None, tq, width), lambda b, q, l, c=c: (b, q, c))

    def full(c, width):
        return pl.BlockSpec((None, T, width), lambda b, q, l, c=c: (b, 0, c))

    def wspec(shape):
        return pl.BlockSpec((None,) + shape, lambda b, q, l: (l[0],) + (0,) * len(shape))

    if has_ctx:
        mod_map = lambda b, q, l: (l[0], 1 + b, 0, 0)
    else:
        mod_map = lambda b, q, l: (l[0], 0, 0, 0)

    in_specs = [tile(1, MLA_Q_RANK),
                full(0, MLA_KV_RANK),
                full(2, LANES),
                tile(C_DQ, GROUP_W), full(C_DK, GROUP_W), full(C_DV, GROUP_W),
                tile(C_ZB, GROUP_W), tile(C_ZA, GROUP_W),
                pl.BlockSpec((None, tq, GROUP_W), lambda b, q, l: (b, q, 0)),
                pl.BlockSpec((None, tq, D_MODEL), lambda b, q, l: (b, q, 0)),
                pl.BlockSpec((None, None, 1, 3 * D_MODEL), mod_map),
                wspec((MLA_Q_RANK, 2 * GROUP_W)), wspec((MLA_KV_RANK, 2 * GROUP_W)),
                wspec((3 * GROUP_W, D_MODEL)),
                wspec((1, MLA_Q_RANK)), wspec((1, MLA_KV_RANK)), wspec((1, HEAD_W)),
                wspec((4, DIFF_D))]
    args = [tail3, tail3, tail3, main3, main3, main3, main3, main3, yc, x3, mod_all,
            w['wuq'], w['wukv'], w['wout'], w['gq'], w['gkv'], w['gdn'], w['lam']]
    if is_last:
        in_specs.append(pl.BlockSpec((1, D_MODEL), lambda b, q, l: (0, 0)))
        args.append(w['gfin'])
    if has_ctx:
        cos_t, sin_t = rope
        in_specs += [pl.BlockSpec((tq, LANES), lambda b, q, l: (q, 0)),
                     pl.BlockSpec((tq, LANES), lambda b, q, l: (q, 0)),
                     pl.BlockSpec((T, LANES), lambda b, q, l: (0, 0)),
                     pl.BlockSpec((T, LANES), lambda b, q, l: (0, 0)),
                     pl.BlockSpec((None, None, PAST_LEN, MLA_KV_RANK), lambda b, q, l: (b, l[0], 0, 0)),
                     pl.BlockSpec((None, None, PAST_LEN, LANES), lambda b, q, l: (b, l[0], 0, 0)),
                     pl.BlockSpec((None, None, PAST_LEN * DIFF_HEADS, HEAD_W), lambda b, q, l: (b, l[0], 0, 0)),
                     pl.BlockSpec((None, None, PAST_LEN * DIFF_HEADS, HEAD_W), lambda b, q, l: (b, l[0], 0, 0))]
        args += [cos_t, sin_t, cos_t, sin_t, ctx['ckv'], ctx['krope'], ctx['dk'], ctx['dv']]
    out_shape = [jax.ShapeDtypeStruct((B, T, D_MODEL), F32)]
    out_specs = [pl.BlockSpec((None, tq, D_MODEL), lambda b, q, l: (b, q, 0))]
    if is_last:
        out_shape.append(jax.ShapeDtypeStruct((B, T, D_MODEL), F32))
        out_specs.append(pl.BlockSpec((None, tq, D_MODEL), lambda b, q, l: (b, q, 0)))
    aliases = {}
    if not has_ctx:
        n_in = 1 + len(args)
        in_specs += [pl.BlockSpec(memory_space=pl.ANY)] * 2
        args += list(side_bufs)
        aliases = {n_in: len(out_shape), n_in + 1: len(out_shape) + 1}
        out_shape += [jax.ShapeDtypeStruct(b.shape, b.dtype) for b in side_bufs]
        out_specs += [pl.BlockSpec((None, None, T, MLA_KV_RANK), lambda b, q, l: (b, l[0], 0, 0)),
                      pl.BlockSpec((None, None, T, MLA_ROPE), lambda b, q, l: (b, l[0], 0, 0))]
    grid_spec = pltpu.PrefetchScalarGridSpec(
        num_scalar_prefetch=1, grid=(B, nq), in_specs=in_specs, out_specs=out_specs,
        scratch_shapes=[pltpu.VMEM((Tk, 2 * GROUP_W), BF16),
                        pltpu.VMEM((Tk, 2 * GROUP_W), BF16),
                        pltpu.VMEM((Tk, GROUP_W), BF16),
                        pltpu.VMEM((Tk, 2 * GROUP_W), BF16),
                        pltpu.VMEM((tq, 3 * GROUP_W), BF16)])
    return pl.pallas_call(
        functools.partial(_attn_kernel, has_ctx, is_last, T),
        grid_spec=grid_spec,
        out_shape=out_shape,
        input_output_aliases=aliases,
        compiler_params=pltpu.CompilerParams(dimension_semantics=("arbitrary", "arbitrary"),
                                             vmem_limit_bytes=VMEM_LIMIT),
        name="attn_lat" if has_ctx else "attn_ctx",
    )(l_arr, *args)


def _rope_tables(n_tok):
    n_freq = MLA_ROPE // 4
    inv = ROPE_THETA ** (-jnp.arange(n_freq, dtype=F32) / n_freq)
    n_rows = n_tok // GRID_W
    rowp = jnp.repeat(jnp.arange(n_rows, dtype=F32), GRID_W)
    colp = jnp.tile(jnp.arange(GRID_W, dtype=F32), n_rows)
    ang = jnp.concatenate([rowp[:, None] * inv, colp[:, None] * inv], axis=-1)
    cos, sin = jnp.cos(ang), jnp.sin(ang)
    cos64 = jnp.concatenate([cos, cos], axis=-1)
    sin64 = jnp.concatenate([-sin, sin], axis=-1)
    return jnp.concatenate([cos64, cos64], axis=-1), jnp.concatenate([sin64, sin64], axis=-1)


def _prep_weights(W_in, W_uq, W_ukv, W_out):
    seg = lambda o, n: W_in[:, :, o:o + n]
    w_mid = seg(_O_ZA, MAIN_W).astype(BF16)
    pad = jnp.zeros((DEPTH, D_MODEL, LANES - MLA_ROPE - N_ML_GATES), W_in.dtype)
    w_tail = jnp.concatenate([seg(_O_CKV, MLA_KV_RANK), seg(_O_KR, MLA_ROPE), seg(_O_MG, N_ML_GATES), pad,
                              seg(_O_CQ, MLA_Q_RANK)], axis=-1).astype(BF16)
    wq = W_uq.reshape(DEPTH, MLA_Q_RANK, MLA_HEADS, MLA_NOPE + MLA_ROPE)
    wq = jnp.pad(wq, ((0, 0), (0, 0), (0, 0), (0, 2 * HEAD_W - MLA_NOPE - MLA_ROPE)))
    wuq_r = wq.reshape(DEPTH, MLA_Q_RANK, MLA_HEADS * 2 * HEAD_W).astype(BF16)
    wkv = W_ukv.reshape(DEPTH, MLA_KV_RANK, MLA_HEADS, 2, HEAD_W)
    wukv_r = jnp.swapaxes(wkv, 2, 3).reshape(DEPTH, MLA_KV_RANK, 2 * GROUP_W).astype(BF16)
    return w_mid, w_tail, wuq_r, wukv_r, W_out.astype(BF16)


def kernel(x_prompt, x_sample, cache_mla_ckv, cache_mla_krope, cache_diff_k, cache_diff_v, state_mlstm_C, state_mlstm_n, state_mlstm_m, c, c_ctx, g_norm, W_mod, b_mod, W_in, mla_q_norm, W_uq, mla_kv_norm, W_ukv, diff_lambda, diff_norm, ml_conv, ml_gate_b, ml_norm, W_out, g_final):
    Bc, Tc, _ = x_prompt.shape
    Bs, Ts, _ = x_sample.shape

    w_mid, w_tail, wuq_r, wukv_r, wout_r = _prep_weights(W_in, W_uq, W_ukv, W_out)
    w = {'wuq': wuq_r, 'wukv': wukv_r, 'wout': wout_r,
         'gq': mla_q_norm.reshape(DEPTH, 1, MLA_Q_RANK), 'gkv': mla_kv_norm.reshape(DEPTH, 1, MLA_KV_RANK),
         'gdn': diff_norm.reshape(DEPTH, 1, 2 * DIFF_D), 'lam': diff_lambda,
         'gfin': g_final.reshape(1, D_MODEL)}
    g_norm3 = g_norm.reshape(DEPTH, 1, D_MODEL)
    ml_norm2 = ml_norm.reshape(DEPTH, 1, GROUP_W)
    gate_row = jnp.pad(ml_gate_b.reshape(DEPTH, 1, N_ML_GATES),
                       ((0, 0), (0, 0), (GATE_LANE0, LANES - GATE_LANE0 - N_ML_GATES)))

    cc = jnp.concatenate([c_ctx[None, :], c, jnp.zeros((8 - 1 - Bs, D_MODEL), F32)], axis=0)
    mod_all = _mod_call(cc, W_mod, b_mod).reshape(DEPTH, 8, 1, 3 * D_MODEL)

    ckvn = jnp.zeros((Bc, DEPTH, Tc, MLA_KV_RANK), F32)
    kro = jnp.zeros((Bc, DEPTH, Tc, MLA_ROPE), F32)
    dk_o = jnp.zeros((Bc, DEPTH, Tc * DIFF_HEADS, HEAD_W), F32)
    dv_o = jnp.zeros((Bc, DEPTH, Tc * DIFF_HEADS, HEAD_W), F32)
    c_o = jnp.zeros((Bc, DEPTH, 2 * ML_HEADS, ML_DK, HEAD_W), F32)
    n_o = jnp.zeros((Bc, DEPTH, 2 * ML_HEADS, ML_DK), F32)
    m_o = jnp.zeros((Bc, DEPTH, 2 * ML_HEADS, LANES), F32)
    x2 = x_prompt.reshape(Bc * Tc, D_MODEL)
    y_prompt = None
    for l in range(DEPTH):
        l_arr = jnp.full((1,), l, jnp.int32)
        main, tail, dk_o, dv_o = _proj_call(l_arr, x2, mod_all, g_norm3, w_mid, w_tail, Tc, False, (dk_o, dv_o))
        yc, c_o, n_o, m_o = _mlstm_call(l_arr, main, tail, ml_conv, gate_row, ml_norm2, Tc, False,
                                        side_bufs=(c_o, n_o, m_o))
        outs = _attn_call(l_arr, main, tail, yc, x2, mod_all, w, Tc, False, l == DEPTH - 1,
                          side_bufs=(ckvn, kro))
        if l == DEPTH - 1:
            x3, y_prompt, ckvn, kro = outs
        else:
            x3, ckvn, kro = outs
        x2 = x3.reshape(Bc * Tc, D_MODEL)
    side_outs = (ckvn, kro,
                 dk_o.reshape(Bc, DEPTH, Tc, DIFF_HEADS, 2 * DIFF_D),
                 dv_o.reshape(Bc, DEPTH, Tc, DIFF_HEADS, 2 * DIFF_D),
                 c_o.reshape(Bc, DEPTH, 2, ML_HEADS, ML_DK, HEAD_W),
                 n_o.reshape(Bc, DEPTH, 2, ML_HEADS, ML_DK),
                 m_o[:, :, :, 0].reshape(Bc, DEPTH, 2, ML_HEADS))

    rope = _rope_tables(Ts)
    ctx = {'ckv': cache_mla_ckv,
           'krope': jnp.pad(cache_mla_krope, ((0, 0), (0, 0), (0, 0), (0, LANES - MLA_ROPE))),
           'dk': cache_diff_k.reshape(Bs, DEPTH, PAST_LEN * DIFF_HEADS, HEAD_W),
           'dv': cache_diff_v.reshape(Bs, DEPTH, PAST_LEN * DIFF_HEADS, HEAD_W)}
    ctx_state = (state_mlstm_C.reshape(Bs, DEPTH, 2 * ML_HEADS, ML_DK, HEAD_W),
                 state_mlstm_n.reshape(Bs, DEPTH, 2 * ML_HEADS, ML_DK),
                 jnp.broadcast_to(state_mlstm_m.reshape(Bs, DEPTH, 2 * ML_HEADS, 1),
                                  (Bs, DEPTH, 2 * ML_HEADS, LANES)))
    x2 = x_sample.reshape(Bs * Ts, D_MODEL)
    y_sample = None
    for l in range(DEPTH):
        l_arr = jnp.full((1,), l, jnp.int32)
        main, tail = _proj_call(l_arr, x2, mod_all, g_norm3, w_mid, w_tail, Ts, True)
        (yc,) = _mlstm_call(l_arr, main, tail, ml_conv, gate_row, ml_norm2, Ts, True, ctx_state)
        outs = _attn_call(l_arr, main, tail, yc, x2, mod_all, w, Ts, True, l == DEPTH - 1, rope, ctx)
        if l == DEPTH - 1:
            x3, y_sample = outs
        else:
            (x3,) = outs
        x2 = x3.reshape(Bs * Ts, D_MODEL)

    return (y_prompt, y_sample, *side_outs)
```

```python
import functools
import math

import jax
import jax.numpy as jnp
import numpy as np
from jax import lax
from jax.experimental import pallas as pl
from jax.experimental.pallas import tpu as pltpu

F32 = jnp.float32
BF16 = jnp.bfloat16

D_MODEL = 1024
DEPTH = 4
PAST_LEN = 256
GRID_W = 64
GROUP_W = 512
MLA_HEADS = 4
MLA_NOPE = 128
MLA_ROPE = 64
MLA_Q_RANK = 384
MLA_KV_RANK = 256
DIFF_HEADS = 4
DIFF_D = 64
ML_HEADS = 4
ML_DK = 128
N_ML_GATES = 16
ROPE_THETA = 10000.0
NORM_EPS = 1e-6
MLA_SCALE = (MLA_NOPE + MLA_ROPE) ** -0.5
DIFF_SCALE = DIFF_D ** -0.5
LOG2E = math.log2(math.e)

LANES = 128
HEAD_W = 128
ML_CHUNK = 256
ML_VT_ROWS = 144
PROJ_TM = 512
ATTN_TQ = 256
ATTN_AHEAD_CTX = 12
ATTN_AHEAD_LAT = 4
MAIN_W = 10 * GROUP_W
TAIL_W = 768
N_IN_PAD = MAIN_W + TAIL_W
VMEM_LIMIT = 56 * 1024 * 1024

C_ZA, C_DQ, C_DK, C_DV, C_ZB, C_MQ, C_MK, C_MV, C_MO, C_ZC = range(10)

_IN_SIZES = (MLA_Q_RANK, MLA_KV_RANK, MLA_ROPE, GROUP_W, GROUP_W, GROUP_W, GROUP_W, GROUP_W,
             GROUP_W, GROUP_W, GROUP_W, GROUP_W, GROUP_W, N_ML_GATES)
_IN_OFF = np.concatenate([[0], np.cumsum(_IN_SIZES)])
(_O_CQ, _O_CKV, _O_KR, _O_ZA, _O_DQ, _O_DK, _O_DV, _O_ZB, _O_MQ, _O_MK, _O_MV, _O_MO, _O_ZC,
 _O_MG) = [int(v) for v in _IN_OFF[:-1]]
GATE_LANE0 = MLA_ROPE


def _rms(x, g):
    ms = jnp.mean(x * x, axis=-1, keepdims=True)
    return x * lax.rsqrt(ms + NORM_EPS) * g


def _silu(x):
    return x * jax.nn.sigmoid(x)


def _log_sigmoid(x):
    return jnp.minimum(x, 0.0) - jnp.log1p(jnp.exp(-jnp.abs(x)))


def _swap32(x):
    lane = lax.broadcasted_iota(jnp.int32, x.shape, 1)
    fwd = pltpu.roll(x, LANES - 32, 1)
    bwd = pltpu.roll(x, 32, 1)
    return jnp.where((lane % 64) < 32, fwd, bwd)


def _rope_tile(x, cos, sin):
    return x * cos + _swap32(x) * sin


def _mod_kernel(c_ref, w_ref, b_ref, o_ref):
    a = _silu(c_ref[...]).astype(BF16)
    o_ref[...] = jnp.dot(a, w_ref[...].astype(BF16), preferred_element_type=F32) + b_ref[...]


def _mod_call(cc, W_mod, b_mod):
    tn = 1024
    return pl.pallas_call(
        _mod_kernel,
        grid=(DEPTH, 3 * D_MODEL // tn),
        in_specs=[pl.BlockSpec((8, D_MODEL), lambda l, j: (0, 0)),
                  pl.BlockSpec((None, D_MODEL, tn), lambda l, j: (l, 0, j)),
                  pl.BlockSpec((None, 1, tn), lambda l, j: (l, 0, j))],
        out_specs=pl.BlockSpec((None, 8, tn), lambda l, j: (l, 0, j)),
        out_shape=jax.ShapeDtypeStruct((DEPTH, 8, 3 * D_MODEL), F32),
        compiler_params=pltpu.CompilerParams(dimension_semantics=("arbitrary", "arbitrary"),
                                             vmem_limit_bytes=VMEM_LIMIT),
        name="mod",
    )(cc, W_mod, b_mod.reshape(DEPTH, 1, 3 * D_MODEL))


def _wprep_kernel(*refs):
    o_ref = refs[-1]
    lane = lax.broadcasted_iota(jnp.int32, (D_MODEL, LANES), 1)
    rolled = [pltpu.roll(r[...], LANES // 2, 1) for r in refs[:-1]]
    for j in range(GROUP_W // LANES):
        o_ref[:, j * LANES:(j + 1) * LANES] = jnp.where(lane < LANES // 2, rolled[j], rolled[j + 1]).astype(BF16)


def _wprep_call(W_in):
    assert _O_ZA % LANES == LANES // 2
    t0 = _O_ZA // LANES
    per = GROUP_W // LANES
    in_specs = [pl.BlockSpec((None, D_MODEL, LANES), lambda l, s, j=j: (l, 0, t0 + per * s + j))
                for j in range(per + 1)]
    return pl.pallas_call(
        _wprep_kernel,
        grid=(DEPTH, MAIN_W // GROUP_W),
        in_specs=in_specs,
        out_specs=pl.BlockSpec((None, D_MODEL, GROUP_W), lambda l, s: (l, 0, s)),
        out_shape=jax.ShapeDtypeStruct((DEPTH, D_MODEL, MAIN_W), BF16),
        compiler_params=pltpu.CompilerParams(dimension_semantics=("arbitrary", "arbitrary"),
                                             vmem_limit_bytes=VMEM_LIMIT),
        name="wprep",
    )(*([W_in] * (per + 1)))


def _proj_kernel(has_side, n_alias, T, *refs):
    l_ref, x_ref, mod_ref, g_ref, w_ref, wt_ref = refs[:6]
    refs = refs[6 + n_alias:]
    if has_side:
        main_ref, tail_ref, dk_ref, dv_ref = refs
    else:
        main_ref, tail_ref = refs
    del l_ref
    tm = x_ref.shape[0]
    mod = mod_ref[...]
    y = _rms(x_ref[...], g_ref[...])
    h = (y * (1.0 + mod[:, D_MODEL:2 * D_MODEL]) + mod[:, :D_MODEL]).astype(BF16)
    for s in range(MAIN_W // GROUP_W):
        cols = slice(s * GROUP_W, (s + 1) * GROUP_W)
        acc = jnp.dot(h, w_ref[:, cols], preferred_element_type=F32)
        if s in (C_ZA, C_ZB, C_ZC):
            main_ref[:, cols] = _silu(acc).astype(BF16)
        elif s == C_MO:
            main_ref[:, cols] = jax.nn.sigmoid(acc).astype(BF16)
        else:
            main_ref[:, cols] = acc.astype(BF16)
        if has_side and s in (C_DK, C_DV):
            side_ref = dk_ref if s == C_DK else dv_ref
            for b in range(tm // T):
                for hd in range(DIFF_HEADS):
                    side_ref[b, pl.ds(hd, T, stride=DIFF_HEADS), :] = (
                        acc[b * T:(b + 1) * T, hd * HEAD_W:(hd + 1) * HEAD_W])
    tail_ref[...] = jnp.dot(h, wt_ref[...], preferred_element_type=F32)


def _proj_call(l_arr, x2, mod_all, g_norm3, w_mid, w_tail, T, has_ctx, side_bufs=None):
    n_tok = x2.shape[0]
    tm = PROJ_TM
    has_side = not has_ctx
    if has_ctx:
        mod_map = lambda i, l: (l[0], 1 + (i * tm) // T, 0, 0)
    else:
        mod_map = lambda i, l: (l[0], 0, 0, 0)
    in_specs = [pl.BlockSpec((tm, D_MODEL), lambda i, l: (i, 0)),
                pl.BlockSpec((None, None, 1, 3 * D_MODEL), mod_map),
                pl.BlockSpec((None, 1, D_MODEL), lambda i, l: (l[0], 0, 0)),
                pl.BlockSpec((None, D_MODEL, MAIN_W), lambda i, l: (l[0], 0, 0), pipeline_mode=pl.Buffered(1)),
                pl.BlockSpec((None, D_MODEL, TAIL_W), lambda i, l: (l[0], 0, 0), pipeline_mode=pl.Buffered(1))]
    args = [x2, mod_all, g_norm3, w_mid, w_tail]
    out_shape = [jax.ShapeDtypeStruct((n_tok, MAIN_W), BF16), jax.ShapeDtypeStruct((n_tok, TAIL_W), F32)]
    out_specs = [pl.BlockSpec((tm, MAIN_W), lambda i, l: (i, 0)), pl.BlockSpec((tm, TAIL_W), lambda i, l: (i, 0))]
    aliases = {}
    n_alias = 0
    if has_side:
        bt = tm // T
        if side_bufs is not None:
            n_alias = len(side_bufs)
            aliases = {6 + i: 2 + i for i in range(n_alias)}
            in_specs += [pl.BlockSpec(memory_space=pl.ANY)] * n_alias
            args += list(side_bufs)
        out_shape += [jax.ShapeDtypeStruct((n_tok // T, DEPTH, T * DIFF_HEADS, HEAD_W), F32)] * 2
        out_specs += [pl.BlockSpec((bt, None, T * DIFF_HEADS, HEAD_W), lambda i, l: (i, l[0], 0, 0))] * 2
    grid_spec = pltpu.PrefetchScalarGridSpec(
        num_scalar_prefetch=1, grid=(n_tok // tm,), in_specs=in_specs, out_specs=out_specs)
    return pl.pallas_call(
        functools.partial(_proj_kernel, has_side, n_alias, T),
        grid_spec=grid_spec,
        out_shape=out_shape,
        input_output_aliases=aliases,
        compiler_params=pltpu.CompilerParams(dimension_semantics=("arbitrary",),
                                             vmem_limit_bytes=VMEM_LIMIT),
        name="proj_lat" if has_ctx else "proj_ctx",
    )(l_arr, *args)


def _mlstm_kernel(has_ctx, T, n_alias, *refs):
    nc = T // ML_CHUNK
    L = ML_CHUNK
    use_inter = has_ctx or nc > 1
    it = iter(refs)
    l_ref = next(it)
    ml_ref, aux_ref, convw_ref, gb_ref, mln_ref = [next(it) for _ in range(5)]
    mq_ref, mk_ref, mv_ref, mo_ref, zc_ref = [ml_ref.at[:, i * GROUP_W:(i + 1) * GROUP_W] for i in range(5)]
    if has_ctx:
        c0_ref, n0_ref, m0_ref = [next(it) for _ in range(3)]
    else:
        for _ in range(n_alias):
            next(it)
    yc_ref = next(it)
    if not has_ctx:
        cout_ref, nout_ref, mout_ref = [next(it) for _ in range(3)]
    q_s, k_s, vt_s, hf_s, hb_s, caug_s, m_s, gt_s = [next(it) for _ in range(8)]
    del l_ref

    row_b = lax.broadcasted_iota(jnp.int32, (L, LANES), 0)
    convw = convw_ref[...]
    halo = 16
    for c in range(nc):
        cs = slice(c * L, (c + 1) * L)
        for j in range(2 * GROUP_W // LANES):
            src = mq_ref if j < GROUP_W // LANES else mk_ref
            ls = slice((j % (GROUP_W // LANES)) * LANES, (j % (GROUP_W // LANES) + 1) * LANES)
            u = src[cs, ls].astype(F32)
            prev = src[c * L - halo:c * L, ls].astype(F32)[halo - 1:halo, :] if c > 0 else 0.0
            nxt = src[(c + 1) * L:(c + 1) * L + halo, ls].astype(F32)[0:1, :] if c < nc - 1 else 0.0
            up = jnp.where(row_b == 0, prev, pltpu.roll(u, 1, 0))
            un = jnp.where(row_b == L - 1, nxt, pltpu.roll(u, L - 1, 0))
            w3 = convw[:, j * LANES:(j + 1) * LANES]
            y = _silu(w3[0:1, :] * up + w3[1:2, :] * u + w3[2:3, :] * un)
            if j < GROUP_W // LANES:
                q_s[cs, ls] = (y * (ML_DK ** -0.5)).astype(BF16)
            else:
                k_s[cs, ls] = y
    row16 = lax.broadcasted_iota(jnp.int32, (ML_VT_ROWS - HEAD_W, L), 0)
    ones_rows = jnp.where(row16 == 0, 1.0, 0.0).astype(BF16)
    g_t = (aux_ref[...] + gb_ref[...]).T
    for c in range(nc):
        cs = slice(c * L, (c + 1) * L)
        gt_s[c] = g_t[GATE_LANE0:GATE_LANE0 + N_ML_GATES, cs]
        for h in range(ML_HEADS):
            vt_s[c, h, 0:HEAD_W, :] = mv_ref[cs, h * HEAD_W:(h + 1) * HEAD_W].astype(F32).T.astype(BF16)
            vt_s[c, h, HEAD_W:, :] = ones_rows

    reps = L // LANES
    if has_ctx:
        n0 = n0_ref[...]
        m0 = m0_ref[...]
        m_s[0] = jnp.concatenate([m0] * reps, axis=1)
        m_s[1] = jnp.concatenate([pltpu.roll(m0, ML_HEADS, 0)] * reps, axis=1)
        row_n = lax.broadcasted_iota(jnp.int32, (ML_VT_ROWS - HEAD_W, ML_DK), 0)
        for r in range(2 * ML_HEADS):
            caug_s[r, 0:HEAD_W, :] = c0_ref[r].T
            caug_s[r, HEAD_W:, :] = jnp.where(row_n == 0, n0[r:r + 1, :], 0.0)
    else:
        caug_s[...] = jnp.zeros_like(caug_s)
        m_s[...] = jnp.zeros_like(m_s)

    ri = lax.broadcasted_iota(jnp.int32, (L, L), 0)
    ci = lax.broadcasted_iota(jnp.int32, (L, L), 1)
    lane_r = lax.broadcasted_iota(jnp.int32, (2 * ML_HEADS, L), 1)
    nt = (((1,), (1,)), ((), ()))

    def split3(x):
        hi = x.astype(BF16).astype(F32)
        mid = (x - hi).astype(BF16).astype(F32)
        return hi, mid, (x - hi - mid).astype(BF16).astype(F32)

    masks = ((ri <= ci), (ri >= ci))

    def gate_rows(d, c):
        fwd = d == 0
        tri_t = jnp.where(masks[d], 1.0, 0.0).astype(BF16)
        last = L - 1 if fwd else 0

        g8 = gt_s[c, d * 2 * ML_HEADS:(d + 1) * 2 * ML_HEADS, :]
        hi8, mid8, lo8 = split3(_log_sigmoid(g8))
        stack = jnp.concatenate([hi8, mid8, lo8, jnp.zeros_like(hi8)], axis=0).astype(BF16)
        part8 = jnp.dot(stack, tri_t, preferred_element_type=F32)
        bc8 = part8[0:8] + part8[8:16] + part8[16:24]
        bcs = pltpu.roll(bc8, ML_HEADS, 0)
        a8 = g8 - bcs
        cm8 = a8
        k = 1
        while k < L:
            if fwd:
                cm8 = jnp.maximum(cm8, jnp.where(lane_r >= k, pltpu.roll(cm8, k, 1), -jnp.inf))
            else:
                cm8 = jnp.maximum(cm8, jnp.where(lane_r < L - k, pltpu.roll(cm8, L - k, 1), -jnp.inf))
            k *= 2
        m8 = m_s[d]
        g_row = bcs + m8
        m_t = jnp.maximum(g_row, bcs + cm8)
        w_inter = jnp.exp(g_row - m_t)
        e_inv = jnp.exp(-m_t)
        c_row = bcs - m_t
        b_last = bcs[:, last:last + 1]
        m_new = m_t[:, last:last + 1]
        a_prev = jnp.exp(b_last + m8[:, 0:1] - m_new)
        w_s = jnp.exp(a8 + (b_last - m_new))
        m_s[d] = jnp.broadcast_to(m_new, (2 * ML_HEADS, L))
        a_n = jnp.concatenate([a8, jnp.zeros((LANES - 2 * ML_HEADS, L), F32)], axis=0).T
        return a_n, c_row, w_inter, e_inv, w_s, a_prev

    def do_pair(c_f, c_b):
        stats = (gate_rows(0, c_f), gate_rows(1, c_b))
        combos = [(d, h) for d in range(2) for h in range(ML_HEADS)]
        rows = []
        for c in (c_f, c_b):
            rows.append(slice(c * L, (c + 1) * L) if isinstance(c, int) else pl.ds(pl.multiple_of(c * L, L), L))
        chunk = (c_f, c_b)
        hsl = [slice(h * HEAD_W, (h + 1) * HEAD_W) for h in range(ML_HEADS)]
        qcs = [q_s[rows[d], hsl[h]] for d, h in combos]
        kcs = [k_s[rows[d], hsl[h]].astype(BF16) for d, h in combos]
        vts = [vt_s[chunk[d], h] for d, h in combos]
        n = range(len(combos))
        s_ts = [lax.dot_general(kcs[i], qcs[i], nt, preferred_element_type=F32) for i in n]
        sws = []
        for i, (d, h) in enumerate(combos):
            a_n, c_row = stats[d][0], stats[d][1]
            w_t = jnp.exp(jnp.where(masks[d], a_n[:, h:h + 1] + c_row[h:h + 1, :], -jnp.inf))
            sws.append((s_ts[i] * w_t).astype(BF16))
        nds = [jnp.dot(vts[i], sws[i], preferred_element_type=F32) for i in n]
        if use_inter:
            inters = [lax.dot_general(caug_s[d * ML_HEADS + h].astype(BF16), qcs[i], nt,
                                      preferred_element_type=F32) for i, (d, h) in enumerate(combos)]
            nds = [nds[i] + stats[d][2][h:h + 1, :] * inters[i] for i, (d, h) in enumerate(combos)]
        for i, (d, h) in enumerate(combos):
            inv = 1.0 / jnp.maximum(jnp.abs(nds[i][HEAD_W:HEAD_W + 1, :]), stats[d][3][h:h + 1, :])
            h_s = hf_s if d == 0 else hb_s
            h_s[rows[d], hsl[h]] = (nds[i][0:HEAD_W, :] * inv).T
        upds = [jnp.dot((vts[i].astype(F32) * stats[d][4][h:h + 1, :]).astype(BF16), kcs[i],
                        preferred_element_type=F32) for i, (d, h) in enumerate(combos)]
        for i, (d, h) in enumerate(combos):
            r = d * ML_HEADS + h
            caug_s[r] = upds[i] + stats[d][5][h:h + 1, :] * caug_s[r] if use_inter else upds[i]

    if nc == 1:
        do_pair(0, 0)
    else:
        def body(i, carry):
            do_pair(i, nc - 1 - i)
            return carry
        lax.fori_loop(0, nc, body, 0)

    mln = mln_ref[...]
    for c in range(nc):
        cs = slice(c * L, (c + 1) * L)
        for h in range(ML_HEADS):
            hs = slice(h * HEAD_W, (h + 1) * HEAD_W)
            hc = mo_ref[cs, hs].astype(F32) * (hf_s[cs, hs] + hb_s[cs, hs])
            yc_ref[cs, hs] = (_rms(hc, mln[:, hs]) * zc_ref[cs, hs].astype(F32)).astype(BF16)

    if not has_ctx:
        for r in range(2 * ML_HEADS):
            cout_ref[r] = caug_s[r, 0:HEAD_W, :].T
            nout_ref[r:r + 1, :] = caug_s[r, HEAD_W:HEAD_W + 1, :]
        for d in range(2):
            mout_ref[d * ML_HEADS:(d + 1) * ML_HEADS, :] = m_s[d, 0:ML_HEADS, 0:LANES]


def _mlstm_call(l_arr, main, tail, ml_conv, gate_row, ml_norm2, T, has_ctx, ctx_state=None, side_bufs=None):
    n_tok = main.shape[0]
    B = n_tok // T
    main3 = main.reshape(B, T, MAIN_W)
    tail3 = tail.reshape(B, T, TAIL_W)

    assert (C_MQ, C_ZC) == (5, 9)
    in_specs = [pl.BlockSpec((None, T, MAIN_W // 2), lambda b, l: (b, 0, 1)),
                pl.BlockSpec((None, T, LANES), lambda b, l: (b, 0, 2)),
                pl.BlockSpec((None, 3, 2 * GROUP_W), lambda b, l: (l[0], 0, 0)),
                pl.BlockSpec((None, 1, LANES), lambda b, l: (l[0], 0, 0)),
                pl.BlockSpec((None, 1, GROUP_W), lambda b, l: (l[0], 0, 0))]
    args = [main3, tail3, ml_conv, gate_row, ml_norm2]
    n_alias = 0 if (has_ctx or side_bufs is None) else len(side_bufs)
    out_shape = [jax.ShapeDtypeStruct((B, T, GROUP_W), BF16)]
    out_specs = [pl.BlockSpec((None, T, GROUP_W), lambda b, l: (b, 0, 0))]
    if has_ctx:
        c0, n0, m0 = ctx_state
        in_specs += [pl.BlockSpec((None, None, 2 * ML_HEADS, ML_DK, HEAD_W), lambda b, l: (b, l[0], 0, 0, 0)),
                     pl.BlockSpec((None, None, 2 * ML_HEADS, ML_DK), lambda b, l: (b, l[0], 0, 0)),
                     pl.BlockSpec((None, None, 2 * ML_HEADS, LANES), lambda b, l: (b, l[0], 0, 0))]
        args += [c0, n0, m0]
        aliases = {}
    else:
        aliases = {}
        if n_alias:
            aliases = {1 + len(args) + i: 1 + i for i in range(n_alias)}
            in_specs += [pl.BlockSpec(memory_space=pl.ANY)] * n_alias
            args += list(side_bufs)
        out_shape += [jax.ShapeDtypeStruct((B, DEPTH, 2 * ML_HEADS, ML_DK, HEAD_W), F32),
                      jax.ShapeDtypeStruct((B, DEPTH, 2 * ML_HEADS, ML_DK), F32),
                      jax.ShapeDtypeStruct((B, DEPTH, 2 * ML_HEADS, LANES), F32)]
        out_specs += [pl.BlockSpec((None, None, 2 * ML_HEADS, ML_DK, HEAD_W), lambda b, l: (b, l[0], 0, 0, 0)),
                      pl.BlockSpec((None, None, 2 * ML_HEADS, ML_DK), lambda b, l: (b, l[0], 0, 0)),
                      pl.BlockSpec((None, None, 2 * ML_HEADS, LANES), lambda b, l: (b, l[0], 0, 0))]
    grid_spec = pltpu.PrefetchScalarGridSpec(
        num_scalar_prefetch=1, grid=(B,), in_specs=in_specs, out_specs=out_specs,
        scratch_shapes=[pltpu.VMEM((T, GROUP_W), BF16),
                        pltpu.VMEM((T, GROUP_W), F32),
                        pltpu.VMEM((T // ML_CHUNK, ML_HEADS, ML_VT_ROWS, ML_CHUNK), BF16),
                        pltpu.VMEM((T, GROUP_W), F32),
                        pltpu.VMEM((T, GROUP_W), F32),
                        pltpu.VMEM((2 * ML_HEADS, ML_VT_ROWS, ML_DK), F32),
                        pltpu.VMEM((2, 2 * ML_HEADS, ML_CHUNK), F32),
                        pltpu.VMEM((T // ML_CHUNK, N_ML_GATES, ML_CHUNK), F32)])
    return pl.pallas_call(
        functools.partial(_mlstm_kernel, has_ctx, T, n_alias),
        grid_spec=grid_spec,
        out_shape=out_shape,
        input_output_aliases=aliases,
        compiler_params=pltpu.CompilerParams(dimension_semantics=("arbitrary",),
                                             vmem_limit_bytes=VMEM_LIMIT),
        name="mlstm_lat" if has_ctx else "mlstm_ctx",
    )(l_arr, *args)


def _attn_kernel(has_ctx, is_last, merged, n_alias, T, *refs):
    Tk = T + (PAST_LEN if has_ctx else 0)
    it = iter(refs)
    l_ref = next(it)
    if merged:
        tail_ref, att_ref = next(it), next(it)
        ckv_ref = tail_ref.at[:, 0:MLA_KV_RANK]
        aux_ref = tail_ref.at[:, MLA_KV_RANK:MLA_KV_RANK + LANES]
        cq_ref = tail_ref.at[:, MLA_KV_RANK + LANES:]
        za_ref, dq_ref, dk_ref, dv_ref, zb_ref = [att_ref.at[:, i * GROUP_W:(i + 1) * GROUP_W] for i in range(5)]
    else:
        cq_ref, ckv_ref, aux_ref, dq_ref, dk_ref, dv_ref, zb_ref, za_ref = [next(it) for _ in range(8)]
    (yc_ref, x_ref, mod_ref, wuq_ref, wukv_ref, wout_ref, gq_ref, gkv_ref, gdn_ref,
     lam_ref) = [next(it) for _ in range(10)]
    if is_last:
        gfin_ref = next(it)
    if has_ctx:
        cosq_ref, sinq_ref, cosk_ref, sink_ref, cckv_ref, ckr_ref, cdk_ref, cdv_ref = [next(it) for _ in range(8)]
    else:
        for _ in range(n_alias):
            next(it)
    xo_ref = next(it)
    if is_last:
        yfin_ref = next(it)
    if not has_ctx:
        ckvn_ref, kro_ref = [next(it) for _ in range(2)]
    ka_s, va_s, kb_s, vb_s, ycat_s = [next(it) for _ in range(5)]

    qi = pl.program_id(1)

    @pl.when(qi == 0)
    def _build_keys():
        wukv = wukv_ref[...]
        ckv_n = _rms(ckv_ref[...], gkv_ref[...])
        aux = aux_ref[...]
        if not has_ctx:
            ckvn_ref[...] = ckv_n
            kro_ref[...] = aux[:, :MLA_ROPE]
        kv = jnp.dot(ckv_n.astype(BF16), wukv, preferred_element_type=F32)
        lane = lax.broadcasted_iota(jnp.int32, aux.shape, 1)
        kr = _rope_tile(aux, cosk_ref[...], sink_ref[...]) if has_ctx else aux
        kr = jnp.where(lane < MLA_ROPE, kr, 0.0).astype(BF16)
        for h in range(MLA_HEADS):
            ka_s[0:T, 2 * h * HEAD_W:(2 * h + 1) * HEAD_W] = kv[:, h * HEAD_W:(h + 1) * HEAD_W].astype(BF16)
            ka_s[0:T, (2 * h + 1) * HEAD_W:(2 * h + 2) * HEAD_W] = kr
        ones_t = jnp.where(lane == 0, 1.0, 0.0).astype(BF16)
        for h in range(MLA_HEADS):
            hs = slice(h * HEAD_W, (h + 1) * HEAD_W)
            va_s[0:T, 2 * h * HEAD_W:(2 * h + 1) * HEAD_W] = kv[:, GROUP_W + h * HEAD_W:GROUP_W + (h + 1) * HEAD_W].astype(BF16)
            va_s[0:T, (2 * h + 1) * HEAD_W:(2 * h + 2) * HEAD_W] = ones_t
            vb_s[0:T, 2 * h * HEAD_W:(2 * h + 1) * HEAD_W] = dv_ref[:, hs]
            vb_s[0:T, (2 * h + 1) * HEAD_W:(2 * h + 2) * HEAD_W] = ones_t
        if has_ctx:
            for h in range(DIFF_HEADS):
                hs = slice(h * HEAD_W, (h + 1) * HEAD_W)
                kb_s[0:T, hs] = _rope_tile(dk_ref[:, hs].astype(F32), cosk_ref[...], sink_ref[...]).astype(BF16)
            kvc = jnp.dot(cckv_ref[...].astype(BF16), wukv, preferred_element_type=F32)
            ckr = ckr_ref[...].astype(BF16)
            for h in range(MLA_HEADS):
                ka_s[T:Tk, 2 * h * HEAD_W:(2 * h + 1) * HEAD_W] = kvc[:, h * HEAD_W:(h + 1) * HEAD_W].astype(BF16)
                ka_s[T:Tk, (2 * h + 1) * HEAD_W:(2 * h + 2) * HEAD_W] = ckr
            lane_c = lax.broadcasted_iota(jnp.int32, (PAST_LEN, LANES), 1)
            ones_c = jnp.where(lane_c == 0, 1.0, 0.0).astype(BF16)
            for h in range(DIFF_HEADS):
                hs = slice(h * HEAD_W, (h + 1) * HEAD_W)
                va_s[T:Tk, 2 * h * HEAD_W:(2 * h + 1) * HEAD_W] = kvc[:, GROUP_W + h * HEAD_W:GROUP_W + (h + 1) * HEAD_W].astype(BF16)
                va_s[T:Tk, (2 * h + 1) * HEAD_W:(2 * h + 2) * HEAD_W] = ones_c
                kb_s[T:Tk, hs] = cdk_ref[pl.ds(h, PAST_LEN, stride=DIFF_HEADS), :].astype(BF16)
                vb_s[T:Tk, 2 * h * HEAD_W:(2 * h + 1) * HEAD_W] = cdv_ref[pl.ds(h, PAST_LEN, stride=DIFF_HEADS), :].astype(BF16)
                vb_s[T:Tk, (2 * h + 1) * HEAD_W:(2 * h + 2) * HEAD_W] = ones_c
        else:
            kb_s[0:T, :] = dk_ref[...]

    nt = (((1,), (1,)), ((), ()))
    tq = cq_ref.shape[0]

    maps = []
    qa = jnp.dot(_rms(cq_ref[...], gq_ref[...]).astype(BF16), wuq_ref[...], preferred_element_type=F32)
    for h in range(MLA_HEADS):
        q_nope = qa[:, 2 * h * HEAD_W:(2 * h + 1) * HEAD_W]
        q_rope = qa[:, (2 * h + 1) * HEAD_W:(2 * h + 2) * HEAD_W]
        if has_ctx:
            q_rope = _rope_tile(q_rope, cosq_ref[...], sinq_ref[...])
        q_h = (jnp.concatenate([q_nope, q_rope], axis=-1) * (MLA_SCALE * LOG2E)).astype(BF16)
        two = slice(2 * h * HEAD_W, (2 * h + 2) * HEAD_W)
        maps.append((q_h, functools.partial(lambda sl: ka_s[:, sl], two),
                     functools.partial(lambda sl: va_s[:, sl], two)))
    lane_q = lax.broadcasted_iota(jnp.int32, (tq, HEAD_W), 1)
    for h in range(DIFF_HEADS):
        hs = slice(h * HEAD_W, (h + 1) * HEAD_W)
        q_h = dq_ref[:, hs].astype(F32)
        if has_ctx:
            q_h = _rope_tile(q_h, cosq_ref[...], sinq_ref[...])
        q_h = q_h * (DIFF_SCALE * LOG2E)
        two = slice(2 * h * HEAD_W, (2 * h + 2) * HEAD_W)
        for q_m in (jnp.where(lane_q < DIFF_D, q_h, 0.0), jnp.where(lane_q >= DIFF_D, q_h, 0.0)):
            maps.append((q_m.astype(BF16), functools.partial(lambda sl: kb_s[:, sl], hs),
                         functools.partial(lambda sl: vb_s[:, sl], two)))

    def scores(i):
        return lax.dot_general(maps[i][0], maps[i][1](), nt, preferred_element_type=F32)

    def softmax_pv(i, s):
        e = jnp.exp2(s - jnp.max(s, axis=-1, keepdims=True))
        pv = jnp.dot(e.astype(BF16), maps[i][2](), preferred_element_type=F32)
        return pv[:, :HEAD_W] * (1.0 / pv[:, HEAD_W:HEAD_W + 1])

    n_maps = len(maps)
    ahead = min(ATTN_AHEAD_CTX if not has_ctx else ATTN_AHEAD_LAT, n_maps)
    pending = {i: scores(i) for i in range(ahead)}
    outs = []
    for i in range(n_maps):
        if i + ahead < n_maps:
            pending[i + ahead] = scores(i + ahead)
        outs.append(softmax_pv(i, pending.pop(i)))

    for h in range(MLA_HEADS):
        hs = slice(h * HEAD_W, (h + 1) * HEAD_W)
        ycat_s[:, hs] = (outs[h] * za_ref[:, hs].astype(F32)).astype(BF16)

    lp = lam_ref[...]
    lf = l_ref[0].astype(F32)
    lam_init = 0.8 - 0.6 * jnp.exp(jnp.full((1, 1), -0.3, F32) * lf)
    lam = (jnp.exp(jnp.sum(lp[0:1, :] * lp[1:2, :], axis=-1, keepdims=True))
           - jnp.exp(jnp.sum(lp[2:3, :] * lp[3:4, :], axis=-1, keepdims=True)) + lam_init)
    for h in range(DIFF_HEADS):
        hs = slice(h * HEAD_W, (h + 1) * HEAD_W)
        o1, o2 = outs[MLA_HEADS + 2 * h], outs[MLA_HEADS + 2 * h + 1]
        o = _rms(o1 - lam * o2, gdn_ref[...]) * (1.0 - lam_init)
        ycat_s[:, GROUP_W + h * HEAD_W:GROUP_W + (h + 1) * HEAD_W] = (o * zb_ref[:, hs].astype(F32)).astype(BF16)

    ycat_s[:, 2 * GROUP_W:] = yc_ref[...]
    y = jnp.dot(ycat_s[...], wout_ref[...], preferred_element_type=F32)
    x_new = x_ref[...] + mod_ref[:, 2 * D_MODEL:] * y
    xo_ref[...] = x_new
    if is_last:
        yfin_ref[...] = _rms(x_new, gfin_ref[...])


def _attn_call(l_arr, main, tail, yc, x2, mod_all, w, T, has_ctx, is_last, rope=None, ctx=None, side_bufs=None):
    n_tok = main.shape[0]
    B = n_tok // T
    tq = ATTN_TQ
    nq = T // tq
    Tk = T + (PAST_LEN if has_ctx else 0)
    main3 = main.reshape(B, T, MAIN_W)
    tail3 = tail.reshape(B, T, TAIL_W)
    x3 = x2.reshape(B, T, D_MODEL)

    def tile(c, width):
        return pl.BlockSpec((None, tq, width), lambda b, q, l, c=c: (b, q, c))

    def full(c, width):
        return pl.BlockSpec((None, T, width), lambda b, q, l, c=c: (b, 0, c))

    def wspec(shape):
        return pl.BlockSpec((None,) + shape, lambda b, q, l: (l[0],) + (0,) * len(shape))

    if has_ctx:
        mod_map = lambda b, q, l: (l[0], 1 + b, 0, 0)
    else:
        mod_map = lambda b, q, l: (l[0], 0, 0, 0)

    merged = nq == 1
    if merged:
        assert (C_ZA, C_ZB) == (0, 4)
        in_specs = [pl.BlockSpec((None, T, TAIL_W), lambda b, q, l: (b, 0, 0)),
                    pl.BlockSpec((None, T, MAIN_W // 2), lambda b, q, l: (b, 0, 0))]
        args = [tail3, main3]
    else:
        in_specs = [tile(1, MLA_Q_RANK),
                    full(0, MLA_KV_RANK),
                    full(2, LANES),
                    tile(C_DQ, GROUP_W), full(C_DK, GROUP_W), full(C_DV, GROUP_W),
                    tile(C_ZB, GROUP_W), tile(C_ZA, GROUP_W)]
        args = [tail3, tail3, tail3, main3, main3, main3, main3, main3]
    in_specs += [pl.BlockSpec((None, tq, GROUP_W), lambda b, q, l: (b, q, 0)),
                 pl.BlockSpec((None, tq, D_MODEL), lambda b, q, l: (b, q, 0)),
                 pl.BlockSpec((None, None, 1, 3 * D_MODEL), mod_map),
                 wspec((MLA_Q_RANK, 2 * GROUP_W)), wspec((MLA_KV_RANK, 2 * GROUP_W)),
                 wspec((3 * GROUP_W, D_MODEL)),
                 wspec((1, MLA_Q_RANK)), wspec((1, MLA_KV_RANK)), wspec((1, HEAD_W)),
                 wspec((4, DIFF_D))]
    args += [yc, x3, mod_all, w['wuq'], w['wukv'], w['wout'], w['gq'], w['gkv'], w['gdn'], w['lam']]
    if is_last:
        in_specs.append(pl.BlockSpec((1, D_MODEL), lambda b, q, l: (0, 0)))
        args.append(w['gfin'])
    if has_ctx:
        cos_t, sin_t = rope
        in_specs += [pl.BlockSpec((tq, LANES), lambda b, q, l: (q, 0)),
                     pl.BlockSpec((tq, LANES), lambda b, q, l: (q, 0)),
                     pl.BlockSpec((T, LANES), lambda b, q, l: (0, 0)),
                     pl.BlockSpec((T, LANES), lambda b, q, l: (0, 0)),
                     pl.BlockSpec((None, None, PAST_LEN, MLA_KV_RANK), lambda b, q, l: (b, l[0], 0, 0)),
                     pl.BlockSpec((None, None, PAST_LEN, LANES), lambda b, q, l: (b, l[0], 0, 0)),
                     pl.BlockSpec((None, None, PAST_LEN * DIFF_HEADS, HEAD_W), lambda b, q, l: (b, l[0], 0, 0)),
                     pl.BlockSpec((None, None, PAST_LEN * DIFF_HEADS, HEAD_W), lambda b, q, l: (b, l[0], 0, 0))]
        args += [cos_t, sin_t, cos_t, sin_t, ctx['ckv'], ctx['krope'], ctx['dk'], ctx['dv']]
    out_shape = [jax.ShapeDtypeStruct((B, T, D_MODEL), F32)]
    out_specs = [pl.BlockSpec((None, tq, D_MODEL), lambda b, q, l: (b, q, 0))]
    if is_last:
        out_shape.append(jax.ShapeDtypeStruct((B, T, D_MODEL), F32))
        out_specs.append(pl.BlockSpec((None, tq, D_MODEL), lambda b, q, l: (b, q, 0)))
    aliases = {}
    n_alias = 0
    if not has_ctx:
        if side_bufs is not None:
            n_alias = len(side_bufs)
            n_in = 1 + len(args)
            aliases = {n_in + i: len(out_shape) + i for i in range(n_alias)}
            in_specs += [pl.BlockSpec(memory_space=pl.ANY)] * n_alias
            args += list(side_bufs)
        out_shape += [jax.ShapeDtypeStruct((B, DEPTH, T, MLA_KV_RANK), F32),
                      jax.ShapeDtypeStruct((B, DEPTH, T, MLA_ROPE), F32)]
        out_specs += [pl.BlockSpec((None, None, T, MLA_KV_RANK), lambda b, q, l: (b, l[0], 0, 0)),
                      pl.BlockSpec((None, None, T, MLA_ROPE), lambda b, q, l: (b, l[0], 0, 0))]
    grid_spec = pltpu.PrefetchScalarGridSpec(
        num_scalar_prefetch=1, grid=(B, nq), in_specs=in_specs, out_specs=out_specs,
        scratch_shapes=[pltpu.VMEM((Tk, 2 * GROUP_W), BF16),
                        pltpu.VMEM((Tk, 2 * GROUP_W), BF16),
                        pltpu.VMEM((Tk, GROUP_W), BF16),
                        pltpu.VMEM((Tk, 2 * GROUP_W), BF16),
                        pltpu.VMEM((tq, 3 * GROUP_W), BF16)])
    return pl.pallas_call(
        functools.partial(_attn_kernel, has_ctx, is_last, merged, n_alias, T),
        grid_spec=grid_spec,
        out_shape=out_shape,
        input_output_aliases=aliases,
        compiler_params=pltpu.CompilerParams(dimension_semantics=("arbitrary", "arbitrary"),
                                             vmem_limit_bytes=VMEM_LIMIT),
        name="attn_lat" if has_ctx else "attn_ctx",
    )(l_arr, *args)


def _rope_tables(n_tok):
    n_freq = MLA_ROPE // 4
    inv = ROPE_THETA ** (-jnp.arange(n_freq, dtype=F32) / n_freq)
    n_rows = n_tok // GRID_W
    rowp = jnp.repeat(jnp.arange(n_rows, dtype=F32), GRID_W)
    colp = jnp.tile(jnp.arange(GRID_W, dtype=F32), n_rows)
    ang = jnp.concatenate([rowp[:, None] * inv, colp[:, None] * inv], axis=-1)
    cos, sin = jnp.cos(ang), jnp.sin(ang)
    cos64 = jnp.concatenate([cos, cos], axis=-1)
    sin64 = jnp.concatenate([-sin, sin], axis=-1)
    return jnp.concatenate([cos64, cos64], axis=-1), jnp.concatenate([sin64, sin64], axis=-1)


def _prep_weights(W_in, W_uq, W_ukv, W_out):
    seg = lambda o, n: W_in[:, :, o:o + n]
    w_mid = _wprep_call(W_in)
    pad = jnp.zeros((DEPTH, D_MODEL, LANES - MLA_ROPE - N_ML_GATES), W_in.dtype)
    w_tail = jnp.concatenate([seg(_O_CKV, MLA_KV_RANK), seg(_O_KR, MLA_ROPE), seg(_O_MG, N_ML_GATES), pad,
                              seg(_O_CQ, MLA_Q_RANK)], axis=-1).astype(BF16)
    wq = W_uq.reshape(DEPTH, MLA_Q_RANK, MLA_HEADS, MLA_NOPE + MLA_ROPE)
    wq = jnp.pad(wq, ((0, 0), (0, 0), (0, 0), (0, 2 * HEAD_W - MLA_NOPE - MLA_ROPE)))
    wuq_r = wq.reshape(DEPTH, MLA_Q_RANK, MLA_HEADS * 2 * HEAD_W).astype(BF16)
    wkv = W_ukv.reshape(DEPTH, MLA_KV_RANK, MLA_HEADS, 2, HEAD_W)
    wukv_r = jnp.swapaxes(wkv, 2, 3).reshape(DEPTH, MLA_KV_RANK, 2 * GROUP_W).astype(BF16)
    return w_mid, w_tail, wuq_r, wukv_r, W_out.astype(BF16)


def kernel(x_prompt, x_sample, cache_mla_ckv, cache_mla_krope, cache_diff_k, cache_diff_v, state_mlstm_C, state_mlstm_n, state_mlstm_m, c, c_ctx, g_norm, W_mod, b_mod, W_in, mla_q_norm, W_uq, mla_kv_norm, W_ukv, diff_lambda, diff_norm, ml_conv, ml_gate_b, ml_norm, W_out, g_final):
    Bc, Tc, _ = x_prompt.shape
    Bs, Ts, _ = x_sample.shape

    w_mid, w_tail, wuq_r, wukv_r, wout_r = _prep_weights(W_in, W_uq, W_ukv, W_out)
    w = {'wuq': wuq_r, 'wukv': wukv_r, 'wout': wout_r,
         'gq': mla_q_norm.reshape(DEPTH, 1, MLA_Q_RANK), 'gkv': mla_kv_norm.reshape(DEPTH, 1, MLA_KV_RANK),
         'gdn': diff_norm.reshape(DEPTH, 1, 2 * DIFF_D), 'lam': diff_lambda,
         'gfin': g_final.reshape(1, D_MODEL)}
    g_norm3 = g_norm.reshape(DEPTH, 1, D_MODEL)
    ml_norm2 = ml_norm.reshape(DEPTH, 1, GROUP_W)
    gate_row = jnp.pad(ml_gate_b.reshape(DEPTH, 1, N_ML_GATES),
                       ((0, 0), (0, 0), (GATE_LANE0, LANES - GATE_LANE0 - N_ML_GATES)))

    cc = jnp.concatenate([c_ctx[None, :], c, jnp.zeros((8 - 1 - Bs, D_MODEL), F32)], axis=0)
    mod_all = _mod_call(cc, W_mod, b_mod).reshape(DEPTH, 8, 1, 3 * D_MODEL)

    dkv = [jnp.zeros((Bc, DEPTH, Tc * DIFF_HEADS, HEAD_W), F32)] * 2
    cnm = [jnp.zeros((Bc, DEPTH, 2 * ML_HEADS, ML_DK, HEAD_W), F32),
           jnp.zeros((Bc, DEPTH, 2 * ML_HEADS, ML_DK), F32), jnp.zeros((Bc, DEPTH, 2 * ML_HEADS, LANES), F32)]
    ckr = [jnp.zeros((Bc, DEPTH, Tc, MLA_KV_RANK), F32), jnp.zeros((Bc, DEPTH, Tc, MLA_ROPE), F32)]
    x2 = x_prompt.reshape(Bc * Tc, D_MODEL)
    y_prompt = None
    for l in range(DEPTH):
        l_arr = jnp.full((1,), l, jnp.int32)
        main, tail, *dkv = _proj_call(l_arr, x2, mod_all, g_norm3, w_mid, w_tail, Tc, False, dkv)
        yc, *cnm = _mlstm_call(l_arr, main, tail, ml_conv, gate_row, ml_norm2, Tc, False, side_bufs=cnm)
        outs = _attn_call(l_arr, main, tail, yc, x2, mod_all, w, Tc, False, l == DEPTH - 1, side_bufs=ckr)
        ckr = outs[-2:]
        dk_o, dv_o = dkv
        c_o, n_o, m_o = cnm
        if l == DEPTH - 1:
            x3, y_prompt, ckvn, kro = outs
        else:
            x3, ckvn, kro = outs
        x2 = x3.reshape(Bc * Tc, D_MODEL)
    side_outs = (ckvn, kro,
                 dk_o.reshape(Bc, DEPTH, Tc, DIFF_HEADS, 2 * DIFF_D),
                 dv_o.reshape(Bc, DEPTH, Tc, DIFF_HEADS, 2 * DIFF_D),
                 c_o.reshape(Bc, DEPTH, 2, ML_HEADS, ML_DK, HEAD_W),
                 n_o.reshape(Bc, DEPTH, 2, ML_HEADS, ML_DK),
                 m_o[:, :, :, 0].reshape(Bc, DEPTH, 2, ML_HEADS))

    rope = _rope_tables(Ts)
    ctx = {'ckv': cache_mla_ckv,
           'krope': jnp.pad(cache_mla_krope, ((0, 0), (0, 0), (0, 0), (0, LANES - MLA_ROPE))),
           'dk': cache_diff_k.reshape(Bs, DEPTH, PAST_LEN * DIFF_HEADS, HEAD_W),
           'dv': cache_diff_v.reshape(Bs, DEPTH, PAST_LEN * DIFF_HEADS, HEAD_W)}
    ctx_state = (state_mlstm_C.reshape(Bs, DEPTH, 2 * ML_HEADS, ML_DK, HEAD_W),
                 state_mlstm_n.reshape(Bs, DEPTH, 2 * ML_HEADS, ML_DK),
                 jnp.broadcast_to(state_mlstm_m.reshape(Bs, DEPTH, 2 * ML_HEADS, 1),
                                  (Bs, DEPTH, 2 * ML_HEADS, LANES)))
    x2 = x_sample.reshape(Bs * Ts, D_MODEL)
    y_sample = None
    for l in range(DEPTH):
        l_arr = jnp.full((1,), l, jnp.int32)
        main, tail = _proj_call(l_arr, x2, mod_all, g_norm3, w_mid, w_tail, Ts, True)
        (yc,) = _mlstm_call(l_arr, main, tail, ml_conv, gate_row, ml_norm2, Ts, True, ctx_state)
        outs = _attn_call(l_arr, main, tail, yc, x2, mod_all, w, Ts, True, l == DEPTH - 1, rope, ctx)
        if l == DEPTH - 1:
            x3, y_sample = outs
        else:
            (x3,) = outs
        x2 = x3.reshape(Bs * Ts, D_MODEL)

    return (y_prompt, y_sample, *side_outs)
```

```python
import functools
import math

import jax
import jax.numpy as jnp
import numpy as np
from jax import lax
from jax.experimental import pallas as pl
from jax.experimental.pallas import tpu as pltpu

F32 = jnp.float32
BF16 = jnp.bfloat16

D_MODEL = 1024
DEPTH = 4
PAST_LEN = 256
GRID_W = 64
GROUP_W = 512
MLA_HEADS = 4
MLA_NOPE = 128
MLA_ROPE = 64
MLA_Q_RANK = 384
MLA_KV_RANK = 256
DIFF_HEADS = 4
DIFF_D = 64
ML_HEADS = 4
ML_DK = 128
N_ML_GATES = 16
ROPE_THETA = 10000.0
NORM_EPS = 1e-6
MLA_SCALE = (MLA_NOPE + MLA_ROPE) ** -0.5
DIFF_SCALE = DIFF_D ** -0.5
LOG2E = math.log2(math.e)

LANES = 128
HEAD_W = 128
ML_CHUNK = 256
VT_ROWS = 144
PROJ_TM = 512
ATTN_TQ = 256
ATTN_AHEAD_CTX = 12
ATTN_AHEAD_LAT = 4
MAIN_W = 10 * GROUP_W
TAIL_W = 768
N_IN = 5840
VMEM_LIMIT = 56 * 1024 * 1024

C_ZA, C_DQ, C_DK, C_DV, C_ZB, C_MQ, C_MK, C_MV, C_MO, C_ZC = range(10)

_IN_SIZES = (MLA_Q_RANK, MLA_KV_RANK, MLA_ROPE, GROUP_W, GROUP_W, GROUP_W, GROUP_W, GROUP_W,
             GROUP_W, GROUP_W, GROUP_W, GROUP_W, GROUP_W, N_ML_GATES)
_IN_OFF = np.concatenate([[0], np.cumsum(_IN_SIZES)])
(_O_CQ, _O_CKV, _O_KR, _O_ZA, _O_DQ, _O_DK, _O_DV, _O_ZB, _O_MQ, _O_MK, _O_MV, _O_MO, _O_ZC,
 _O_MG) = [int(v) for v in _IN_OFF[:-1]]
GATE_LANE0 = MLA_ROPE


def _rms(x, g):
    ms = jnp.mean(x * x, axis=-1, keepdims=True)
    return x * lax.rsqrt(ms + NORM_EPS) * g


def _silu(x):
    return x * jax.nn.sigmoid(x)


def _log_sigmoid(x):
    return jnp.minimum(x, 0.0) - jnp.log1p(jnp.exp(-jnp.abs(x)))


def _swap32(x):
    lane = lax.broadcasted_iota(jnp.int32, x.shape, 1)
    fwd = pltpu.roll(x, LANES - 32, 1)
    bwd = pltpu.roll(x, 32, 1)
    return jnp.where((lane % 64) < 32, fwd, bwd)


def _rope_tile(x, cos, sin):
    return x * cos + _swap32(x) * sin


def _mod_kernel(c_ref, w_ref, b_ref, o_ref):
    a = _silu(c_ref[...]).astype(BF16)
    o_ref[...] = jnp.dot(a, w_ref[...].astype(BF16), preferred_element_type=F32) + b_ref[...]


def _mod_call(cc, W_mod, b_mod):
    tn = 1024
    return pl.pallas_call(
        _mod_kernel,
        grid=(DEPTH, 3 * D_MODEL // tn),
        in_specs=[pl.BlockSpec((8, D_MODEL), lambda l, j: (0, 0)),
                  pl.BlockSpec((None, D_MODEL, tn), lambda l, j: (l, 0, j)),
                  pl.BlockSpec((None, 1, tn), lambda l, j: (l, 0, j))],
        out_specs=pl.BlockSpec((None, 8, tn), lambda l, j: (l, 0, j)),
        out_shape=jax.ShapeDtypeStruct((DEPTH, 8, 3 * D_MODEL), F32),
        compiler_params=pltpu.CompilerParams(dimension_semantics=("arbitrary", "arbitrary"),
                                             vmem_limit_bytes=VMEM_LIMIT),
        name="mod",
    )(cc, W_mod, b_mod.reshape(DEPTH, 1, 3 * D_MODEL))


def _proj_kernel(has_side, n_alias, T, *refs):
    l_ref, x_ref, mod_ref, g_ref, w_ref = refs[:5]
    refs = refs[5 + n_alias:]
    if has_side:
        main_ref, tail_ref, dk_ref, dv_ref, wt_s = refs
    else:
        main_ref, tail_ref, wt_s = refs
    del l_ref
    tm = x_ref.shape[0]
    nt = (((1,), (1,)), ((), ()))

    @pl.when(pl.program_id(0) == 0)
    def _gather_tail_rows():
        wt_s[0:MLA_ROPE, :] = w_ref[_O_KR:_O_KR + MLA_ROPE, :]
        wt_s[MLA_ROPE:MLA_ROPE + N_ML_GATES, :] = w_ref[_O_MG:_O_MG + N_ML_GATES, :]
        wt_s[MLA_ROPE + N_ML_GATES:LANES, :] = jnp.zeros((LANES - MLA_ROPE - N_ML_GATES, D_MODEL), BF16)
        wt_s[LANES:, :] = w_ref[_O_CQ:_O_CQ + MLA_Q_RANK, :]

    mod = mod_ref[...]
    y = _rms(x_ref[...], g_ref[...])
    h = (y * (1.0 + mod[:, D_MODEL:2 * D_MODEL]) + mod[:, :D_MODEL]).astype(BF16)
    for s in range(MAIN_W // GROUP_W):
        cols = slice(s * GROUP_W, (s + 1) * GROUP_W)
        acc = lax.dot_general(h, w_ref[_O_ZA + s * GROUP_W:_O_ZA + (s + 1) * GROUP_W, :], nt,
                              preferred_element_type=F32)
        if s in (C_ZA, C_ZB, C_ZC):
            main_ref[:, cols] = _silu(acc).astype(BF16)
        elif s == C_MO:
            main_ref[:, cols] = jax.nn.sigmoid(acc).astype(BF16)
        else:
            main_ref[:, cols] = acc.astype(BF16)
        if has_side and s in (C_DK, C_DV):
            side_ref = dk_ref if s == C_DK else dv_ref
            for b in range(tm // T):
                for hd in range(DIFF_HEADS):
                    side_ref[b, pl.ds(hd, T, stride=DIFF_HEADS), :] = (
                        acc[b * T:(b + 1) * T, hd * HEAD_W:(hd + 1) * HEAD_W])
    tail_ref[:, 0:MLA_KV_RANK] = lax.dot_general(h, w_ref[_O_CKV:_O_CKV + MLA_KV_RANK, :], nt,
                                                 preferred_element_type=F32)
    tail_ref[:, MLA_KV_RANK:] = lax.dot_general(h, wt_s[...], nt, preferred_element_type=F32)


def _proj_call(l_arr, x2, mod_all, g_norm3, w_in_t, T, has_ctx, side_bufs=None):
    n_tok = x2.shape[0]
    tm = PROJ_TM
    has_side = not has_ctx
    if has_ctx:
        mod_map = lambda i, l: (l[0], 1 + (i * tm) // T, 0, 0)
    else:
        mod_map = lambda i, l: (l[0], 0, 0, 0)
    in_specs = [pl.BlockSpec((tm, D_MODEL), lambda i, l: (i, 0)),
                pl.BlockSpec((None, None, 1, 3 * D_MODEL), mod_map),
                pl.BlockSpec((None, 1, D_MODEL), lambda i, l: (l[0], 0, 0)),
                pl.BlockSpec((None, N_IN, D_MODEL), lambda i, l: (l[0], 0, 0), pipeline_mode=pl.Buffered(1))]
    args = [x2, mod_all, g_norm3, w_in_t]
    out_shape = [jax.ShapeDtypeStruct((n_tok, MAIN_W), BF16), jax.ShapeDtypeStruct((n_tok, TAIL_W), F32)]
    out_specs = [pl.BlockSpec((tm, MAIN_W), lambda i, l: (i, 0)), pl.BlockSpec((tm, TAIL_W), lambda i, l: (i, 0))]
    aliases = {}
    n_alias = 0
    if has_side:
        bt = tm // T
        if side_bufs is not None:
            n_alias = len(side_bufs)
            aliases = {5 + i: 2 + i for i in range(n_alias)}
            in_specs += [pl.BlockSpec(memory_space=pl.ANY)] * n_alias
            args += list(side_bufs)
        out_shape += [jax.ShapeDtypeStruct((n_tok // T, DEPTH, T * DIFF_HEADS, HEAD_W), F32)] * 2
        out_specs += [pl.BlockSpec((bt, None, T * DIFF_HEADS, HEAD_W), lambda i, l: (i, l[0], 0, 0))] * 2
    grid_spec = pltpu.PrefetchScalarGridSpec(
        num_scalar_prefetch=1, grid=(n_tok // tm,), in_specs=in_specs, out_specs=out_specs,
        scratch_shapes=[pltpu.VMEM((TAIL_W - MLA_KV_RANK, D_MODEL), BF16)])
    return pl.pallas_call(
        functools.partial(_proj_kernel, has_side, n_alias, T),
        grid_spec=grid_spec,
        out_shape=out_shape,
        input_output_aliases=aliases,
        compiler_params=pltpu.CompilerParams(dimension_semantics=("arbitrary",),
                                             vmem_limit_bytes=VMEM_LIMIT),
        name="proj_lat" if has_ctx else "proj_ctx",
    )(l_arr, *args)


def _mlstm_kernel(has_ctx, T, n_alias, *refs):
    nc = T // ML_CHUNK
    L = ML_CHUNK
    use_inter = has_ctx or nc > 1
    it = iter(refs)
    l_ref = next(it)
    ml_ref, aux_ref, convw_ref, gb_ref, mln_ref = [next(it) for _ in range(5)]
    mq_ref, mk_ref, mv_ref, mo_ref, zc_ref = [ml_ref.at[:, i * GROUP_W:(i + 1) * GROUP_W] for i in range(5)]
    if has_ctx:
        c0_ref, n0_ref, m0_ref = [next(it) for _ in range(3)]
    else:
        for _ in range(n_alias):
            next(it)
    yc_ref = next(it)
    if not has_ctx:
        cout_ref, nout_ref, mout_ref = [next(it) for _ in range(3)]
    q_s, k_s, vt_s, hf_s, hb_s, caug_s, m_s, gt_s = [next(it) for _ in range(8)]
    del l_ref

    row_b = lax.broadcasted_iota(jnp.int32, (L, LANES), 0)
    convw = convw_ref[...]
    halo = 16
    for c in range(nc):
        cs = slice(c * L, (c + 1) * L)
        for j in range(2 * GROUP_W // LANES):
            src = mq_ref if j < GROUP_W // LANES else mk_ref
            ls = slice((j % (GROUP_W // LANES)) * LANES, (j % (GROUP_W // LANES) + 1) * LANES)
            u = src[cs, ls].astype(F32)
            prev = src[c * L - halo:c * L, ls].astype(F32)[halo - 1:halo, :] if c > 0 else 0.0
            nxt = src[(c + 1) * L:(c + 1) * L + halo, ls].astype(F32)[0:1, :] if c < nc - 1 else 0.0
            up = jnp.where(row_b == 0, prev, pltpu.roll(u, 1, 0))
            un = jnp.where(row_b == L - 1, nxt, pltpu.roll(u, L - 1, 0))
            w3 = convw[:, j * LANES:(j + 1) * LANES]
            y = _silu(w3[0:1, :] * up + w3[1:2, :] * u + w3[2:3, :] * un)
            if j < GROUP_W // LANES:
                q_s[cs, ls] = (y * (ML_DK ** -0.5)).astype(BF16)
            else:
                k_s[cs, ls] = y
    row16 = lax.broadcasted_iota(jnp.int32, (VT_ROWS - HEAD_W, L), 0)
    ones_rows = jnp.where(row16 == 0, 1.0, 0.0).astype(BF16)
    g_t = (aux_ref[...] + gb_ref[...]).T
    for c in range(nc):
        cs = slice(c * L, (c + 1) * L)
        gt_s[c] = g_t[GATE_LANE0:GATE_LANE0 + N_ML_GATES, cs]
        for h in range(ML_HEADS):
            vt_s[c, h, 0:HEAD_W, :] = mv_ref[cs, h * HEAD_W:(h + 1) * HEAD_W].astype(F32).T.astype(BF16)
            vt_s[c, h, HEAD_W:, :] = ones_rows

    reps = L // LANES
    if has_ctx:
        n0 = n0_ref[...]
        m0 = m0_ref[...]
        m_s[0] = jnp.concatenate([m0] * reps, axis=1)
        m_s[1] = jnp.concatenate([pltpu.roll(m0, ML_HEADS, 0)] * reps, axis=1)
        row_n = lax.broadcasted_iota(jnp.int32, (VT_ROWS - HEAD_W, ML_DK), 0)
        for r in range(2 * ML_HEADS):
            caug_s[r, 0:HEAD_W, :] = c0_ref[r].T
            caug_s[r, HEAD_W:, :] = jnp.where(row_n == 0, n0[r:r + 1, :], 0.0)
    else:
        caug_s[...] = jnp.zeros_like(caug_s)
        m_s[...] = jnp.zeros_like(m_s)

    ri = lax.broadcasted_iota(jnp.int32, (L, L), 0)
    ci = lax.broadcasted_iota(jnp.int32, (L, L), 1)
    lane_r = lax.broadcasted_iota(jnp.int32, (2 * ML_HEADS, L), 1)
    nt = (((1,), (1,)), ((), ()))

    def split3(x):
        hi = x.astype(BF16).astype(F32)
        mid = (x - hi).astype(BF16).astype(F32)
        return hi, mid, (x - hi - mid).astype(BF16).astype(F32)

    masks = ((ri <= ci), (ri >= ci))

    def gate_rows(d, c):
        fwd = d == 0
        tri_t = jnp.where(masks[d], 1.0, 0.0).astype(BF16)
        last = L - 1 if fwd else 0

        g8 = gt_s[c, d * 2 * ML_HEADS:(d + 1) * 2 * ML_HEADS, :]
        hi8, mid8, lo8 = split3(_log_sigmoid(g8))
        stack = jnp.concatenate([hi8, mid8, lo8, jnp.zeros_like(hi8)], axis=0).astype(BF16)
        part8 = jnp.dot(stack, tri_t, preferred_element_type=F32)
        bc8 = part8[0:8] + part8[8:16] + part8[16:24]
        bcs = pltpu.roll(bc8, ML_HEADS, 0)
        a8 = g8 - bcs
        cm8 = a8
        k = 1
        while k < L:
            if fwd:
                cm8 = jnp.maximum(cm8, jnp.where(lane_r >= k, pltpu.roll(cm8, k, 1), -jnp.inf))
            else:
                cm8 = jnp.maximum(cm8, jnp.where(lane_r < L - k, pltpu.roll(cm8, L - k, 1), -jnp.inf))
            k *= 2
        m8 = m_s[d]
        g_row = bcs + m8
        m_t = jnp.maximum(g_row, bcs + cm8)
        w_inter = jnp.exp(g_row - m_t)
        e_inv = jnp.exp(-m_t)
        c_row = bcs - m_t
        b_last = bcs[:, last:last + 1]
        m_new = m_t[:, last:last + 1]
        a_prev = jnp.exp(b_last + m8[:, 0:1] - m_new)
        w_s = jnp.exp(a8 + (b_last - m_new))
        m_s[d] = jnp.broadcast_to(m_new, (2 * ML_HEADS, L))
        a_n = jnp.concatenate([a8, jnp.zeros((LANES - 2 * ML_HEADS, L), F32)], axis=0).T
        return a_n, c_row, w_inter, e_inv, w_s, a_prev

    def do_pair(c_f, c_b):
        stats = (gate_rows(0, c_f), gate_rows(1, c_b))
        combos = [(d, h) for d in range(2) for h in range(ML_HEADS)]
        rows = []
        for c in (c_f, c_b):
            rows.append(slice(c * L, (c + 1) * L) if isinstance(c, int) else pl.ds(pl.multiple_of(c * L, L), L))
        chunk = (c_f, c_b)
        hsl = [slice(h * HEAD_W, (h + 1) * HEAD_W) for h in range(ML_HEADS)]
        qcs = [q_s[rows[d], hsl[h]] for d, h in combos]
        kcs = [k_s[rows[d], hsl[h]].astype(BF16) for d, h in combos]
        vts = [vt_s[chunk[d], h] for d, h in combos]
        n = range(len(combos))
        s_ts = [lax.dot_general(kcs[i], qcs[i], nt, preferred_element_type=F32) for i in n]
        sws = []
        for i, (d, h) in enumerate(combos):
            a_n, c_row = stats[d][0], stats[d][1]
            w_t = jnp.exp(jnp.where(masks[d], a_n[:, h:h + 1] + c_row[h:h + 1, :], -jnp.inf))
            sws.append((s_ts[i] * w_t).astype(BF16))
        nds = [jnp.dot(vts[i], sws[i], preferred_element_type=F32) for i in n]
        if use_inter:
            inters = [lax.dot_general(caug_s[d * ML_HEADS + h].astype(BF16), qcs[i], nt,
                                      preferred_element_type=F32) for i, (d, h) in enumerate(combos)]
            nds = [nds[i] + stats[d][2][h:h + 1, :] * inters[i] for i, (d, h) in enumerate(combos)]
        for i, (d, h) in enumerate(combos):
            inv = 1.0 / jnp.maximum(jnp.abs(nds[i][HEAD_W:HEAD_W + 1, :]), stats[d][3][h:h + 1, :])
            h_s = hf_s if d == 0 else hb_s
            h_s[rows[d], hsl[h]] = (nds[i][0:HEAD_W, :] * inv).T
        upds = [jnp.dot((vts[i].astype(F32) * stats[d][4][h:h + 1, :]).astype(BF16), kcs[i],
                        preferred_element_type=F32) for i, (d, h) in enumerate(combos)]
        for i, (d, h) in enumerate(combos):
            r = d * ML_HEADS + h
            caug_s[r] = upds[i] + stats[d][5][h:h + 1, :] * caug_s[r] if use_inter else upds[i]

    if nc == 1:
        do_pair(0, 0)
    else:
        def body(i, carry):
            do_pair(i, nc - 1 - i)
            return carry
        lax.fori_loop(0, nc, body, 0)

    mln = mln_ref[...]
    for c in range(nc):
        cs = slice(c * L, (c + 1) * L)
        for h in range(ML_HEADS):
            hs = slice(h * HEAD_W, (h + 1) * HEAD_W)
            hc = mo_ref[cs, hs].astype(F32) * (hf_s[cs, hs] + hb_s[cs, hs])
            yc_ref[cs, hs] = (_rms(hc, mln[:, hs]) * zc_ref[cs, hs].astype(F32)).astype(BF16)

    if not has_ctx:
        for r in range(2 * ML_HEADS):
            cout_ref[r] = caug_s[r, 0:HEAD_W, :].T
            nout_ref[r:r + 1, :] = caug_s[r, HEAD_W:HEAD_W + 1, :]
        for d in range(2):
            mout_ref[d * ML_HEADS:(d + 1) * ML_HEADS, :] = m_s[d, 0:ML_HEADS, 0:LANES]


def _mlstm_call(l_arr, main, tail, ml_conv, gate_row, ml_norm2, T, has_ctx, ctx_state=None, side_bufs=None):
    n_tok = main.shape[0]
    B = n_tok // T
    main3 = main.reshape(B, T, MAIN_W)
    tail3 = tail.reshape(B, T, TAIL_W)

    assert (C_MQ, C_ZC) == (5, 9)
    in_specs = [pl.BlockSpec((None, T, MAIN_W // 2), lambda b, l: (b, 0, 1)),
                pl.BlockSpec((None, T, LANES), lambda b, l: (b, 0, 2)),
                pl.BlockSpec((None, 3, 2 * GROUP_W), lambda b, l: (l[0], 0, 0)),
                pl.BlockSpec((None, 1, LANES), lambda b, l: (l[0], 0, 0)),
                pl.BlockSpec((None, 1, GROUP_W), lambda b, l: (l[0], 0, 0))]
    args = [main3, tail3, ml_conv, gate_row, ml_norm2]
    n_alias = 0 if (has_ctx or side_bufs is None) else len(side_bufs)
    out_shape = [jax.ShapeDtypeStruct((B, T, GROUP_W), BF16)]
    out_specs = [pl.BlockSpec((None, T, GROUP_W), lambda b, l: (b, 0, 0))]
    if has_ctx:
        c0, n0, m0 = ctx_state
        in_specs += [pl.BlockSpec((None, None, 2 * ML_HEADS, ML_DK, HEAD_W), lambda b, l: (b, l[0], 0, 0, 0)),
                     pl.BlockSpec((None, None, 2 * ML_HEADS, ML_DK), lambda b, l: (b, l[0], 0, 0)),
                     pl.BlockSpec((None, None, 2 * ML_HEADS, LANES), lambda b, l: (b, l[0], 0, 0))]
        args += [c0, n0, m0]
        aliases = {}
    else:
        aliases = {}
        if n_alias:
            aliases = {1 + len(args) + i: 1 + i for i in range(n_alias)}
            in_specs += [pl.BlockSpec(memory_space=pl.ANY)] * n_alias
            args += list(side_bufs)
        out_shape += [jax.ShapeDtypeStruct((B, DEPTH, 2 * ML_HEADS, ML_DK, HEAD_W), F32),
                      jax.ShapeDtypeStruct((B, DEPTH, 2 * ML_HEADS, ML_DK), F32),
                      jax.ShapeDtypeStruct((B, DEPTH, 2 * ML_HEADS, LANES), F32)]
        out_specs += [pl.BlockSpec((None, None, 2 * ML_HEADS, ML_DK, HEAD_W), lambda b, l: (b, l[0], 0, 0, 0)),
                      pl.BlockSpec((None, None, 2 * ML_HEADS, ML_DK), lambda b, l: (b, l[0], 0, 0)),
                      pl.BlockSpec((None, None, 2 * ML_HEADS, LANES), lambda b, l: (b, l[0], 0, 0))]
    grid_spec = pltpu.PrefetchScalarGridSpec(
        num_scalar_prefetch=1, grid=(B,), in_specs=in_specs, out_specs=out_specs,
        scratch_shapes=[pltpu.VMEM((T, GROUP_W), BF16),
                        pltpu.VMEM((T, GROUP_W), F32),
                        pltpu.VMEM((T // ML_CHUNK, ML_HEADS, VT_ROWS, ML_CHUNK), BF16),
                        pltpu.VMEM((T, GROUP_W), F32),
                        pltpu.VMEM((T, GROUP_W), F32),
                        pltpu.VMEM((2 * ML_HEADS, VT_ROWS, ML_DK), F32),
                        pltpu.VMEM((2, 2 * ML_HEADS, ML_CHUNK), F32),
                        pltpu.VMEM((T // ML_CHUNK, N_ML_GATES, ML_CHUNK), F32)])
    return pl.pallas_call(
        functools.partial(_mlstm_kernel, has_ctx, T, n_alias),
        grid_spec=grid_spec,
        out_shape=out_shape,
        input_output_aliases=aliases,
        compiler_params=pltpu.CompilerParams(dimension_semantics=("arbitrary",),
                                             vmem_limit_bytes=VMEM_LIMIT),
        name="mlstm_lat" if has_ctx else "mlstm_ctx",
    )(l_arr, *args)


def _attn_kernel(has_ctx, is_last, merged, n_alias, T, *refs):
    Tk = T + (PAST_LEN if has_ctx else 0)
    it = iter(refs)
    l_ref = next(it)
    if merged:
        tail_ref, att_ref = next(it), next(it)
        ckv_ref = tail_ref.at[:, 0:MLA_KV_RANK]
        aux_ref = tail_ref.at[:, MLA_KV_RANK:MLA_KV_RANK + LANES]
        cq_ref = tail_ref.at[:, MLA_KV_RANK + LANES:]
        za_ref, dq_ref, dk_ref, dv_ref, zb_ref = [att_ref.at[:, i * GROUP_W:(i + 1) * GROUP_W] for i in range(5)]
    else:
        cq_ref, ckv_ref, aux_ref, dq_ref, dk_ref, dv_ref, zb_ref, za_ref = [next(it) for _ in range(8)]
    (yc_ref, x_ref, mod_ref, wuq_ref, wukv_ref, wout_ref, gq_ref, gkv_ref, gdn_ref,
     lam_ref) = [next(it) for _ in range(10)]
    if is_last:
        gfin_ref = next(it)
    if has_ctx:
        cosq_ref, sinq_ref, cosk_ref, sink_ref, cckv_ref, ckr_ref, cdk_ref, cdv_ref = [next(it) for _ in range(8)]
    else:
        for _ in range(n_alias):
            next(it)
    xo_ref = next(it)
    if is_last:
        yfin_ref = next(it)
    if not has_ctx:
        ckvn_ref, kro_ref = [next(it) for _ in range(2)]
    ka_s, va_s, kb_s, vb_s, ycat_s = [next(it) for _ in range(5)]

    qi = pl.program_id(1)

    @pl.when(qi == 0)
    def _build_keys():
        wukv = wukv_ref[...]
        ckv_n = _rms(ckv_ref[...], gkv_ref[...])
        aux = aux_ref[...]
        if not has_ctx:
            ckvn_ref[...] = ckv_n
            kro_ref[...] = aux[:, :MLA_ROPE]
        kv = jnp.dot(ckv_n.astype(BF16), wukv, preferred_element_type=F32)
        lane = lax.broadcasted_iota(jnp.int32, aux.shape, 1)
        kr = _rope_tile(aux, cosk_ref[...], sink_ref[...]) if has_ctx else aux
        kr = jnp.where(lane < MLA_ROPE, kr, 0.0).astype(BF16)
        for h in range(MLA_HEADS):
            ka_s[0:T, 2 * h * HEAD_W:(2 * h + 1) * HEAD_W] = kv[:, h * HEAD_W:(h + 1) * HEAD_W].astype(BF16)
            ka_s[0:T, (2 * h + 1) * HEAD_W:(2 * h + 2) * HEAD_W] = kr
        row16 = lax.broadcasted_iota(jnp.int32, (VT_ROWS - HEAD_W, Tk), 0)
        ones_rows = jnp.where(row16 == 0, 1.0, 0.0).astype(BF16)
        for h in range(MLA_HEADS):
            hs = slice(h * HEAD_W, (h + 1) * HEAD_W)
            va_s[h, 0:HEAD_W, 0:T] = kv[:, GROUP_W + h * HEAD_W:GROUP_W + (h + 1) * HEAD_W].T.astype(BF16)
            va_s[h, HEAD_W:, :] = ones_rows
            vb_s[h, 0:HEAD_W, 0:T] = dv_ref[:, hs].astype(F32).T.astype(BF16)
            vb_s[h, HEAD_W:, :] = ones_rows
        if has_ctx:
            for h in range(DIFF_HEADS):
                hs = slice(h * HEAD_W, (h + 1) * HEAD_W)
                kb_s[0:T, hs] = _rope_tile(dk_ref[:, hs].astype(F32), cosk_ref[...], sink_ref[...]).astype(BF16)
            kvc = jnp.dot(cckv_ref[...].astype(BF16), wukv, preferred_element_type=F32)
            ckr = ckr_ref[...].astype(BF16)
            for h in range(MLA_HEADS):
                ka_s[T:Tk, 2 * h * HEAD_W:(2 * h + 1) * HEAD_W] = kvc[:, h * HEAD_W:(h + 1) * HEAD_W].astype(BF16)
                ka_s[T:Tk, (2 * h + 1) * HEAD_W:(2 * h + 2) * HEAD_W] = ckr
            for h in range(DIFF_HEADS):
                hs = slice(h * HEAD_W, (h + 1) * HEAD_W)
                va_s[h, 0:HEAD_W, T:Tk] = kvc[:, GROUP_W + h * HEAD_W:GROUP_W + (h + 1) * HEAD_W].T.astype(BF16)
                kb_s[T:Tk, hs] = cdk_ref[pl.ds(h, PAST_LEN, stride=DIFF_HEADS), :].astype(BF16)
                vb_s[h, 0:HEAD_W, T:Tk] = cdv_ref[pl.ds(h, PAST_LEN, stride=DIFF_HEADS), :].T.astype(BF16)
        else:
            kb_s[0:T, :] = dk_ref[...]

    nt = (((1,), (1,)), ((), ()))
    tq = cq_ref.shape[0]

    maps = []
    qa = jnp.dot(_rms(cq_ref[...], gq_ref[...]).astype(BF16), wuq_ref[...], preferred_element_type=F32)
    for h in range(MLA_HEADS):
        q_nope = qa[:, 2 * h * HEAD_W:(2 * h + 1) * HEAD_W]
        q_rope = qa[:, (2 * h + 1) * HEAD_W:(2 * h + 2) * HEAD_W]
        if has_ctx:
            q_rope = _rope_tile(q_rope, cosq_ref[...], sinq_ref[...])
        q_h = (jnp.concatenate([q_nope, q_rope], axis=-1) * (MLA_SCALE * LOG2E)).astype(BF16)
        two = slice(2 * h * HEAD_W, (2 * h + 2) * HEAD_W)
        maps.append((q_h, functools.partial(lambda sl: ka_s[:, sl], two), functools.partial(lambda hh: va_s[hh], h)))
    lane_q = lax.broadcasted_iota(jnp.int32, (tq, HEAD_W), 1)
    for h in range(DIFF_HEADS):
        hs = slice(h * HEAD_W, (h + 1) * HEAD_W)
        q_h = dq_ref[:, hs].astype(F32)
        if has_ctx:
            q_h = _rope_tile(q_h, cosq_ref[...], sinq_ref[...])
        q_h = q_h * (DIFF_SCALE * LOG2E)
        for q_m in (jnp.where(lane_q < DIFF_D, q_h, 0.0), jnp.where(lane_q >= DIFF_D, q_h, 0.0)):
            maps.append((q_m.astype(BF16), functools.partial(lambda sl: kb_s[:, sl], hs),
                         functools.partial(lambda hh: vb_s[hh], h)))

    def scores(i):
        return lax.dot_general(maps[i][1](), maps[i][0], nt, preferred_element_type=F32)

    def softmax_pv(i, s):
        e = jnp.exp2(s - jnp.max(s, axis=0, keepdims=True))
        pv = jnp.dot(maps[i][2](), e.astype(BF16), preferred_element_type=F32)
        return (pv[0:HEAD_W, :] * (1.0 / pv[HEAD_W:HEAD_W + 1, :])).T

    n_maps = len(maps)
    ahead = min(ATTN_AHEAD_CTX if not has_ctx else ATTN_AHEAD_LAT, n_maps)
    pending = {i: scores(i) for i in range(ahead)}
    outs = []
    for i in range(n_maps):
        if i + ahead < n_maps:
            pending[i + ahead] = scores(i + ahead)
        outs.append(softmax_pv(i, pending.pop(i)))

    for h in range(MLA_HEADS):
        hs = slice(h * HEAD_W, (h + 1) * HEAD_W)
        ycat_s[:, hs] = (outs[h] * za_ref[:, hs].astype(F32)).astype(BF16)

    lp = lam_ref[...]
    lf = l_ref[0].astype(F32)
    lam_init = 0.8 - 0.6 * jnp.exp(jnp.full((1, 1), -0.3, F32) * lf)
    lam = (jnp.exp(jnp.sum(lp[0:1, :] * lp[1:2, :], axis=-1, keepdims=True))
           - jnp.exp(jnp.sum(lp[2:3, :] * lp[3:4, :], axis=-1, keepdims=True)) + lam_init)
    for h in range(DIFF_HEADS):
        hs = slice(h * HEAD_W, (h + 1) * HEAD_W)
        o1, o2 = outs[MLA_HEADS + 2 * h], outs[MLA_HEADS + 2 * h + 1]
        o = _rms(o1 - lam * o2, gdn_ref[...]) * (1.0 - lam_init)
        ycat_s[:, GROUP_W + h * HEAD_W:GROUP_W + (h + 1) * HEAD_W] = (o * zb_ref[:, hs].astype(F32)).astype(BF16)

    ycat_s[:, 2 * GROUP_W:] = yc_ref[...]
    y = jnp.dot(ycat_s[...], wout_ref[...], preferred_element_type=F32)
    x_new = x_ref[...] + mod_ref[:, 2 * D_MODEL:] * y
    xo_ref[...] = x_new
    if is_last:
        yfin_ref[...] = _rms(x_new, gfin_ref[...])


def _attn_call(l_arr, main, tail, yc, x2, mod_all, w, T, has_ctx, is_last, rope=None, ctx=None, side_bufs=None):
    n_tok = main.shape[0]
    B = n_tok // T
    tq = ATTN_TQ
    nq = T // tq
    Tk = T + (PAST_LEN if has_ctx else 0)
    main3 = main.reshape(B, T, MAIN_W)
    tail3 = tail.reshape(B, T, TAIL_W)
    x3 = x2.reshape(B, T, D_MODEL)

    def tile(c, width):
        return pl.BlockSpec((None, tq, width), lambda b, q, l, c=c: (b, q, c))

    def full(c, width):
        return pl.BlockSpec((None, T, width), lambda b, q, l, c=c: (b, 0, c))

    def wspec(shape):
        return pl.BlockSpec((None,) + shape, lambda b, q, l: (l[0],) + (0,) * len(shape))

    if has_ctx:
        mod_map = lambda b, q, l: (l[0], 1 + b, 0, 0)
    else:
        mod_map = lambda b, q, l: (l[0], 0, 0, 0)

    merged = nq == 1
    if merged:
        assert (C_ZA, C_ZB) == (0, 4)
        in_specs = [pl.BlockSpec((None, T, TAIL_W), lambda b, q, l: (b, 0, 0)),
                    pl.BlockSpec((None, T, MAIN_W // 2), lambda b, q, l: (b, 0, 0))]
        args = [tail3, main3]
    else:
        in_specs = [tile(1, MLA_Q_RANK),
                    full(0, MLA_KV_RANK),
                    full(2, LANES),
                    tile(C_DQ, GROUP_W), full(C_DK, GROUP_W), full(C_DV, GROUP_W),
                    tile(C_ZB, GROUP_W), tile(C_ZA, GROUP_W)]
        args = [tail3, tail3, tail3, main3, main3, main3, main3, main3]
    in_specs += [pl.BlockSpec((None, tq, GROUP_W), lambda b, q, l: (b, q, 0)),
                 pl.BlockSpec((None, tq, D_MODEL), lambda b, q, l: (b, q, 0)),
                 pl.BlockSpec((None, None, 1, 3 * D_MODEL), mod_map),
                 wspec((MLA_Q_RANK, 2 * GROUP_W)), wspec((MLA_KV_RANK, 2 * GROUP_W)),
                 wspec((3 * GROUP_W, D_MODEL)),
                 wspec((1, MLA_Q_RANK)), wspec((1, MLA_KV_RANK)), wspec((1, HEAD_W)),
                 wspec((4, DIFF_D))]
    args += [yc, x3, mod_all, w['wuq'], w['wukv'], w['wout'], w['gq'], w['gkv'], w['gdn'], w['lam']]
    if is_last:
        in_specs.append(pl.BlockSpec((1, D_MODEL), lambda b, q, l: (0, 0)))
        args.append(w['gfin'])
    if has_ctx:
        cos_t, sin_t = rope
        in_specs += [pl.BlockSpec((tq, LANES), lambda b, q, l: (q, 0)),
                     pl.BlockSpec((tq, LANES), lambda b, q, l: (q, 0)),
                     pl.BlockSpec((T, LANES), lambda b, q, l: (0, 0)),
                     pl.BlockSpec((T, LANES), lambda b, q, l: (0, 0)),
                     pl.BlockSpec((None, None, PAST_LEN, MLA_KV_RANK), lambda b, q, l: (b, l[0], 0, 0)),
                     pl.BlockSpec((None, None, PAST_LEN, LANES), lambda b, q, l: (b, l[0], 0, 0)),
                     pl.BlockSpec((None, None, PAST_LEN * DIFF_HEADS, HEAD_W), lambda b, q, l: (b, l[0], 0, 0)),
                     pl.BlockSpec((None, None, PAST_LEN * DIFF_HEADS, HEAD_W), lambda b, q, l: (b, l[0], 0, 0))]
        args += [cos_t, sin_t, cos_t, sin_t, ctx['ckv'], ctx['krope'], ctx['dk'], ctx['dv']]
    out_shape = [jax.ShapeDtypeStruct((B, T, D_MODEL), F32)]
    out_specs = [pl.BlockSpec((None, tq, D_MODEL), lambda b, q, l: (b, q, 0))]
    if is_last:
        out_shape.append(jax.ShapeDtypeStruct((B, T, D_MODEL), F32))
        out_specs.append(pl.BlockSpec((None, tq, D_MODEL), lambda b, q, l: (b, q, 0)))
    aliases = {}
    n_alias = 0
    if not has_ctx:
        if side_bufs is not None:
            n_alias = len(side_bufs)
            n_in = 1 + len(args)
            aliases = {n_in + i: len(out_shape) + i for i in range(n_alias)}
            in_specs += [pl.BlockSpec(memory_space=pl.ANY)] * n_alias
            args += list(side_bufs)
        out_shape += [jax.ShapeDtypeStruct((B, DEPTH, T, MLA_KV_RANK), F32),
                      jax.ShapeDtypeStruct((B, DEPTH, T, MLA_ROPE), F32)]
        out_specs += [pl.BlockSpec((None, None, T, MLA_KV_RANK), lambda b, q, l: (b, l[0], 0, 0)),
                      pl.BlockSpec((None, None, T, MLA_ROPE), lambda b, q, l: (b, l[0], 0, 0))]
    grid_spec = pltpu.PrefetchScalarGridSpec(
        num_scalar_prefetch=1, grid=(B, nq), in_specs=in_specs, out_specs=out_specs,
        scratch_shapes=[pltpu.VMEM((Tk, 2 * GROUP_W), BF16),
                        pltpu.VMEM((MLA_HEADS, VT_ROWS, Tk), BF16),
                        pltpu.VMEM((Tk, GROUP_W), BF16),
                        pltpu.VMEM((DIFF_HEADS, VT_ROWS, Tk), BF16),
                        pltpu.VMEM((tq, 3 * GROUP_W), BF16)])
    return pl.pallas_call(
        functools.partial(_attn_kernel, has_ctx, is_last, merged, n_alias, T),
        grid_spec=grid_spec,
        out_shape=out_shape,
        input_output_aliases=aliases,
        compiler_params=pltpu.CompilerParams(dimension_semantics=("arbitrary", "arbitrary"),
                                             vmem_limit_bytes=VMEM_LIMIT),
        name="attn_lat" if has_ctx else "attn_ctx",
    )(l_arr, *args)


def _rope_tables(n_tok):
    n_freq = MLA_ROPE // 4
    inv = ROPE_THETA ** (-jnp.arange(n_freq, dtype=F32) / n_freq)
    n_rows = n_tok // GRID_W
    rowp = jnp.repeat(jnp.arange(n_rows, dtype=F32), GRID_W)
    colp = jnp.tile(jnp.arange(GRID_W, dtype=F32), n_rows)
    ang = jnp.concatenate([rowp[:, None] * inv, colp[:, None] * inv], axis=-1)
    cos, sin = jnp.cos(ang), jnp.sin(ang)
    cos64 = jnp.concatenate([cos, cos], axis=-1)
    sin64 = jnp.concatenate([-sin, sin], axis=-1)
    return jnp.concatenate([cos64, cos64], axis=-1), jnp.concatenate([sin64, sin64], axis=-1)


def _prep_weights(W_in, W_uq, W_ukv, W_out):
    w_in_t = jnp.swapaxes(W_in, 1, 2).astype(BF16)
    wq = W_uq.reshape(DEPTH, MLA_Q_RANK, MLA_HEADS, MLA_NOPE + MLA_ROPE)
    wq = jnp.pad(wq, ((0, 0), (0, 0), (0, 0), (0, 2 * HEAD_W - MLA_NOPE - MLA_ROPE)))
    wuq_r = wq.reshape(DEPTH, MLA_Q_RANK, MLA_HEADS * 2 * HEAD_W).astype(BF16)
    wkv = W_ukv.reshape(DEPTH, MLA_KV_RANK, MLA_HEADS, 2, HEAD_W)
    wukv_r = jnp.swapaxes(wkv, 2, 3).reshape(DEPTH, MLA_KV_RANK, 2 * GROUP_W).astype(BF16)
    return w_in_t, wuq_r, wukv_r, W_out.astype(BF16)


def kernel(x_prompt, x_sample, cache_mla_ckv, cache_mla_krope, cache_diff_k, cache_diff_v, state_mlstm_C, state_mlstm_n, state_mlstm_m, c, c_ctx, g_norm, W_mod, b_mod, W_in, mla_q_norm, W_uq, mla_kv_norm, W_ukv, diff_lambda, diff_norm, ml_conv, ml_gate_b, ml_norm, W_out, g_final):
    Bc, Tc, _ = x_prompt.shape
    Bs, Ts, _ = x_sample.shape

    w_in_t, wuq_r, wukv_r, wout_r = _prep_weights(W_in, W_uq, W_ukv, W_out)
    w = {'wuq': wuq_r, 'wukv': wukv_r, 'wout': wout_r,
         'gq': mla_q_norm.reshape(DEPTH, 1, MLA_Q_RANK), 'gkv': mla_kv_norm.reshape(DEPTH, 1, MLA_KV_RANK),
         'gdn': diff_norm.reshape(DEPTH, 1, 2 * DIFF_D), 'lam': diff_lambda,
         'gfin': g_final.reshape(1, D_MODEL)}
    g_norm3 = g_norm.reshape(DEPTH, 1, D_MODEL)
    ml_norm2 = ml_norm.reshape(DEPTH, 1, GROUP_W)
    gate_row = jnp.pad(ml_gate_b.reshape(DEPTH, 1, N_ML_GATES),
                       ((0, 0), (0, 0), (GATE_LANE0, LANES - GATE_LANE0 - N_ML_GATES)))

    cc = jnp.concatenate([c_ctx[None, :], c, jnp.zeros((8 - 1 - Bs, D_MODEL), F32)], axis=0)
    mod_all = _mod_call(cc, W_mod, b_mod).reshape(DEPTH, 8, 1, 3 * D_MODEL)

    dkv = [jnp.zeros((Bc, DEPTH, Tc * DIFF_HEADS, HEAD_W), F32)] * 2
    cnm = [jnp.zeros((Bc, DEPTH, 2 * ML_HEADS, ML_DK, HEAD_W), F32),
           jnp.zeros((Bc, DEPTH, 2 * ML_HEADS, ML_DK), F32), jnp.zeros((Bc, DEPTH, 2 * ML_HEADS, LANES), F32)]
    ckr = [jnp.zeros((Bc, DEPTH, Tc, MLA_KV_RANK), F32), jnp.zeros((Bc, DEPTH, Tc, MLA_ROPE), F32)]
    x2 = x_prompt.reshape(Bc * Tc, D_MODEL)
    y_prompt = None
    for l in range(DEPTH):
        l_arr = jnp.full((1,), l, jnp.int32)
        main, tail, *dkv = _proj_call(l_arr, x2, mod_all, g_norm3, w_in_t, Tc, False, dkv)
        yc, *cnm = _mlstm_call(l_arr, main, tail, ml_conv, gate_row, ml_norm2, Tc, False, side_bufs=cnm)
        outs = _attn_call(l_arr, main, tail, yc, x2, mod_all, w, Tc, False, l == DEPTH - 1, side_bufs=ckr)
        ckr = outs[-2:]
        dk_o, dv_o = dkv
        c_o, n_o, m_o = cnm
        if l == DEPTH - 1:
            x3, y_prompt, ckvn, kro = outs
        else:
            x3, ckvn, kro = outs
        x2 = x3.reshape(Bc * Tc, D_MODEL)
    side_outs = (ckvn, kro,
                 dk_o.reshape(Bc, DEPTH, Tc, DIFF_HEADS, 2 * DIFF_D),
                 dv_o.reshape(Bc, DEPTH, Tc, DIFF_HEADS, 2 * DIFF_D),
                 c_o.reshape(Bc, DEPTH, 2, ML_HEADS, ML_DK, HEAD_W),
                 n_o.reshape(Bc, DEPTH, 2, ML_HEADS, ML_DK),
                 m_o[:, :, :, 0].reshape(Bc, DEPTH, 2, ML_HEADS))

    rope = _rope_tables(Ts)
    ctx = {'ckv': cache_mla_ckv,
           'krope': jnp.pad(cache_mla_krope, ((0, 0), (0, 0), (0, 0), (0, LANES - MLA_ROPE))),
           'dk': cache_diff_k.reshape(Bs, DEPTH, PAST_LEN * DIFF_HEADS, HEAD_W),
           'dv': cache_diff_v.reshape(Bs, DEPTH, PAST_LEN * DIFF_HEADS, HEAD_W)}
    ctx_state = (state_mlstm_C.reshape(Bs, DEPTH, 2 * ML_HEADS, ML_DK, HEAD_W),
                 state_mlstm_n.reshape(Bs, DEPTH, 2 * ML_HEADS, ML_DK),
                 jnp.broadcast_to(state_mlstm_m.reshape(Bs, DEPTH, 2 * ML_HEADS, 1),
                                  (Bs, DEPTH, 2 * ML_HEADS, LANES)))
    x2 = x_sample.reshape(Bs * Ts, D_MODEL)
    y_sample = None
    for l in range(DEPTH):
        l_arr = jnp.full((1,), l, jnp.int32)
        main, tail = _proj_call(l_arr, x2, mod_all, g_norm3, w_in_t, Ts, True)
        (yc,) = _mlstm_call(l_arr, main, tail, ml_conv, gate_row, ml_norm2, Ts, True, ctx_state)
        outs = _attn_call(l_arr, main, tail, yc, x2, mod_all, w, Ts, True, l == DEPTH - 1, rope, ctx)
        if l == DEPTH - 1:
            x3, y_sample = outs
        else:
            (x3,) = outs
        x2 = x3.reshape(Bs * Ts, D_MODEL)

    return (y_prompt, y_sample, *side_outs)
```

```python
import functools
import math

import jax
import jax.numpy as jnp
import numpy as np
from jax import lax
from jax.experimental import pallas as pl
from jax.experimental.pallas import tpu as pltpu

F32 = jnp.float32
BF16 = jnp.bfloat16

D_MODEL = 1024
DEPTH = 4
PAST_LEN = 256
GRID_W = 64
GROUP_W = 512
MLA_HEADS = 4
MLA_NOPE = 128
MLA_ROPE = 64
MLA_Q_RANK = 384
MLA_KV_RANK = 256
DIFF_HEADS = 4
DIFF_D = 64
ML_HEADS = 4
ML_DK = 128
N_ML_GATES = 16
ROPE_THETA = 10000.0
NORM_EPS = 1e-6
MLA_SCALE = (MLA_NOPE + MLA_ROPE) ** -0.5
DIFF_SCALE = DIFF_D ** -0.5
LOG2E = math.log2(math.e)

LANES = 128
HEAD_W = 128
ML_CHUNK = 256
VT_ROWS = 144
PROJ_TM = 512
ATTN_TQ = 256
ATTN_PVT_MAX_KEYS = 256
ATTN_AHEAD_CTX = 12
ATTN_AHEAD_LAT = 4
MAIN_W = 10 * GROUP_W
TAIL_W = 768
N_IN = 5840
VMEM_LIMIT = 56 * 1024 * 1024

C_ZA, C_DQ, C_DK, C_DV, C_ZB, C_MQ, C_MK, C_MV, C_MO, C_ZC = range(10)

_IN_SIZES = (MLA_Q_RANK, MLA_KV_RANK, MLA_ROPE, GROUP_W, GROUP_W, GROUP_W, GROUP_W, GROUP_W,
             GROUP_W, GROUP_W, GROUP_W, GROUP_W, GROUP_W, N_ML_GATES)
_IN_OFF = np.concatenate([[0], np.cumsum(_IN_SIZES)])
(_O_CQ, _O_CKV, _O_KR, _O_ZA, _O_DQ, _O_DK, _O_DV, _O_ZB, _O_MQ, _O_MK, _O_MV, _O_MO, _O_ZC,
 _O_MG) = [int(v) for v in _IN_OFF[:-1]]
GATE_LANE0 = MLA_ROPE


def _rms(x, g):
    ms = jnp.mean(x * x, axis=-1, keepdims=True)
    return x * lax.rsqrt(ms + NORM_EPS) * g


def _sigmoid(x):
    return 0.5 + 0.5 * jnp.tanh(0.5 * x)


def _silu(x):
    hx = 0.5 * x
    return hx + hx * jnp.tanh(hx)


def _log_sigmoid(x):
    return jnp.minimum(x, 0.0) - jnp.log1p(jnp.exp(-jnp.abs(x)))


def _swap32(x):
    lane = lax.broadcasted_iota(jnp.int32, x.shape, 1)
    fwd = pltpu.roll(x, LANES - 32, 1)
    bwd = pltpu.roll(x, 32, 1)
    return jnp.where((lane % 64) < 32, fwd, bwd)


def _rope_tile(x, cos, sin):
    return x * cos + _swap32(x) * sin


def _mod_kernel(c_ref, w_ref, b_ref, o_ref):
    a = _silu(c_ref[...]).astype(BF16)
    o_ref[...] = jnp.dot(a, w_ref[...].astype(BF16), preferred_element_type=F32) + b_ref[...]


def _mod_call(cc, W_mod, b_mod):
    tn = 1024
    return pl.pallas_call(
        _mod_kernel,
        grid=(DEPTH, 3 * D_MODEL // tn),
        in_specs=[pl.BlockSpec((8, D_MODEL), lambda l, j: (0, 0)),
                  pl.BlockSpec((None, D_MODEL, tn), lambda l, j: (l, 0, j)),
                  pl.BlockSpec((None, 1, tn), lambda l, j: (l, 0, j))],
        out_specs=pl.BlockSpec((None, 8, tn), lambda l, j: (l, 0, j)),
        out_shape=jax.ShapeDtypeStruct((DEPTH, 8, 3 * D_MODEL), F32),
        compiler_params=pltpu.CompilerParams(dimension_semantics=("arbitrary", "arbitrary"),
                                             vmem_limit_bytes=VMEM_LIMIT),
        name="mod",
    )(cc, W_mod, b_mod.reshape(DEPTH, 1, 3 * D_MODEL))


def _proj_kernel(has_side, n_alias, T, *refs):
    l_ref, x_ref, mod_ref, g_ref, w_ref = refs[:5]
    refs = refs[5 + n_alias:]
    if has_side:
        main_ref, tail_ref, dk_ref, dv_ref, wt_s = refs
    else:
        main_ref, tail_ref, wt_s = refs
    del l_ref
    if has_side and n_alias == 0:
        for r in (dk_ref, dv_ref):
            r[:, 1:] = jnp.zeros((r.shape[0], DEPTH - 1) + r.shape[2:], F32)
        dk_ref, dv_ref = dk_ref.at[:, 0], dv_ref.at[:, 0]
    tm = x_ref.shape[0]
    nt = (((1,), (1,)), ((), ()))

    @pl.when(pl.program_id(0) == 0)
    def _gather_tail_rows():
        wt_s[0:MLA_ROPE, :] = w_ref[_O_KR:_O_KR + MLA_ROPE, :]
        wt_s[MLA_ROPE:MLA_ROPE + N_ML_GATES, :] = w_ref[_O_MG:_O_MG + N_ML_GATES, :]
        wt_s[MLA_ROPE + N_ML_GATES:LANES, :] = jnp.zeros((LANES - MLA_ROPE - N_ML_GATES, D_MODEL), BF16)
        wt_s[LANES:, :] = w_ref[_O_CQ:_O_CQ + MLA_Q_RANK, :]

    mod = mod_ref[...]
    y = _rms(x_ref[...], g_ref[...])
    h = (y * (1.0 + mod[:, D_MODEL:2 * D_MODEL]) + mod[:, :D_MODEL]).astype(BF16)
    for s in range(MAIN_W // GROUP_W):
        cols = slice(s * GROUP_W, (s + 1) * GROUP_W)
        acc = lax.dot_general(h, w_ref[_O_ZA + s * GROUP_W:_O_ZA + (s + 1) * GROUP_W, :], nt,
                              preferred_element_type=F32)
        if s in (C_ZA, C_ZB, C_ZC):
            main_ref[:, cols] = _silu(acc).astype(BF16)
        elif s == C_MO:
            main_ref[:, cols] = _sigmoid(acc).astype(BF16)
        else:
            main_ref[:, cols] = acc.astype(BF16)
        if has_side and s in (C_DK, C_DV):
            side_ref = dk_ref if s == C_DK else dv_ref
            for b in range(tm // T):
                for hd in range(DIFF_HEADS):
                    side_ref[b, pl.ds(hd, T, stride=DIFF_HEADS), :] = (
                        acc[b * T:(b + 1) * T, hd * HEAD_W:(hd + 1) * HEAD_W])
    tail_ref[:, 0:MLA_KV_RANK] = lax.dot_general(h, w_ref[_O_CKV:_O_CKV + MLA_KV_RANK, :], nt,
                                                 preferred_element_type=F32)
    tail_ref[:, MLA_KV_RANK:] = lax.dot_general(h, wt_s[...], nt, preferred_element_type=F32)


def _proj_call(l_arr, x2, mod_all, g_norm3, w_in_t, T, has_ctx, side_bufs=None):
    n_tok = x2.shape[0]
    tm = PROJ_TM
    has_side = not has_ctx
    if has_ctx:
        mod_map = lambda i, l: (l[0], 1 + (i * tm) // T, 0, 0)
    else:
        mod_map = lambda i, l: (l[0], 0, 0, 0)
    in_specs = [pl.BlockSpec((tm, D_MODEL), lambda i, l: (i, 0)),
                pl.BlockSpec((None, None, 1, 3 * D_MODEL), mod_map),
                pl.BlockSpec((None, 1, D_MODEL), lambda i, l: (l[0], 0, 0)),
                pl.BlockSpec((None, N_IN, D_MODEL), lambda i, l: (l[0], 0, 0), pipeline_mode=pl.Buffered(1))]
    args = [x2, mod_all, g_norm3, w_in_t]
    out_shape = [jax.ShapeDtypeStruct((n_tok, MAIN_W), BF16), jax.ShapeDtypeStruct((n_tok, TAIL_W), F32)]
    out_specs = [pl.BlockSpec((tm, MAIN_W), lambda i, l: (i, 0)), pl.BlockSpec((tm, TAIL_W), lambda i, l: (i, 0))]
    aliases = {}
    n_alias = 0
    if has_side:
        bt = tm // T
        out_shape += [jax.ShapeDtypeStruct((n_tok // T, DEPTH, T * DIFF_HEADS, HEAD_W), F32)] * 2
        if side_bufs is not None:
            n_alias = len(side_bufs)
            aliases = {1 + len(args) + i: 2 + i for i in range(n_alias)}
            in_specs += [pl.BlockSpec(memory_space=pl.ANY)] * n_alias
            args += list(side_bufs)
            out_specs += [pl.BlockSpec((bt, None, T * DIFF_HEADS, HEAD_W), lambda i, l: (i, l[0], 0, 0))] * 2
        else:
            out_specs += [pl.BlockSpec((bt, DEPTH, T * DIFF_HEADS, HEAD_W), lambda i, l: (i, 0, 0, 0))] * 2
    grid_spec = pltpu.PrefetchScalarGridSpec(
        num_scalar_prefetch=1, grid=(n_tok // tm,), in_specs=in_specs, out_specs=out_specs,
        scratch_shapes=[pltpu.VMEM((TAIL_W - MLA_KV_RANK, D_MODEL), BF16)])
    return pl.pallas_call(
        functools.partial(_proj_kernel, has_side, n_alias, T),
        grid_spec=grid_spec,
        out_shape=out_shape,
        input_output_aliases=aliases,
        compiler_params=pltpu.CompilerParams(dimension_semantics=("arbitrary",),
                                             vmem_limit_bytes=VMEM_LIMIT),
        name="proj_lat" if has_ctx else "proj_ctx",
    )(l_arr, *args)


def _mlstm_kernel(has_ctx, T, n_alias, *refs):
    nc = T // ML_CHUNK
    L = ML_CHUNK
    use_inter = has_ctx or nc > 1
    it = iter(refs)
    l_ref = next(it)
    ml_ref, aux_ref, convw_ref, gb_ref, mln_ref = [next(it) for _ in range(5)]
    mq_ref, mk_ref, mv_ref, mo_ref, zc_ref = [ml_ref.at[:, i * GROUP_W:(i + 1) * GROUP_W] for i in range(5)]
    if has_ctx:
        c0_ref, n0_ref, m0_ref = [next(it) for _ in range(3)]
    else:
        for _ in range(n_alias):
            next(it)
    yc_ref = next(it)
    if not has_ctx:
        cout_ref, nout_ref, mout_ref = [next(it) for _ in range(3)]
    q_s, k_s, vt_s, hf_s, hb_s, caug_s, m_s, gt_s = [next(it) for _ in range(8)]
    del l_ref
    if not has_ctx and n_alias == 0:
        for r in (cout_ref, nout_ref, mout_ref):
            r[1:] = jnp.zeros((DEPTH - 1,) + r.shape[1:], F32)
        cout_ref, nout_ref, mout_ref = cout_ref.at[0], nout_ref.at[0], mout_ref.at[0]

    row_b = lax.broadcasted_iota(jnp.int32, (L, LANES), 0)
    convw = convw_ref[...]
    halo = 16
    for c in range(nc):
        cs = slice(c * L, (c + 1) * L)
        for j in range(2 * GROUP_W // LANES):
            src = mq_ref if j < GROUP_W // LANES else mk_ref
            ls = slice((j % (GROUP_W // LANES)) * LANES, (j % (GROUP_W // LANES) + 1) * LANES)
            u = src[cs, ls].astype(F32)
            prev = src[c * L - halo:c * L, ls].astype(F32)[halo - 1:halo, :] if c > 0 else 0.0
            nxt = src[(c + 1) * L:(c + 1) * L + halo, ls].astype(F32)[0:1, :] if c < nc - 1 else 0.0
            up, un = pltpu.roll(u, 1, 0), pltpu.roll(u, L - 1, 0)
            up = jnp.concatenate([jnp.where(row_b[0:8] == 0, prev, up[0:8]), up[8:]], axis=0)
            un = jnp.concatenate([un[:L - 8], jnp.where(row_b[L - 8:] == L - 1, nxt, un[L - 8:])], axis=0)
            w3 = convw[:, j * LANES:(j + 1) * LANES]
            y = _silu(w3[0:1, :] * up + w3[1:2, :] * u + w3[2:3, :] * un)
            if j < GROUP_W // LANES:
                q_s[cs, ls] = (y * (ML_DK ** -0.5)).astype(BF16)
            else:
                k_s[cs, ls] = y
    row16 = lax.broadcasted_iota(jnp.int32, (VT_ROWS - HEAD_W, L), 0)
    ones_rows = jnp.where(row16 == 0, 1.0, 0.0).astype(BF16)
    g_t = (aux_ref[...] + gb_ref[...]).T
    for c in range(nc):
        cs = slice(c * L, (c + 1) * L)
        gt_s[c] = g_t[GATE_LANE0:GATE_LANE0 + N_ML_GATES, cs]
        for h in range(ML_HEADS):
            vt_s[c, h, 0:HEAD_W, :] = mv_ref[cs, h * HEAD_W:(h + 1) * HEAD_W].astype(F32).T.astype(BF16)
            vt_s[c, h, HEAD_W:, :] = ones_rows

    reps = L // LANES
    if has_ctx:
        n0 = n0_ref[...]
        m0 = m0_ref[...]
        m_s[0] = jnp.concatenate([m0] * reps, axis=1)
        m_s[1] = jnp.concatenate([pltpu.roll(m0, ML_HEADS, 0)] * reps, axis=1)
        row_n = lax.broadcasted_iota(jnp.int32, (VT_ROWS - HEAD_W, ML_DK), 0)
        for r in range(2 * ML_HEADS):
            caug_s[r, 0:HEAD_W, :] = c0_ref[r].T
            caug_s[r, HEAD_W:, :] = jnp.where(row_n == 0, n0[r:r + 1, :], 0.0)
    else:
        caug_s[...] = jnp.zeros_like(caug_s)
        m_s[...] = jnp.zeros_like(m_s)

    ri = lax.broadcasted_iota(jnp.int32, (L, L), 0)
    ci = lax.broadcasted_iota(jnp.int32, (L, L), 1)
    lane_r = lax.broadcasted_iota(jnp.int32, (2 * ML_HEADS, L), 1)
    nt = (((1,), (1,)), ((), ()))

    def split3(x):
        hi = x.astype(BF16).astype(F32)
        mid = (x - hi).astype(BF16).astype(F32)
        return hi, mid, (x - hi - mid).astype(BF16).astype(F32)

    masks = ((ri <= ci), (ri >= ci))

    def gate_rows(d, c):
        fwd = d == 0
        tri_t = jnp.where(masks[d], 1.0, 0.0).astype(BF16)
        last = L - 1 if fwd else 0

        g8 = gt_s[c, d * 2 * ML_HEADS:(d + 1) * 2 * ML_HEADS, :]
        hi8, mid8, lo8 = split3(_log_sigmoid(g8))
        stack = jnp.concatenate([hi8, mid8, lo8, jnp.zeros_like(hi8)], axis=0).astype(BF16)
        part8 = jnp.dot(stack, tri_t, preferred_element_type=F32)
        bc8 = part8[0:8] + part8[8:16] + part8[16:24]
        bcs = pltpu.roll(bc8, ML_HEADS, 0)
        a8 = g8 - bcs
        cm8 = a8
        k = 1
        while k < L:
            if fwd:
                cm8 = jnp.maximum(cm8, jnp.where(lane_r >= k, pltpu.roll(cm8, k, 1), -jnp.inf))
            else:
                cm8 = jnp.maximum(cm8, jnp.where(lane_r < L - k, pltpu.roll(cm8, L - k, 1), -jnp.inf))
            k *= 2
        m8 = m_s[d]
        g_row = bcs + m8
        m_t = jnp.maximum(g_row, bcs + cm8)
        w_inter = jnp.exp(g_row - m_t)
        e_inv = jnp.exp(-m_t)
        c_row = bcs - m_t
        b_last = bcs[:, last:last + 1]
        m_new = m_t[:, last:last + 1]
        a_prev = jnp.exp(b_last + m8[:, 0:1] - m_new)
        w_s = jnp.exp(a8 + (b_last - m_new))
        m_s[d] = jnp.broadcast_to(m_new, (2 * ML_HEADS, L))
        a_n = jnp.concatenate([a8, jnp.zeros((LANES - 2 * ML_HEADS, L), F32)], axis=0).T
        return a_n, c_row, w_inter, e_inv, w_s, a_prev

    def do_pair(c_f, c_b):
        stats = (gate_rows(0, c_f), gate_rows(1, c_b))
        combos = [(d, h) for d in range(2) for h in range(ML_HEADS)]
        rows = []
        for c in (c_f, c_b):
            rows.append(slice(c * L, (c + 1) * L) if isinstance(c, int) else pl.ds(pl.multiple_of(c * L, L), L))
        chunk = (c_f, c_b)
        hsl = [slice(h * HEAD_W, (h + 1) * HEAD_W) for h in range(ML_HEADS)]
        qcs = [q_s[rows[d], hsl[h]] for d, h in combos]
        kcs = [k_s[rows[d], hsl[h]].astype(BF16) for d, h in combos]
        vts = [vt_s[chunk[d], h] for d, h in combos]
        n = range(len(combos))
        s_ts = [lax.dot_general(kcs[i], qcs[i], nt, preferred_element_type=F32) for i in n]
        sws = []
        for i, (d, h) in enumerate(combos):
            a_n, c_row = stats[d][0], stats[d][1]
            w_t = jnp.exp(jnp.where(masks[d], a_n[:, h:h + 1] + c_row[h:h + 1, :], -jnp.inf))
            sws.append((s_ts[i] * w_t).astype(BF16))
        nds = [jnp.dot(vts[i], sws[i], preferred_element_type=F32) for i in n]
        if use_inter:
            inters = [lax.dot_general(caug_s[d * ML_HEADS + h].astype(BF16), qcs[i], nt,
                                      preferred_element_type=F32) for i, (d, h) in enumerate(combos)]
            nds = [nds[i] + stats[d][2][h:h + 1, :] * inters[i] for i, (d, h) in enumerate(combos)]
        for i, (d, h) in enumerate(combos):
            inv = 1.0 / jnp.maximum(jnp.abs(nds[i][HEAD_W:HEAD_W + 1, :]), stats[d][3][h:h + 1, :])
            h_s = hf_s if d == 0 else hb_s
            h_s[rows[d], hsl[h]] = (nds[i][0:HEAD_W, :] * inv).T
        upds = [jnp.dot((vts[i].astype(F32) * stats[d][4][h:h + 1, :]).astype(BF16), kcs[i],
                        preferred_element_type=F32) for i, (d, h) in enumerate(combos)]
        for i, (d, h) in enumerate(combos):
            r = d * ML_HEADS + h
            caug_s[r] = upds[i] + stats[d][5][h:h + 1, :] * caug_s[r] if use_inter else upds[i]

    if nc == 1:
        do_pair(0, 0)
    else:
        def body(i, carry):
            do_pair(i, nc - 1 - i)
            return carry
        lax.fori_loop(0, nc, body, 0)

    mln = mln_ref[...]
    for c in range(nc):
        cs = slice(c * L, (c + 1) * L)
        for h in range(ML_HEADS):
            hs = slice(h * HEAD_W, (h + 1) * HEAD_W)
            hc = mo_ref[cs, hs].astype(F32) * (hf_s[cs, hs] + hb_s[cs, hs])
            yc_ref[cs, hs] = (_rms(hc, mln[:, hs]) * zc_ref[cs, hs].astype(F32)).astype(BF16)

    if not has_ctx:
        for r in range(2 * ML_HEADS):
            cout_ref[r] = caug_s[r, 0:HEAD_W, :].T
            nout_ref[r:r + 1, :] = caug_s[r, HEAD_W:HEAD_W + 1, :]
        for d in range(2):
            mout_ref[d * ML_HEADS:(d + 1) * ML_HEADS, :] = m_s[d, 0:ML_HEADS, 0:LANES]


def _mlstm_call(l_arr, main, tail, ml_conv, gate_row, ml_norm2, T, has_ctx, ctx_state=None, side_bufs=None):
    n_tok = main.shape[0]
    B = n_tok // T
    main3 = main.reshape(B, T, MAIN_W)
    tail3 = tail.reshape(B, T, TAIL_W)

    assert (C_MQ, C_ZC) == (5, 9)
    in_specs = [pl.BlockSpec((None, T, MAIN_W // 2), lambda b, l: (b, 0, 1)),
                pl.BlockSpec((None, T, LANES), lambda b, l: (b, 0, 2)),
                pl.BlockSpec((None, 3, 2 * GROUP_W), lambda b, l: (l[0], 0, 0)),
                pl.BlockSpec((None, 1, LANES), lambda b, l: (l[0], 0, 0)),
                pl.BlockSpec((None, 1, GROUP_W), lambda b, l: (l[0], 0, 0))]
    args = [main3, tail3, ml_conv, gate_row, ml_norm2]
    n_alias = 0 if (has_ctx or side_bufs is None) else len(side_bufs)
    out_shape = [jax.ShapeDtypeStruct((B, T, GROUP_W), BF16)]
    out_specs = [pl.BlockSpec((None, T, GROUP_W), lambda b, l: (b, 0, 0))]
    if has_ctx:
        c0, n0, m0 = ctx_state
        in_specs += [pl.BlockSpec((None, None, 2 * ML_HEADS, ML_DK, HEAD_W), lambda b, l: (b, l[0], 0, 0, 0)),
                     pl.BlockSpec((None, None, 2 * ML_HEADS, ML_DK), lambda b, l: (b, l[0], 0, 0)),
                     pl.BlockSpec((None, None, 2 * ML_HEADS, LANES), lambda b, l: (b, l[0], 0, 0))]
        args += [c0, n0, m0]
        aliases = {}
    else:
        aliases = {}
        out_shape += [jax.ShapeDtypeStruct((B, DEPTH, 2 * ML_HEADS, ML_DK, HEAD_W), F32),
                      jax.ShapeDtypeStruct((B, DEPTH, 2 * ML_HEADS, ML_DK), F32),
                      jax.ShapeDtypeStruct((B, DEPTH, 2 * ML_HEADS, LANES), F32)]
        if n_alias:
            aliases = {1 + len(args) + i: 1 + i for i in range(n_alias)}
            in_specs += [pl.BlockSpec(memory_space=pl.ANY)] * n_alias
            args += list(side_bufs)
            out_specs += [pl.BlockSpec((None, None, 2 * ML_HEADS, ML_DK, HEAD_W), lambda b, l: (b, l[0], 0, 0, 0)),
                          pl.BlockSpec((None, None, 2 * ML_HEADS, ML_DK), lambda b, l: (b, l[0], 0, 0)),
                          pl.BlockSpec((None, None, 2 * ML_HEADS, LANES), lambda b, l: (b, l[0], 0, 0))]
        else:
            out_specs += [pl.BlockSpec((None, DEPTH, 2 * ML_HEADS, ML_DK, HEAD_W), lambda b, l: (b, 0, 0, 0, 0)),
                          pl.BlockSpec((None, DEPTH, 2 * ML_HEADS, ML_DK), lambda b, l: (b, 0, 0, 0)),
                          pl.BlockSpec((None, DEPTH, 2 * ML_HEADS, LANES), lambda b, l: (b, 0, 0, 0))]
    grid_spec = pltpu.PrefetchScalarGridSpec(
        num_scalar_prefetch=1, grid=(B,), in_specs=in_specs, out_specs=out_specs,
        scratch_shapes=[pltpu.VMEM((T, GROUP_W), BF16),
                        pltpu.VMEM((T, GROUP_W), F32),
                        pltpu.VMEM((T // ML_CHUNK, ML_HEADS, VT_ROWS, ML_CHUNK), BF16),
                        pltpu.VMEM((T, GROUP_W), F32),
                        pltpu.VMEM((T, GROUP_W), F32),
                        pltpu.VMEM((2 * ML_HEADS, VT_ROWS, ML_DK), F32),
                        pltpu.VMEM((2, 2 * ML_HEADS, ML_CHUNK), F32),
                        pltpu.VMEM((T // ML_CHUNK, N_ML_GATES, ML_CHUNK), F32)])
    return pl.pallas_call(
        functools.partial(_mlstm_kernel, has_ctx, T, n_alias),
        grid_spec=grid_spec,
        out_shape=out_shape,
        input_output_aliases=aliases,
        compiler_params=pltpu.CompilerParams(dimension_semantics=("arbitrary",),
                                             vmem_limit_bytes=VMEM_LIMIT),
        name="mlstm_lat" if has_ctx else "mlstm_ctx",
    )(l_arr, *args)


def _attn_kernel(has_ctx, is_last, merged, n_alias, T, *refs):
    Tk = T + (PAST_LEN if has_ctx else 0)
    pv_t = Tk <= ATTN_PVT_MAX_KEYS
    it = iter(refs)
    l_ref = next(it)
    if merged:
        tail_ref, att_ref = next(it), next(it)
        ckv_ref = tail_ref.at[:, 0:MLA_KV_RANK]
        aux_ref = tail_ref.at[:, MLA_KV_RANK:MLA_KV_RANK + LANES]
        cq_ref = tail_ref.at[:, MLA_KV_RANK + LANES:]
        za_ref, dq_ref, dk_ref, dv_ref, zb_ref = [att_ref.at[:, i * GROUP_W:(i + 1) * GROUP_W] for i in range(5)]
    else:
        cq_ref, ckv_ref, aux_ref, dq_ref, dk_ref, dv_ref, zb_ref, za_ref = [next(it) for _ in range(8)]
    (yc_ref, x_ref, mod_ref, wuq_ref, wukv_ref, wout_ref, gq_ref, gkv_ref, gdn_ref,
     lam_ref) = [next(it) for _ in range(10)]
    if is_last:
        gfin_ref = next(it)
    if has_ctx:
        cosq_ref, sinq_ref, cosk_ref, sink_ref, cckv_ref, ckr_ref, cdk_ref, cdv_ref = [next(it) for _ in range(8)]
    else:
        for _ in range(n_alias):
            next(it)
    xo_ref = next(it)
    if is_last:
        yfin_ref = next(it)
    if not has_ctx:
        ckvn_ref, kro_ref = [next(it) for _ in range(2)]
    ka_s, va_s, kb_s, vb_s, ycat_s = [next(it) for _ in range(5)]
    first = not has_ctx and n_alias == 0
    if first:
        ckvn_all, kro_all = ckvn_ref, kro_ref
        ckvn_ref, kro_ref = ckvn_ref.at[0], kro_ref.at[0]

    qi = pl.program_id(1)

    @pl.when(qi == 0)
    def _build_keys():
        wukv = wukv_ref[...]
        ckv_n = _rms(ckv_ref[...], gkv_ref[...])
        aux = aux_ref[...]
        if not has_ctx:
            ckvn_ref[...] = ckv_n
            kro_ref[...] = aux[:, :MLA_ROPE]
        if first:
            for r in (ckvn_all, kro_all):
                r[1:] = jnp.zeros((DEPTH - 1,) + r.shape[1:], F32)
        kv = jnp.dot(ckv_n.astype(BF16), wukv, preferred_element_type=F32)
        lane = lax.broadcasted_iota(jnp.int32, aux.shape, 1)
        kr = _rope_tile(aux, cosk_ref[...], sink_ref[...]) if has_ctx else aux
        kr = jnp.where(lane < MLA_ROPE, kr, 0.0).astype(BF16)
        for h in range(MLA_HEADS):
            ka_s[0:T, 2 * h * HEAD_W:(2 * h + 1) * HEAD_W] = kv[:, h * HEAD_W:(h + 1) * HEAD_W].astype(BF16)
            ka_s[0:T, (2 * h + 1) * HEAD_W:(2 * h + 2) * HEAD_W] = kr
        if pv_t:
            row16 = lax.broadcasted_iota(jnp.int32, (VT_ROWS - HEAD_W, Tk), 0)
            ones_rows = jnp.where(row16 == 0, 1.0, 0.0).astype(BF16)
            for h in range(MLA_HEADS):
                hs = slice(h * HEAD_W, (h + 1) * HEAD_W)
                va_s[h, 0:HEAD_W, 0:T] = kv[:, GROUP_W + h * HEAD_W:GROUP_W + (h + 1) * HEAD_W].T.astype(BF16)
                va_s[h, HEAD_W:, :] = ones_rows
                vb_s[h, 0:HEAD_W, 0:T] = dv_ref[:, hs].astype(F32).T.astype(BF16)
                vb_s[h, HEAD_W:, :] = ones_rows
        else:
            lane_k = lax.broadcasted_iota(jnp.int32, (Tk, LANES), 1)
            ones_cols = jnp.where(lane_k == 0, 1.0, 0.0).astype(BF16)
            for h in range(MLA_HEADS):
                hs = slice(h * HEAD_W, (h + 1) * HEAD_W)
                va_s[0:T, 2 * h * HEAD_W:(2 * h + 1) * HEAD_W] = kv[:, GROUP_W + h * HEAD_W:GROUP_W + (h + 1) * HEAD_W].astype(BF16)
                va_s[:, (2 * h + 1) * HEAD_W:(2 * h + 2) * HEAD_W] = ones_cols
                vb_s[0:T, 2 * h * HEAD_W:(2 * h + 1) * HEAD_W] = dv_ref[:, hs]
                vb_s[:, (2 * h + 1) * HEAD_W:(2 * h + 2) * HEAD_W] = ones_cols
        if has_ctx:
            for h in range(DIFF_HEADS):
                hs = slice(h * HEAD_W, (h + 1) * HEAD_W)
                kb_s[0:T, hs] = _rope_tile(dk_ref[:, hs].astype(F32), cosk_ref[...], sink_ref[...]).astype(BF16)
            kvc = jnp.dot(cckv_ref[...].astype(BF16), wukv, preferred_element_type=F32)
            ckr = ckr_ref[...].astype(BF16)
            for h in range(MLA_HEADS):
                ka_s[T:Tk, 2 * h * HEAD_W:(2 * h + 1) * HEAD_W] = kvc[:, h * HEAD_W:(h + 1) * HEAD_W].astype(BF16)
                ka_s[T:Tk, (2 * h + 1) * HEAD_W:(2 * h + 2) * HEAD_W] = ckr
            for h in range(DIFF_HEADS):
                hs = slice(h * HEAD_W, (h + 1) * HEAD_W)
                kb_s[T:Tk, hs] = cdk_ref[pl.ds(h, PAST_LEN, stride=DIFF_HEADS), :].astype(BF16)
                v_c = kvc[:, GROUP_W + h * HEAD_W:GROUP_W + (h + 1) * HEAD_W]
                vd_c = cdv_ref[pl.ds(h, PAST_LEN, stride=DIFF_HEADS), :]
                if pv_t:
                    va_s[h, 0:HEAD_W, T:Tk] = v_c.T.astype(BF16)
                    vb_s[h, 0:HEAD_W, T:Tk] = vd_c.T.astype(BF16)
                else:
                    va_s[T:Tk, 2 * h * HEAD_W:(2 * h + 1) * HEAD_W] = v_c.astype(BF16)
                    vb_s[T:Tk, 2 * h * HEAD_W:(2 * h + 1) * HEAD_W] = vd_c.astype(BF16)
        else:
            kb_s[0:T, :] = dk_ref[...]

    nt = (((1,), (1,)), ((), ()))
    tq = cq_ref.shape[0]

    maps = []
    qa = jnp.dot(_rms(cq_ref[...], gq_ref[...]).astype(BF16), wuq_ref[...], preferred_element_type=F32)
    for h in range(MLA_HEADS):
        q_nope = qa[:, 2 * h * HEAD_W:(2 * h + 1) * HEAD_W]
        q_rope = qa[:, (2 * h + 1) * HEAD_W:(2 * h + 2) * HEAD_W]
        if has_ctx:
            q_rope = _rope_tile(q_rope, cosq_ref[...], sinq_ref[...])
        q_h = (jnp.concatenate([q_nope, q_rope], axis=-1) * (MLA_SCALE * LOG2E)).astype(BF16)
        two = slice(2 * h * HEAD_W, (2 * h + 2) * HEAD_W)
        v_load = functools.partial(lambda hh: va_s[hh], h) if pv_t else functools.partial(lambda sl: va_s[:, sl], two)
        maps.append((q_h, functools.partial(lambda sl: ka_s[:, sl], two), v_load))
    lane_q = lax.broadcasted_iota(jnp.int32, (tq, HEAD_W), 1)
    for h in range(DIFF_HEADS):
        hs = slice(h * HEAD_W, (h + 1) * HEAD_W)
        q_h = dq_ref[:, hs].astype(F32)
        if has_ctx:
            q_h = _rope_tile(q_h, cosq_ref[...], sinq_ref[...])
        q_h = q_h * (DIFF_SCALE * LOG2E)
        two = slice(2 * h * HEAD_W, (2 * h + 2) * HEAD_W)
        v_load = functools.partial(lambda hh: vb_s[hh], h) if pv_t else functools.partial(lambda sl: vb_s[:, sl], two)
        for q_m in (jnp.where(lane_q < DIFF_D, q_h, 0.0), jnp.where(lane_q >= DIFF_D, q_h, 0.0)):
            maps.append((q_m.astype(BF16), functools.partial(lambda sl: kb_s[:, sl], hs), v_load))

    def scores(i):
        if pv_t:
            return lax.dot_general(maps[i][1](), maps[i][0], nt, preferred_element_type=F32)
        return lax.dot_general(maps[i][0], maps[i][1](), nt, preferred_element_type=F32)

    def softmax_pv(i, s):
        if pv_t:
            e = jnp.exp2(s - jnp.max(s, axis=0, keepdims=True))
            pv = jnp.dot(maps[i][2](), e.astype(BF16), preferred_element_type=F32)
            return (pv[0:HEAD_W, :] * (1.0 / pv[HEAD_W:HEAD_W + 1, :])).T
        e = jnp.exp2(s - jnp.max(s, axis=-1, keepdims=True))
        pv = jnp.dot(e.astype(BF16), maps[i][2](), preferred_element_type=F32)
        return pv[:, :HEAD_W] * (1.0 / pv[:, HEAD_W:HEAD_W + 1])

    n_maps = len(maps)
    ahead = min(ATTN_AHEAD_CTX if not has_ctx else ATTN_AHEAD_LAT, n_maps)
    pending = {i: scores(i) for i in range(ahead)}
    outs = []
    for i in range(n_maps):
        if i + ahead < n_maps:
            pending[i + ahead] = scores(i + ahead)
        outs.append(softmax_pv(i, pending.pop(i)))

    for h in range(MLA_HEADS):
        hs = slice(h * HEAD_W, (h + 1) * HEAD_W)
        ycat_s[:, hs] = (outs[h] * za_ref[:, hs].astype(F32)).astype(BF16)

    lp = lam_ref[...]
    lf = l_ref[0].astype(F32)
    lam_init = 0.8 - 0.6 * jnp.exp(jnp.full((1, 1), -0.3, F32) * lf)
    lam = (jnp.exp(jnp.sum(lp[0:1, :] * lp[1:2, :], axis=-1, keepdims=True))
           - jnp.exp(jnp.sum(lp[2:3, :] * lp[3:4, :], axis=-1, keepdims=True)) + lam_init)
    for h in range(DIFF_HEADS):
        hs = slice(h * HEAD_W, (h + 1) * HEAD_W)
        o1, o2 = outs[MLA_HEADS + 2 * h], outs[MLA_HEADS + 2 * h + 1]
        o = _rms(o1 - lam * o2, gdn_ref[...]) * (1.0 - lam_init)
        ycat_s[:, GROUP_W + h * HEAD_W:GROUP_W + (h + 1) * HEAD_W] = (o * zb_ref[:, hs].astype(F32)).astype(BF16)

    ycat_s[:, 2 * GROUP_W:] = yc_ref[...]
    y = jnp.dot(ycat_s[...], wout_ref[...], preferred_element_type=F32)
    x_new = x_ref[...] + mod_ref[:, 2 * D_MODEL:] * y
    xo_ref[...] = x_new
    if is_last:
        yfin_ref[...] = _rms(x_new, gfin_ref[...])


def _attn_call(l_arr, main, tail, yc, x2, mod_all, w, T, has_ctx, is_last, rope=None, ctx=None, side_bufs=None):
    n_tok = main.shape[0]
    B = n_tok // T
    tq = ATTN_TQ
    nq = T // tq
    Tk = T + (PAST_LEN if has_ctx else 0)
    main3 = main.reshape(B, T, MAIN_W)
    tail3 = tail.reshape(B, T, TAIL_W)
    x3 = x2.reshape(B, T, D_MODEL)

    def tile(c, width):
        return pl.BlockSpec((None, tq, width), lambda b, q, l, c=c: (b, q, c))

    def full(c, width):
        return pl.BlockSpec((None, T, width), lambda b, q, l, c=c: (b, 0, c))

    def wspec(shape):
        return pl.BlockSpec((None,) + shape, lambda b, q, l: (l[0],) + (0,) * len(shape))

    if has_ctx:
        mod_map = lambda b, q, l: (l[0], 1 + b, 0, 0)
    else:
        mod_map = lambda b, q, l: (l[0], 0, 0, 0)

    merged = nq == 1
    if merged:
        assert (C_ZA, C_ZB) == (0, 4)
        in_specs = [pl.BlockSpec((None, T, TAIL_W), lambda b, q, l: (b, 0, 0)),
                    pl.BlockSpec((None, T, MAIN_W // 2), lambda b, q, l: (b, 0, 0))]
        args = [tail3, main3]
    else:
        in_specs = [tile(1, MLA_Q_RANK),
                    full(0, MLA_KV_RANK),
                    full(2, LANES),
                    tile(C_DQ, GROUP_W), full(C_DK, GROUP_W), full(C_DV, GROUP_W),
                    tile(C_ZB, GROUP_W), tile(C_ZA, GROUP_W)]
        args = [tail3, tail3, tail3, main3, main3, main3, main3, main3]
    in_specs += [pl.BlockSpec((None, tq, GROUP_W), lambda b, q, l: (b, q, 0)),
                 pl.BlockSpec((None, tq, D_MODEL), lambda b, q, l: (b, q, 0)),
                 pl.BlockSpec((None, None, 1, 3 * D_MODEL), mod_map),
                 wspec((MLA_Q_RANK, 2 * GROUP_W)), wspec((MLA_KV_RANK, 2 * GROUP_W)),
                 wspec((3 * GROUP_W, D_MODEL)),
                 wspec((1, MLA_Q_RANK)), wspec((1, MLA_KV_RANK)), wspec((1, HEAD_W)),
                 wspec((4, DIFF_D))]
    args += [yc, x3, mod_all, w['wuq'], w['wukv'], w['wout'], w['gq'], w['gkv'], w['gdn'], w['lam']]
    if is_last:
        in_specs.append(pl.BlockSpec((1, D_MODEL), lambda b, q, l: (0, 0)))
        args.append(w['gfin'])
    if has_ctx:
        cos_t, sin_t = rope
        in_specs += [pl.BlockSpec((tq, LANES), lambda b, q, l: (q, 0)),
                     pl.BlockSpec((tq, LANES), lambda b, q, l: (q, 0)),
                     pl.BlockSpec((T, LANES), lambda b, q, l: (0, 0)),
                     pl.BlockSpec((T, LANES), lambda b, q, l: (0, 0)),
                     pl.BlockSpec((None, None, PAST_LEN, MLA_KV_RANK), lambda b, q, l: (b, l[0], 0, 0)),
                     pl.BlockSpec((None, None, PAST_LEN, LANES), lambda b, q, l: (b, l[0], 0, 0)),
                     pl.BlockSpec((None, None, PAST_LEN * DIFF_HEADS, HEAD_W), lambda b, q, l: (b, l[0], 0, 0)),
                     pl.BlockSpec((None, None, PAST_LEN * DIFF_HEADS, HEAD_W), lambda b, q, l: (b, l[0], 0, 0))]
        args += [cos_t, sin_t, cos_t, sin_t, ctx['ckv'], ctx['krope'], ctx['dk'], ctx['dv']]
    out_shape = [jax.ShapeDtypeStruct((B, T, D_MODEL), F32)]
    out_specs = [pl.BlockSpec((None, tq, D_MODEL), lambda b, q, l: (b, q, 0))]
    if is_last:
        out_shape.append(jax.ShapeDtypeStruct((B, T, D_MODEL), F32))
        out_specs.append(pl.BlockSpec((None, tq, D_MODEL), lambda b, q, l: (b, q, 0)))
    aliases = {}
    n_alias = 0
    if not has_ctx:
        if side_bufs is not None:
            n_alias = len(side_bufs)
            n_in = 1 + len(args)
            aliases = {n_in + i: len(out_shape) + i for i in range(n_alias)}
            in_specs += [pl.BlockSpec(memory_space=pl.ANY)] * n_alias
            args += list(side_bufs)
            out_specs += [pl.BlockSpec((None, None, T, MLA_KV_RANK), lambda b, q, l: (b, l[0], 0, 0)),
                          pl.BlockSpec((None, None, T, MLA_ROPE), lambda b, q, l: (b, l[0], 0, 0))]
        else:
            assert nq == 1
            out_specs += [pl.BlockSpec((None, DEPTH, T, MLA_KV_RANK), lambda b, q, l: (b, 0, 0, 0)),
                          pl.BlockSpec((None, DEPTH, T, MLA_ROPE), lambda b, q, l: (b, 0, 0, 0))]
        out_shape += [jax.ShapeDtypeStruct((B, DEPTH, T, MLA_KV_RANK), F32),
                      jax.ShapeDtypeStruct((B, DEPTH, T, MLA_ROPE), F32)]
    if Tk <= ATTN_PVT_MAX_KEYS:
        v_scratch = pltpu.VMEM((MLA_HEADS, VT_ROWS, Tk), BF16)
    else:
        v_scratch = pltpu.VMEM((Tk, 2 * GROUP_W), BF16)
    grid_spec = pltpu.PrefetchScalarGridSpec(
        num_scalar_prefetch=1, grid=(B, nq), in_specs=in_specs, out_specs=out_specs,
        scratch_shapes=[pltpu.VMEM((Tk, 2 * GROUP_W), BF16),
                        v_scratch,
                        pltpu.VMEM((Tk, GROUP_W), BF16),
                        v_scratch,
                        pltpu.VMEM((tq, 3 * GROUP_W), BF16)])
    return pl.pallas_call(
        functools.partial(_attn_kernel, has_ctx, is_last, merged, n_alias, T),
        grid_spec=grid_spec,
        out_shape=out_shape,
        input_output_aliases=aliases,
        compiler_params=pltpu.CompilerParams(dimension_semantics=("arbitrary", "arbitrary"),
                                             vmem_limit_bytes=VMEM_LIMIT),
        name="attn_lat" if has_ctx else "attn_ctx",
    )(l_arr, *args)


def _rope_tables(n_tok):
    n_freq = MLA_ROPE // 4
    inv = ROPE_THETA ** (-jnp.arange(n_freq, dtype=F32) / n_freq)
    n_rows = n_tok // GRID_W
    rowp = jnp.repeat(jnp.arange(n_rows, dtype=F32), GRID_W)
    colp = jnp.tile(jnp.arange(GRID_W, dtype=F32), n_rows)
    ang = jnp.concatenate([rowp[:, None] * inv, colp[:, None] * inv], axis=-1)
    cos, sin = jnp.cos(ang), jnp.sin(ang)
    cos64 = jnp.concatenate([cos, cos], axis=-1)
    sin64 = jnp.concatenate([-sin, sin], axis=-1)
    return jnp.concatenate([cos64, cos64], axis=-1), jnp.concatenate([sin64, sin64], axis=-1)


def _prep_weights(W_in, W_uq, W_ukv, W_out):
    w_in_t = jnp.swapaxes(W_in, 1, 2).astype(BF16)
    wq = W_uq.reshape(DEPTH, MLA_Q_RANK, MLA_HEADS, MLA_NOPE + MLA_ROPE)
    wq = jnp.pad(wq, ((0, 0), (0, 0), (0, 0), (0, 2 * HEAD_W - MLA_NOPE - MLA_ROPE)))
    wuq_r = wq.reshape(DEPTH, MLA_Q_RANK, MLA_HEADS * 2 * HEAD_W).astype(BF16)
    wkv = W_ukv.reshape(DEPTH, MLA_KV_RANK, MLA_HEADS, 2, HEAD_W)
    wukv_r = jnp.swapaxes(wkv, 2, 3).reshape(DEPTH, MLA_KV_RANK, 2 * GROUP_W).astype(BF16)
    return w_in_t, wuq_r, wukv_r, W_out.astype(BF16)


def kernel(x_prompt, x_sample, cache_mla_ckv, cache_mla_krope, cache_diff_k, cache_diff_v, state_mlstm_C, state_mlstm_n, state_mlstm_m, c, c_ctx, g_norm, W_mod, b_mod, W_in, mla_q_norm, W_uq, mla_kv_norm, W_ukv, diff_lambda, diff_norm, ml_conv, ml_gate_b, ml_norm, W_out, g_final):
    Bc, Tc, _ = x_prompt.shape
    Bs, Ts, _ = x_sample.shape

    w_in_t, wuq_r, wukv_r, wout_r = _prep_weights(W_in, W_uq, W_ukv, W_out)
    w = {'wuq': wuq_r, 'wukv': wukv_r, 'wout': wout_r,
         'gq': mla_q_norm.reshape(DEPTH, 1, MLA_Q_RANK), 'gkv': mla_kv_norm.reshape(DEPTH, 1, MLA_KV_RANK),
         'gdn': diff_norm.reshape(DEPTH, 1, 2 * DIFF_D), 'lam': diff_lambda,
         'gfin': g_final.reshape(1, D_MODEL)}
    g_norm3 = g_norm.reshape(DEPTH, 1, D_MODEL)
    ml_norm2 = ml_norm.reshape(DEPTH, 1, GROUP_W)
    gate_row = jnp.pad(ml_gate_b.reshape(DEPTH, 1, N_ML_GATES),
                       ((0, 0), (0, 0), (GATE_LANE0, LANES - GATE_LANE0 - N_ML_GATES)))

    cc = jnp.concatenate([c_ctx[None, :], c, jnp.zeros((8 - 1 - Bs, D_MODEL), F32)], axis=0)
    mod_all = _mod_call(cc, W_mod, b_mod).reshape(DEPTH, 8, 1, 3 * D_MODEL)

    dkv = cnm = ckr = None
    x2 = x_prompt.reshape(Bc * Tc, D_MODEL)
    y_prompt = None
    for l in range(DEPTH):
        l_arr = jnp.full((1,), l, jnp.int32)
        main, tail, *dkv = _proj_call(l_arr, x2, mod_all, g_norm3, w_in_t, Tc, False, dkv)
        yc, *cnm = _mlstm_call(l_arr, main, tail, ml_conv, gate_row, ml_norm2, Tc, False, side_bufs=cnm)
        outs = _attn_call(l_arr, main, tail, yc, x2, mod_all, w, Tc, False, l == DEPTH - 1, side_bufs=ckr)
        ckr = outs[-2:]
        dk_o, dv_o = dkv
        c_o, n_o, m_o = cnm
        if l == DEPTH - 1:
            x3, y_prompt, ckvn, kro = outs
        else:
            x3, ckvn, kro = outs
        x2 = x3.reshape(Bc * Tc, D_MODEL)
    side_outs = (ckvn, kro,
                 dk_o.reshape(Bc, DEPTH, Tc, DIFF_HEADS, 2 * DIFF_D),
                 dv_o.reshape(Bc, DEPTH, Tc, DIFF_HEADS, 2 * DIFF_D),
                 c_o.reshape(Bc, DEPTH, 2, ML_HEADS, ML_DK, HEAD_W),
                 n_o.reshape(Bc, DEPTH, 2, ML_HEADS, ML_DK),
                 m_o[:, :, :, 0].reshape(Bc, DEPTH, 2, ML_HEADS))

    rope = _rope_tables(Ts)
    ctx = {'ckv': cache_mla_ckv,
           'krope': jnp.pad(cache_mla_krope, ((0, 0), (0, 0), (0, 0), (0, LANES - MLA_ROPE))),
           'dk': cache_diff_k.reshape(Bs, DEPTH, PAST_LEN * DIFF_HEADS, HEAD_W),
           'dv': cache_diff_v.reshape(Bs, DEPTH, PAST_LEN * DIFF_HEADS, HEAD_W)}
    ctx_state = (state_mlstm_C.reshape(Bs, DEPTH, 2 * ML_HEADS, ML_DK, HEAD_W),
                 state_mlstm_n.reshape(Bs, DEPTH, 2 * ML_HEADS, ML_DK),
                 jnp.broadcast_to(state_mlstm_m.reshape(Bs, DEPTH, 2 * ML_HEADS, 1),
                                  (Bs, DEPTH, 2 * ML_HEADS, LANES)))
    x2 = x_sample.reshape(Bs * Ts, D_MODEL)
    y_sample = None
    for l in range(DEPTH):
        l_arr = jnp.full((1,), l, jnp.int32)
        main, tail = _proj_call(l_arr, x2, mod_all, g_norm3, w_in_t, Ts, True)
        (yc,) = _mlstm_call(l_arr, main, tail, ml_conv, gate_row, ml_norm2, Ts, True, ctx_state)
        outs = _attn_call(l_arr, main, tail, yc, x2, mod_all, w, Ts, True, l == DEPTH - 1, rope, ctx)
        if l == DEPTH - 1:
            x3, y_sample = outs
        else:
            (x3,) = outs
        x2 = x3.reshape(Bs * Ts, D_MODEL)

    return (y_prompt, y_sample, *side_outs)
```

```python
import functools
import math

import jax
import jax.numpy as jnp
import numpy as np
from jax import lax
from jax.experimental import pallas as pl
from jax.experimental.pallas import tpu as pltpu

F32 = jnp.float32
BF16 = jnp.bfloat16

D_MODEL = 1024
DEPTH = 4
PAST_LEN = 256
GRID_W = 64
GROUP_W = 512
MLA_HEADS = 4
MLA_NOPE = 128
MLA_ROPE = 64
MLA_Q_RANK = 384
MLA_KV_RANK = 256
DIFF_HEADS = 4
DIFF_D = 64
ML_HEADS = 4
ML_DK = 128
N_ML_GATES = 16
ROPE_THETA = 10000.0
NORM_EPS = 1e-6
MLA_SCALE = (MLA_NOPE + MLA_ROPE) ** -0.5
DIFF_SCALE = DIFF_D ** -0.5
LOG2E = math.log2(math.e)

LANES = 128
HEAD_W = 128
ML_CHUNK = 256
VT_ROWS = 144
PROJ_TM = 512
ATTN_TQ = 512
ATTN_PVT_MAX_KEYS = 256
ATTN_AHEAD_CTX = 12
ATTN_AHEAD_LAT = 2
MAIN_W = 10 * GROUP_W
TAIL_W = 768
N_IN = 5840
VMEM_LIMIT = 56 * 1024 * 1024

C_ZA, C_DQ, C_DK, C_DV, C_ZB, C_MQ, C_MK, C_MV, C_MO, C_ZC = range(10)

_IN_SIZES = (MLA_Q_RANK, MLA_KV_RANK, MLA_ROPE, GROUP_W, GROUP_W, GROUP_W, GROUP_W, GROUP_W,
             GROUP_W, GROUP_W, GROUP_W, GROUP_W, GROUP_W, N_ML_GATES)
_IN_OFF = np.concatenate([[0], np.cumsum(_IN_SIZES)])
(_O_CQ, _O_CKV, _O_KR, _O_ZA, _O_DQ, _O_DK, _O_DV, _O_ZB, _O_MQ, _O_MK, _O_MV, _O_MO, _O_ZC,
 _O_MG) = [int(v) for v in _IN_OFF[:-1]]
GATE_LANE0 = MLA_ROPE


def _rms(x, g):
    ms = jnp.mean(x * x, axis=-1, keepdims=True)
    return x * lax.rsqrt(ms + NORM_EPS) * g


def _sigmoid(x):
    return 0.5 + 0.5 * jnp.tanh(0.5 * x)


def _silu(x):
    hx = 0.5 * x
    return hx + hx * jnp.tanh(hx)


def _log_sigmoid(x):
    return jnp.minimum(x, 0.0) - jnp.log1p(jnp.exp(-jnp.abs(x)))


def _swap32(x):
    lane = lax.broadcasted_iota(jnp.int32, x.shape, 1)
    fwd = pltpu.roll(x, LANES - 32, 1)
    bwd = pltpu.roll(x, 32, 1)
    return jnp.where((lane % 64) < 32, fwd, bwd)


def _rope_tile(x, cos, sin):
    return x * cos + _swap32(x) * sin


def _mod_kernel(c_ref, w_ref, b_ref, o_ref):
    a = _silu(c_ref[...]).astype(BF16)
    o_ref[...] = jnp.dot(a, w_ref[...].astype(BF16), preferred_element_type=F32) + b_ref[...]


def _mod_call(cc, W_mod, b_mod):
    tn = 1024
    return pl.pallas_call(
        _mod_kernel,
        grid=(DEPTH, 3 * D_MODEL // tn),
        in_specs=[pl.BlockSpec((8, D_MODEL), lambda l, j: (0, 0)),
                  pl.BlockSpec((None, D_MODEL, tn), lambda l, j: (l, 0, j)),
                  pl.BlockSpec((None, 1, tn), lambda l, j: (l, 0, j))],
        out_specs=pl.BlockSpec((None, 8, tn), lambda l, j: (l, 0, j)),
        out_shape=jax.ShapeDtypeStruct((DEPTH, 8, 3 * D_MODEL), F32),
        compiler_params=pltpu.CompilerParams(dimension_semantics=("arbitrary", "arbitrary"),
                                             vmem_limit_bytes=VMEM_LIMIT),
        name="mod",
    )(cc, W_mod, b_mod.reshape(DEPTH, 1, 3 * D_MODEL))


def _proj_kernel(has_side, n_alias, T, *refs):
    l_ref, x_ref, mod_ref, g_ref, w_ref = refs[:5]
    refs = refs[5 + n_alias:]
    if has_side:
        main_ref, tail_ref, dk_ref, dv_ref, wt_s = refs
    else:
        main_ref, tail_ref, wt_s = refs
    del l_ref
    if has_side and n_alias == 0:
        for r in (dk_ref, dv_ref):
            r[:, 1:] = jnp.zeros((r.shape[0], DEPTH - 1) + r.shape[2:], F32)
        dk_ref, dv_ref = dk_ref.at[:, 0], dv_ref.at[:, 0]
    tm = x_ref.shape[0]
    nt = (((1,), (1,)), ((), ()))

    @pl.when(pl.program_id(0) == 0)
    def _gather_tail_rows():
        wt_s[0:MLA_ROPE, :] = w_ref[_O_KR:_O_KR + MLA_ROPE, :]
        wt_s[MLA_ROPE:MLA_ROPE + N_ML_GATES, :] = w_ref[_O_MG:_O_MG + N_ML_GATES, :]
        wt_s[MLA_ROPE + N_ML_GATES:LANES, :] = jnp.zeros((LANES - MLA_ROPE - N_ML_GATES, D_MODEL), BF16)
        wt_s[LANES:, :] = w_ref[_O_CQ:_O_CQ + MLA_Q_RANK, :]

    mod = mod_ref[...]
    y = _rms(x_ref[...], g_ref[...])
    h = (y * (1.0 + mod[:, D_MODEL:2 * D_MODEL]) + mod[:, :D_MODEL]).astype(BF16)
    for s in range(MAIN_W // GROUP_W):
        cols = slice(s * GROUP_W, (s + 1) * GROUP_W)
        acc = lax.dot_general(h, w_ref[_O_ZA + s * GROUP_W:_O_ZA + (s + 1) * GROUP_W, :], nt,
                              preferred_element_type=F32)
        if s in (C_ZA, C_ZB, C_ZC):
            main_ref[:, cols] = _silu(acc).astype(BF16)
        elif s == C_MO:
            main_ref[:, cols] = _sigmoid(acc).astype(BF16)
        else:
            main_ref[:, cols] = acc.astype(BF16)
        if has_side and s in (C_DK, C_DV):
            side_ref = dk_ref if s == C_DK else dv_ref
            for b in range(tm // T):
                for hd in range(DIFF_HEADS):
                    side_ref[b, pl.ds(hd, T, stride=DIFF_HEADS), :] = (
                        acc[b * T:(b + 1) * T, hd * HEAD_W:(hd + 1) * HEAD_W])
    tail_ref[:, 0:MLA_KV_RANK] = lax.dot_general(h, w_ref[_O_CKV:_O_CKV + MLA_KV_RANK, :], nt,
                                                 preferred_element_type=F32)
    tail_ref[:, MLA_KV_RANK:] = lax.dot_general(h, wt_s[...], nt, preferred_element_type=F32)


def _proj_call(l_arr, x2, mod_all, g_norm3, w_in_t, T, has_ctx, side_bufs=None):
    n_tok = x2.shape[0]
    tm = PROJ_TM
    has_side = not has_ctx
    if has_ctx:
        mod_map = lambda i, l: (l[0], 1 + (i * tm) // T, 0, 0)
    else:
        mod_map = lambda i, l: (l[0], 0, 0, 0)
    in_specs = [pl.BlockSpec((tm, D_MODEL), lambda i, l: (i, 0)),
                pl.BlockSpec((None, None, 1, 3 * D_MODEL), mod_map),
                pl.BlockSpec((None, 1, D_MODEL), lambda i, l: (l[0], 0, 0)),
                pl.BlockSpec((None, N_IN, D_MODEL), lambda i, l: (l[0], 0, 0), pipeline_mode=pl.Buffered(1))]
    args = [x2, mod_all, g_norm3, w_in_t]
    out_shape = [jax.ShapeDtypeStruct((n_tok, MAIN_W), BF16), jax.ShapeDtypeStruct((n_tok, TAIL_W), F32)]
    out_specs = [pl.BlockSpec((tm, MAIN_W), lambda i, l: (i, 0)), pl.BlockSpec((tm, TAIL_W), lambda i, l: (i, 0))]
    aliases = {}
    n_alias = 0
    if has_side:
        bt = tm // T
        out_shape += [jax.ShapeDtypeStruct((n_tok // T, DEPTH, T * DIFF_HEADS, HEAD_W), F32)] * 2
        if side_bufs is not None:
            n_alias = len(side_bufs)
            aliases = {1 + len(args) + i: 2 + i for i in range(n_alias)}
            in_specs += [pl.BlockSpec(memory_space=pl.ANY)] * n_alias
            args += list(side_bufs)
            out_specs += [pl.BlockSpec((bt, None, T * DIFF_HEADS, HEAD_W), lambda i, l: (i, l[0], 0, 0))] * 2
        else:
            out_specs += [pl.BlockSpec((bt, DEPTH, T * DIFF_HEADS, HEAD_W), lambda i, l: (i, 0, 0, 0))] * 2
    grid_spec = pltpu.PrefetchScalarGridSpec(
        num_scalar_prefetch=1, grid=(n_tok // tm,), in_specs=in_specs, out_specs=out_specs,
        scratch_shapes=[pltpu.VMEM((TAIL_W - MLA_KV_RANK, D_MODEL), BF16)])
    return pl.pallas_call(
        functools.partial(_proj_kernel, has_side, n_alias, T),
        grid_spec=grid_spec,
        out_shape=out_shape,
        input_output_aliases=aliases,
        compiler_params=pltpu.CompilerParams(dimension_semantics=("arbitrary",),
                                             vmem_limit_bytes=VMEM_LIMIT),
        name="proj_lat" if has_ctx else "proj_ctx",
    )(l_arr, *args)


def _mlstm_kernel(has_ctx, T, n_alias, *refs):
    nc = T // ML_CHUNK
    L = ML_CHUNK
    use_inter = has_ctx or nc > 1
    it = iter(refs)
    l_ref = next(it)
    ml_ref, aux_ref, convw_ref, gb_ref, mln_ref = [next(it) for _ in range(5)]
    mq_ref, mk_ref, mv_ref, mo_ref, zc_ref = [ml_ref.at[:, i * GROUP_W:(i + 1) * GROUP_W] for i in range(5)]
    if has_ctx:
        c0_ref, n0_ref, m0_ref = [next(it) for _ in range(3)]
    else:
        for _ in range(n_alias):
            next(it)
    yc_ref = next(it)
    if not has_ctx:
        cout_ref, nout_ref, mout_ref = [next(it) for _ in range(3)]
    q_s, k_s, vt_s, hf_s, hb_s, caug_s, m_s, gt_s = [next(it) for _ in range(8)]
    del l_ref
    if not has_ctx and n_alias == 0:
        for r in (cout_ref, nout_ref, mout_ref):
            r[1:] = jnp.zeros((DEPTH - 1,) + r.shape[1:], F32)
        cout_ref, nout_ref, mout_ref = cout_ref.at[0], nout_ref.at[0], mout_ref.at[0]

    row_b = lax.broadcasted_iota(jnp.int32, (L, LANES), 0)
    convw = convw_ref[...]
    halo = 16
    for c in range(nc):
        cs = slice(c * L, (c + 1) * L)
        for j in range(2 * GROUP_W // LANES):
            src = mq_ref if j < GROUP_W // LANES else mk_ref
            ls = slice((j % (GROUP_W // LANES)) * LANES, (j % (GROUP_W // LANES) + 1) * LANES)
            u = src[cs, ls].astype(F32)
            prev = src[c * L - halo:c * L, ls].astype(F32)[halo - 1:halo, :] if c > 0 else 0.0
            nxt = src[(c + 1) * L:(c + 1) * L + halo, ls].astype(F32)[0:1, :] if c < nc - 1 else 0.0
            up, un = pltpu.roll(u, 1, 0), pltpu.roll(u, L - 1, 0)
            up = jnp.concatenate([jnp.where(row_b[0:8] == 0, prev, up[0:8]), up[8:]], axis=0)
            un = jnp.concatenate([un[:L - 8], jnp.where(row_b[L - 8:] == L - 1, nxt, un[L - 8:])], axis=0)
            w3 = convw[:, j * LANES:(j + 1) * LANES]
            y = _silu(w3[0:1, :] * up + w3[1:2, :] * u + w3[2:3, :] * un)
            if j < GROUP_W // LANES:
                q_s[cs, ls] = (y * (ML_DK ** -0.5)).astype(BF16)
            else:
                k_s[cs, ls] = y
    row16 = lax.broadcasted_iota(jnp.int32, (VT_ROWS - HEAD_W, L), 0)
    ones_rows = jnp.where(row16 == 0, 1.0, 0.0).astype(BF16)
    g_t = (aux_ref[...] + gb_ref[...]).T
    for c in range(nc):
        cs = slice(c * L, (c + 1) * L)
        gt_s[c] = g_t[GATE_LANE0:GATE_LANE0 + N_ML_GATES, cs]
        for h in range(ML_HEADS):
            vt_s[c, h, 0:HEAD_W, :] = mv_ref[cs, h * HEAD_W:(h + 1) * HEAD_W].astype(F32).T.astype(BF16)
            vt_s[c, h, HEAD_W:, :] = ones_rows

    reps = L // LANES
    if has_ctx:
        n0 = n0_ref[...]
        m0 = m0_ref[...]
        m_s[0] = jnp.concatenate([m0] * reps, axis=1)
        m_s[1] = jnp.concatenate([pltpu.roll(m0, ML_HEADS, 0)] * reps, axis=1)
        row_n = lax.broadcasted_iota(jnp.int32, (VT_ROWS - HEAD_W, ML_DK), 0)
        for r in range(2 * ML_HEADS):
            caug_s[r, 0:HEAD_W, :] = c0_ref[r].T
            caug_s[r, HEAD_W:, :] = jnp.where(row_n == 0, n0[r:r + 1, :], 0.0)
    else:
        caug_s[...] = jnp.zeros_like(caug_s)
        m_s[...] = jnp.zeros_like(m_s)

    ri = lax.broadcasted_iota(jnp.int32, (L, L), 0)
    ci = lax.broadcasted_iota(jnp.int32, (L, L), 1)
    row_r = lax.broadcasted_iota(jnp.int32, (2 * ML_HEADS, L), 0)
    nt = (((1,), (1,)), ((), ()))

    def split3(x):
        hi = x.astype(BF16).astype(F32)
        mid = (x - hi).astype(BF16).astype(F32)
        return hi, mid, (x - hi - mid).astype(BF16).astype(F32)

    masks = ((ri <= ci), (ri >= ci))

    def gate_rows(d, c):
        fwd = d == 0
        tri_t = jnp.where(masks[d], 1.0, 0.0).astype(BF16)
        last = L - 1 if fwd else 0

        g8 = gt_s[c, d * 2 * ML_HEADS:(d + 1) * 2 * ML_HEADS, :]
        hi8, mid8, lo8 = split3(_log_sigmoid(g8))
        stack = jnp.concatenate([hi8, mid8, lo8, jnp.zeros_like(hi8)], axis=0).astype(BF16)
        part8 = jnp.dot(stack, tri_t, preferred_element_type=F32)
        bc8 = part8[0:8] + part8[8:16] + part8[16:24]
        bcs = pltpu.roll(bc8, ML_HEADS, 0)
        a8 = g8 - bcs
        a_n = jnp.concatenate([a8, jnp.zeros((LANES - 2 * ML_HEADS, L), F32)], axis=0).T
        cm8 = jnp.zeros((2 * ML_HEADS, L), F32)
        for h in range(ML_HEADS):
            col_max = jnp.max(jnp.where(masks[d], a_n[:, h:h + 1], -jnp.inf), axis=0, keepdims=True)
            cm8 = jnp.where(row_r == h, col_max, cm8)
        m8 = m_s[d]
        g_row = bcs + m8
        m_t = jnp.maximum(g_row, bcs + cm8)
        w_inter = jnp.exp(g_row - m_t)
        e_inv = jnp.exp(-m_t)
        c_row = bcs - m_t
        b_last = bcs[:, last:last + 1]
        m_new = m_t[:, last:last + 1]
        a_prev = jnp.exp(b_last + m8[:, 0:1] - m_new)
        w_s = jnp.exp(a8 + (b_last - m_new))
        m_s[d] = jnp.broadcast_to(m_new, (2 * ML_HEADS, L))
        return a_n, c_row, w_inter, e_inv, w_s, a_prev

    def do_pair(c_f, c_b):
        stats = (gate_rows(0, c_f), gate_rows(1, c_b))
        combos = [(d, h) for d in range(2) for h in range(ML_HEADS)]
        rows = []
        for c in (c_f, c_b):
            rows.append(slice(c * L, (c + 1) * L) if isinstance(c, int) else pl.ds(pl.multiple_of(c * L, L), L))
        chunk = (c_f, c_b)
        hsl = [slice(h * HEAD_W, (h + 1) * HEAD_W) for h in range(ML_HEADS)]
        qcs = [q_s[rows[d], hsl[h]] for d, h in combos]
        kcs = [k_s[rows[d], hsl[h]].astype(BF16) for d, h in combos]
        vts = [vt_s[chunk[d], h] for d, h in combos]
        n = range(len(combos))
        s_ts = [lax.dot_general(kcs[i], qcs[i], nt, preferred_element_type=F32) for i in n]
        sws = []
        for i, (d, h) in enumerate(combos):
            a_n, c_row = stats[d][0], stats[d][1]
            w_t = jnp.exp(jnp.where(masks[d], a_n[:, h:h + 1] + c_row[h:h + 1, :], -jnp.inf))
            sws.append((s_ts[i] * w_t).astype(BF16))
        nds = [jnp.dot(vts[i], sws[i], preferred_element_type=F32) for i in n]
        if use_inter:
            inters = [lax.dot_general(caug_s[d * ML_HEADS + h].astype(BF16), qcs[i], nt,
                                      preferred_element_type=F32) for i, (d, h) in enumerate(combos)]
            nds = [nds[i] + stats[d][2][h:h + 1, :] * inters[i] for i, (d, h) in enumerate(combos)]
        for i, (d, h) in enumerate(combos):
            inv = 1.0 / jnp.maximum(jnp.abs(nds[i][HEAD_W:HEAD_W + 1, :]), stats[d][3][h:h + 1, :])
            h_s = hf_s if d == 0 else hb_s
            h_s[rows[d], hsl[h]] = (nds[i][0:HEAD_W, :] * inv).T
        upds = [jnp.dot((vts[i].astype(F32) * stats[d][4][h:h + 1, :]).astype(BF16), kcs[i],
                        preferred_element_type=F32) for i, (d, h) in enumerate(combos)]
        for i, (d, h) in enumerate(combos):
            r = d * ML_HEADS + h
            caug_s[r] = upds[i] + stats[d][5][h:h + 1, :] * caug_s[r] if use_inter else upds[i]

    if nc == 1:
        do_pair(0, 0)
    else:
        def body(i, carry):
            do_pair(i, nc - 1 - i)
            return carry
        lax.fori_loop(0, nc, body, 0)

    mln = mln_ref[...]
    for c in range(nc):
        cs = slice(c * L, (c + 1) * L)
        for h in range(ML_HEADS):
            hs = slice(h * HEAD_W, (h + 1) * HEAD_W)
            hc = mo_ref[cs, hs].astype(F32) * (hf_s[cs, hs] + hb_s[cs, hs])
            yc_ref[cs, hs] = (_rms(hc, mln[:, hs]) * zc_ref[cs, hs].astype(F32)).astype(BF16)

    if not has_ctx:
        for r in range(2 * ML_HEADS):
            cout_ref[r] = caug_s[r, 0:HEAD_W, :].T
            nout_ref[r:r + 1, :] = caug_s[r, HEAD_W:HEAD_W + 1, :]
        for d in range(2):
            mout_ref[d * ML_HEADS:(d + 1) * ML_HEADS, :] = m_s[d, 0:ML_HEADS, 0:LANES]


def _mlstm_call(l_arr, main, tail, ml_conv, gate_row, ml_norm2, T, has_ctx, ctx_state=None, side_bufs=None):
    n_tok = main.shape[0]
    B = n_tok // T
    main3 = main.reshape(B, T, MAIN_W)
    tail3 = tail.reshape(B, T, TAIL_W)

    assert (C_MQ, C_ZC) == (5, 9)
    in_specs = [pl.BlockSpec((None, T, MAIN_W // 2), lambda b, l: (b, 0, 1)),
                pl.BlockSpec((None, T, LANES), lambda b, l: (b, 0, 2)),
                pl.BlockSpec((None, 3, 2 * GROUP_W), lambda b, l: (l[0], 0, 0)),
                pl.BlockSpec((None, 1, LANES), lambda b, l: (l[0], 0, 0)),
                pl.BlockSpec((None, 1, GROUP_W), lambda b, l: (l[0], 0, 0))]
    args = [main3, tail3, ml_conv, gate_row, ml_norm2]
    n_alias = 0 if (has_ctx or side_bufs is None) else len(side_bufs)
    out_shape = [jax.ShapeDtypeStruct((B, T, GROUP_W), BF16)]
    out_specs = [pl.BlockSpec((None, T, GROUP_W), lambda b, l: (b, 0, 0))]
    if has_ctx:
        c0, n0, m0 = ctx_state
        in_specs += [pl.BlockSpec((None, None, 2 * ML_HEADS, ML_DK, HEAD_W), lambda b, l: (b, l[0], 0, 0, 0)),
                     pl.BlockSpec((None, None, 2 * ML_HEADS, ML_DK), lambda b, l: (b, l[0], 0, 0)),
                     pl.BlockSpec((None, None, 2 * ML_HEADS, LANES), lambda b, l: (b, l[0], 0, 0))]
        args += [c0, n0, m0]
        aliases = {}
    else:
        aliases = {}
        out_shape += [jax.ShapeDtypeStruct((B, DEPTH, 2 * ML_HEADS, ML_DK, HEAD_W), F32),
                      jax.ShapeDtypeStruct((B, DEPTH, 2 * ML_HEADS, ML_DK), F32),
                      jax.ShapeDtypeStruct((B, DEPTH, 2 * ML_HEADS, LANES), F32)]
        if n_alias:
            aliases = {1 + len(args) + i: 1 + i for i in range(n_alias)}
            in_specs += [pl.BlockSpec(memory_space=pl.ANY)] * n_alias
            args += list(side_bufs)
            out_specs += [pl.BlockSpec((None, None, 2 * ML_HEADS, ML_DK, HEAD_W), lambda b, l: (b, l[0], 0, 0, 0)),
                          pl.BlockSpec((None, None, 2 * ML_HEADS, ML_DK), lambda b, l: (b, l[0], 0, 0)),
                          pl.BlockSpec((None, None, 2 * ML_HEADS, LANES), lambda b, l: (b, l[0], 0, 0))]
        else:
            out_specs += [pl.BlockSpec((None, DEPTH, 2 * ML_HEADS, ML_DK, HEAD_W), lambda b, l: (b, 0, 0, 0, 0)),
                          pl.BlockSpec((None, DEPTH, 2 * ML_HEADS, ML_DK), lambda b, l: (b, 0, 0, 0)),
                          pl.BlockSpec((None, DEPTH, 2 * ML_HEADS, LANES), lambda b, l: (b, 0, 0, 0))]
    grid_spec = pltpu.PrefetchScalarGridSpec(
        num_scalar_prefetch=1, grid=(B,), in_specs=in_specs, out_specs=out_specs,
        scratch_shapes=[pltpu.VMEM((T, GROUP_W), BF16),
                        pltpu.VMEM((T, GROUP_W), F32),
                        pltpu.VMEM((T // ML_CHUNK, ML_HEADS, VT_ROWS, ML_CHUNK), BF16),
                        pltpu.VMEM((T, GROUP_W), F32),
                        pltpu.VMEM((T, GROUP_W), F32),
                        pltpu.VMEM((2 * ML_HEADS, VT_ROWS, ML_DK), F32),
                        pltpu.VMEM((2, 2 * ML_HEADS, ML_CHUNK), F32),
                        pltpu.VMEM((T // ML_CHUNK, N_ML_GATES, ML_CHUNK), F32)])
    return pl.pallas_call(
        functools.partial(_mlstm_kernel, has_ctx, T, n_alias),
        grid_spec=grid_spec,
        out_shape=out_shape,
        input_output_aliases=aliases,
        compiler_params=pltpu.CompilerParams(dimension_semantics=("arbitrary",),
                                             vmem_limit_bytes=VMEM_LIMIT),
        name="mlstm_lat" if has_ctx else "mlstm_ctx",
    )(l_arr, *args)


def _attn_kernel(has_ctx, is_last, merged, n_alias, T, *refs):
    Tk = T + (PAST_LEN if has_ctx else 0)
    pv_t = Tk <= ATTN_PVT_MAX_KEYS
    it = iter(refs)
    l_ref = next(it)
    if merged:
        tail_ref, att_ref = next(it), next(it)
        ckv_ref = tail_ref.at[:, 0:MLA_KV_RANK]
        aux_ref = tail_ref.at[:, MLA_KV_RANK:MLA_KV_RANK + LANES]
        cq_ref = tail_ref.at[:, MLA_KV_RANK + LANES:]
        za_ref, dq_ref, dk_ref, dv_ref, zb_ref = [att_ref.at[:, i * GROUP_W:(i + 1) * GROUP_W] for i in range(5)]
    else:
        cq_ref, ckv_ref, aux_ref, dq_ref, dk_ref, dv_ref, zb_ref, za_ref = [next(it) for _ in range(8)]
    (yc_ref, x_ref, mod_ref, wuq_ref, wukv_ref, wout_ref, gq_ref, gkv_ref, gdn_ref,
     lam_ref) = [next(it) for _ in range(10)]
    if is_last:
        gfin_ref = next(it)
    if has_ctx:
        cosq_ref, sinq_ref, cosk_ref, sink_ref, cckv_ref, ckr_ref, cdk_ref, cdv_ref = [next(it) for _ in range(8)]
    else:
        for _ in range(n_alias):
            next(it)
    xo_ref = next(it)
    if is_last:
        yfin_ref = next(it)
    if not has_ctx:
        ckvn_ref, kro_ref = [next(it) for _ in range(2)]
    ka_s, va_s, kb_s, vb_s, ycat_s = [next(it) for _ in range(5)]
    first = not has_ctx and n_alias == 0
    if first:
        ckvn_all, kro_all = ckvn_ref, kro_ref
        ckvn_ref, kro_ref = ckvn_ref.at[0], kro_ref.at[0]

    qi = pl.program_id(1)

    @pl.when(qi == 0)
    def _build_keys():
        wukv = wukv_ref[...]
        ckv_n = _rms(ckv_ref[...], gkv_ref[...])
        aux = aux_ref[...]
        if not has_ctx:
            ckvn_ref[...] = ckv_n
            kro_ref[...] = aux[:, :MLA_ROPE]
        if first:
            for r in (ckvn_all, kro_all):
                r[1:] = jnp.zeros((DEPTH - 1,) + r.shape[1:], F32)
        kv = jnp.dot(ckv_n.astype(BF16), wukv, preferred_element_type=F32)
        lane = lax.broadcasted_iota(jnp.int32, aux.shape, 1)
        kr = _rope_tile(aux, cosk_ref[...], sink_ref[...]) if has_ctx else aux
        kr = jnp.where(lane < MLA_ROPE, kr, 0.0).astype(BF16)
        for h in range(MLA_HEADS):
            ka_s[0:T, 2 * h * HEAD_W:(2 * h + 1) * HEAD_W] = kv[:, h * HEAD_W:(h + 1) * HEAD_W].astype(BF16)
            ka_s[0:T, (2 * h + 1) * HEAD_W:(2 * h + 2) * HEAD_W] = kr
        if pv_t:
            row16 = lax.broadcasted_iota(jnp.int32, (VT_ROWS - HEAD_W, Tk), 0)
            ones_rows = jnp.where(row16 == 0, 1.0, 0.0).astype(BF16)
            for h in range(MLA_HEADS):
                hs = slice(h * HEAD_W, (h + 1) * HEAD_W)
                va_s[h, 0:HEAD_W, 0:T] = kv[:, GROUP_W + h * HEAD_W:GROUP_W + (h + 1) * HEAD_W].T.astype(BF16)
                va_s[h, HEAD_W:, :] = ones_rows
                vb_s[h, 0:HEAD_W, 0:T] = dv_ref[:, hs].astype(F32).T.astype(BF16)
                vb_s[h, HEAD_W:, :] = ones_rows
        else:
            lane_k = lax.broadcasted_iota(jnp.int32, (Tk, LANES), 1)
            ones_cols = jnp.where(lane_k == 0, 1.0, 0.0).astype(BF16)
            for h in range(MLA_HEADS):
                hs = slice(h * HEAD_W, (h + 1) * HEAD_W)
                va_s[0:T, 2 * h * HEAD_W:(2 * h + 1) * HEAD_W] = kv[:, GROUP_W + h * HEAD_W:GROUP_W + (h + 1) * HEAD_W].astype(BF16)
                va_s[:, (2 * h + 1) * HEAD_W:(2 * h + 2) * HEAD_W] = ones_cols
                vb_s[0:T, 2 * h * HEAD_W:(2 * h + 1) * HEAD_W] = dv_ref[:, hs]
                vb_s[:, (2 * h + 1) * HEAD_W:(2 * h + 2) * HEAD_W] = ones_cols
        if has_ctx:
            for h in range(DIFF_HEADS):
                hs = slice(h * HEAD_W, (h + 1) * HEAD_W)
                kb_s[0:T, hs] = _rope_tile(dk_ref[:, hs].astype(F32), cosk_ref[...], sink_ref[...]).astype(BF16)
            kvc = jnp.dot(cckv_ref[...].astype(BF16), wukv, preferred_element_type=F32)
            ckr = ckr_ref[...].astype(BF16)
            for h in range(MLA_HEADS):
                ka_s[T:Tk, 2 * h * HEAD_W:(2 * h + 1) * HEAD_W] = kvc[:, h * HEAD_W:(h + 1) * HEAD_W].astype(BF16)
                ka_s[T:Tk, (2 * h + 1) * HEAD_W:(2 * h + 2) * HEAD_W] = ckr
            for h in range(DIFF_HEADS):
                hs = slice(h * HEAD_W, (h + 1) * HEAD_W)
                kb_s[T:Tk, hs] = cdk_ref[pl.ds(h, PAST_LEN, stride=DIFF_HEADS), :].astype(BF16)
                v_c = kvc[:, GROUP_W + h * HEAD_W:GROUP_W + (h + 1) * HEAD_W]
                vd_c = cdv_ref[pl.ds(h, PAST_LEN, stride=DIFF_HEADS), :]
                if pv_t:
                    va_s[h, 0:HEAD_W, T:Tk] = v_c.T.astype(BF16)
                    vb_s[h, 0:HEAD_W, T:Tk] = vd_c.T.astype(BF16)
                else:
                    va_s[T:Tk, 2 * h * HEAD_W:(2 * h + 1) * HEAD_W] = v_c.astype(BF16)
                    vb_s[T:Tk, 2 * h * HEAD_W:(2 * h + 1) * HEAD_W] = vd_c.astype(BF16)
        else:
            kb_s[0:T, :] = dk_ref[...]

    nt = (((1,), (1,)), ((), ()))
    tq = cq_ref.shape[0]

    maps = []
    qa = jnp.dot(_rms(cq_ref[...], gq_ref[...]).astype(BF16), wuq_ref[...], preferred_element_type=F32)
    for h in range(MLA_HEADS):
        q_nope = qa[:, 2 * h * HEAD_W:(2 * h + 1) * HEAD_W]
        q_rope = qa[:, (2 * h + 1) * HEAD_W:(2 * h + 2) * HEAD_W]
        if has_ctx:
            q_rope = _rope_tile(q_rope, cosq_ref[...], sinq_ref[...])
        q_h = (jnp.concatenate([q_nope, q_rope], axis=-1) * (MLA_SCALE * LOG2E)).astype(BF16)
        two = slice(2 * h * HEAD_W, (2 * h + 2) * HEAD_W)
        v_load = functools.partial(lambda hh: va_s[hh], h) if pv_t else functools.partial(lambda sl: va_s[:, sl], two)
        maps.append((q_h, functools.partial(lambda sl: ka_s[:, sl], two), v_load))
    lane_q = lax.broadcasted_iota(jnp.int32, (tq, HEAD_W), 1)
    for h in range(DIFF_HEADS):
        hs = slice(h * HEAD_W, (h + 1) * HEAD_W)
        q_h = dq_ref[:, hs].astype(F32)
        if has_ctx:
            q_h = _rope_tile(q_h, cosq_ref[...], sinq_ref[...])
        q_h = q_h * (DIFF_SCALE * LOG2E)
        two = slice(2 * h * HEAD_W, (2 * h + 2) * HEAD_W)
        v_load = functools.partial(lambda hh: vb_s[hh], h) if pv_t else functools.partial(lambda sl: vb_s[:, sl], two)
        for q_m in (jnp.where(lane_q < DIFF_D, q_h, 0.0), jnp.where(lane_q >= DIFF_D, q_h, 0.0)):
            maps.append((q_m.astype(BF16), functools.partial(lambda sl: kb_s[:, sl], hs), v_load))

    def scores(i):
        if pv_t:
            return lax.dot_general(maps[i][1](), maps[i][0], nt, preferred_element_type=F32)
        return lax.dot_general(maps[i][0], maps[i][1](), nt, preferred_element_type=F32)

    def softmax_pv(i, s):
        if pv_t:
            e = jnp.exp2(s - jnp.max(s, axis=0, keepdims=True))
            pv = jnp.dot(maps[i][2](), e.astype(BF16), preferred_element_type=F32)
            return (pv[0:HEAD_W, :] * (1.0 / pv[HEAD_W:HEAD_W + 1, :])).T
        e = jnp.exp2(s - jnp.max(s, axis=-1, keepdims=True))
        pv = jnp.dot(e.astype(BF16), maps[i][2](), preferred_element_type=F32)
        return pv[:, :HEAD_W] * (1.0 / pv[:, HEAD_W:HEAD_W + 1])

    n_maps = len(maps)
    ahead = min(ATTN_AHEAD_CTX if not has_ctx else ATTN_AHEAD_LAT, n_maps)
    pending = {i: scores(i) for i in range(ahead)}
    outs = []
    for i in range(n_maps):
        if i + ahead < n_maps:
            pending[i + ahead] = scores(i + ahead)
        outs.append(softmax_pv(i, pending.pop(i)))

    for h in range(MLA_HEADS):
        hs = slice(h * HEAD_W, (h + 1) * HEAD_W)
        ycat_s[:, hs] = (outs[h] * za_ref[:, hs].astype(F32)).astype(BF16)

    lp = lam_ref[...]
    lf = l_ref[0].astype(F32)
    lam_init = 0.8 - 0.6 * jnp.exp(jnp.full((1, 1), -0.3, F32) * lf)
    lam = (jnp.exp(jnp.sum(lp[0:1, :] * lp[1:2, :], axis=-1, keepdims=True))
           - jnp.exp(jnp.sum(lp[2:3, :] * lp[3:4, :], axis=-1, keepdims=True)) + lam_init)
    for h in range(DIFF_HEADS):
        hs = slice(h * HEAD_W, (h + 1) * HEAD_W)
        o1, o2 = outs[MLA_HEADS + 2 * h], outs[MLA_HEADS + 2 * h + 1]
        o = _rms(o1 - lam * o2, gdn_ref[...]) * (1.0 - lam_init)
        ycat_s[:, GROUP_W + h * HEAD_W:GROUP_W + (h + 1) * HEAD_W] = (o * zb_ref[:, hs].astype(F32)).astype(BF16)

    ycat_s[:, 2 * GROUP_W:] = yc_ref[...]
    y = jnp.dot(ycat_s[...], wout_ref[...], preferred_element_type=F32)
    x_new = x_ref[...] + mod_ref[:, 2 * D_MODEL:] * y
    xo_ref[...] = x_new
    if is_last:
        yfin_ref[...] = _rms(x_new, gfin_ref[...])


def _attn_call(l_arr, main, tail, yc, x2, mod_all, w, T, has_ctx, is_last, rope=None, ctx=None, side_bufs=None):
    n_tok = main.shape[0]
    B = n_tok // T
    tq = min(ATTN_TQ, T)
    nq = T // tq
    Tk = T + (PAST_LEN if has_ctx else 0)
    main3 = main.reshape(B, T, MAIN_W)
    tail3 = tail.reshape(B, T, TAIL_W)
    x3 = x2.reshape(B, T, D_MODEL)

    def tile(c, width):
        return pl.BlockSpec((None, tq, width), lambda b, q, l, c=c: (b, q, c))

    def full(c, width):
        return pl.BlockSpec((None, T, width), lambda b, q, l, c=c: (b, 0, c))

    def wspec(shape):
        return pl.BlockSpec((None,) + shape, lambda b, q, l: (l[0],) + (0,) * len(shape))

    if has_ctx:
        mod_map = lambda b, q, l: (l[0], 1 + b, 0, 0)
    else:
        mod_map = lambda b, q, l: (l[0], 0, 0, 0)

    merged = nq == 1
    if merged:
        assert (C_ZA, C_ZB) == (0, 4)
        in_specs = [pl.BlockSpec((None, T, TAIL_W), lambda b, q, l: (b, 0, 0)),
                    pl.BlockSpec((None, T, MAIN_W // 2), lambda b, q, l: (b, 0, 0))]
        args = [tail3, main3]
    else:
        in_specs = [tile(1, MLA_Q_RANK),
                    full(0, MLA_KV_RANK),
                    full(2, LANES),
                    tile(C_DQ, GROUP_W), full(C_DK, GROUP_W), full(C_DV, GROUP_W),
                    tile(C_ZB, GROUP_W), tile(C_ZA, GROUP_W)]
        args = [tail3, tail3, tail3, main3, main3, main3, main3, main3]
    in_specs += [pl.BlockSpec((None, tq, GROUP_W), lambda b, q, l: (b, q, 0)),
                 pl.BlockSpec((None, tq, D_MODEL), lambda b, q, l: (b, q, 0)),
                 pl.BlockSpec((None, None, 1, 3 * D_MODEL), mod_map),
                 wspec((MLA_Q_RANK, 2 * GROUP_W)), wspec((MLA_KV_RANK, 2 * GROUP_W)),
                 wspec((3 * GROUP_W, D_MODEL)),
                 wspec((1, MLA_Q_RANK)), wspec((1, MLA_KV_RANK)), wspec((1, HEAD_W)),
                 wspec((4, DIFF_D))]
    args += [yc, x3, mod_all, w['wuq'], w['wukv'], w['wout'], w['gq'], w['gkv'], w['gdn'], w['lam']]
    if is_last:
        in_specs.append(pl.BlockSpec((1, D_MODEL), lambda b, q, l: (0, 0)))
        args.append(w['gfin'])
    if has_ctx:
        cos_t, sin_t = rope
        in_specs += [pl.BlockSpec((tq, LANES), lambda b, q, l: (q, 0)),
                     pl.BlockSpec((tq, LANES), lambda b, q, l: (q, 0)),
                     pl.BlockSpec((T, LANES), lambda b, q, l: (0, 0)),
                     pl.BlockSpec((T, LANES), lambda b, q, l: (0, 0)),
                     pl.BlockSpec((None, None, PAST_LEN, MLA_KV_RANK), lambda b, q, l: (b, l[0], 0, 0)),
                     pl.BlockSpec((None, None, PAST_LEN, LANES), lambda b, q, l: (b, l[0], 0, 0)),
                     pl.BlockSpec((None, None, PAST_LEN * DIFF_HEADS, HEAD_W), lambda b, q, l: (b, l[0], 0, 0)),
                     pl.BlockSpec((None, None, PAST_LEN * DIFF_HEADS, HEAD_W), lambda b, q, l: (b, l[0], 0, 0))]
        args += [cos_t, sin_t, cos_t, sin_t, ctx['ckv'], ctx['krope'], ctx['dk'], ctx['dv']]
    out_shape = [jax.ShapeDtypeStruct((B, T, D_MODEL), F32)]
    out_specs = [pl.BlockSpec((None, tq, D_MODEL), lambda b, q, l: (b, q, 0))]
    if is_last:
        out_shape.append(jax.ShapeDtypeStruct((B, T, D_MODEL), F32))
        out_specs.append(pl.BlockSpec((None, tq, D_MODEL), lambda b, q, l: (b, q, 0)))
    aliases = {}
    n_alias = 0
    if not has_ctx:
        if side_bufs is not None:
            n_alias = len(side_bufs)
            n_in = 1 + len(args)
            aliases = {n_in + i: len(out_shape) + i for i in range(n_alias)}
            in_specs += [pl.BlockSpec(memory_space=pl.ANY)] * n_alias
            args += list(side_bufs)
            out_specs += [pl.BlockSpec((None, None, T, MLA_KV_RANK), lambda b, q, l: (b, l[0], 0, 0)),
                          pl.BlockSpec((None, None, T, MLA_ROPE), lambda b, q, l: (b, l[0], 0, 0))]
        else:
            assert nq == 1
            out_specs += [pl.BlockSpec((None, DEPTH, T, MLA_KV_RANK), lambda b, q, l: (b, 0, 0, 0)),
                          pl.BlockSpec((None, DEPTH, T, MLA_ROPE), lambda b, q, l: (b, 0, 0, 0))]
        out_shape += [jax.ShapeDtypeStruct((B, DEPTH, T, MLA_KV_RANK), F32),
                      jax.ShapeDtypeStruct((B, DEPTH, T, MLA_ROPE), F32)]
    if Tk <= ATTN_PVT_MAX_KEYS:
        v_scratch = pltpu.VMEM((MLA_HEADS, VT_ROWS, Tk), BF16)
    else:
        v_scratch = pltpu.VMEM((Tk, 2 * GROUP_W), BF16)
    grid_spec = pltpu.PrefetchScalarGridSpec(
        num_scalar_prefetch=1, grid=(B, nq), in_specs=in_specs, out_specs=out_specs,
        scratch_shapes=[pltpu.VMEM((Tk, 2 * GROUP_W), BF16),
                        v_scratch,
                        pltpu.VMEM((Tk, GROUP_W), BF16),
                        v_scratch,
                        pltpu.VMEM((tq, 3 * GROUP_W), BF16)])
    return pl.pallas_call(
        functools.partial(_attn_kernel, has_ctx, is_last, merged, n_alias, T),
        grid_spec=grid_spec,
        out_shape=out_shape,
        input_output_aliases=aliases,
        compiler_params=pltpu.CompilerParams(dimension_semantics=("arbitrary", "arbitrary"),
                                             vmem_limit_bytes=VMEM_LIMIT),
        name="attn_lat" if has_ctx else "attn_ctx",
    )(l_arr, *args)


def _rope_tables(n_tok):
    n_freq = MLA_ROPE // 4
    inv = ROPE_THETA ** (-jnp.arange(n_freq, dtype=F32) / n_freq)
    n_rows = n_tok // GRID_W
    rowp = jnp.repeat(jnp.arange(n_rows, dtype=F32), GRID_W)
    colp = jnp.tile(jnp.arange(GRID_W, dtype=F32), n_rows)
    ang = jnp.concatenate([rowp[:, None] * inv, colp[:, None] * inv], axis=-1)
    cos, sin = jnp.cos(ang), jnp.sin(ang)
    cos64 = jnp.concatenate([cos, cos], axis=-1)
    sin64 = jnp.concatenate([-sin, sin], axis=-1)
    return jnp.concatenate([cos64, cos64], axis=-1), jnp.concatenate([sin64, sin64], axis=-1)


def _prep_weights(W_in, W_uq, W_ukv, W_out):
    w_in_t = jnp.swapaxes(W_in, 1, 2).astype(BF16)
    wq = W_uq.reshape(DEPTH, MLA_Q_RANK, MLA_HEADS, MLA_NOPE + MLA_ROPE)
    wq = jnp.pad(wq, ((0, 0), (0, 0), (0, 0), (0, 2 * HEAD_W - MLA_NOPE - MLA_ROPE)))
    wuq_r = wq.reshape(DEPTH, MLA_Q_RANK, MLA_HEADS * 2 * HEAD_W).astype(BF16)
    wkv = W_ukv.reshape(DEPTH, MLA_KV_RANK, MLA_HEADS, 2, HEAD_W)
    wukv_r = jnp.swapaxes(wkv, 2, 3).reshape(DEPTH, MLA_KV_RANK, 2 * GROUP_W).astype(BF16)
    return w_in_t, wuq_r, wukv_r, W_out.astype(BF16)


def kernel(x_prompt, x_sample, cache_mla_ckv, cache_mla_krope, cache_diff_k, cache_diff_v, state_mlstm_C, state_mlstm_n, state_mlstm_m, c, c_ctx, g_norm, W_mod, b_mod, W_in, mla_q_norm, W_uq, mla_kv_norm, W_ukv, diff_lambda, diff_norm, ml_conv, ml_gate_b, ml_norm, W_out, g_final):
    Bc, Tc, _ = x_prompt.shape
    Bs, Ts, _ = x_sample.shape

    w_in_t, wuq_r, wukv_r, wout_r = _prep_weights(W_in, W_uq, W_ukv, W_out)
    w = {'wuq': wuq_r, 'wukv': wukv_r, 'wout': wout_r,
         'gq': mla_q_norm.reshape(DEPTH, 1, MLA_Q_RANK), 'gkv': mla_kv_norm.reshape(DEPTH, 1, MLA_KV_RANK),
         'gdn': diff_norm.reshape(DEPTH, 1, 2 * DIFF_D), 'lam': diff_lambda,
         'gfin': g_final.reshape(1, D_MODEL)}
    g_norm3 = g_norm.reshape(DEPTH, 1, D_MODEL)
    ml_norm2 = ml_norm.reshape(DEPTH, 1, GROUP_W)
    gate_row = jnp.pad(ml_gate_b.reshape(DEPTH, 1, N_ML_GATES),
                       ((0, 0), (0, 0), (GATE_LANE0, LANES - GATE_LANE0 - N_ML_GATES)))

    cc = jnp.concatenate([c_ctx[None, :], c, jnp.zeros((8 - 1 - Bs, D_MODEL), F32)], axis=0)
    mod_all = _mod_call(cc, W_mod, b_mod).reshape(DEPTH, 8, 1, 3 * D_MODEL)

    dkv = cnm = ckr = None
    x2 = x_prompt.reshape(Bc * Tc, D_MODEL)
    y_prompt = None
    for l in range(DEPTH):
        l_arr = jnp.full((1,), l, jnp.int32)
        main, tail, *dkv = _proj_call(l_arr, x2, mod_all, g_norm3, w_in_t, Tc, False, dkv)
        yc, *cnm = _mlstm_call(l_arr, main, tail, ml_conv, gate_row, ml_norm2, Tc, False, side_bufs=cnm)
        outs = _attn_call(l_arr, main, tail, yc, x2, mod_all, w, Tc, False, l == DEPTH - 1, side_bufs=ckr)
        ckr = outs[-2:]
        dk_o, dv_o = dkv
        c_o, n_o, m_o = cnm
        if l == DEPTH - 1:
            x3, y_prompt, ckvn, kro = outs
        else:
            x3, ckvn, kro = outs
        x2 = x3.reshape(Bc * Tc, D_MODEL)
    side_outs = (ckvn, kro,
                 dk_o.reshape(Bc, DEPTH, Tc, DIFF_HEADS, 2 * DIFF_D),
                 dv_o.reshape(Bc, DEPTH, Tc, DIFF_HEADS, 2 * DIFF_D),
                 c_o.reshape(Bc, DEPTH, 2, ML_HEADS, ML_DK, HEAD_W),
                 n_o.reshape(Bc, DEPTH, 2, ML_HEADS, ML_DK),
                 m_o[:, :, :, 0].reshape(Bc, DEPTH, 2, ML_HEADS))

    rope = _rope_tables(Ts)
    ctx = {'ckv': cache_mla_ckv,
           'krope': jnp.pad(cache_mla_krope, ((0, 0), (0, 0), (0, 0), (0, LANES - MLA_ROPE))),
           'dk': cache_diff_k.reshape(Bs, DEPTH, PAST_LEN * DIFF_HEADS, HEAD_W),
           'dv': cache_diff_v.reshape(Bs, DEPTH, PAST_LEN * DIFF_HEADS, HEAD_W)}
    ctx_state = (state_mlstm_C.reshape(Bs, DEPTH, 2 * ML_HEADS, ML_DK, HEAD_W),
                 state_mlstm_n.reshape(Bs, DEPTH, 2 * ML_HEADS, ML_DK),
                 jnp.broadcast_to(state_mlstm_m.reshape(Bs, DEPTH, 2 * ML_HEADS, 1),
                                  (Bs, DEPTH, 2 * ML_HEADS, LANES)))
    x2 = x_sample.reshape(Bs * Ts, D_MODEL)
    y_sample = None
    for l in range(DEPTH):
        l_arr = jnp.full((1,), l, jnp.int32)
        main, tail = _proj_call(l_arr, x2, mod_all, g_norm3, w_in_t, Ts, True)
        (yc,) = _mlstm_call(l_arr, main, tail, ml_conv, gate_row, ml_norm2, Ts, True, ctx_state)
        outs = _attn_call(l_arr, main, tail, yc, x2, mod_all, w, Ts, True, l == DEPTH - 1, rope, ctx)
        if l == DEPTH - 1:
            x3, y_sample = outs
        else:
            (x3,) = outs
        x2 = x3.reshape(Bs * Ts, D_MODEL)

    return (y_prompt, y_sample, *side_outs)
```

```python
import functools
import math

import jax
import jax.numpy as jnp
import numpy as np
from jax import lax
from jax.experimental import pallas as pl
from jax.experimental.pallas import tpu as pltpu

F32 = jnp.float32
BF16 = jnp.bfloat16

D_MODEL = 1024
DEPTH = 4
PAST_LEN = 256
GRID_W = 64
GROUP_W = 512
MLA_HEADS = 4
MLA_NOPE = 128
MLA_ROPE = 64
MLA_Q_RANK = 384
MLA_KV_RANK = 256
DIFF_HEADS = 4
DIFF_D = 64
ML_HEADS = 4
ML_DK = 128
N_ML_GATES = 16
ROPE_THETA = 10000.0
NORM_EPS = 1e-6
MLA_SCALE = (MLA_NOPE + MLA_ROPE) ** -0.5
DIFF_SCALE = DIFF_D ** -0.5
LOG2E = math.log2(math.e)

LANES = 128
HEAD_W = 128
ML_CHUNK = 256
VT_ROWS = 144
PROJ_TM = 512
ATTN_TQ = 512
ATTN_PVT_MAX_KEYS = 256
ATTN_AHEAD_CTX = 12
ATTN_AHEAD_LAT = 2
MAIN_W = 10 * GROUP_W
TAIL_W = 768
N_IN = 5840
VMEM_LIMIT = 56 * 1024 * 1024

C_ZA, C_DQ, C_DK, C_DV, C_ZB, C_MQ, C_MK, C_MV, C_MO, C_ZC = range(10)

_IN_SIZES = (MLA_Q_RANK, MLA_KV_RANK, MLA_ROPE, GROUP_W, GROUP_W, GROUP_W, GROUP_W, GROUP_W,
             GROUP_W, GROUP_W, GROUP_W, GROUP_W, GROUP_W, N_ML_GATES)
_IN_OFF = np.concatenate([[0], np.cumsum(_IN_SIZES)])
(_O_CQ, _O_CKV, _O_KR, _O_ZA, _O_DQ, _O_DK, _O_DV, _O_ZB, _O_MQ, _O_MK, _O_MV, _O_MO, _O_ZC,
 _O_MG) = [int(v) for v in _IN_OFF[:-1]]
GATE_LANE0 = MLA_ROPE


def _rms(x, g):
    ms = jnp.mean(x * x, axis=-1, keepdims=True)
    return x * lax.rsqrt(ms + NORM_EPS) * g


def _sigmoid(x):
    return 0.5 + 0.5 * jnp.tanh(0.5 * x)


def _silu(x):
    hx = 0.5 * x
    return hx + hx * jnp.tanh(hx)


def _log_sigmoid(x):
    return jnp.minimum(x, 0.0) - jnp.log1p(jnp.exp(-jnp.abs(x)))


def _swap32(x):
    lane = lax.broadcasted_iota(jnp.int32, x.shape, 1)
    fwd = pltpu.roll(x, LANES - 32, 1)
    bwd = pltpu.roll(x, 32, 1)
    return jnp.where((lane % 64) < 32, fwd, bwd)


def _rope_tile(x, cos, sin):
    return x * cos + _swap32(x) * sin


def _mod_kernel(c_ref, w_ref, b_ref, o_ref):
    a = _silu(c_ref[...]).astype(BF16)
    o_ref[...] = jnp.dot(a, w_ref[...].astype(BF16), preferred_element_type=F32) + b_ref[...]


def _mod_call(cc, W_mod, b_mod):
    tn = 1024
    return pl.pallas_call(
        _mod_kernel,
        grid=(DEPTH, 3 * D_MODEL // tn),
        in_specs=[pl.BlockSpec((8, D_MODEL), lambda l, j: (0, 0)),
                  pl.BlockSpec((None, D_MODEL, tn), lambda l, j: (l, 0, j)),
                  pl.BlockSpec((None, 1, tn), lambda l, j: (l, 0, j))],
        out_specs=pl.BlockSpec((None, 8, tn), lambda l, j: (l, 0, j)),
        out_shape=jax.ShapeDtypeStruct((DEPTH, 8, 3 * D_MODEL), F32),
        compiler_params=pltpu.CompilerParams(dimension_semantics=("arbitrary", "arbitrary"),
                                             vmem_limit_bytes=VMEM_LIMIT),
        name="mod",
    )(cc, W_mod, b_mod.reshape(DEPTH, 1, 3 * D_MODEL))


def _proj_kernel(has_side, n_alias, T, *refs):
    l_ref, x_ref, mod_ref, g_ref, w_ref = refs[:5]
    refs = refs[5 + n_alias:]
    if has_side:
        main_ref, tail_ref, dk_ref, dv_ref, wt_s = refs
    else:
        main_ref, tail_ref, wt_s = refs
    del l_ref
    if has_side and n_alias == 0:
        for r in (dk_ref, dv_ref):
            r[:, 1:] = jnp.zeros((r.shape[0], DEPTH - 1) + r.shape[2:], F32)
        dk_ref, dv_ref = dk_ref.at[:, 0], dv_ref.at[:, 0]
    tm = x_ref.shape[0]
    nt = (((1,), (1,)), ((), ()))

    @pl.when(pl.program_id(0) == 0)
    def _gather_tail_rows():
        wt_s[0:MLA_ROPE, :] = w_ref[_O_KR:_O_KR + MLA_ROPE, :]
        wt_s[MLA_ROPE:MLA_ROPE + N_ML_GATES, :] = w_ref[_O_MG:_O_MG + N_ML_GATES, :]
        wt_s[MLA_ROPE + N_ML_GATES:LANES, :] = jnp.zeros((LANES - MLA_ROPE - N_ML_GATES, D_MODEL), BF16)
        wt_s[LANES:, :] = w_ref[_O_CQ:_O_CQ + MLA_Q_RANK, :]

    mod = mod_ref[...]
    y = _rms(x_ref[...], g_ref[...])
    h = (y * (1.0 + mod[:, D_MODEL:2 * D_MODEL]) + mod[:, :D_MODEL]).astype(BF16)
    for s in range(MAIN_W // GROUP_W):
        cols = slice(s * GROUP_W, (s + 1) * GROUP_W)
        acc = lax.dot_general(h, w_ref[_O_ZA + s * GROUP_W:_O_ZA + (s + 1) * GROUP_W, :], nt,
                              preferred_element_type=F32)
        if s in (C_ZA, C_ZB, C_ZC):
            main_ref[:, cols] = _silu(acc).astype(BF16)
        elif s == C_MO:
            main_ref[:, cols] = _sigmoid(acc).astype(BF16)
        else:
            main_ref[:, cols] = acc.astype(BF16)
        if has_side and s in (C_DK, C_DV):
            side_ref = dk_ref if s == C_DK else dv_ref
            for b in range(tm // T):
                for hd in range(DIFF_HEADS):
                    side_ref[b, pl.ds(hd, T, stride=DIFF_HEADS), :] = (
                        acc[b * T:(b + 1) * T, hd * HEAD_W:(hd + 1) * HEAD_W])
    tail_ref[:, 0:MLA_KV_RANK] = lax.dot_general(h, w_ref[_O_CKV:_O_CKV + MLA_KV_RANK, :], nt,
                                                 preferred_element_type=F32)
    tail_ref[:, MLA_KV_RANK:] = lax.dot_general(h, wt_s[...], nt, preferred_element_type=F32)


def _proj_call(l_arr, x2, mod_all, g_norm3, w_in_t, T, has_ctx, side_bufs=None):
    n_tok = x2.shape[0]
    tm = PROJ_TM
    has_side = not has_ctx
    if has_ctx:
        mod_map = lambda i, l: (l[0], 1 + (i * tm) // T, 0, 0)
    else:
        mod_map = lambda i, l: (l[0], 0, 0, 0)
    in_specs = [pl.BlockSpec((tm, D_MODEL), lambda i, l: (i, 0)),
                pl.BlockSpec((None, None, 1, 3 * D_MODEL), mod_map),
                pl.BlockSpec((None, 1, D_MODEL), lambda i, l: (l[0], 0, 0)),
                pl.BlockSpec((None, N_IN, D_MODEL), lambda i, l: (l[0], 0, 0), pipeline_mode=pl.Buffered(1))]
    args = [x2, mod_all, g_norm3, w_in_t]
    out_shape = [jax.ShapeDtypeStruct((n_tok, MAIN_W), BF16), jax.ShapeDtypeStruct((n_tok, TAIL_W), F32)]
    out_specs = [pl.BlockSpec((tm, MAIN_W), lambda i, l: (i, 0)), pl.BlockSpec((tm, TAIL_W), lambda i, l: (i, 0))]
    aliases = {}
    n_alias = 0
    if has_side:
        bt = tm // T
        out_shape += [jax.ShapeDtypeStruct((n_tok // T, DEPTH, T * DIFF_HEADS, HEAD_W), F32)] * 2
        if side_bufs is not None:
            n_alias = len(side_bufs)
            aliases = {1 + len(args) + i: 2 + i for i in range(n_alias)}
            in_specs += [pl.BlockSpec(memory_space=pl.ANY)] * n_alias
            args += list(side_bufs)
            out_specs += [pl.BlockSpec((bt, None, T * DIFF_HEADS, HEAD_W), lambda i, l: (i, l[0], 0, 0))] * 2
        else:
            out_specs += [pl.BlockSpec((bt, DEPTH, T * DIFF_HEADS, HEAD_W), lambda i, l: (i, 0, 0, 0))] * 2
    grid_spec = pltpu.PrefetchScalarGridSpec(
        num_scalar_prefetch=1, grid=(n_tok // tm,), in_specs=in_specs, out_specs=out_specs,
        scratch_shapes=[pltpu.VMEM((TAIL_W - MLA_KV_RANK, D_MODEL), BF16)])
    return pl.pallas_call(
        functools.partial(_proj_kernel, has_side, n_alias, T),
        grid_spec=grid_spec,
        out_shape=out_shape,
        input_output_aliases=aliases,
        compiler_params=pltpu.CompilerParams(dimension_semantics=("arbitrary",),
                                             vmem_limit_bytes=VMEM_LIMIT),
        name="proj_lat" if has_ctx else "proj_ctx",
    )(l_arr, *args)


def _mlstm_kernel(has_ctx, T, n_alias, *refs):
    nc = T // ML_CHUNK
    L = ML_CHUNK
    use_inter = has_ctx or nc > 1
    it = iter(refs)
    l_ref = next(it)
    ml_ref, aux_ref, convw_ref, gb_ref, mln_ref = [next(it) for _ in range(5)]
    mq_ref, mk_ref, mv_ref, mo_ref, zc_ref = [ml_ref.at[:, i * GROUP_W:(i + 1) * GROUP_W] for i in range(5)]
    if has_ctx:
        c0_ref, n0_ref, m0_ref = [next(it) for _ in range(3)]
    else:
        for _ in range(n_alias):
            next(it)
    yc_ref = next(it)
    if not has_ctx:
        cout_ref, nout_ref, mout_ref = [next(it) for _ in range(3)]
    q_s, k_s, vt_s, hf_s, hb_s, caug_s, m_s, gt_s = [next(it) for _ in range(8)]
    del l_ref
    if not has_ctx and n_alias == 0:
        for r in (cout_ref, nout_ref, mout_ref):
            r[1:] = jnp.zeros((DEPTH - 1,) + r.shape[1:], F32)
        cout_ref, nout_ref, mout_ref = cout_ref.at[0], nout_ref.at[0], mout_ref.at[0]

    row_b = lax.broadcasted_iota(jnp.int32, (L, LANES), 0)
    convw = convw_ref[...]
    halo = 16
    for c in range(nc):
        cs = slice(c * L, (c + 1) * L)
        for j in range(2 * GROUP_W // LANES):
            src = mq_ref if j < GROUP_W // LANES else mk_ref
            ls = slice((j % (GROUP_W // LANES)) * LANES, (j % (GROUP_W // LANES) + 1) * LANES)
            u = src[cs, ls].astype(F32)
            prev = src[c * L - halo:c * L, ls].astype(F32)[halo - 1:halo, :] if c > 0 else 0.0
            nxt = src[(c + 1) * L:(c + 1) * L + halo, ls].astype(F32)[0:1, :] if c < nc - 1 else 0.0
            up, un = pltpu.roll(u, 1, 0), pltpu.roll(u, L - 1, 0)
            up = jnp.concatenate([jnp.where(row_b[0:8] == 0, prev, up[0:8]), up[8:]], axis=0)
            un = jnp.concatenate([un[:L - 8], jnp.where(row_b[L - 8:] == L - 1, nxt, un[L - 8:])], axis=0)
            w3 = convw[:, j * LANES:(j + 1) * LANES]
            y = _silu(w3[0:1, :] * up + w3[1:2, :] * u + w3[2:3, :] * un)
            if j < GROUP_W // LANES:
                q_s[cs, ls] = (y * (ML_DK ** -0.5)).astype(BF16)
            else:
                k_s[cs, ls] = y
    row16 = lax.broadcasted_iota(jnp.int32, (VT_ROWS - HEAD_W, L), 0)
    ones_rows = jnp.where(row16 == 0, 1.0, 0.0).astype(BF16)
    g_t = (aux_ref[...] + gb_ref[...]).T
    for c in range(nc):
        cs = slice(c * L, (c + 1) * L)
        gt_s[c] = g_t[GATE_LANE0:GATE_LANE0 + N_ML_GATES, cs]
        for h in range(ML_HEADS):
            vt_s[c, h, 0:HEAD_W, :] = mv_ref[cs, h * HEAD_W:(h + 1) * HEAD_W].astype(F32).T.astype(BF16)
            vt_s[c, h, HEAD_W:, :] = ones_rows

    reps = L // LANES
    if has_ctx:
        n0 = n0_ref[...]
        m0 = m0_ref[...]
        m_s[0] = jnp.concatenate([m0] * reps, axis=1)
        m_s[1] = jnp.concatenate([pltpu.roll(m0, ML_HEADS, 0)] * reps, axis=1)
        row_n = lax.broadcasted_iota(jnp.int32, (VT_ROWS - HEAD_W, ML_DK), 0)
        for r in range(2 * ML_HEADS):
            caug_s[r, 0:HEAD_W, :] = c0_ref[r].T
            caug_s[r, HEAD_W:, :] = jnp.where(row_n == 0, n0[r:r + 1, :], 0.0)
    else:
        caug_s[...] = jnp.zeros_like(caug_s)
        m_s[...] = jnp.zeros_like(m_s)

    ri = lax.broadcasted_iota(jnp.int32, (L, L), 0)
    ci = lax.broadcasted_iota(jnp.int32, (L, L), 1)
    row_r = lax.broadcasted_iota(jnp.int32, (2 * ML_HEADS, L), 0)
    nt = (((1,), (1,)), ((), ()))

    def split3(x):
        hi = x.astype(BF16).astype(F32)
        mid = (x - hi).astype(BF16).astype(F32)
        return hi, mid, (x - hi - mid).astype(BF16).astype(F32)

    masks = ((ri <= ci), (ri >= ci))

    def gate_rows(d, c):
        fwd = d == 0
        tri_t = jnp.where(masks[d], 1.0, 0.0).astype(BF16)
        last = L - 1 if fwd else 0

        g8 = gt_s[c, d * 2 * ML_HEADS:(d + 1) * 2 * ML_HEADS, :]
        hi8, mid8, lo8 = split3(_log_sigmoid(g8))
        stack = jnp.concatenate([hi8, mid8, lo8, jnp.zeros_like(hi8)], axis=0).astype(BF16)
        part8 = jnp.dot(stack, tri_t, preferred_element_type=F32)
        bc8 = part8[0:8] + part8[8:16] + part8[16:24]
        bcs = pltpu.roll(bc8, ML_HEADS, 0)
        a8 = g8 - bcs
        a_n = jnp.concatenate([a8, jnp.zeros((LANES - 2 * ML_HEADS, L), F32)], axis=0).T
        cm8 = jnp.zeros((2 * ML_HEADS, L), F32)
        for h in range(ML_HEADS):
            col_max = jnp.max(jnp.where(masks[d], a_n[:, h:h + 1], -jnp.inf), axis=0, keepdims=True)
            cm8 = jnp.where(row_r == h, col_max, cm8)
        m8 = m_s[d]
        g_row = bcs + m8
        m_t = jnp.maximum(g_row, bcs + cm8)
        w_inter = jnp.exp(g_row - m_t)
        e_inv = jnp.exp(-m_t)
        c_row = bcs - m_t
        b_last = bcs[:, last:last + 1]
        m_new = m_t[:, last:last + 1]
        a_prev = jnp.exp(b_last + m8[:, 0:1] - m_new)
        w_s = jnp.exp(a8 + (b_last - m_new))
        m_s[d] = jnp.broadcast_to(m_new, (2 * ML_HEADS, L))
        return a_n, c_row, w_inter, e_inv, w_s, a_prev

    def do_pair(c_f, c_b):
        stats = (gate_rows(0, c_f), gate_rows(1, c_b))
        combos = [(d, h) for d in range(2) for h in range(ML_HEADS)]
        rows = []
        for c in (c_f, c_b):
            rows.append(slice(c * L, (c + 1) * L) if isinstance(c, int) else pl.ds(pl.multiple_of(c * L, L), L))
        chunk = (c_f, c_b)
        hsl = [slice(h * HEAD_W, (h + 1) * HEAD_W) for h in range(ML_HEADS)]
        qcs = [q_s[rows[d], hsl[h]] for d, h in combos]
        kcs = [k_s[rows[d], hsl[h]].astype(BF16) for d, h in combos]
        vts = [vt_s[chunk[d], h] for d, h in combos]
        n = range(len(combos))
        s_ts = [lax.dot_general(kcs[i], qcs[i], nt, preferred_element_type=F32) for i in n]
        sws = []
        for i, (d, h) in enumerate(combos):
            a_n, c_row = stats[d][0], stats[d][1]
            w_t = jnp.exp(jnp.where(masks[d], a_n[:, h:h + 1] + c_row[h:h + 1, :], -jnp.inf))
            sws.append((s_ts[i] * w_t).astype(BF16))
        nds = [jnp.dot(vts[i], sws[i], preferred_element_type=F32) for i in n]
        if use_inter:
            inters = [lax.dot_general(caug_s[d * ML_HEADS + h].astype(BF16), qcs[i], nt,
                                      preferred_element_type=F32) for i, (d, h) in enumerate(combos)]
            nds = [nds[i] + stats[d][2][h:h + 1, :] * inters[i] for i, (d, h) in enumerate(combos)]
        for i, (d, h) in enumerate(combos):
            inv = 1.0 / jnp.maximum(jnp.abs(nds[i][HEAD_W:HEAD_W + 1, :]), stats[d][3][h:h + 1, :])
            h_s = hf_s if d == 0 else hb_s
            h_s[rows[d], hsl[h]] = (nds[i][0:HEAD_W, :] * inv).T
        upds = [jnp.dot((vts[i].astype(F32) * stats[d][4][h:h + 1, :]).astype(BF16), kcs[i],
                        preferred_element_type=F32) for i, (d, h) in enumerate(combos)]
        for i, (d, h) in enumerate(combos):
            r = d * ML_HEADS + h
            caug_s[r] = upds[i] + stats[d][5][h:h + 1, :] * caug_s[r] if use_inter else upds[i]

    if nc == 1:
        do_pair(0, 0)
    else:
        def body(i, carry):
            do_pair(i, nc - 1 - i)
            return carry
        lax.fori_loop(0, nc, body, 0)

    mln = mln_ref[...]
    for c in range(nc):
        cs = slice(c * L, (c + 1) * L)
        for h in range(ML_HEADS):
            hs = slice(h * HEAD_W, (h + 1) * HEAD_W)
            hc = mo_ref[cs, hs].astype(F32) * (hf_s[cs, hs] + hb_s[cs, hs])
            yc_ref[cs, hs] = (_rms(hc, mln[:, hs]) * zc_ref[cs, hs].astype(F32)).astype(BF16)

    if not has_ctx:
        for r in range(2 * ML_HEADS):
            cout_ref[r] = caug_s[r, 0:HEAD_W, :].T
            nout_ref[r:r + 1, :] = caug_s[r, HEAD_W:HEAD_W + 1, :]
        for d in range(2):
            mout_ref[d * ML_HEADS:(d + 1) * ML_HEADS, :] = m_s[d, 0:ML_HEADS, 0:LANES]


def _mlstm_call(l_arr, main, tail, ml_conv, gate_row, ml_norm2, T, has_ctx, ctx_state=None, side_bufs=None):
    n_tok = main.shape[0]
    B = n_tok // T
    main3 = main.reshape(B, T, MAIN_W)
    tail3 = tail.reshape(B, T, TAIL_W)

    assert (C_MQ, C_ZC) == (5, 9)
    in_specs = [pl.BlockSpec((None, T, MAIN_W // 2), lambda b, l: (b, 0, 1)),
                pl.BlockSpec((None, T, LANES), lambda b, l: (b, 0, 2)),
                pl.BlockSpec((None, 3, 2 * GROUP_W), lambda b, l: (l[0], 0, 0)),
                pl.BlockSpec((None, 1, LANES), lambda b, l: (l[0], 0, 0)),
                pl.BlockSpec((None, 1, GROUP_W), lambda b, l: (l[0], 0, 0))]
    args = [main3, tail3, ml_conv, gate_row, ml_norm2]
    n_alias = 0 if (has_ctx or side_bufs is None) else len(side_bufs)
    out_shape = [jax.ShapeDtypeStruct((B, T, GROUP_W), BF16)]
    out_specs = [pl.BlockSpec((None, T, GROUP_W), lambda b, l: (b, 0, 0))]
    if has_ctx:
        c0, n0, m0 = ctx_state
        in_specs += [pl.BlockSpec((None, None, 2 * ML_HEADS, ML_DK, HEAD_W), lambda b, l: (b, l[0], 0, 0, 0)),
                     pl.BlockSpec((None, None, 2 * ML_HEADS, ML_DK), lambda b, l: (b, l[0], 0, 0)),
                     pl.BlockSpec((None, None, 2 * ML_HEADS, LANES), lambda b, l: (b, l[0], 0, 0))]
        args += [c0, n0, m0]
        aliases = {}
    else:
        aliases = {}
        out_shape += [jax.ShapeDtypeStruct((B, DEPTH, 2 * ML_HEADS, ML_DK, HEAD_W), F32),
                      jax.ShapeDtypeStruct((B, DEPTH, 2 * ML_HEADS, ML_DK), F32),
                      jax.ShapeDtypeStruct((B, DEPTH, 2 * ML_HEADS, LANES), F32)]
        if n_alias:
            aliases = {1 + len(args) + i: 1 + i for i in range(n_alias)}
            in_specs += [pl.BlockSpec(memory_space=pl.ANY)] * n_alias
            args += list(side_bufs)
            out_specs += [pl.BlockSpec((None, None, 2 * ML_HEADS, ML_DK, HEAD_W), lambda b, l: (b, l[0], 0, 0, 0)),
                          pl.BlockSpec((None, None, 2 * ML_HEADS, ML_DK), lambda b, l: (b, l[0], 0, 0)),
                          pl.BlockSpec((None, None, 2 * ML_HEADS, LANES), lambda b, l: (b, l[0], 0, 0))]
        else:
            out_specs += [pl.BlockSpec((None, DEPTH, 2 * ML_HEADS, ML_DK, HEAD_W), lambda b, l: (b, 0, 0, 0, 0)),
                          pl.BlockSpec((None, DEPTH, 2 * ML_HEADS, ML_DK), lambda b, l: (b, 0, 0, 0)),
                          pl.BlockSpec((None, DEPTH, 2 * ML_HEADS, LANES), lambda b, l: (b, 0, 0, 0))]
    grid_spec = pltpu.PrefetchScalarGridSpec(
        num_scalar_prefetch=1, grid=(B,), in_specs=in_specs, out_specs=out_specs,
        scratch_shapes=[pltpu.VMEM((T, GROUP_W), BF16),
                        pltpu.VMEM((T, GROUP_W), F32),
                        pltpu.VMEM((T // ML_CHUNK, ML_HEADS, VT_ROWS, ML_CHUNK), BF16),
                        pltpu.VMEM((T, GROUP_W), F32),
                        pltpu.VMEM((T, GROUP_W), F32),
                        pltpu.VMEM((2 * ML_HEADS, VT_ROWS, ML_DK), F32),
                        pltpu.VMEM((2, 2 * ML_HEADS, ML_CHUNK), F32),
                        pltpu.VMEM((T // ML_CHUNK, N_ML_GATES, ML_CHUNK), F32)])
    return pl.pallas_call(
        functools.partial(_mlstm_kernel, has_ctx, T, n_alias),
        grid_spec=grid_spec,
        out_shape=out_shape,
        input_output_aliases=aliases,
        compiler_params=pltpu.CompilerParams(dimension_semantics=("arbitrary",),
                                             vmem_limit_bytes=VMEM_LIMIT),
        name="mlstm_lat" if has_ctx else "mlstm_ctx",
    )(l_arr, *args)


def _attn_kernel(has_ctx, is_last, merged, n_alias, T, *refs):
    Tk = T + (PAST_LEN if has_ctx else 0)
    pv_t = Tk <= ATTN_PVT_MAX_KEYS
    it = iter(refs)
    l_ref = next(it)
    if merged:
        tail_ref, att_ref = next(it), next(it)
        ckv_ref = tail_ref.at[:, 0:MLA_KV_RANK]
        aux_ref = tail_ref.at[:, MLA_KV_RANK:MLA_KV_RANK + LANES]
        cq_ref = tail_ref.at[:, MLA_KV_RANK + LANES:]
        za_ref, dq_ref, dk_ref, dv_ref, zb_ref = [att_ref.at[:, i * GROUP_W:(i + 1) * GROUP_W] for i in range(5)]
    else:
        cq_ref, ckv_ref, aux_ref, dq_ref, dk_ref, dv_ref, zb_ref, za_ref = [next(it) for _ in range(8)]
    (yc_ref, x_ref, mod_ref, wuq_ref, wukv_ref, wout_ref, gq_ref, gkv_ref, gdn_ref,
     lam_ref) = [next(it) for _ in range(10)]
    if is_last:
        gfin_ref = next(it)
    if has_ctx:
        cosq_ref, sinq_ref, cosk_ref, sink_ref, cckv_ref, ckr_ref, cdk_ref, cdv_ref = [next(it) for _ in range(8)]
    else:
        for _ in range(n_alias):
            next(it)
    xo_ref = next(it)
    if is_last:
        yfin_ref = next(it)
    if not has_ctx:
        ckvn_ref, kro_ref = [next(it) for _ in range(2)]
    ka_s, va_s, kb_s, vb_s, ycat_s = [next(it) for _ in range(5)]
    first = not has_ctx and n_alias == 0
    if first:
        ckvn_all, kro_all = ckvn_ref, kro_ref
        ckvn_ref, kro_ref = ckvn_ref.at[0], kro_ref.at[0]

    qi = pl.program_id(1)

    @pl.when(qi == 0)
    def _build_keys():
        wukv = wukv_ref[...]
        ckv_n = _rms(ckv_ref[...], gkv_ref[...])
        aux = aux_ref[...]
        if not has_ctx:
            ckvn_ref[...] = ckv_n
            kro_ref[...] = aux.T[0:MLA_ROPE, :]
        if first:
            for r in (ckvn_all, kro_all):
                r[1:] = jnp.zeros((DEPTH - 1,) + r.shape[1:], F32)
        kv = jnp.dot(ckv_n.astype(BF16), wukv, preferred_element_type=F32)
        lane = lax.broadcasted_iota(jnp.int32, aux.shape, 1)
        kr = _rope_tile(aux, cosk_ref[...], sink_ref[...]) if has_ctx else aux
        kr = jnp.where(lane < MLA_ROPE, kr, 0.0).astype(BF16)
        for h in range(MLA_HEADS):
            ka_s[0:T, 2 * h * HEAD_W:(2 * h + 1) * HEAD_W] = kv[:, 2 * h * HEAD_W:(2 * h + 1) * HEAD_W].astype(BF16)
            ka_s[0:T, (2 * h + 1) * HEAD_W:(2 * h + 2) * HEAD_W] = kr
        if pv_t:
            row16 = lax.broadcasted_iota(jnp.int32, (VT_ROWS - HEAD_W, Tk), 0)
            ones_rows = jnp.where(row16 == 0, 1.0, 0.0).astype(BF16)
            for h in range(MLA_HEADS):
                hs = slice(h * HEAD_W, (h + 1) * HEAD_W)
                va_s[h, 0:HEAD_W, 0:T] = kv[:, (2 * h + 1) * HEAD_W:(2 * h + 2) * HEAD_W].T.astype(BF16)
                va_s[h, HEAD_W:, :] = ones_rows
                vb_s[h, 0:HEAD_W, 0:T] = dv_ref[:, hs].astype(F32).T.astype(BF16)
                vb_s[h, HEAD_W:, :] = ones_rows
        else:
            lane_k = lax.broadcasted_iota(jnp.int32, (Tk, LANES), 1)
            ones_cols = jnp.where(lane_k == 0, 1.0, 0.0).astype(BF16)
            for h in range(MLA_HEADS):
                hs = slice(h * HEAD_W, (h + 1) * HEAD_W)
                va_s[0:T, 2 * h * HEAD_W:(2 * h + 1) * HEAD_W] = kv[:, (2 * h + 1) * HEAD_W:(2 * h + 2) * HEAD_W].astype(BF16)
                va_s[:, (2 * h + 1) * HEAD_W:(2 * h + 2) * HEAD_W] = ones_cols
                vb_s[0:T, 2 * h * HEAD_W:(2 * h + 1) * HEAD_W] = dv_ref[:, hs]
                vb_s[:, (2 * h + 1) * HEAD_W:(2 * h + 2) * HEAD_W] = ones_cols
        if has_ctx:
            for h in range(DIFF_HEADS):
                hs = slice(h * HEAD_W, (h + 1) * HEAD_W)
                kb_s[0:T, hs] = _rope_tile(dk_ref[:, hs].astype(F32), cosk_ref[...], sink_ref[...]).astype(BF16)
            kvc = jnp.dot(cckv_ref[...].astype(BF16), wukv, preferred_element_type=F32)
            ckr = jnp.concatenate([ckr_ref[...], jnp.zeros((LANES - MLA_ROPE, PAST_LEN), F32)], axis=0).T.astype(BF16)
            for h in range(MLA_HEADS):
                ka_s[T:Tk, 2 * h * HEAD_W:(2 * h + 1) * HEAD_W] = kvc[:, 2 * h * HEAD_W:(2 * h + 1) * HEAD_W].astype(BF16)
                ka_s[T:Tk, (2 * h + 1) * HEAD_W:(2 * h + 2) * HEAD_W] = ckr
            for h in range(DIFF_HEADS):
                hs = slice(h * HEAD_W, (h + 1) * HEAD_W)
                kb_s[T:Tk, hs] = cdk_ref[pl.ds(h, PAST_LEN, stride=DIFF_HEADS), :].astype(BF16)
                v_c = kvc[:, (2 * h + 1) * HEAD_W:(2 * h + 2) * HEAD_W]
                vd_c = cdv_ref[pl.ds(h, PAST_LEN, stride=DIFF_HEADS), :]
                if pv_t:
                    va_s[h, 0:HEAD_W, T:Tk] = v_c.T.astype(BF16)
                    vb_s[h, 0:HEAD_W, T:Tk] = vd_c.T.astype(BF16)
                else:
                    va_s[T:Tk, 2 * h * HEAD_W:(2 * h + 1) * HEAD_W] = v_c.astype(BF16)
                    vb_s[T:Tk, 2 * h * HEAD_W:(2 * h + 1) * HEAD_W] = vd_c.astype(BF16)
        else:
            kb_s[0:T, :] = dk_ref[...]

    nt = (((1,), (1,)), ((), ()))
    tq = cq_ref.shape[0]

    maps = []
    qa = jnp.dot(_rms(cq_ref[...], gq_ref[...]).astype(BF16), wuq_ref[...], preferred_element_type=F32)
    for h in range(MLA_HEADS):
        q_nope = qa[:, 2 * h * HEAD_W:(2 * h + 1) * HEAD_W]
        q_rope = qa[:, (2 * h + 1) * HEAD_W:(2 * h + 2) * HEAD_W]
        if has_ctx:
            q_rope = _rope_tile(q_rope, cosq_ref[...], sinq_ref[...])
        q_h = (jnp.concatenate([q_nope, q_rope], axis=-1) * (MLA_SCALE * LOG2E)).astype(BF16)
        two = slice(2 * h * HEAD_W, (2 * h + 2) * HEAD_W)
        v_load = functools.partial(lambda hh: va_s[hh], h) if pv_t else functools.partial(lambda sl: va_s[:, sl], two)
        maps.append((q_h, functools.partial(lambda sl: ka_s[:, sl], two), v_load))
    lane_q = lax.broadcasted_iota(jnp.int32, (tq, HEAD_W), 1)
    for h in range(DIFF_HEADS):
        hs = slice(h * HEAD_W, (h + 1) * HEAD_W)
        q_h = dq_ref[:, hs].astype(F32)
        if has_ctx:
            q_h = _rope_tile(q_h, cosq_ref[...], sinq_ref[...])
        q_h = q_h * (DIFF_SCALE * LOG2E)
        two = slice(2 * h * HEAD_W, (2 * h + 2) * HEAD_W)
        v_load = functools.partial(lambda hh: vb_s[hh], h) if pv_t else functools.partial(lambda sl: vb_s[:, sl], two)
        for q_m in (jnp.where(lane_q < DIFF_D, q_h, 0.0), jnp.where(lane_q >= DIFF_D, q_h, 0.0)):
            maps.append((q_m.astype(BF16), functools.partial(lambda sl: kb_s[:, sl], hs), v_load))

    def scores(i):
        if pv_t:
            return lax.dot_general(maps[i][1](), maps[i][0], nt, preferred_element_type=F32)
        return lax.dot_general(maps[i][0], maps[i][1](), nt, preferred_element_type=F32)

    def softmax_pv(i, s):
        if pv_t:
            e = jnp.exp2(s - jnp.max(s, axis=0, keepdims=True))
            pv = jnp.dot(maps[i][2](), e.astype(BF16), preferred_element_type=F32)
            return (pv[0:HEAD_W, :] * (1.0 / pv[HEAD_W:HEAD_W + 1, :])).T
        e = jnp.exp2(s - jnp.max(s, axis=-1, keepdims=True))
        pv = jnp.dot(e.astype(BF16), maps[i][2](), preferred_element_type=F32)
        return pv[:, :HEAD_W] * (1.0 / pv[:, HEAD_W:HEAD_W + 1])

    n_maps = len(maps)
    ahead = min(ATTN_AHEAD_CTX if not has_ctx else ATTN_AHEAD_LAT, n_maps)
    pending = {i: scores(i) for i in range(ahead)}
    outs = []
    for i in range(n_maps):
        if i + ahead < n_maps:
            pending[i + ahead] = scores(i + ahead)
        outs.append(softmax_pv(i, pending.pop(i)))

    for h in range(MLA_HEADS):
        hs = slice(h * HEAD_W, (h + 1) * HEAD_W)
        ycat_s[:, hs] = (outs[h] * za_ref[:, hs].astype(F32)).astype(BF16)

    lp = lam_ref[...]
    lf = l_ref[0].astype(F32)
    lam_init = 0.8 - 0.6 * jnp.exp(jnp.full((1, 1), -0.3, F32) * lf)
    lam = (jnp.exp(jnp.sum(lp[0:1, :] * lp[1:2, :], axis=-1, keepdims=True))
           - jnp.exp(jnp.sum(lp[2:3, :] * lp[3:4, :], axis=-1, keepdims=True)) + lam_init)
    for h in range(DIFF_HEADS):
        hs = slice(h * HEAD_W, (h + 1) * HEAD_W)
        o1, o2 = outs[MLA_HEADS + 2 * h], outs[MLA_HEADS + 2 * h + 1]
        o = _rms(o1 - lam * o2, gdn_ref[...]) * (1.0 - lam_init)
        ycat_s[:, GROUP_W + h * HEAD_W:GROUP_W + (h + 1) * HEAD_W] = (o * zb_ref[:, hs].astype(F32)).astype(BF16)

    ycat_s[:, 2 * GROUP_W:] = yc_ref[...]
    y = jnp.dot(ycat_s[...], wout_ref[...], preferred_element_type=F32)
    x_new = x_ref[...] + mod_ref[:, 2 * D_MODEL:] * y
    xo_ref[...] = x_new
    if is_last:
        yfin_ref[...] = _rms(x_new, gfin_ref[...])


def _attn_call(l_arr, main, tail, yc, x2, mod_all, w, T, has_ctx, is_last, rope=None, ctx=None, side_bufs=None):
    n_tok = main.shape[0]
    B = n_tok // T
    tq = min(ATTN_TQ, T)
    nq = T // tq
    Tk = T + (PAST_LEN if has_ctx else 0)
    main3 = main.reshape(B, T, MAIN_W)
    tail3 = tail.reshape(B, T, TAIL_W)
    x3 = x2.reshape(B, T, D_MODEL)

    def tile(c, width):
        return pl.BlockSpec((None, tq, width), lambda b, q, l, c=c: (b, q, c))

    def full(c, width):
        return pl.BlockSpec((None, T, width), lambda b, q, l, c=c: (b, 0, c))

    def wspec(shape):
        return pl.BlockSpec((None,) + shape, lambda b, q, l: (l[0],) + (0,) * len(shape))

    if has_ctx:
        mod_map = lambda b, q, l: (l[0], 1 + b, 0, 0)
    else:
        mod_map = lambda b, q, l: (l[0], 0, 0, 0)

    merged = nq == 1
    if merged:
        assert (C_ZA, C_ZB) == (0, 4)
        in_specs = [pl.BlockSpec((None, T, TAIL_W), lambda b, q, l: (b, 0, 0)),
                    pl.BlockSpec((None, T, MAIN_W // 2), lambda b, q, l: (b, 0, 0))]
        args = [tail3, main3]
    else:
        in_specs = [tile(1, MLA_Q_RANK),
                    full(0, MLA_KV_RANK),
                    full(2, LANES),
                    tile(C_DQ, GROUP_W), full(C_DK, GROUP_W), full(C_DV, GROUP_W),
                    tile(C_ZB, GROUP_W), tile(C_ZA, GROUP_W)]
        args = [tail3, tail3, tail3, main3, main3, main3, main3, main3]
    in_specs += [pl.BlockSpec((None, tq, GROUP_W), lambda b, q, l: (b, q, 0)),
                 pl.BlockSpec((None, tq, D_MODEL), lambda b, q, l: (b, q, 0)),
                 pl.BlockSpec((None, None, 1, 3 * D_MODEL), mod_map),
                 wspec((MLA_Q_RANK, 2 * GROUP_W)), wspec((MLA_KV_RANK, 2 * GROUP_W)),
                 wspec((3 * GROUP_W, D_MODEL)),
                 wspec((1, MLA_Q_RANK)), wspec((1, MLA_KV_RANK)), wspec((1, HEAD_W)),
                 wspec((4, DIFF_D))]
    args += [yc, x3, mod_all, w['wuq'], w['wukv'], w['wout'], w['gq'], w['gkv'], w['gdn'], w['lam']]
    if is_last:
        in_specs.append(pl.BlockSpec((1, D_MODEL), lambda b, q, l: (0, 0)))
        args.append(w['gfin'])
    if has_ctx:
        cos_t, sin_t = rope
        in_specs += [pl.BlockSpec((tq, LANES), lambda b, q, l: (q, 0)),
                     pl.BlockSpec((tq, LANES), lambda b, q, l: (q, 0)),
                     pl.BlockSpec((T, LANES), lambda b, q, l: (0, 0)),
                     pl.BlockSpec((T, LANES), lambda b, q, l: (0, 0)),
                     pl.BlockSpec((None, None, PAST_LEN, MLA_KV_RANK), lambda b, q, l: (b, l[0], 0, 0)),
                     pl.BlockSpec((None, None, MLA_ROPE, PAST_LEN), lambda b, q, l: (b, l[0], 0, 0)),
                     pl.BlockSpec((None, None, PAST_LEN * DIFF_HEADS, HEAD_W), lambda b, q, l: (b, l[0], 0, 0)),
                     pl.BlockSpec((None, None, PAST_LEN * DIFF_HEADS, HEAD_W), lambda b, q, l: (b, l[0], 0, 0))]
        args += [cos_t, sin_t, cos_t, sin_t, ctx['ckv'], ctx['krope'], ctx['dk'], ctx['dv']]
    out_shape = [jax.ShapeDtypeStruct((B, T, D_MODEL), F32)]
    out_specs = [pl.BlockSpec((None, tq, D_MODEL), lambda b, q, l: (b, q, 0))]
    if is_last:
        out_shape.append(jax.ShapeDtypeStruct((B, T, D_MODEL), F32))
        out_specs.append(pl.BlockSpec((None, tq, D_MODEL), lambda b, q, l: (b, q, 0)))
    aliases = {}
    n_alias = 0
    if not has_ctx:
        if side_bufs is not None:
            n_alias = len(side_bufs)
            n_in = 1 + len(args)
            aliases = {n_in + i: len(out_shape) + i for i in range(n_alias)}
            in_specs += [pl.BlockSpec(memory_space=pl.ANY)] * n_alias
            args += list(side_bufs)
            out_specs += [pl.BlockSpec((None, None, T, MLA_KV_RANK), lambda b, q, l: (b, l[0], 0, 0)),
                          pl.BlockSpec((None, None, MLA_ROPE, T), lambda b, q, l: (b, l[0], 0, 0))]
        else:
            assert nq == 1
            out_specs += [pl.BlockSpec((None, DEPTH, T, MLA_KV_RANK), lambda b, q, l: (b, 0, 0, 0)),
                          pl.BlockSpec((None, DEPTH, MLA_ROPE, T), lambda b, q, l: (b, 0, 0, 0))]
        out_shape += [jax.ShapeDtypeStruct((B, DEPTH, T, MLA_KV_RANK), F32),
                      jax.ShapeDtypeStruct((B, DEPTH, MLA_ROPE, T), F32)]
    if Tk <= ATTN_PVT_MAX_KEYS:
        v_scratch = pltpu.VMEM((MLA_HEADS, VT_ROWS, Tk), BF16)
    else:
        v_scratch = pltpu.VMEM((Tk, 2 * GROUP_W), BF16)
    grid_spec = pltpu.PrefetchScalarGridSpec(
        num_scalar_prefetch=1, grid=(B, nq), in_specs=in_specs, out_specs=out_specs,
        scratch_shapes=[pltpu.VMEM((Tk, 2 * GROUP_W), BF16),
                        v_scratch,
                        pltpu.VMEM((Tk, GROUP_W), BF16),
                        v_scratch,
                        pltpu.VMEM((tq, 3 * GROUP_W), BF16)])
    return pl.pallas_call(
        functools.partial(_attn_kernel, has_ctx, is_last, merged, n_alias, T),
        grid_spec=grid_spec,
        out_shape=out_shape,
        input_output_aliases=aliases,
        compiler_params=pltpu.CompilerParams(dimension_semantics=("arbitrary", "arbitrary"),
                                             vmem_limit_bytes=VMEM_LIMIT),
        name="attn_lat" if has_ctx else "attn_ctx",
    )(l_arr, *args)


def _rope_tables(n_tok):
    n_freq = MLA_ROPE // 4
    inv = ROPE_THETA ** (-jnp.arange(n_freq, dtype=F32) / n_freq)
    n_rows = n_tok // GRID_W
    rowp = jnp.repeat(jnp.arange(n_rows, dtype=F32), GRID_W)
    colp = jnp.tile(jnp.arange(GRID_W, dtype=F32), n_rows)
    ang = jnp.concatenate([rowp[:, None] * inv, colp[:, None] * inv], axis=-1)
    cos, sin = jnp.cos(ang), jnp.sin(ang)
    cos64 = jnp.concatenate([cos, cos], axis=-1)
    sin64 = jnp.concatenate([-sin, sin], axis=-1)
    return jnp.concatenate([cos64, cos64], axis=-1), jnp.concatenate([sin64, sin64], axis=-1)


def _prep_weights(W_in, W_uq, W_ukv, W_out):
    w_in_t = jnp.swapaxes(W_in, 1, 2).astype(BF16)
    wq = W_uq.reshape(DEPTH, MLA_Q_RANK, MLA_HEADS, MLA_NOPE + MLA_ROPE)
    wq = jnp.pad(wq, ((0, 0), (0, 0), (0, 0), (0, 2 * HEAD_W - MLA_NOPE - MLA_ROPE)))
    wuq_r = wq.reshape(DEPTH, MLA_Q_RANK, MLA_HEADS * 2 * HEAD_W).astype(BF16)
    return w_in_t, wuq_r, W_ukv.astype(BF16), W_out.astype(BF16)


def kernel(x_prompt, x_sample, cache_mla_ckv, cache_mla_krope, cache_diff_k, cache_diff_v, state_mlstm_C, state_mlstm_n, state_mlstm_m, c, c_ctx, g_norm, W_mod, b_mod, W_in, mla_q_norm, W_uq, mla_kv_norm, W_ukv, diff_lambda, diff_norm, ml_conv, ml_gate_b, ml_norm, W_out, g_final):
    Bc, Tc, _ = x_prompt.shape
    Bs, Ts, _ = x_sample.shape

    w_in_t, wuq_r, wukv_r, wout_r = _prep_weights(W_in, W_uq, W_ukv, W_out)
    w = {'wuq': wuq_r, 'wukv': wukv_r, 'wout': wout_r,
         'gq': mla_q_norm.reshape(DEPTH, 1, MLA_Q_RANK), 'gkv': mla_kv_norm.reshape(DEPTH, 1, MLA_KV_RANK),
         'gdn': diff_norm.reshape(DEPTH, 1, 2 * DIFF_D), 'lam': diff_lambda,
         'gfin': g_final.reshape(1, D_MODEL)}
    g_norm3 = g_norm.reshape(DEPTH, 1, D_MODEL)
    ml_norm2 = ml_norm.reshape(DEPTH, 1, GROUP_W)
    gate_row = jnp.pad(ml_gate_b.reshape(DEPTH, 1, N_ML_GATES),
                       ((0, 0), (0, 0), (GATE_LANE0, LANES - GATE_LANE0 - N_ML_GATES)))

    cc = jnp.concatenate([c_ctx[None, :], c, jnp.zeros((8 - 1 - Bs, D_MODEL), F32)], axis=0)
    mod_all = _mod_call(cc, W_mod, b_mod).reshape(DEPTH, 8, 1, 3 * D_MODEL)

    dkv = cnm = ckr = None
    x2 = x_prompt.reshape(Bc * Tc, D_MODEL)
    y_prompt = None
    for l in range(DEPTH):
        l_arr = jnp.full((1,), l, jnp.int32)
        main, tail, *dkv = _proj_call(l_arr, x2, mod_all, g_norm3, w_in_t, Tc, False, dkv)
        yc, *cnm = _mlstm_call(l_arr, main, tail, ml_conv, gate_row, ml_norm2, Tc, False, side_bufs=cnm)
        outs = _attn_call(l_arr, main, tail, yc, x2, mod_all, w, Tc, False, l == DEPTH - 1, side_bufs=ckr)
        ckr = outs[-2:]
        dk_o, dv_o = dkv
        c_o, n_o, m_o = cnm
        if l == DEPTH - 1:
            x3, y_prompt, ckvn, kro = outs
        else:
            x3, ckvn, kro = outs
        x2 = x3.reshape(Bc * Tc, D_MODEL)
    side_outs = (ckvn, jnp.swapaxes(kro, 2, 3),
                 dk_o.reshape(Bc, DEPTH, Tc, DIFF_HEADS, 2 * DIFF_D),
                 dv_o.reshape(Bc, DEPTH, Tc, DIFF_HEADS, 2 * DIFF_D),
                 c_o.reshape(Bc, DEPTH, 2, ML_HEADS, ML_DK, HEAD_W),
                 n_o.reshape(Bc, DEPTH, 2, ML_HEADS, ML_DK),
                 m_o[:, :, :, 0].reshape(Bc, DEPTH, 2, ML_HEADS))

    rope = _rope_tables(Ts)
    ctx = {'ckv': cache_mla_ckv,
           'krope': jnp.swapaxes(cache_mla_krope, 2, 3),
           'dk': cache_diff_k.reshape(Bs, DEPTH, PAST_LEN * DIFF_HEADS, HEAD_W),
           'dv': cache_diff_v.reshape(Bs, DEPTH, PAST_LEN * DIFF_HEADS, HEAD_W)}
    ctx_state = (state_mlstm_C.reshape(Bs, DEPTH, 2 * ML_HEADS, ML_DK, HEAD_W),
                 state_mlstm_n.reshape(Bs, DEPTH, 2 * ML_HEADS, ML_DK),
                 jnp.broadcast_to(state_mlstm_m.reshape(Bs, DEPTH, 2 * ML_HEADS, 1),
                                  (Bs, DEPTH, 2 * ML_HEADS, LANES)))
    x2 = x_sample.reshape(Bs * Ts, D_MODEL)
    y_sample = None
    for l in range(DEPTH):
        l_arr = jnp.full((1,), l, jnp.int32)
        main, tail = _proj_call(l_arr, x2, mod_all, g_norm3, w_in_t, Ts, True)
        (yc,) = _mlstm_call(l_arr, main, tail, ml_conv, gate_row, ml_norm2, Ts, True, ctx_state)
        outs = _attn_call(l_arr, main, tail, yc, x2, mod_all, w, Ts, True, l == DEPTH - 1, rope, ctx)
        if l == DEPTH - 1:
            x3, y_sample = outs
        else:
            (x3,) = outs
        x2 = x3.reshape(Bs * Ts, D_MODEL)

    return (y_prompt, y_sample, *side_outs)
```

```python
import functools
import math

import jax
import jax.numpy as jnp
import numpy as np
from jax import lax
from jax.experimental import pallas as pl
from jax.experimental.pallas import tpu as pltpu

F32 = jnp.float32
BF16 = jnp.bfloat16

D_MODEL = 1024
DEPTH = 4
PAST_LEN = 256
GRID_W = 64
GROUP_W = 512
MLA_HEADS = 4
MLA_NOPE = 128
MLA_ROPE = 64
MLA_Q_RANK = 384
MLA_KV_RANK = 256
DIFF_HEADS = 4
DIFF_D = 64
ML_HEADS = 4
ML_DK = 128
N_ML_GATES = 16
ROPE_THETA = 10000.0
NORM_EPS = 1e-6
MLA_SCALE = (MLA_NOPE + MLA_ROPE) ** -0.5
DIFF_SCALE = DIFF_D ** -0.5
LOG2E = math.log2(math.e)

LANES = 128
HEAD_W = 128
ML_CHUNK = 256
VT_ROWS = 144
PROJ_TM = 512
ATTN_TQ = 512
ATTN_PVT_MAX_KEYS = 256
ATTN_AHEAD_CTX = 12
ATTN_AHEAD_LAT = 2
MAIN_W = 10 * GROUP_W
TAIL_W = 768
N_IN = 5840
VMEM_LIMIT = 56 * 1024 * 1024

C_ZA, C_DQ, C_DK, C_DV, C_ZB, C_MQ, C_MK, C_MV, C_MO, C_ZC = range(10)

_IN_SIZES = (MLA_Q_RANK, MLA_KV_RANK, MLA_ROPE, GROUP_W, GROUP_W, GROUP_W, GROUP_W, GROUP_W,
             GROUP_W, GROUP_W, GROUP_W, GROUP_W, GROUP_W, N_ML_GATES)
_IN_OFF = np.concatenate([[0], np.cumsum(_IN_SIZES)])
(_O_CQ, _O_CKV, _O_KR, _O_ZA, _O_DQ, _O_DK, _O_DV, _O_ZB, _O_MQ, _O_MK, _O_MV, _O_MO, _O_ZC,
 _O_MG) = [int(v) for v in _IN_OFF[:-1]]
GATE_LANE0 = MLA_ROPE


def _rms(x, g):
    ms = jnp.mean(x * x, axis=-1, keepdims=True)
    return x * lax.rsqrt(ms + NORM_EPS) * g


def _sigmoid(x):
    return 0.5 + 0.5 * jnp.tanh(0.5 * x)


def _silu(x):
    hx = 0.5 * x
    return hx + hx * jnp.tanh(hx)


def _log_sigmoid(x):
    return jnp.minimum(x, 0.0) - jnp.log1p(jnp.exp(-jnp.abs(x)))


def _swap32(x):
    lane = lax.broadcasted_iota(jnp.int32, x.shape, 1)
    fwd = pltpu.roll(x, LANES - 32, 1)
    bwd = pltpu.roll(x, 32, 1)
    return jnp.where((lane % 64) < 32, fwd, bwd)


def _rope_tile(x, cos, sin):
    return x * cos + _swap32(x) * sin


def _mod_kernel(c_ref, w_ref, b_ref, o_ref):
    a = _silu(c_ref[...]).astype(BF16)
    o_ref[...] = jnp.dot(a, w_ref[...].astype(BF16), preferred_element_type=F32) + b_ref[...]


def _mod_call(cc, W_mod, b_mod):
    tn = 1024
    return pl.pallas_call(
        _mod_kernel,
        grid=(DEPTH, 3 * D_MODEL // tn),
        in_specs=[pl.BlockSpec((8, D_MODEL), lambda l, j: (0, 0)),
                  pl.BlockSpec((None, D_MODEL, tn), lambda l, j: (l, 0, j)),
                  pl.BlockSpec((None, 1, tn), lambda l, j: (l, 0, j))],
        out_specs=pl.BlockSpec((None, 8, tn), lambda l, j: (l, 0, j)),
        out_shape=jax.ShapeDtypeStruct((DEPTH, 8, 3 * D_MODEL), F32),
        compiler_params=pltpu.CompilerParams(dimension_semantics=("arbitrary", "arbitrary"),
                                             vmem_limit_bytes=VMEM_LIMIT),
        name="mod",
    )(cc, W_mod, b_mod.reshape(DEPTH, 1, 3 * D_MODEL))


def _proj_kernel(has_side, n_alias, T, *refs):
    l_ref, x_ref, mod_ref, g_ref, w_ref = refs[:5]
    refs = refs[5 + n_alias:]
    if has_side:
        main_ref, tail_ref, dk_ref, dv_ref, wt_s = refs
    else:
        main_ref, tail_ref, wt_s = refs
    del l_ref
    if has_side and n_alias == 0:
        for r in (dk_ref, dv_ref):
            r[:, 1:] = jnp.zeros((r.shape[0], DEPTH - 1) + r.shape[2:], F32)
        dk_ref, dv_ref = dk_ref.at[:, 0], dv_ref.at[:, 0]
    tm = x_ref.shape[0]
    nt = (((1,), (1,)), ((), ()))

    @pl.when(pl.program_id(0) == 0)
    def _gather_tail_rows():
        wt_s[0:MLA_ROPE, :] = w_ref[_O_KR:_O_KR + MLA_ROPE, :]
        wt_s[MLA_ROPE:MLA_ROPE + N_ML_GATES, :] = w_ref[_O_MG:_O_MG + N_ML_GATES, :]
        wt_s[MLA_ROPE + N_ML_GATES:LANES, :] = jnp.zeros((LANES - MLA_ROPE - N_ML_GATES, D_MODEL), BF16)
        wt_s[LANES:, :] = w_ref[_O_CQ:_O_CQ + MLA_Q_RANK, :]

    mod = mod_ref[...]
    y = _rms(x_ref[...], g_ref[...])
    h = (y * (1.0 + mod[:, D_MODEL:2 * D_MODEL]) + mod[:, :D_MODEL]).astype(BF16)
    for s in range(MAIN_W // GROUP_W):
        cols = slice(s * GROUP_W, (s + 1) * GROUP_W)
        acc = lax.dot_general(h, w_ref[_O_ZA + s * GROUP_W:_O_ZA + (s + 1) * GROUP_W, :], nt,
                              preferred_element_type=F32)
        if s in (C_ZA, C_ZB, C_ZC):
            main_ref[:, cols] = _silu(acc).astype(BF16)
        elif s == C_MO:
            main_ref[:, cols] = _sigmoid(acc).astype(BF16)
        else:
            main_ref[:, cols] = acc.astype(BF16)
        if has_side and s in (C_DK, C_DV):
            side_ref = dk_ref if s == C_DK else dv_ref
            for b in range(tm // T):
                for hd in range(DIFF_HEADS):
                    side_ref[b, pl.ds(hd, T, stride=DIFF_HEADS), :] = (
                        acc[b * T:(b + 1) * T, hd * HEAD_W:(hd + 1) * HEAD_W])
    tail_ref[:, 0:MLA_KV_RANK] = lax.dot_general(h, w_ref[_O_CKV:_O_CKV + MLA_KV_RANK, :], nt,
                                                 preferred_element_type=F32)
    tail_ref[:, MLA_KV_RANK:] = lax.dot_general(h, wt_s[...], nt, preferred_element_type=F32)


def _proj_call(l_arr, x2, mod_all, g_norm3, w_in_t, T, has_ctx, side_bufs=None):
    n_tok = x2.shape[0]
    tm = PROJ_TM
    has_side = not has_ctx
    if has_ctx:
        mod_map = lambda i, l: (l[0], 1 + (i * tm) // T, 0, 0)
    else:
        mod_map = lambda i, l: (l[0], 0, 0, 0)
    in_specs = [pl.BlockSpec((tm, D_MODEL), lambda i, l: (i, 0)),
                pl.BlockSpec((None, None, 1, 3 * D_MODEL), mod_map),
                pl.BlockSpec((None, 1, D_MODEL), lambda i, l: (l[0], 0, 0)),
                pl.BlockSpec((None, N_IN, D_MODEL), lambda i, l: (l[0], 0, 0), pipeline_mode=pl.Buffered(1))]
    args = [x2, mod_all, g_norm3, w_in_t]
    out_shape = [jax.ShapeDtypeStruct((n_tok, MAIN_W), BF16), jax.ShapeDtypeStruct((n_tok, TAIL_W), F32)]
    out_specs = [pl.BlockSpec((tm, MAIN_W), lambda i, l: (i, 0)), pl.BlockSpec((tm, TAIL_W), lambda i, l: (i, 0))]
    aliases = {}
    n_alias = 0
    if has_side:
        bt = tm // T
        out_shape += [jax.ShapeDtypeStruct((n_tok // T, DEPTH, T * DIFF_HEADS, HEAD_W), F32)] * 2
        if side_bufs is not None:
            n_alias = len(side_bufs)
            aliases = {1 + len(args) + i: 2 + i for i in range(n_alias)}
            in_specs += [pl.BlockSpec(memory_space=pl.ANY)] * n_alias
            args += list(side_bufs)
            out_specs += [pl.BlockSpec((bt, None, T * DIFF_HEADS, HEAD_W), lambda i, l: (i, l[0], 0, 0))] * 2
        else:
            out_specs += [pl.BlockSpec((bt, DEPTH, T * DIFF_HEADS, HEAD_W), lambda i, l: (i, 0, 0, 0))] * 2
    grid_spec = pltpu.PrefetchScalarGridSpec(
        num_scalar_prefetch=1, grid=(n_tok // tm,), in_specs=in_specs, out_specs=out_specs,
        scratch_shapes=[pltpu.VMEM((TAIL_W - MLA_KV_RANK, D_MODEL), BF16)])
    return pl.pallas_call(
        functools.partial(_proj_kernel, has_side, n_alias, T),
        grid_spec=grid_spec,
        out_shape=out_shape,
        input_output_aliases=aliases,
        compiler_params=pltpu.CompilerParams(dimension_semantics=("arbitrary",),
                                             vmem_limit_bytes=VMEM_LIMIT),
        name="proj_lat" if has_ctx else "proj_ctx",
    )(l_arr, *args)


def _mlstm_kernel(has_ctx, T, n_alias, *refs):
    nc = T // ML_CHUNK
    L = ML_CHUNK
    use_inter = has_ctx or nc > 1
    it = iter(refs)
    l_ref = next(it)
    ml_ref, aux_ref, convw_ref, gb_ref, mln_ref = [next(it) for _ in range(5)]
    mq_ref, mk_ref, mv_ref, mo_ref, zc_ref = [ml_ref.at[:, i * GROUP_W:(i + 1) * GROUP_W] for i in range(5)]
    if has_ctx:
        c0_ref, n0_ref, m0_ref = [next(it) for _ in range(3)]
    else:
        for _ in range(n_alias):
            next(it)
    yc_ref = next(it)
    if not has_ctx:
        cout_ref, nout_ref, mout_ref = [next(it) for _ in range(3)]
    q_s, k_s, vt_s, hf_s, hb_s, caug_s, m_s, gt_s = [next(it) for _ in range(8)]
    del l_ref
    if not has_ctx and n_alias == 0:
        for r in (cout_ref, nout_ref, mout_ref):
            r[1:] = jnp.zeros((DEPTH - 1,) + r.shape[1:], F32)
        cout_ref, nout_ref, mout_ref = cout_ref.at[0], nout_ref.at[0], mout_ref.at[0]

    row_b = lax.broadcasted_iota(jnp.int32, (L, LANES), 0)
    convw = convw_ref[...]
    halo = 16
    for c in range(nc):
        cs = slice(c * L, (c + 1) * L)
        for j in range(2 * GROUP_W // LANES):
            src = mq_ref if j < GROUP_W // LANES else mk_ref
            ls = slice((j % (GROUP_W // LANES)) * LANES, (j % (GROUP_W // LANES) + 1) * LANES)
            u = src[cs, ls].astype(F32)
            prev = src[c * L - halo:c * L, ls].astype(F32)[halo - 1:halo, :] if c > 0 else 0.0
            nxt = src[(c + 1) * L:(c + 1) * L + halo, ls].astype(F32)[0:1, :] if c < nc - 1 else 0.0
            up, un = pltpu.roll(u, 1, 0), pltpu.roll(u, L - 1, 0)
            up = jnp.concatenate([jnp.where(row_b[0:8] == 0, prev, up[0:8]), up[8:]], axis=0)
            un = jnp.concatenate([un[:L - 8], jnp.where(row_b[L - 8:] == L - 1, nxt, un[L - 8:])], axis=0)
            w3 = convw[:, j * LANES:(j + 1) * LANES]
            y = _silu(w3[0:1, :] * up + w3[1:2, :] * u + w3[2:3, :] * un)
            if j < GROUP_W // LANES:
                q_s[cs, ls] = (y * (ML_DK ** -0.5)).astype(BF16)
            else:
                k_s[cs, ls] = y
    row16 = lax.broadcasted_iota(jnp.int32, (VT_ROWS - HEAD_W, L), 0)
    ones_rows = jnp.where(row16 == 0, 1.0, 0.0).astype(BF16)
    g_t = (aux_ref[...] + gb_ref[...]).T
    for c in range(nc):
        cs = slice(c * L, (c + 1) * L)
        gt_s[c] = g_t[GATE_LANE0:GATE_LANE0 + N_ML_GATES, cs]
        for h in range(ML_HEADS):
            vt_s[c, h, 0:HEAD_W, :] = mv_ref[cs, h * HEAD_W:(h + 1) * HEAD_W].astype(F32).T.astype(BF16)
            vt_s[c, h, HEAD_W:, :] = ones_rows

    reps = L // LANES
    if has_ctx:
        n0 = n0_ref[...]
        m0 = m0_ref[...]
        m_s[0] = jnp.concatenate([m0] * reps, axis=1)
        m_s[1] = jnp.concatenate([pltpu.roll(m0, ML_HEADS, 0)] * reps, axis=1)
        row_n = lax.broadcasted_iota(jnp.int32, (VT_ROWS - HEAD_W, ML_DK), 0)
        for r in range(2 * ML_HEADS):
            caug_s[r, 0:HEAD_W, :] = c0_ref[r].T
            caug_s[r, HEAD_W:, :] = jnp.where(row_n == 0, n0[r:r + 1, :], 0.0)
    else:
        caug_s[...] = jnp.zeros_like(caug_s)
        m_s[...] = jnp.zeros_like(m_s)

    ri = lax.broadcasted_iota(jnp.int32, (L, L), 0)
    ci = lax.broadcasted_iota(jnp.int32, (L, L), 1)
    row_r = lax.broadcasted_iota(jnp.int32, (2 * ML_HEADS, L), 0)
    nt = (((1,), (1,)), ((), ()))

    def split3(x):
        hi = x.astype(BF16).astype(F32)
        mid = (x - hi).astype(BF16).astype(F32)
        return hi, mid, (x - hi - mid).astype(BF16).astype(F32)

    masks = ((ri <= ci), (ri >= ci))

    def gate_rows(d, c):
        fwd = d == 0
        tri_t = jnp.where(masks[d], 1.0, 0.0).astype(BF16)
        last = L - 1 if fwd else 0

        g8 = gt_s[c, d * 2 * ML_HEADS:(d + 1) * 2 * ML_HEADS, :]
        hi8, mid8, lo8 = split3(_log_sigmoid(g8))
        stack = jnp.concatenate([hi8, mid8, lo8, jnp.zeros_like(hi8)], axis=0).astype(BF16)
        part8 = jnp.dot(stack, tri_t, preferred_element_type=F32)
        bc8 = part8[0:8] + part8[8:16] + part8[16:24]
        bcs = pltpu.roll(bc8, ML_HEADS, 0)
        a8 = g8 - bcs
        a_n = jnp.concatenate([a8, jnp.zeros((LANES - 2 * ML_HEADS, L), F32)], axis=0).T
        cm8 = jnp.zeros((2 * ML_HEADS, L), F32)
        for h in range(ML_HEADS):
            col_max = jnp.max(jnp.where(masks[d], a_n[:, h:h + 1], -jnp.inf), axis=0, keepdims=True)
            cm8 = jnp.where(row_r == h, col_max, cm8)
        m8 = m_s[d]
        g_row = bcs + m8
        m_t = jnp.maximum(g_row, bcs + cm8)
        w_inter = jnp.exp(g_row - m_t)
        e_inv = jnp.exp(-m_t)
        c_row = bcs - m_t
        b_last = bcs[:, last:last + 1]
        m_new = m_t[:, last:last + 1]
        a_prev = jnp.exp(b_last + m8[:, 0:1] - m_new)
        w_s = jnp.exp(a8 + (b_last - m_new))
        m_s[d] = jnp.broadcast_to(m_new, (2 * ML_HEADS, L))
        return a_n, c_row, w_inter, e_inv, w_s, a_prev

    def do_pair(c_f, c_b):
        stats = (gate_rows(0, c_f), gate_rows(1, c_b))
        combos = [(d, h) for d in range(2) for h in range(ML_HEADS)]
        rows = []
        for c in (c_f, c_b):
            rows.append(slice(c * L, (c + 1) * L) if isinstance(c, int) else pl.ds(pl.multiple_of(c * L, L), L))
        chunk = (c_f, c_b)
        hsl = [slice(h * HEAD_W, (h + 1) * HEAD_W) for h in range(ML_HEADS)]
        qcs = [q_s[rows[d], hsl[h]] for d, h in combos]
        kcs = [k_s[rows[d], hsl[h]].astype(BF16) for d, h in combos]
        vts = [vt_s[chunk[d], h] for d, h in combos]
        n = range(len(combos))
        s_ts = [lax.dot_general(kcs[i], qcs[i], nt, preferred_element_type=F32) for i in n]
        sws = []
        for i, (d, h) in enumerate(combos):
            a_n, c_row = stats[d][0], stats[d][1]
            w_t = jnp.exp(jnp.where(masks[d], a_n[:, h:h + 1] + c_row[h:h + 1, :], -jnp.inf))
            sws.append((s_ts[i] * w_t).astype(BF16))
        nds = [jnp.dot(vts[i], sws[i], preferred_element_type=F32) for i in n]
        if use_inter:
            inters = [lax.dot_general(caug_s[d * ML_HEADS + h].astype(BF16), qcs[i], nt,
                                      preferred_element_type=F32) for i, (d, h) in enumerate(combos)]
            nds = [nds[i] + stats[d][2][h:h + 1, :] * inters[i] for i, (d, h) in enumerate(combos)]
        for i, (d, h) in enumerate(combos):
            inv = 1.0 / jnp.maximum(jnp.abs(nds[i][HEAD_W:HEAD_W + 1, :]), stats[d][3][h:h + 1, :])
            h_s = hf_s if d == 0 else hb_s
            h_s[rows[d], hsl[h]] = (nds[i][0:HEAD_W, :] * inv).T
        upds = [jnp.dot((vts[i].astype(F32) * stats[d][4][h:h + 1, :]).astype(BF16), kcs[i],
                        preferred_element_type=F32) for i, (d, h) in enumerate(combos)]
        for i, (d, h) in enumerate(combos):
            r = d * ML_HEADS + h
            caug_s[r] = upds[i] + stats[d][5][h:h + 1, :] * caug_s[r] if use_inter else upds[i]

    if nc == 1:
        do_pair(0, 0)
    else:
        def body(i, carry):
            do_pair(i, nc - 1 - i)
            return carry
        lax.fori_loop(0, nc, body, 0)

    mln = mln_ref[...]
    for c in range(nc):
        cs = slice(c * L, (c + 1) * L)
        for h in range(ML_HEADS):
            hs = slice(h * HEAD_W, (h + 1) * HEAD_W)
            hc = mo_ref[cs, hs].astype(F32) * (hf_s[cs, hs] + hb_s[cs, hs])
            yc_ref[cs, hs] = (_rms(hc, mln[:, hs]) * zc_ref[cs, hs].astype(F32)).astype(BF16)

    if not has_ctx:
        for r in range(2 * ML_HEADS):
            cout_ref[r] = caug_s[r, 0:HEAD_W, :].T
            nout_ref[r:r + 1, :] = caug_s[r, HEAD_W:HEAD_W + 1, :]
        for d in range(2):
            mout_ref[d * ML_HEADS:(d + 1) * ML_HEADS, :] = m_s[d, 0:ML_HEADS, 0:LANES]


def _mlstm_call(l_arr, main, tail, ml_conv, gate_row, ml_norm2, T, has_ctx, ctx_state=None, side_bufs=None):
    n_tok = main.shape[0]
    B = n_tok // T
    main3 = main.reshape(B, T, MAIN_W)
    tail3 = tail.reshape(B, T, TAIL_W)

    assert (C_MQ, C_ZC) == (5, 9)
    in_specs = [pl.BlockSpec((None, T, MAIN_W // 2), lambda b, l: (b, 0, 1)),
                pl.BlockSpec((None, T, LANES), lambda b, l: (b, 0, 2)),
                pl.BlockSpec((None, 3, 2 * GROUP_W), lambda b, l: (l[0], 0, 0)),
                pl.BlockSpec((None, 1, LANES), lambda b, l: (l[0], 0, 0)),
                pl.BlockSpec((None, 1, GROUP_W), lambda b, l: (l[0], 0, 0))]
    args = [main3, tail3, ml_conv, gate_row, ml_norm2]
    n_alias = 0 if (has_ctx or side_bufs is None) else len(side_bufs)
    out_shape = [jax.ShapeDtypeStruct((B, T, GROUP_W), BF16)]
    out_specs = [pl.BlockSpec((None, T, GROUP_W), lambda b, l: (b, 0, 0))]
    if has_ctx:
        c0, n0, m0 = ctx_state
        in_specs += [pl.BlockSpec((None, None, 2 * ML_HEADS, ML_DK, HEAD_W), lambda b, l: (b, l[0], 0, 0, 0)),
                     pl.BlockSpec((None, None, 2 * ML_HEADS, ML_DK), lambda b, l: (b, l[0], 0, 0)),
                     pl.BlockSpec((None, None, 2 * ML_HEADS, LANES), lambda b, l: (b, l[0], 0, 0))]
        args += [c0, n0, m0]
        aliases = {}
    else:
        aliases = {}
        out_shape += [jax.ShapeDtypeStruct((B, DEPTH, 2 * ML_HEADS, ML_DK, HEAD_W), F32),
                      jax.ShapeDtypeStruct((B, DEPTH, 2 * ML_HEADS, ML_DK), F32),
                      jax.ShapeDtypeStruct((B, DEPTH, 2 * ML_HEADS, LANES), F32)]
        if n_alias:
            aliases = {1 + len(args) + i: 1 + i for i in range(n_alias)}
            in_specs += [pl.BlockSpec(memory_space=pl.ANY)] * n_alias
            args += list(side_bufs)
            out_specs += [pl.BlockSpec((None, None, 2 * ML_HEADS, ML_DK, HEAD_W), lambda b, l: (b, l[0], 0, 0, 0)),
                          pl.BlockSpec((None, None, 2 * ML_HEADS, ML_DK), lambda b, l: (b, l[0], 0, 0)),
                          pl.BlockSpec((None, None, 2 * ML_HEADS, LANES), lambda b, l: (b, l[0], 0, 0))]
        else:
            out_specs += [pl.BlockSpec((None, DEPTH, 2 * ML_HEADS, ML_DK, HEAD_W), lambda b, l: (b, 0, 0, 0, 0)),
                          pl.BlockSpec((None, DEPTH, 2 * ML_HEADS, ML_DK), lambda b, l: (b, 0, 0, 0)),
                          pl.BlockSpec((None, DEPTH, 2 * ML_HEADS, LANES), lambda b, l: (b, 0, 0, 0))]
    grid_spec = pltpu.PrefetchScalarGridSpec(
        num_scalar_prefetch=1, grid=(B,), in_specs=in_specs, out_specs=out_specs,
        scratch_shapes=[pltpu.VMEM((T, GROUP_W), BF16),
                        pltpu.VMEM((T, GROUP_W), F32),
                        pltpu.VMEM((T // ML_CHUNK, ML_HEADS, VT_ROWS, ML_CHUNK), BF16),
                        pltpu.VMEM((T, GROUP_W), F32),
                        pltpu.VMEM((T, GROUP_W), F32),
                        pltpu.VMEM((2 * ML_HEADS, VT_ROWS, ML_DK), F32),
                        pltpu.VMEM((2, 2 * ML_HEADS, ML_CHUNK), F32),
                        pltpu.VMEM((T // ML_CHUNK, N_ML_GATES, ML_CHUNK), F32)])
    return pl.pallas_call(
        functools.partial(_mlstm_kernel, has_ctx, T, n_alias),
        grid_spec=grid_spec,
        out_shape=out_shape,
        input_output_aliases=aliases,
        compiler_params=pltpu.CompilerParams(dimension_semantics=("arbitrary",),
                                             vmem_limit_bytes=VMEM_LIMIT),
        name="mlstm_lat" if has_ctx else "mlstm_ctx",
    )(l_arr, *args)


def _attn_kernel(has_ctx, is_last, merged, n_alias, T, *refs):
    Tk = T + (PAST_LEN if has_ctx else 0)
    pv_t = Tk <= ATTN_PVT_MAX_KEYS
    it = iter(refs)
    l_ref = next(it)
    if merged:
        tail_ref, att_ref = next(it), next(it)
        ckv_ref = tail_ref.at[:, 0:MLA_KV_RANK]
        aux_ref = tail_ref.at[:, MLA_KV_RANK:MLA_KV_RANK + LANES]
        cq_ref = tail_ref.at[:, MLA_KV_RANK + LANES:]
        za_ref, dq_ref, dk_ref, dv_ref, zb_ref = [att_ref.at[:, i * GROUP_W:(i + 1) * GROUP_W] for i in range(5)]
    else:
        cq_ref, ckv_ref, aux_ref, dq_ref, dk_ref, dv_ref, zb_ref, za_ref = [next(it) for _ in range(8)]
    (yc_ref, x_ref, mod_ref, wuq_ref, wukv_ref, wout_ref, gq_ref, gkv_ref, gdn_ref,
     lam_ref) = [next(it) for _ in range(10)]
    if is_last:
        gfin_ref = next(it)
    if has_ctx:
        cosq_ref, sinq_ref, cosk_ref, sink_ref, cckv_ref, ckr_ref, cdk_ref, cdv_ref = [next(it) for _ in range(8)]
    else:
        for _ in range(n_alias):
            next(it)
    xo_ref = next(it)
    if is_last:
        yfin_ref = next(it)
    if not has_ctx:
        ckvn_ref, kro_ref = [next(it) for _ in range(2)]
    ka_s, va_s, kb_s, vb_s, ycat_s = [next(it) for _ in range(5)]
    first = not has_ctx and n_alias == 0
    if first:
        ckvn_all, kro_all = ckvn_ref, kro_ref
        ckvn_ref, kro_ref = ckvn_ref.at[0], kro_ref.at[0]

    def _build_keys():
        wukv = wukv_ref[...]
        ckv_n = _rms(ckv_ref[...], gkv_ref[...])
        aux = aux_ref[...]
        if not has_ctx:
            ckvn_ref[...] = ckv_n
            kro_ref[...] = aux.T[0:MLA_ROPE, :]
        if first:
            for r in (ckvn_all, kro_all):
                r[1:] = jnp.zeros((DEPTH - 1,) + r.shape[1:], F32)
        kv = jnp.dot(ckv_n.astype(BF16), wukv, preferred_element_type=F32)
        lane = lax.broadcasted_iota(jnp.int32, aux.shape, 1)
        kr = _rope_tile(aux, cosk_ref[...], sink_ref[...]) if has_ctx else aux
        kr = jnp.where(lane < MLA_ROPE, kr, 0.0).astype(BF16)
        for h in range(MLA_HEADS):
            ka_s[0:T, 2 * h * HEAD_W:(2 * h + 1) * HEAD_W] = kv[:, 2 * h * HEAD_W:(2 * h + 1) * HEAD_W].astype(BF16)
            ka_s[0:T, (2 * h + 1) * HEAD_W:(2 * h + 2) * HEAD_W] = kr
        if pv_t:
            row16 = lax.broadcasted_iota(jnp.int32, (VT_ROWS - HEAD_W, Tk), 0)
            ones_rows = jnp.where(row16 == 0, 1.0, 0.0).astype(BF16)
            for h in range(MLA_HEADS):
                hs = slice(h * HEAD_W, (h + 1) * HEAD_W)
                va_s[h, 0:HEAD_W, 0:T] = kv[:, (2 * h + 1) * HEAD_W:(2 * h + 2) * HEAD_W].T.astype(BF16)
                va_s[h, HEAD_W:, :] = ones_rows
                vb_s[h, 0:HEAD_W, 0:T] = dv_ref[:, hs].astype(F32).T.astype(BF16)
                vb_s[h, HEAD_W:, :] = ones_rows
        else:
            lane_k = lax.broadcasted_iota(jnp.int32, (Tk, LANES), 1)
            ones_cols = jnp.where(lane_k == 0, 1.0, 0.0).astype(BF16)
            for h in range(MLA_HEADS):
                hs = slice(h * HEAD_W, (h + 1) * HEAD_W)
                va_s[0:T, 2 * h * HEAD_W:(2 * h + 1) * HEAD_W] = kv[:, (2 * h + 1) * HEAD_W:(2 * h + 2) * HEAD_W].astype(BF16)
                va_s[:, (2 * h + 1) * HEAD_W:(2 * h + 2) * HEAD_W] = ones_cols
                vb_s[0:T, 2 * h * HEAD_W:(2 * h + 1) * HEAD_W] = dv_ref[:, hs]
                vb_s[:, (2 * h + 1) * HEAD_W:(2 * h + 2) * HEAD_W] = ones_cols
        if has_ctx:
            for h in range(DIFF_HEADS):
                hs = slice(h * HEAD_W, (h + 1) * HEAD_W)
                kb_s[0:T, hs] = _rope_tile(dk_ref[:, hs].astype(F32), cosk_ref[...], sink_ref[...]).astype(BF16)
            kvc = jnp.dot(cckv_ref[...].astype(BF16), wukv, preferred_element_type=F32)
            ckr = jnp.concatenate([ckr_ref[...], jnp.zeros((LANES - MLA_ROPE, PAST_LEN), F32)], axis=0).T.astype(BF16)
            for h in range(MLA_HEADS):
                ka_s[T:Tk, 2 * h * HEAD_W:(2 * h + 1) * HEAD_W] = kvc[:, 2 * h * HEAD_W:(2 * h + 1) * HEAD_W].astype(BF16)
                ka_s[T:Tk, (2 * h + 1) * HEAD_W:(2 * h + 2) * HEAD_W] = ckr
            for h in range(DIFF_HEADS):
                hs = slice(h * HEAD_W, (h + 1) * HEAD_W)
                kb_s[T:Tk, hs] = cdk_ref[pl.ds(h, PAST_LEN, stride=DIFF_HEADS), :].astype(BF16)
                v_c = kvc[:, (2 * h + 1) * HEAD_W:(2 * h + 2) * HEAD_W]
                vd_c = cdv_ref[pl.ds(h, PAST_LEN, stride=DIFF_HEADS), :]
                if pv_t:
                    va_s[h, 0:HEAD_W, T:Tk] = v_c.T.astype(BF16)
                    vb_s[h, 0:HEAD_W, T:Tk] = vd_c.T.astype(BF16)
                else:
                    va_s[T:Tk, 2 * h * HEAD_W:(2 * h + 1) * HEAD_W] = v_c.astype(BF16)
                    vb_s[T:Tk, 2 * h * HEAD_W:(2 * h + 1) * HEAD_W] = vd_c.astype(BF16)
        else:
            kb_s[0:T, :] = dk_ref[...]

    if merged:
        _build_keys()
    else:
        pl.when(pl.program_id(1) == 0)(_build_keys)

    nt = (((1,), (1,)), ((), ()))
    tq = cq_ref.shape[0]

    maps = []
    qa = jnp.dot(_rms(cq_ref[...], gq_ref[...]).astype(BF16), wuq_ref[...], preferred_element_type=F32)
    for h in range(MLA_HEADS):
        q_nope = qa[:, 2 * h * HEAD_W:(2 * h + 1) * HEAD_W]
        q_rope = qa[:, (2 * h + 1) * HEAD_W:(2 * h + 2) * HEAD_W]
        if has_ctx:
            q_rope = _rope_tile(q_rope, cosq_ref[...], sinq_ref[...])
        q_h = (jnp.concatenate([q_nope, q_rope], axis=-1) * (MLA_SCALE * LOG2E)).astype(BF16)
        two = slice(2 * h * HEAD_W, (2 * h + 2) * HEAD_W)
        v_load = functools.partial(lambda hh: va_s[hh], h) if pv_t else functools.partial(lambda sl: va_s[:, sl], two)
        maps.append((q_h, functools.partial(lambda sl: ka_s[:, sl], two), v_load))
    lane_q = lax.broadcasted_iota(jnp.int32, (tq, HEAD_W), 1)
    for h in range(DIFF_HEADS):
        hs = slice(h * HEAD_W, (h + 1) * HEAD_W)
        q_h = dq_ref[:, hs].astype(F32)
        if has_ctx:
            q_h = _rope_tile(q_h, cosq_ref[...], sinq_ref[...])
        q_h = q_h * (DIFF_SCALE * LOG2E)
        two = slice(2 * h * HEAD_W, (2 * h + 2) * HEAD_W)
        v_load = functools.partial(lambda hh: vb_s[hh], h) if pv_t else functools.partial(lambda sl: vb_s[:, sl], two)
        for q_m in (jnp.where(lane_q < DIFF_D, q_h, 0.0), jnp.where(lane_q >= DIFF_D, q_h, 0.0)):
            maps.append((q_m.astype(BF16), functools.partial(lambda sl: kb_s[:, sl], hs), v_load))

    def scores(i):
        if pv_t:
            return lax.dot_general(maps[i][1](), maps[i][0], nt, preferred_element_type=F32)
        return lax.dot_general(maps[i][0], maps[i][1](), nt, preferred_element_type=F32)

    def softmax_pv(i, s):
        if pv_t:
            e = jnp.exp2(s - jnp.max(s, axis=0, keepdims=True))
            pv = jnp.dot(maps[i][2](), e.astype(BF16), preferred_element_type=F32)
            return (pv[0:HEAD_W, :] * (1.0 / pv[HEAD_W:HEAD_W + 1, :])).T
        e = jnp.exp2(s - jnp.max(s, axis=-1, keepdims=True))
        pv = jnp.dot(e.astype(BF16), maps[i][2](), preferred_element_type=F32)
        return pv[:, :HEAD_W] * (1.0 / pv[:, HEAD_W:HEAD_W + 1])

    n_maps = len(maps)
    ahead = min(ATTN_AHEAD_CTX if not has_ctx else ATTN_AHEAD_LAT, n_maps)
    pending = {i: scores(i) for i in range(ahead)}
    outs = []
    for i in range(n_maps):
        if i + ahead < n_maps:
            pending[i + ahead] = scores(i + ahead)
        outs.append(softmax_pv(i, pending.pop(i)))

    for h in range(MLA_HEADS):
        hs = slice(h * HEAD_W, (h + 1) * HEAD_W)
        ycat_s[:, hs] = (outs[h] * za_ref[:, hs].astype(F32)).astype(BF16)

    lp = lam_ref[...]
    lf = l_ref[0].astype(F32)
    lam_init = 0.8 - 0.6 * jnp.exp(jnp.full((1, 1), -0.3, F32) * lf)
    lam = (jnp.exp(jnp.sum(lp[0:1, :] * lp[1:2, :], axis=-1, keepdims=True))
           - jnp.exp(jnp.sum(lp[2:3, :] * lp[3:4, :], axis=-1, keepdims=True)) + lam_init)
    for h in range(DIFF_HEADS):
        hs = slice(h * HEAD_W, (h + 1) * HEAD_W)
        o1, o2 = outs[MLA_HEADS + 2 * h], outs[MLA_HEADS + 2 * h + 1]
        o = _rms(o1 - lam * o2, gdn_ref[...]) * (1.0 - lam_init)
        ycat_s[:, GROUP_W + h * HEAD_W:GROUP_W + (h + 1) * HEAD_W] = (o * zb_ref[:, hs].astype(F32)).astype(BF16)

    ycat_s[:, 2 * GROUP_W:] = yc_ref[...]
    y = jnp.dot(ycat_s[...], wout_ref[...], preferred_element_type=F32)
    x_new = x_ref[...] + mod_ref[:, 2 * D_MODEL:] * y
    xo_ref[...] = x_new
    if is_last:
        yfin_ref[...] = _rms(x_new, gfin_ref[...])


def _attn_call(l_arr, main, tail, yc, x2, mod_all, w, T, has_ctx, is_last, rope=None, ctx=None, side_bufs=None):
    n_tok = main.shape[0]
    B = n_tok // T
    tq = min(ATTN_TQ, T)
    nq = T // tq
    Tk = T + (PAST_LEN if has_ctx else 0)
    main3 = main.reshape(B, T, MAIN_W)
    tail3 = tail.reshape(B, T, TAIL_W)
    x3 = x2.reshape(B, T, D_MODEL)

    def tile(c, width):
        return pl.BlockSpec((None, tq, width), lambda b, q, l, c=c: (b, q, c))

    def full(c, width):
        return pl.BlockSpec((None, T, width), lambda b, q, l, c=c: (b, 0, c))

    def wspec(shape):
        return pl.BlockSpec((None,) + shape, lambda b, q, l: (l[0],) + (0,) * len(shape))

    if has_ctx:
        mod_map = lambda b, q, l: (l[0], 1 + b, 0, 0)
    else:
        mod_map = lambda b, q, l: (l[0], 0, 0, 0)

    merged = nq == 1
    if merged:
        assert (C_ZA, C_ZB) == (0, 4)
        in_specs = [pl.BlockSpec((None, T, TAIL_W), lambda b, q, l: (b, 0, 0)),
                    pl.BlockSpec((None, T, MAIN_W // 2), lambda b, q, l: (b, 0, 0))]
        args = [tail3, main3]
    else:
        in_specs = [tile(1, MLA_Q_RANK),
                    full(0, MLA_KV_RANK),
                    full(2, LANES),
                    tile(C_DQ, GROUP_W), full(C_DK, GROUP_W), full(C_DV, GROUP_W),
                    tile(C_ZB, GROUP_W), tile(C_ZA, GROUP_W)]
        args = [tail3, tail3, tail3, main3, main3, main3, main3, main3]
    in_specs += [pl.BlockSpec((None, tq, GROUP_W), lambda b, q, l: (b, q, 0)),
                 pl.BlockSpec((None, tq, D_MODEL), lambda b, q, l: (b, q, 0)),
                 pl.BlockSpec((None, None, 1, 3 * D_MODEL), mod_map),
                 wspec((MLA_Q_RANK, 2 * GROUP_W)), wspec((MLA_KV_RANK, 2 * GROUP_W)),
                 wspec((3 * GROUP_W, D_MODEL)),
                 wspec((1, MLA_Q_RANK)), wspec((1, MLA_KV_RANK)), wspec((1, HEAD_W)),
                 wspec((4, DIFF_D))]
    args += [yc, x3, mod_all, w['wuq'], w['wukv'], w['wout'], w['gq'], w['gkv'], w['gdn'], w['lam']]
    if is_last:
        in_specs.append(pl.BlockSpec((1, D_MODEL), lambda b, q, l: (0, 0)))
        args.append(w['gfin'])
    if has_ctx:
        cos_t, sin_t = rope
        in_specs += [pl.BlockSpec((tq, LANES), lambda b, q, l: (q, 0)),
                     pl.BlockSpec((tq, LANES), lambda b, q, l: (q, 0)),
                     pl.BlockSpec((T, LANES), lambda b, q, l: (0, 0)),
                     pl.BlockSpec((T, LANES), lambda b, q, l: (0, 0)),
                     pl.BlockSpec((None, None, PAST_LEN, MLA_KV_RANK), lambda b, q, l: (b, l[0], 0, 0)),
                     pl.BlockSpec((None, None, MLA_ROPE, PAST_LEN), lambda b, q, l: (b, l[0], 0, 0)),
                     pl.BlockSpec((None, None, PAST_LEN * DIFF_HEADS, HEAD_W), lambda b, q, l: (b, l[0], 0, 0)),
                     pl.BlockSpec((None, None, PAST_LEN * DIFF_HEADS, HEAD_W), lambda b, q, l: (b, l[0], 0, 0))]
        args += [cos_t, sin_t, cos_t, sin_t, ctx['ckv'], ctx['krope'], ctx['dk'], ctx['dv']]
    out_shape = [jax.ShapeDtypeStruct((B, T, D_MODEL), F32)]
    out_specs = [pl.BlockSpec((None, tq, D_MODEL), lambda b, q, l: (b, q, 0))]
    if is_last:
        out_shape.append(jax.ShapeDtypeStruct((B, T, D_MODEL), F32))
        out_specs.append(pl.BlockSpec((None, tq, D_MODEL), lambda b, q, l: (b, q, 0)))
    aliases = {}
    n_alias = 0
    if not has_ctx:
        if side_bufs is not None:
            n_alias = len(side_bufs)
            n_in = 1 + len(args)
            aliases = {n_in + i: len(out_shape) + i for i in range(n_alias)}
            in_specs += [pl.BlockSpec(memory_space=pl.ANY)] * n_alias
            args += list(side_bufs)
            out_specs += [pl.BlockSpec((None, None, T, MLA_KV_RANK), lambda b, q, l: (b, l[0], 0, 0)),
                          pl.BlockSpec((None, None, MLA_ROPE, T), lambda b, q, l: (b, l[0], 0, 0))]
        else:
            assert nq == 1
            out_specs += [pl.BlockSpec((None, DEPTH, T, MLA_KV_RANK), lambda b, q, l: (b, 0, 0, 0)),
                          pl.BlockSpec((None, DEPTH, MLA_ROPE, T), lambda b, q, l: (b, 0, 0, 0))]
        out_shape += [jax.ShapeDtypeStruct((B, DEPTH, T, MLA_KV_RANK), F32),
                      jax.ShapeDtypeStruct((B, DEPTH, MLA_ROPE, T), F32)]
    if Tk <= ATTN_PVT_MAX_KEYS:
        v_scratch = pltpu.VMEM((MLA_HEADS, VT_ROWS, Tk), BF16)
    else:
        v_scratch = pltpu.VMEM((Tk, 2 * GROUP_W), BF16)
    grid_spec = pltpu.PrefetchScalarGridSpec(
        num_scalar_prefetch=1, grid=(B, nq), in_specs=in_specs, out_specs=out_specs,
        scratch_shapes=[pltpu.VMEM((Tk, 2 * GROUP_W), BF16),
                        v_scratch,
                        pltpu.VMEM((Tk, GROUP_W), BF16),
                        v_scratch,
                        pltpu.VMEM((tq, 3 * GROUP_W), BF16)])
    return pl.pallas_call(
        functools.partial(_attn_kernel, has_ctx, is_last, merged, n_alias, T),
        grid_spec=grid_spec,
        out_shape=out_shape,
        input_output_aliases=aliases,
        compiler_params=pltpu.CompilerParams(dimension_semantics=("arbitrary", "arbitrary"),
                                             vmem_limit_bytes=VMEM_LIMIT),
        name="attn_lat" if has_ctx else "attn_ctx",
    )(l_arr, *args)


def _ctx_kernel(is_last, n_alias_m, n_alias_a, T, *refs):
    l_ref = refs[0]
    n_m_in = 5 + n_alias_m
    n_a_in = 2 + 9 + (1 if is_last else 0) + n_alias_a
    m_in = refs[1:1 + n_m_in]
    a_in = refs[1 + n_m_in:1 + n_m_in + n_a_in]
    rest = refs[1 + n_m_in + n_a_in:]
    m_out, rest = rest[:3], rest[3:]
    n_a_out = 3 + (1 if is_last else 0)
    a_out, rest = rest[:n_a_out], rest[n_a_out:]
    m_scr, a_scr, yc_s = rest[:8], rest[8:13], rest[13]
    _mlstm_kernel(False, T, n_alias_m, l_ref, *m_in, yc_s, *m_out, *m_scr)
    _attn_kernel(False, is_last, True, n_alias_a, T, l_ref, a_in[0], a_in[1], yc_s, *a_in[2:], *a_out, *a_scr)


def _ctx_call(l_arr, main, tail, x2, mod_all, w, ml_conv, gate_row, ml_norm2, T, is_last, state_bufs, kv_bufs):
    n_tok = main.shape[0]
    B = n_tok // T
    main3 = main.reshape(B, T, MAIN_W)
    tail3 = tail.reshape(B, T, TAIL_W)
    x3 = x2.reshape(B, T, D_MODEL)
    assert (C_ZA, C_ZB, C_MQ, C_ZC) == (0, 4, 5, 9) and T == ML_CHUNK and T <= ATTN_TQ

    def wspec(shape):
        return pl.BlockSpec((None,) + shape, lambda b, q, l: (l[0],) + (0,) * len(shape))

    def side(first, shapes):
        if first:
            return [pl.BlockSpec((None, DEPTH) + s, lambda b, q, l, n=len(s): (b, 0) + (0,) * n) for s in shapes]
        return [pl.BlockSpec((None, None) + s, lambda b, q, l, n=len(s): (b, l[0]) + (0,) * n) for s in shapes]

    any_spec = pl.BlockSpec(memory_space=pl.ANY)
    n_alias_m = 0 if state_bufs is None else len(state_bufs)
    n_alias_a = 0 if kv_bufs is None else len(kv_bufs)
    in_specs = [pl.BlockSpec((None, T, MAIN_W // 2), lambda b, q, l: (b, 0, 1)),
                pl.BlockSpec((None, T, LANES), lambda b, q, l: (b, 0, 2)),
                wspec((3, 2 * GROUP_W)), wspec((1, LANES)), wspec((1, GROUP_W))] + [any_spec] * n_alias_m
    args = [main3, tail3, ml_conv, gate_row, ml_norm2] + list(state_bufs or ())
    alias_m_at = 1 + 5
    in_specs += [pl.BlockSpec((None, T, TAIL_W), lambda b, q, l: (b, 0, 0)),
                 pl.BlockSpec((None, T, MAIN_W // 2), lambda b, q, l: (b, 0, 0)),
                 pl.BlockSpec((None, T, D_MODEL), lambda b, q, l: (b, 0, 0)),
                 pl.BlockSpec((None, None, 1, 3 * D_MODEL), lambda b, q, l: (l[0], 0, 0, 0)),
                 wspec((MLA_Q_RANK, 2 * GROUP_W)), wspec((MLA_KV_RANK, 2 * GROUP_W)), wspec((3 * GROUP_W, D_MODEL)),
                 wspec((1, MLA_Q_RANK)), wspec((1, MLA_KV_RANK)), wspec((1, HEAD_W)), wspec((4, DIFF_D))]
    args += [tail3, main3, x3, mod_all, w['wuq'], w['wukv'], w['wout'], w['gq'], w['gkv'], w['gdn'], w['lam']]
    if is_last:
        in_specs.append(pl.BlockSpec((1, D_MODEL), lambda b, q, l: (0, 0)))
        args.append(w['gfin'])
    alias_a_at = 1 + len(args)
    in_specs += [any_spec] * n_alias_a
    args += list(kv_bufs or ())

    state_shapes = [(2 * ML_HEADS, ML_DK, HEAD_W), (2 * ML_HEADS, ML_DK), (2 * ML_HEADS, LANES)]
    kv_shapes = [(T, MLA_KV_RANK), (MLA_ROPE, T)]
    out_shape = [jax.ShapeDtypeStruct((B, DEPTH) + s, F32) for s in state_shapes]
    out_specs = side(n_alias_m == 0, state_shapes)
    out_shape.append(jax.ShapeDtypeStruct((B, T, D_MODEL), F32))
    out_specs.append(pl.BlockSpec((None, T, D_MODEL), lambda b, q, l: (b, 0, 0)))
    if is_last:
        out_shape.append(jax.ShapeDtypeStruct((B, T, D_MODEL), F32))
        out_specs.append(pl.BlockSpec((None, T, D_MODEL), lambda b, q, l: (b, 0, 0)))
    kv_out_at = len(out_shape)
    out_shape += [jax.ShapeDtypeStruct((B, DEPTH) + s, F32) for s in kv_shapes]
    out_specs += side(n_alias_a == 0, kv_shapes)
    aliases = {alias_m_at + i: i for i in range(n_alias_m)}
    aliases.update({alias_a_at + i: kv_out_at + i for i in range(n_alias_a)})

    grid_spec = pltpu.PrefetchScalarGridSpec(
        num_scalar_prefetch=1, grid=(B, 1), in_specs=in_specs, out_specs=out_specs,
        scratch_shapes=[pltpu.VMEM((T, GROUP_W), BF16),
                        pltpu.VMEM((T, GROUP_W), F32),
                        pltpu.VMEM((1, ML_HEADS, VT_ROWS, ML_CHUNK), BF16),
                        pltpu.VMEM((T, GROUP_W), F32),
                        pltpu.VMEM((T, GROUP_W), F32),
                        pltpu.VMEM((2 * ML_HEADS, VT_ROWS, ML_DK), F32),
                        pltpu.VMEM((2, 2 * ML_HEADS, ML_CHUNK), F32),
                        pltpu.VMEM((1, N_ML_GATES, ML_CHUNK), F32),
                        pltpu.VMEM((T, 2 * GROUP_W), BF16),
                        pltpu.VMEM((MLA_HEADS, VT_ROWS, T), BF16),
                        pltpu.VMEM((T, GROUP_W), BF16),
                        pltpu.VMEM((DIFF_HEADS, VT_ROWS, T), BF16),
                        pltpu.VMEM((T, 3 * GROUP_W), BF16),
                        pltpu.VMEM((T, GROUP_W), BF16)])
    return pl.pallas_call(
        functools.partial(_ctx_kernel, is_last, n_alias_m, n_alias_a, T),
        grid_spec=grid_spec,
        out_shape=out_shape,
        input_output_aliases=aliases,
        compiler_params=pltpu.CompilerParams(dimension_semantics=("arbitrary", "arbitrary"),
                                             vmem_limit_bytes=VMEM_LIMIT),
        name="ctx_mix",
    )(l_arr, *args)


def _rope_tables(n_tok):
    n_freq = MLA_ROPE // 4
    inv = ROPE_THETA ** (-jnp.arange(n_freq, dtype=F32) / n_freq)
    n_rows = n_tok // GRID_W
    rowp = jnp.repeat(jnp.arange(n_rows, dtype=F32), GRID_W)
    colp = jnp.tile(jnp.arange(GRID_W, dtype=F32), n_rows)
    ang = jnp.concatenate([rowp[:, None] * inv, colp[:, None] * inv], axis=-1)
    cos, sin = jnp.cos(ang), jnp.sin(ang)
    cos64 = jnp.concatenate([cos, cos], axis=-1)
    sin64 = jnp.concatenate([-sin, sin], axis=-1)
    return jnp.concatenate([cos64, cos64], axis=-1), jnp.concatenate([sin64, sin64], axis=-1)


def _prep_weights(W_in, W_uq, W_ukv, W_out):
    w_in_t = jnp.swapaxes(W_in, 1, 2).astype(BF16)
    wq = W_uq.reshape(DEPTH, MLA_Q_RANK, MLA_HEADS, MLA_NOPE + MLA_ROPE)
    wq = jnp.pad(wq, ((0, 0), (0, 0), (0, 0), (0, 2 * HEAD_W - MLA_NOPE - MLA_ROPE)))
    wuq_r = wq.reshape(DEPTH, MLA_Q_RANK, MLA_HEADS * 2 * HEAD_W).astype(BF16)
    return w_in_t, wuq_r, W_ukv.astype(BF16), W_out.astype(BF16)


def kernel(x_prompt, x_sample, cache_mla_ckv, cache_mla_krope, cache_diff_k, cache_diff_v, state_mlstm_C, state_mlstm_n, state_mlstm_m, c, c_ctx, g_norm, W_mod, b_mod, W_in, mla_q_norm, W_uq, mla_kv_norm, W_ukv, diff_lambda, diff_norm, ml_conv, ml_gate_b, ml_norm, W_out, g_final):
    Bc, Tc, _ = x_prompt.shape
    Bs, Ts, _ = x_sample.shape

    w_in_t, wuq_r, wukv_r, wout_r = _prep_weights(W_in, W_uq, W_ukv, W_out)
    w = {'wuq': wuq_r, 'wukv': wukv_r, 'wout': wout_r,
         'gq': mla_q_norm.reshape(DEPTH, 1, MLA_Q_RANK), 'gkv': mla_kv_norm.reshape(DEPTH, 1, MLA_KV_RANK),
         'gdn': diff_norm.reshape(DEPTH, 1, 2 * DIFF_D), 'lam': diff_lambda,
         'gfin': g_final.reshape(1, D_MODEL)}
    g_norm3 = g_norm.reshape(DEPTH, 1, D_MODEL)
    ml_norm2 = ml_norm.reshape(DEPTH, 1, GROUP_W)
    gate_row = jnp.pad(ml_gate_b.reshape(DEPTH, 1, N_ML_GATES),
                       ((0, 0), (0, 0), (GATE_LANE0, LANES - GATE_LANE0 - N_ML_GATES)))

    cc = jnp.concatenate([c_ctx[None, :], c, jnp.zeros((8 - 1 - Bs, D_MODEL), F32)], axis=0)
    mod_all = _mod_call(cc, W_mod, b_mod).reshape(DEPTH, 8, 1, 3 * D_MODEL)

    dkv = cnm = ckr = None
    x2 = x_prompt.reshape(Bc * Tc, D_MODEL)
    y_prompt = None
    for l in range(DEPTH):
        l_arr = jnp.full((1,), l, jnp.int32)
        main, tail, *dkv = _proj_call(l_arr, x2, mod_all, g_norm3, w_in_t, Tc, False, dkv)
        outs = _ctx_call(l_arr, main, tail, x2, mod_all, w, ml_conv, gate_row, ml_norm2, Tc, l == DEPTH - 1, cnm, ckr)
        cnm, outs = outs[:3], outs[3:]
        ckr = outs[-2:]
        dk_o, dv_o = dkv
        c_o, n_o, m_o = cnm
        if l == DEPTH - 1:
            x3, y_prompt, ckvn, kro = outs
        else:
            x3, ckvn, kro = outs
        x2 = x3.reshape(Bc * Tc, D_MODEL)
    side_outs = (ckvn, jnp.swapaxes(kro, 2, 3),
                 dk_o.reshape(Bc, DEPTH, Tc, DIFF_HEADS, 2 * DIFF_D),
                 dv_o.reshape(Bc, DEPTH, Tc, DIFF_HEADS, 2 * DIFF_D),
                 c_o.reshape(Bc, DEPTH, 2, ML_HEADS, ML_DK, HEAD_W),
                 n_o.reshape(Bc, DEPTH, 2, ML_HEADS, ML_DK),
                 m_o[:, :, :, 0].reshape(Bc, DEPTH, 2, ML_HEADS))

    rope = _rope_tables(Ts)
    ctx = {'ckv': cache_mla_ckv,
           'krope': jnp.swapaxes(cache_mla_krope, 2, 3),
           'dk': cache_diff_k.reshape(Bs, DEPTH, PAST_LEN * DIFF_HEADS, HEAD_W),
           'dv': cache_diff_v.reshape(Bs, DEPTH, PAST_LEN * DIFF_HEADS, HEAD_W)}
    ctx_state = (state_mlstm_C.reshape(Bs, DEPTH, 2 * ML_HEADS, ML_DK, HEAD_W),
                 state_mlstm_n.reshape(Bs, DEPTH, 2 * ML_HEADS, ML_DK),
                 jnp.broadcast_to(state_mlstm_m.reshape(Bs, DEPTH, 2 * ML_HEADS, 1),
                                  (Bs, DEPTH, 2 * ML_HEADS, LANES)))
    x2 = x_sample.reshape(Bs * Ts, D_MODEL)
    y_sample = None
    for l in range(DEPTH):
        l_arr = jnp.full((1,), l, jnp.int32)
        main, tail = _proj_call(l_arr, x2, mod_all, g_norm3, w_in_t, Ts, True)
        (yc,) = _mlstm_call(l_arr, main, tail, ml_conv, gate_row, ml_norm2, Ts, True, ctx_state)
        outs = _attn_call(l_arr, main, tail, yc, x2, mod_all, w, Ts, True, l == DEPTH - 1, rope, ctx)
        if l == DEPTH - 1:
            x3, y_sample = outs
        else:
            (x3,) = outs
        x2 = x3.reshape(Bs * Ts, D_MODEL)

    return (y_prompt, y_sample, *side_outs)
```

```python
import functools
import math

import jax
import jax.numpy as jnp
import numpy as np
from jax import lax
from jax.experimental import pallas as pl
from jax.experimental.pallas import tpu as pltpu

F32 = jnp.float32
BF16 = jnp.bfloat16

D_MODEL = 1024
DEPTH = 4
PAST_LEN = 256
GRID_W = 64
GROUP_W = 512
MLA_HEADS = 4
MLA_NOPE = 128
MLA_ROPE = 64
MLA_Q_RANK = 384
MLA_KV_RANK = 256
DIFF_HEADS = 4
DIFF_D = 64
ML_HEADS = 4
ML_DK = 128
N_ML_GATES = 16
ROPE_THETA = 10000.0
NORM_EPS = 1e-6
MLA_SCALE = (MLA_NOPE + MLA_ROPE) ** -0.5
DIFF_SCALE = DIFF_D ** -0.5
LOG2E = math.log2(math.e)

LANES = 128
HEAD_W = 128
ML_CHUNK = 256
VT_ROWS = 144
PROJ_TM = 512
ATTN_TQ = 512
ATTN_PVT_MAX_KEYS = 256
ATTN_AHEAD_CTX = 12
ATTN_AHEAD_LAT = 2
MAIN_W = 10 * GROUP_W
TAIL_W = 768
N_IN = 5840
VMEM_LIMIT = 56 * 1024 * 1024

C_ZA, C_DQ, C_DK, C_DV, C_ZB, C_MQ, C_MK, C_MV, C_MO, C_ZC = range(10)

_IN_SIZES = (MLA_Q_RANK, MLA_KV_RANK, MLA_ROPE, GROUP_W, GROUP_W, GROUP_W, GROUP_W, GROUP_W,
             GROUP_W, GROUP_W, GROUP_W, GROUP_W, GROUP_W, N_ML_GATES)
_IN_OFF = np.concatenate([[0], np.cumsum(_IN_SIZES)])
(_O_CQ, _O_CKV, _O_KR, _O_ZA, _O_DQ, _O_DK, _O_DV, _O_ZB, _O_MQ, _O_MK, _O_MV, _O_MO, _O_ZC,
 _O_MG) = [int(v) for v in _IN_OFF[:-1]]
GATE_LANE0 = MLA_ROPE


def _rms(x, g):
    ms = jnp.mean(x * x, axis=-1, keepdims=True)
    return x * lax.rsqrt(ms + NORM_EPS) * g


def _sigmoid(x):
    return 0.5 + 0.5 * jnp.tanh(0.5 * x)


def _silu(x):
    hx = 0.5 * x
    return hx + hx * jnp.tanh(hx)


def _log_sigmoid(x):
    return jnp.minimum(x, 0.0) - jnp.log1p(jnp.exp(-jnp.abs(x)))


def _swap32(x):
    lane = lax.broadcasted_iota(jnp.int32, x.shape, 1)
    fwd = pltpu.roll(x, LANES - 32, 1)
    bwd = pltpu.roll(x, 32, 1)
    return jnp.where((lane % 64) < 32, fwd, bwd)


def _rope_tile(x, cos, sin):
    return x * cos + _swap32(x) * sin


def _mod_kernel(c_ref, w_ref, b_ref, o_ref):
    a = _silu(c_ref[...]).astype(BF16)
    o_ref[...] = jnp.dot(a, w_ref[...].astype(BF16), preferred_element_type=F32) + b_ref[...]


def _mod_call(cc, W_mod, b_mod):
    tn = 1024
    return pl.pallas_call(
        _mod_kernel,
        grid=(DEPTH, 3 * D_MODEL // tn),
        in_specs=[pl.BlockSpec((8, D_MODEL), lambda l, j: (0, 0)),
                  pl.BlockSpec((None, D_MODEL, tn), lambda l, j: (l, 0, j)),
                  pl.BlockSpec((None, 1, tn), lambda l, j: (l, 0, j))],
        out_specs=pl.BlockSpec((None, 8, tn), lambda l, j: (l, 0, j)),
        out_shape=jax.ShapeDtypeStruct((DEPTH, 8, 3 * D_MODEL), F32),
        compiler_params=pltpu.CompilerParams(dimension_semantics=("arbitrary", "arbitrary"),
                                             vmem_limit_bytes=VMEM_LIMIT),
        name="mod",
    )(cc, W_mod, b_mod.reshape(DEPTH, 1, 3 * D_MODEL))


def _proj_kernel(has_side, n_alias, T, *refs):
    l_ref, x_ref, mod_ref, g_ref, w_ref = refs[:5]
    refs = refs[5 + n_alias:]
    if has_side:
        main_ref, tail_ref, dk_ref, dv_ref, wt_s = refs
    else:
        main_ref, tail_ref, wt_s = refs
    del l_ref
    if has_side and n_alias == 0:
        for r in (dk_ref, dv_ref):
            r[:, 1:] = jnp.zeros((r.shape[0], DEPTH - 1) + r.shape[2:], F32)
        dk_ref, dv_ref = dk_ref.at[:, 0], dv_ref.at[:, 0]
    tm = x_ref.shape[0]
    nt = (((1,), (1,)), ((), ()))

    @pl.when(pl.program_id(0) == 0)
    def _gather_tail_rows():
        wt_s[0:MLA_ROPE, :] = w_ref[_O_KR:_O_KR + MLA_ROPE, :]
        wt_s[MLA_ROPE:MLA_ROPE + N_ML_GATES, :] = w_ref[_O_MG:_O_MG + N_ML_GATES, :]
        wt_s[MLA_ROPE + N_ML_GATES:LANES, :] = jnp.zeros((LANES - MLA_ROPE - N_ML_GATES, D_MODEL), BF16)
        wt_s[LANES:, :] = w_ref[_O_CQ:_O_CQ + MLA_Q_RANK, :]

    mod = mod_ref[...]
    y = _rms(x_ref[...], g_ref[...])
    h = (y * (1.0 + mod[:, D_MODEL:2 * D_MODEL]) + mod[:, :D_MODEL]).astype(BF16)
    for s in range(MAIN_W // GROUP_W):
        cols = slice(s * GROUP_W, (s + 1) * GROUP_W)
        acc = lax.dot_general(h, w_ref[_O_ZA + s * GROUP_W:_O_ZA + (s + 1) * GROUP_W, :], nt,
                              preferred_element_type=F32)
        if s in (C_ZA, C_ZB, C_ZC):
            main_ref[:, cols] = _silu(acc).astype(BF16)
        elif s == C_MO:
            main_ref[:, cols] = _sigmoid(acc).astype(BF16)
        else:
            main_ref[:, cols] = acc.astype(BF16)
        if has_side and s in (C_DK, C_DV):
            side_ref = dk_ref if s == C_DK else dv_ref
            for b in range(tm // T):
                for hd in range(DIFF_HEADS):
                    side_ref[b, pl.ds(hd, T, stride=DIFF_HEADS), :] = (
                        acc[b * T:(b + 1) * T, hd * HEAD_W:(hd + 1) * HEAD_W])
    tail_ref[:, 0:MLA_KV_RANK] = lax.dot_general(h, w_ref[_O_CKV:_O_CKV + MLA_KV_RANK, :], nt,
                                                 preferred_element_type=F32)
    tail_ref[:, MLA_KV_RANK:] = lax.dot_general(h, wt_s[...], nt, preferred_element_type=F32)


def _proj_call(l_arr, x2, mod_all, g_norm3, w_in_t, T, has_ctx, side_bufs=None):
    n_tok = x2.shape[0]
    tm = PROJ_TM
    has_side = not has_ctx
    if has_ctx:
        mod_map = lambda i, l: (l[0], 1 + (i * tm) // T, 0, 0)
    else:
        mod_map = lambda i, l: (l[0], 0, 0, 0)
    in_specs = [pl.BlockSpec((tm, D_MODEL), lambda i, l: (i, 0)),
                pl.BlockSpec((None, None, 1, 3 * D_MODEL), mod_map),
                pl.BlockSpec((None, 1, D_MODEL), lambda i, l: (l[0], 0, 0)),
                pl.BlockSpec((None, N_IN, D_MODEL), lambda i, l: (l[0], 0, 0), pipeline_mode=pl.Buffered(1))]
    args = [x2, mod_all, g_norm3, w_in_t]
    out_shape = [jax.ShapeDtypeStruct((n_tok, MAIN_W), BF16), jax.ShapeDtypeStruct((n_tok, TAIL_W), F32)]
    out_specs = [pl.BlockSpec((tm, MAIN_W), lambda i, l: (i, 0)), pl.BlockSpec((tm, TAIL_W), lambda i, l: (i, 0))]
    aliases = {}
    n_alias = 0
    if has_side:
        bt = tm // T
        out_shape += [jax.ShapeDtypeStruct((n_tok // T, DEPTH, T * DIFF_HEADS, HEAD_W), F32)] * 2
        if side_bufs is not None:
            n_alias = len(side_bufs)
            aliases = {1 + len(args) + i: 2 + i for i in range(n_alias)}
            in_specs += [pl.BlockSpec(memory_space=pl.ANY)] * n_alias
            args += list(side_bufs)
            out_specs += [pl.BlockSpec((bt, None, T * DIFF_HEADS, HEAD_W), lambda i, l: (i, l[0], 0, 0))] * 2
        else:
            out_specs += [pl.BlockSpec((bt, DEPTH, T * DIFF_HEADS, HEAD_W), lambda i, l: (i, 0, 0, 0))] * 2
    grid_spec = pltpu.PrefetchScalarGridSpec(
        num_scalar_prefetch=1, grid=(n_tok // tm,), in_specs=in_specs, out_specs=out_specs,
        scratch_shapes=[pltpu.VMEM((TAIL_W - MLA_KV_RANK, D_MODEL), BF16)])
    return pl.pallas_call(
        functools.partial(_proj_kernel, has_side, n_alias, T),
        grid_spec=grid_spec,
        out_shape=out_shape,
        input_output_aliases=aliases,
        compiler_params=pltpu.CompilerParams(dimension_semantics=("arbitrary",),
                                             vmem_limit_bytes=VMEM_LIMIT),
        name="proj_lat" if has_ctx else "proj_ctx",
    )(l_arr, *args)


def _mlstm_kernel(has_ctx, T, n_alias, *refs):
    nc = T // ML_CHUNK
    L = ML_CHUNK
    use_inter = has_ctx or nc > 1
    it = iter(refs)
    l_ref = next(it)
    ml_ref, aux_ref, convw_ref, gb_ref, mln_ref = [next(it) for _ in range(5)]
    mq_ref, mk_ref, mv_ref, mo_ref, zc_ref = [ml_ref.at[:, i * GROUP_W:(i + 1) * GROUP_W] for i in range(5)]
    if has_ctx:
        c0_ref, n0_ref, m0_ref = [next(it) for _ in range(3)]
    else:
        for _ in range(n_alias):
            next(it)
    yc_ref = next(it)
    if not has_ctx:
        cout_ref, nout_ref, mout_ref = [next(it) for _ in range(3)]
    q_s, k_s, vt_s, hf_s, hb_s, caug_s, m_s, gt_s, ub_s = [next(it) for _ in range(9)]
    del l_ref
    if not has_ctx and n_alias == 0:
        for r in (cout_ref, nout_ref, mout_ref):
            r[1:] = jnp.zeros((DEPTH - 1,) + r.shape[1:], F32)
        cout_ref, nout_ref, mout_ref = cout_ref.at[0], nout_ref.at[0], mout_ref.at[0]

    convw = convw_ref[...]
    halo = 16
    for c in range(nc):
        cs = slice(c * L, (c + 1) * L)
        for j in range(2 * GROUP_W // LANES):
            src = mq_ref if j < GROUP_W // LANES else mk_ref
            ls = slice((j % (GROUP_W // LANES)) * LANES, (j % (GROUP_W // LANES) + 1) * LANES)
            u = src[cs, ls].astype(F32)
            prev = src[c * L - halo:c * L, ls].astype(F32)[halo - 1:halo, :] if c > 0 else 0.0
            nxt = src[(c + 1) * L:(c + 1) * L + halo, ls].astype(F32)[0:1, :] if c < nc - 1 else 0.0
            ub_s[0:8, :] = jnp.broadcast_to(prev, (8, LANES)) if c > 0 else jnp.zeros((8, LANES), F32)
            ub_s[8:8 + L, :] = u
            ub_s[8 + L:16 + L, :] = jnp.broadcast_to(nxt, (8, LANES)) if c < nc - 1 else jnp.zeros((8, LANES), F32)
            up, un = ub_s[7:7 + L, :], ub_s[9:9 + L, :]
            w3 = convw[:, j * LANES:(j + 1) * LANES]
            y = _silu(w3[0:1, :] * up + w3[1:2, :] * u + w3[2:3, :] * un)
            if j < GROUP_W // LANES:
                q_s[cs, ls] = (y * (ML_DK ** -0.5)).astype(BF16)
            else:
                k_s[cs, ls] = y
    row16 = lax.broadcasted_iota(jnp.int32, (VT_ROWS - HEAD_W, L), 0)
    ones_rows = jnp.where(row16 == 0, 1.0, 0.0).astype(BF16)
    g_t = (aux_ref[...] + gb_ref[...]).T
    for c in range(nc):
        cs = slice(c * L, (c + 1) * L)
        gt_s[c] = g_t[GATE_LANE0:GATE_LANE0 + N_ML_GATES, cs]
        for h in range(ML_HEADS):
            vt_s[c, h, 0:HEAD_W, :] = mv_ref[cs, h * HEAD_W:(h + 1) * HEAD_W].astype(F32).T.astype(BF16)
            vt_s[c, h, HEAD_W:, :] = ones_rows

    reps = L // LANES
    if has_ctx:
        n0 = n0_ref[...]
        m0 = m0_ref[...]
        m_s[0] = jnp.concatenate([m0] * reps, axis=1)
        m_s[1] = jnp.concatenate([pltpu.roll(m0, ML_HEADS, 0)] * reps, axis=1)
        row_n = lax.broadcasted_iota(jnp.int32, (VT_ROWS - HEAD_W, ML_DK), 0)
        for r in range(2 * ML_HEADS):
            caug_s[r, 0:HEAD_W, :] = c0_ref[r].T
            caug_s[r, HEAD_W:, :] = jnp.where(row_n == 0, n0[r:r + 1, :], 0.0)
    else:
        caug_s[...] = jnp.zeros_like(caug_s)
        m_s[...] = jnp.zeros_like(m_s)

    ri = lax.broadcasted_iota(jnp.int32, (L, L), 0)
    ci = lax.broadcasted_iota(jnp.int32, (L, L), 1)
    row_r = lax.broadcasted_iota(jnp.int32, (2 * ML_HEADS, L), 0)
    nt = (((1,), (1,)), ((), ()))

    def split3(x):
        hi = x.astype(BF16).astype(F32)
        mid = (x - hi).astype(BF16).astype(F32)
        return hi, mid, (x - hi - mid).astype(BF16).astype(F32)

    masks = ((ri <= ci), (ri >= ci))

    def gate_rows(d, c):
        fwd = d == 0
        tri_t = jnp.where(masks[d], 1.0, 0.0).astype(BF16)
        last = L - 1 if fwd else 0

        g8 = gt_s[c, d * 2 * ML_HEADS:(d + 1) * 2 * ML_HEADS, :]
        hi8, mid8, lo8 = split3(_log_sigmoid(g8))
        stack = jnp.concatenate([hi8, mid8, lo8, jnp.zeros_like(hi8)], axis=0).astype(BF16)
        part8 = jnp.dot(stack, tri_t, preferred_element_type=F32)
        bc8 = part8[0:8] + part8[8:16] + part8[16:24]
        bcs = pltpu.roll(bc8, ML_HEADS, 0)
        a8 = g8 - bcs
        a_n = jnp.concatenate([a8, jnp.zeros((LANES - 2 * ML_HEADS, L), F32)], axis=0).T
        cm8 = jnp.zeros((2 * ML_HEADS, L), F32)
        for h in range(ML_HEADS):
            col_max = jnp.max(jnp.where(masks[d], a_n[:, h:h + 1], -jnp.inf), axis=0, keepdims=True)
            cm8 = jnp.where(row_r == h, col_max, cm8)
        m8 = m_s[d]
        g_row = bcs + m8
        m_t = jnp.maximum(g_row, bcs + cm8)
        w_inter = jnp.exp(g_row - m_t)
        e_inv = jnp.exp(-m_t)
        c_row = bcs - m_t
        b_last = bcs[:, last:last + 1]
        m_new = m_t[:, last:last + 1]
        a_prev = jnp.exp(b_last + m8[:, 0:1] - m_new)
        w_s = jnp.exp(a8 + (b_last - m_new))
        m_s[d] = jnp.broadcast_to(m_new, (2 * ML_HEADS, L))
        return a_n, c_row, w_inter, e_inv, w_s, a_prev

    def do_pair(c_f, c_b):
        stats = (gate_rows(0, c_f), gate_rows(1, c_b))
        combos = [(d, h) for d in range(2) for h in range(ML_HEADS)]
        rows = []
        for c in (c_f, c_b):
            rows.append(slice(c * L, (c + 1) * L) if isinstance(c, int) else pl.ds(pl.multiple_of(c * L, L), L))
        chunk = (c_f, c_b)
        hsl = [slice(h * HEAD_W, (h + 1) * HEAD_W) for h in range(ML_HEADS)]
        qcs = [q_s[rows[d], hsl[h]] for d, h in combos]
        kcs = [k_s[rows[d], hsl[h]].astype(BF16) for d, h in combos]
        vts = [vt_s[chunk[d], h] for d, h in combos]
        n = range(len(combos))
        s_ts = [lax.dot_general(kcs[i], qcs[i], nt, preferred_element_type=F32) for i in n]
        sws = []
        for i, (d, h) in enumerate(combos):
            a_n, c_row = stats[d][0], stats[d][1]
            w_t = jnp.exp(jnp.where(masks[d], a_n[:, h:h + 1] + c_row[h:h + 1, :], -jnp.inf))
            sws.append((s_ts[i] * w_t).astype(BF16))
        nds = [jnp.dot(vts[i], sws[i], preferred_element_type=F32) for i in n]
        if use_inter:
            inters = [lax.dot_general(caug_s[d * ML_HEADS + h].astype(BF16), qcs[i], nt,
                                      preferred_element_type=F32) for i, (d, h) in enumerate(combos)]
            nds = [nds[i] + stats[d][2][h:h + 1, :] * inters[i] for i, (d, h) in enumerate(combos)]
        for i, (d, h) in enumerate(combos):
            inv = 1.0 / jnp.maximum(jnp.abs(nds[i][HEAD_W:HEAD_W + 1, :]), stats[d][3][h:h + 1, :])
            h_s = hf_s if d == 0 else hb_s
            h_s[rows[d], hsl[h]] = (nds[i][0:HEAD_W, :] * inv).T
        upds = [jnp.dot((vts[i].astype(F32) * stats[d][4][h:h + 1, :]).astype(BF16), kcs[i],
                        preferred_element_type=F32) for i, (d, h) in enumerate(combos)]
        for i, (d, h) in enumerate(combos):
            r = d * ML_HEADS + h
            caug_s[r] = upds[i] + stats[d][5][h:h + 1, :] * caug_s[r] if use_inter else upds[i]

    if nc == 1:
        do_pair(0, 0)
    else:
        def body(i, carry):
            do_pair(i, nc - 1 - i)
            return carry
        lax.fori_loop(0, nc, body, 0)

    mln = mln_ref[...]
    for c in range(nc):
        cs = slice(c * L, (c + 1) * L)
        for h in range(ML_HEADS):
            hs = slice(h * HEAD_W, (h + 1) * HEAD_W)
            hc = mo_ref[cs, hs].astype(F32) * (hf_s[cs, hs] + hb_s[cs, hs])
            yc_ref[cs, hs] = (_rms(hc, mln[:, hs]) * zc_ref[cs, hs].astype(F32)).astype(BF16)

    if not has_ctx:
        for r in range(2 * ML_HEADS):
            cout_ref[r] = caug_s[r, 0:HEAD_W, :].T
            nout_ref[r:r + 1, :] = caug_s[r, HEAD_W:HEAD_W + 1, :]
        for d in range(2):
            mout_ref[d * ML_HEADS:(d + 1) * ML_HEADS, :] = m_s[d, 0:ML_HEADS, 0:LANES]


def _mlstm_call(l_arr, main, tail, ml_conv, gate_row, ml_norm2, T, has_ctx, ctx_state=None, side_bufs=None):
    n_tok = main.shape[0]
    B = n_tok // T
    main3 = main.reshape(B, T, MAIN_W)
    tail3 = tail.reshape(B, T, TAIL_W)

    assert (C_MQ, C_ZC) == (5, 9)
    in_specs = [pl.BlockSpec((None, T, MAIN_W // 2), lambda b, l: (b, 0, 1)),
                pl.BlockSpec((None, T, LANES), lambda b, l: (b, 0, 2)),
                pl.BlockSpec((None, 3, 2 * GROUP_W), lambda b, l: (l[0], 0, 0)),
                pl.BlockSpec((None, 1, LANES), lambda b, l: (l[0], 0, 0)),
                pl.BlockSpec((None, 1, GROUP_W), lambda b, l: (l[0], 0, 0))]
    args = [main3, tail3, ml_conv, gate_row, ml_norm2]
    n_alias = 0 if (has_ctx or side_bufs is None) else len(side_bufs)
    out_shape = [jax.ShapeDtypeStruct((B, T, GROUP_W), BF16)]
    out_specs = [pl.BlockSpec((None, T, GROUP_W), lambda b, l: (b, 0, 0))]
    if has_ctx:
        c0, n0, m0 = ctx_state
        in_specs += [pl.BlockSpec((None, None, 2 * ML_HEADS, ML_DK, HEAD_W), lambda b, l: (b, l[0], 0, 0, 0)),
                     pl.BlockSpec((None, None, 2 * ML_HEADS, ML_DK), lambda b, l: (b, l[0], 0, 0)),
                     pl.BlockSpec((None, None, 2 * ML_HEADS, LANES), lambda b, l: (b, l[0], 0, 0))]
        args += [c0, n0, m0]
        aliases = {}
    else:
        aliases = {}
        out_shape += [jax.ShapeDtypeStruct((B, DEPTH, 2 * ML_HEADS, ML_DK, HEAD_W), F32),
                      jax.ShapeDtypeStruct((B, DEPTH, 2 * ML_HEADS, ML_DK), F32),
                      jax.ShapeDtypeStruct((B, DEPTH, 2 * ML_HEADS, LANES), F32)]
        if n_alias:
            aliases = {1 + len(args) + i: 1 + i for i in range(n_alias)}
            in_specs += [pl.BlockSpec(memory_space=pl.ANY)] * n_alias
            args += list(side_bufs)
            out_specs += [pl.BlockSpec((None, None, 2 * ML_HEADS, ML_DK, HEAD_W), lambda b, l: (b, l[0], 0, 0, 0)),
                          pl.BlockSpec((None, None, 2 * ML_HEADS, ML_DK), lambda b, l: (b, l[0], 0, 0)),
                          pl.BlockSpec((None, None, 2 * ML_HEADS, LANES), lambda b, l: (b, l[0], 0, 0))]
        else:
            out_specs += [pl.BlockSpec((None, DEPTH, 2 * ML_HEADS, ML_DK, HEAD_W), lambda b, l: (b, 0, 0, 0, 0)),
                          pl.BlockSpec((None, DEPTH, 2 * ML_HEADS, ML_DK), lambda b, l: (b, 0, 0, 0)),
                          pl.BlockSpec((None, DEPTH, 2 * ML_HEADS, LANES), lambda b, l: (b, 0, 0, 0))]
    grid_spec = pltpu.PrefetchScalarGridSpec(
        num_scalar_prefetch=1, grid=(B,), in_specs=in_specs, out_specs=out_specs,
        scratch_shapes=[pltpu.VMEM((T, GROUP_W), BF16),
                        pltpu.VMEM((T, GROUP_W), F32),
                        pltpu.VMEM((T // ML_CHUNK, ML_HEADS, VT_ROWS, ML_CHUNK), BF16),
                        pltpu.VMEM((T, GROUP_W), F32),
                        pltpu.VMEM((T, GROUP_W), F32),
                        pltpu.VMEM((2 * ML_HEADS, VT_ROWS, ML_DK), F32),
                        pltpu.VMEM((2, 2 * ML_HEADS, ML_CHUNK), F32),
                        pltpu.VMEM((T // ML_CHUNK, N_ML_GATES, ML_CHUNK), F32),
                        pltpu.VMEM((ML_CHUNK + 16, LANES), F32)])
    return pl.pallas_call(
        functools.partial(_mlstm_kernel, has_ctx, T, n_alias),
        grid_spec=grid_spec,
        out_shape=out_shape,
        input_output_aliases=aliases,
        compiler_params=pltpu.CompilerParams(dimension_semantics=("arbitrary",),
                                             vmem_limit_bytes=VMEM_LIMIT),
        name="mlstm_lat" if has_ctx else "mlstm_ctx",
    )(l_arr, *args)


def _attn_kernel(has_ctx, is_last, merged, n_alias, T, *refs):
    Tk = T + (PAST_LEN if has_ctx else 0)
    pv_t = Tk <= ATTN_PVT_MAX_KEYS
    it = iter(refs)
    l_ref = next(it)
    if merged:
        tail_ref, att_ref = next(it), next(it)
        ckv_ref = tail_ref.at[:, 0:MLA_KV_RANK]
        aux_ref = tail_ref.at[:, MLA_KV_RANK:MLA_KV_RANK + LANES]
        cq_ref = tail_ref.at[:, MLA_KV_RANK + LANES:]
        za_ref, dq_ref, dk_ref, dv_ref, zb_ref = [att_ref.at[:, i * GROUP_W:(i + 1) * GROUP_W] for i in range(5)]
    else:
        cq_ref, ckv_ref, aux_ref, dq_ref, dk_ref, dv_ref, zb_ref, za_ref = [next(it) for _ in range(8)]
    (yc_ref, x_ref, mod_ref, wuq_ref, wukv_ref, wout_ref, gq_ref, gkv_ref, gdn_ref,
     lam_ref) = [next(it) for _ in range(10)]
    if is_last:
        gfin_ref = next(it)
    if has_ctx:
        cosq_ref, sinq_ref, cosk_ref, sink_ref, cckv_ref, ckr_ref, cdk_ref, cdv_ref = [next(it) for _ in range(8)]
    else:
        for _ in range(n_alias):
            next(it)
    xo_ref = next(it)
    if is_last:
        yfin_ref = next(it)
    if not has_ctx:
        ckvn_ref, kro_ref = [next(it) for _ in range(2)]
    ka_s, va_s, kb_s, vb_s, ycat_s = [next(it) for _ in range(5)]
    first = not has_ctx and n_alias == 0
    if first:
        ckvn_all, kro_all = ckvn_ref, kro_ref
        ckvn_ref, kro_ref = ckvn_ref.at[0], kro_ref.at[0]

    def _build_keys():
        wukv = wukv_ref[...]
        ckv_n = _rms(ckv_ref[...], gkv_ref[...])
        aux = aux_ref[...]
        if not has_ctx:
            ckvn_ref[...] = ckv_n
            kro_ref[...] = aux.T[0:MLA_ROPE, :]
        if first:
            for r in (ckvn_all, kro_all):
                r[1:] = jnp.zeros((DEPTH - 1,) + r.shape[1:], F32)
        kv = jnp.dot(ckv_n.astype(BF16), wukv, preferred_element_type=F32)
        lane = lax.broadcasted_iota(jnp.int32, aux.shape, 1)
        kr = _rope_tile(aux, cosk_ref[...], sink_ref[...]) if has_ctx else aux
        kr = jnp.where(lane < MLA_ROPE, kr, 0.0).astype(BF16)
        for h in range(MLA_HEADS):
            ka_s[0:T, 2 * h * HEAD_W:(2 * h + 1) * HEAD_W] = kv[:, 2 * h * HEAD_W:(2 * h + 1) * HEAD_W].astype(BF16)
            ka_s[0:T, (2 * h + 1) * HEAD_W:(2 * h + 2) * HEAD_W] = kr
        if pv_t:
            row16 = lax.broadcasted_iota(jnp.int32, (VT_ROWS - HEAD_W, Tk), 0)
            ones_rows = jnp.where(row16 == 0, 1.0, 0.0).astype(BF16)
            for h in range(MLA_HEADS):
                hs = slice(h * HEAD_W, (h + 1) * HEAD_W)
                va_s[h, 0:HEAD_W, 0:T] = kv[:, (2 * h + 1) * HEAD_W:(2 * h + 2) * HEAD_W].T.astype(BF16)
                va_s[h, HEAD_W:, :] = ones_rows
                vb_s[h, 0:HEAD_W, 0:T] = dv_ref[:, hs].astype(F32).T.astype(BF16)
                vb_s[h, HEAD_W:, :] = ones_rows
        else:
            lane_k = lax.broadcasted_iota(jnp.int32, (Tk, LANES), 1)
            ones_cols = jnp.where(lane_k == 0, 1.0, 0.0).astype(BF16)
            for h in range(MLA_HEADS):
                hs = slice(h * HEAD_W, (h + 1) * HEAD_W)
                va_s[0:T, 2 * h * HEAD_W:(2 * h + 1) * HEAD_W] = kv[:, (2 * h + 1) * HEAD_W:(2 * h + 2) * HEAD_W].astype(BF16)
                va_s[:, (2 * h + 1) * HEAD_W:(2 * h + 2) * HEAD_W] = ones_cols
                vb_s[0:T, 2 * h * HEAD_W:(2 * h + 1) * HEAD_W] = dv_ref[:, hs]
                vb_s[:, (2 * h + 1) * HEAD_W:(2 * h + 2) * HEAD_W] = ones_cols
        if has_ctx:
            for h in range(DIFF_HEADS):
                hs = slice(h * HEAD_W, (h + 1) * HEAD_W)
                kb_s[0:T, hs] = _rope_tile(dk_ref[:, hs].astype(F32), cosk_ref[...], sink_ref[...]).astype(BF16)
            kvc = jnp.dot(cckv_ref[...].astype(BF16), wukv, preferred_element_type=F32)
            ckr = jnp.concatenate([ckr_ref[...], jnp.zeros((LANES - MLA_ROPE, PAST_LEN), F32)], axis=0).T.astype(BF16)
            for h in range(MLA_HEADS):
                ka_s[T:Tk, 2 * h * HEAD_W:(2 * h + 1) * HEAD_W] = kvc[:, 2 * h * HEAD_W:(2 * h + 1) * HEAD_W].astype(BF16)
                ka_s[T:Tk, (2 * h + 1) * HEAD_W:(2 * h + 2) * HEAD_W] = ckr
            for h in range(DIFF_HEADS):
                hs = slice(h * HEAD_W, (h + 1) * HEAD_W)
                kb_s[T:Tk, hs] = cdk_ref[pl.ds(h, PAST_LEN, stride=DIFF_HEADS), :].astype(BF16)
                v_c = kvc[:, (2 * h + 1) * HEAD_W:(2 * h + 2) * HEAD_W]
                vd_c = cdv_ref[pl.ds(h, PAST_LEN, stride=DIFF_HEADS), :]
                if pv_t:
                    va_s[h, 0:HEAD_W, T:Tk] = v_c.T.astype(BF16)
                    vb_s[h, 0:HEAD_W, T:Tk] = vd_c.T.astype(BF16)
                else:
                    va_s[T:Tk, 2 * h * HEAD_W:(2 * h + 1) * HEAD_W] = v_c.astype(BF16)
                    vb_s[T:Tk, 2 * h * HEAD_W:(2 * h + 1) * HEAD_W] = vd_c.astype(BF16)
        else:
            kb_s[0:T, :] = dk_ref[...]

    if merged:
        _build_keys()
    else:
        pl.when(pl.program_id(1) == 0)(_build_keys)

    nt = (((1,), (1,)), ((), ()))
    tq = cq_ref.shape[0]

    maps = []
    qa = jnp.dot(_rms(cq_ref[...], gq_ref[...]).astype(BF16), wuq_ref[...], preferred_element_type=F32)
    for h in range(MLA_HEADS):
        q_nope = qa[:, 2 * h * HEAD_W:(2 * h + 1) * HEAD_W]
        q_rope = qa[:, (2 * h + 1) * HEAD_W:(2 * h + 2) * HEAD_W]
        if has_ctx:
            q_rope = _rope_tile(q_rope, cosq_ref[...], sinq_ref[...])
        q_h = (jnp.concatenate([q_nope, q_rope], axis=-1) * (MLA_SCALE * LOG2E)).astype(BF16)
        two = slice(2 * h * HEAD_W, (2 * h + 2) * HEAD_W)
        v_load = functools.partial(lambda hh: va_s[hh], h) if pv_t else functools.partial(lambda sl: va_s[:, sl], two)
        maps.append((q_h, functools.partial(lambda sl: ka_s[:, sl], two), v_load))
    lane_q = lax.broadcasted_iota(jnp.int32, (tq, HEAD_W), 1)
    for h in range(DIFF_HEADS):
        hs = slice(h * HEAD_W, (h + 1) * HEAD_W)
        q_h = dq_ref[:, hs].astype(F32)
        if has_ctx:
            q_h = _rope_tile(q_h, cosq_ref[...], sinq_ref[...])
        q_h = q_h * (DIFF_SCALE * LOG2E)
        two = slice(2 * h * HEAD_W, (2 * h + 2) * HEAD_W)
        v_load = functools.partial(lambda hh: vb_s[hh], h) if pv_t else functools.partial(lambda sl: vb_s[:, sl], two)
        for q_m in (jnp.where(lane_q < DIFF_D, q_h, 0.0), jnp.where(lane_q >= DIFF_D, q_h, 0.0)):
            maps.append((q_m.astype(BF16), functools.partial(lambda sl: kb_s[:, sl], hs), v_load))

    def scores(i):
        if pv_t:
            return lax.dot_general(maps[i][1](), maps[i][0], nt, preferred_element_type=F32)
        return lax.dot_general(maps[i][0], maps[i][1](), nt, preferred_element_type=F32)

    def softmax_pv(i, s):
        if pv_t:
            e = jnp.exp2(s - jnp.max(s, axis=0, keepdims=True))
            pv = jnp.dot(maps[i][2](), e.astype(BF16), preferred_element_type=F32)
            return (pv[0:HEAD_W, :] * (1.0 / pv[HEAD_W:HEAD_W + 1, :])).T
        e = jnp.exp2(s - jnp.max(s, axis=-1, keepdims=True))
        pv = jnp.dot(e.astype(BF16), maps[i][2](), preferred_element_type=F32)
        return pv[:, :HEAD_W] * (1.0 / pv[:, HEAD_W:HEAD_W + 1])

    n_maps = len(maps)
    ahead = min(ATTN_AHEAD_CTX if not has_ctx else ATTN_AHEAD_LAT, n_maps)
    pending = {i: scores(i) for i in range(ahead)}
    outs = []
    for i in range(n_maps):
        if i + ahead < n_maps:
            pending[i + ahead] = scores(i + ahead)
        outs.append(softmax_pv(i, pending.pop(i)))

    for h in range(MLA_HEADS):
        hs = slice(h * HEAD_W, (h + 1) * HEAD_W)
        ycat_s[:, hs] = (outs[h] * za_ref[:, hs].astype(F32)).astype(BF16)

    lp = lam_ref[...]
    lf = l_ref[0].astype(F32)
    lam_init = 0.8 - 0.6 * jnp.exp(jnp.full((1, 1), -0.3, F32) * lf)
    lam = (jnp.exp(jnp.sum(lp[0:1, :] * lp[1:2, :], axis=-1, keepdims=True))
           - jnp.exp(jnp.sum(lp[2:3, :] * lp[3:4, :], axis=-1, keepdims=True)) + lam_init)
    for h in range(DIFF_HEADS):
        hs = slice(h * HEAD_W, (h + 1) * HEAD_W)
        o1, o2 = outs[MLA_HEADS + 2 * h], outs[MLA_HEADS + 2 * h + 1]
        o = _rms(o1 - lam * o2, gdn_ref[...]) * (1.0 - lam_init)
        ycat_s[:, GROUP_W + h * HEAD_W:GROUP_W + (h + 1) * HEAD_W] = (o * zb_ref[:, hs].astype(F32)).astype(BF16)

    ycat_s[:, 2 * GROUP_W:] = yc_ref[...]
    y = jnp.dot(ycat_s[...], wout_ref[...], preferred_element_type=F32)
    x_new = x_ref[...] + mod_ref[:, 2 * D_MODEL:] * y
    xo_ref[...] = x_new
    if is_last:
        yfin_ref[...] = _rms(x_new, gfin_ref[...])


def _attn_call(l_arr, main, tail, yc, x2, mod_all, w, T, has_ctx, is_last, rope=None, ctx=None, side_bufs=None):
    n_tok = main.shape[0]
    B = n_tok // T
    tq = min(ATTN_TQ, T)
    nq = T // tq
    Tk = T + (PAST_LEN if has_ctx else 0)
    main3 = main.reshape(B, T, MAIN_W)
    tail3 = tail.reshape(B, T, TAIL_W)
    x3 = x2.reshape(B, T, D_MODEL)

    def tile(c, width):
        return pl.BlockSpec((None, tq, width), lambda b, q, l, c=c: (b, q, c))

    def full(c, width):
        return pl.BlockSpec((None, T, width), lambda b, q, l, c=c: (b, 0, c))

    def wspec(shape):
        return pl.BlockSpec((None,) + shape, lambda b, q, l: (l[0],) + (0,) * len(shape))

    if has_ctx:
        mod_map = lambda b, q, l: (l[0], 1 + b, 0, 0)
    else:
        mod_map = lambda b, q, l: (l[0], 0, 0, 0)

    merged = nq == 1
    if merged:
        assert (C_ZA, C_ZB) == (0, 4)
        in_specs = [pl.BlockSpec((None, T, TAIL_W), lambda b, q, l: (b, 0, 0)),
                    pl.BlockSpec((None, T, MAIN_W // 2), lambda b, q, l: (b, 0, 0))]
        args = [tail3, main3]
    else:
        in_specs = [tile(1, MLA_Q_RANK),
                    full(0, MLA_KV_RANK),
                    full(2, LANES),
                    tile(C_DQ, GROUP_W), full(C_DK, GROUP_W), full(C_DV, GROUP_W),
                    tile(C_ZB, GROUP_W), tile(C_ZA, GROUP_W)]
        args = [tail3, tail3, tail3, main3, main3, main3, main3, main3]
    in_specs += [pl.BlockSpec((None, tq, GROUP_W), lambda b, q, l: (b, q, 0)),
                 pl.BlockSpec((None, tq, D_MODEL), lambda b, q, l: (b, q, 0)),
                 pl.BlockSpec((None, None, 1, 3 * D_MODEL), mod_map),
                 wspec((MLA_Q_RANK, 2 * GROUP_W)), wspec((MLA_KV_RANK, 2 * GROUP_W)),
                 wspec((3 * GROUP_W, D_MODEL)),
                 wspec((1, MLA_Q_RANK)), wspec((1, MLA_KV_RANK)), wspec((1, HEAD_W)),
                 wspec((4, DIFF_D))]
    args += [yc, x3, mod_all, w['wuq'], w['wukv'], w['wout'], w['gq'], w['gkv'], w['gdn'], w['lam']]
    if is_last:
        in_specs.append(pl.BlockSpec((1, D_MODEL), lambda b, q, l: (0, 0)))
        args.append(w['gfin'])
    if has_ctx:
        cos_t, sin_t = rope
        in_specs += [pl.BlockSpec((tq, LANES), lambda b, q, l: (q, 0)),
                     pl.BlockSpec((tq, LANES), lambda b, q, l: (q, 0)),
                     pl.BlockSpec((T, LANES), lambda b, q, l: (0, 0)),
                     pl.BlockSpec((T, LANES), lambda b, q, l: (0, 0)),
                     pl.BlockSpec((None, None, PAST_LEN, MLA_KV_RANK), lambda b, q, l: (b, l[0], 0, 0)),
                     pl.BlockSpec((None, None, MLA_ROPE, PAST_LEN), lambda b, q, l: (b, l[0], 0, 0)),
                     pl.BlockSpec((None, None, PAST_LEN * DIFF_HEADS, HEAD_W), lambda b, q, l: (b, l[0], 0, 0)),
                     pl.BlockSpec((None, None, PAST_LEN * DIFF_HEADS, HEAD_W), lambda b, q, l: (b, l[0], 0, 0))]
        args += [cos_t, sin_t, cos_t, sin_t, ctx['ckv'], ctx['krope'], ctx['dk'], ctx['dv']]
    out_shape = [jax.ShapeDtypeStruct((B, T, D_MODEL), F32)]
    out_specs = [pl.BlockSpec((None, tq, D_MODEL), lambda b, q, l: (b, q, 0))]
    if is_last:
        out_shape.append(jax.ShapeDtypeStruct((B, T, D_MODEL), F32))
        out_specs.append(pl.BlockSpec((None, tq, D_MODEL), lambda b, q, l: (b, q, 0)))
    aliases = {}
    n_alias = 0
    if not has_ctx:
        if side_bufs is not None:
            n_alias = len(side_bufs)
            n_in = 1 + len(args)
            aliases = {n_in + i: len(out_shape) + i for i in range(n_alias)}
            in_specs += [pl.BlockSpec(memory_space=pl.ANY)] * n_alias
            args += list(side_bufs)
            out_specs += [pl.BlockSpec((None, None, T, MLA_KV_RANK), lambda b, q, l: (b, l[0], 0, 0)),
                          pl.BlockSpec((None, None, MLA_ROPE, T), lambda b, q, l: (b, l[0], 0, 0))]
        else:
            assert nq == 1
            out_specs += [pl.BlockSpec((None, DEPTH, T, MLA_KV_RANK), lambda b, q, l: (b, 0, 0, 0)),
                          pl.BlockSpec((None, DEPTH, MLA_ROPE, T), lambda b, q, l: (b, 0, 0, 0))]
        out_shape += [jax.ShapeDtypeStruct((B, DEPTH, T, MLA_KV_RANK), F32),
                      jax.ShapeDtypeStruct((B, DEPTH, MLA_ROPE, T), F32)]
    if Tk <= ATTN_PVT_MAX_KEYS:
        v_scratch = pltpu.VMEM((MLA_HEADS, VT_ROWS, Tk), BF16)
    else:
        v_scratch = pltpu.VMEM((Tk, 2 * GROUP_W), BF16)
    grid_spec = pltpu.PrefetchScalarGridSpec(
        num_scalar_prefetch=1, grid=(B, nq), in_specs=in_specs, out_specs=out_specs,
        scratch_shapes=[pltpu.VMEM((Tk, 2 * GROUP_W), BF16),
                        v_scratch,
                        pltpu.VMEM((Tk, GROUP_W), BF16),
                        v_scratch,
                        pltpu.VMEM((tq, 3 * GROUP_W), BF16)])
    return pl.pallas_call(
        functools.partial(_attn_kernel, has_ctx, is_last, merged, n_alias, T),
        grid_spec=grid_spec,
        out_shape=out_shape,
        input_output_aliases=aliases,
        compiler_params=pltpu.CompilerParams(dimension_semantics=("arbitrary", "arbitrary"),
                                             vmem_limit_bytes=VMEM_LIMIT),
        name="attn_lat" if has_ctx else "attn_ctx",
    )(l_arr, *args)


def _ctx_kernel(is_last, n_alias_m, n_alias_a, T, *refs):
    l_ref = refs[0]
    n_m_in = 5 + n_alias_m
    n_a_in = 2 + 9 + (1 if is_last else 0) + n_alias_a
    m_in = refs[1:1 + n_m_in]
    a_in = refs[1 + n_m_in:1 + n_m_in + n_a_in]
    rest = refs[1 + n_m_in + n_a_in:]
    m_out, rest = rest[:3], rest[3:]
    n_a_out = 3 + (1 if is_last else 0)
    a_out, rest = rest[:n_a_out], rest[n_a_out:]
    m_scr, a_scr, yc_s = rest[:9], rest[9:14], rest[14]
    _mlstm_kernel(False, T, n_alias_m, l_ref, *m_in, yc_s, *m_out, *m_scr)
    _attn_kernel(False, is_last, True, n_alias_a, T, l_ref, a_in[0], a_in[1], yc_s, *a_in[2:], *a_out, *a_scr)


def _ctx_call(l_arr, main, tail, x2, mod_all, w, ml_conv, gate_row, ml_norm2, T, is_last, state_bufs, kv_bufs):
    n_tok = main.shape[0]
    B = n_tok // T
    main3 = main.reshape(B, T, MAIN_W)
    tail3 = tail.reshape(B, T, TAIL_W)
    x3 = x2.reshape(B, T, D_MODEL)
    assert (C_ZA, C_ZB, C_MQ, C_ZC) == (0, 4, 5, 9) and T == ML_CHUNK and T <= ATTN_TQ

    def wspec(shape):
        return pl.BlockSpec((None,) + shape, lambda b, q, l: (l[0],) + (0,) * len(shape))

    def side(first, shapes):
        if first:
            return [pl.BlockSpec((None, DEPTH) + s, lambda b, q, l, n=len(s): (b, 0) + (0,) * n) for s in shapes]
        return [pl.BlockSpec((None, None) + s, lambda b, q, l, n=len(s): (b, l[0]) + (0,) * n) for s in shapes]

    any_spec = pl.BlockSpec(memory_space=pl.ANY)
    n_alias_m = 0 if state_bufs is None else len(state_bufs)
    n_alias_a = 0 if kv_bufs is None else len(kv_bufs)
    in_specs = [pl.BlockSpec((None, T, MAIN_W // 2), lambda b, q, l: (b, 0, 1)),
                pl.BlockSpec((None, T, LANES), lambda b, q, l: (b, 0, 2)),
                wspec((3, 2 * GROUP_W)), wspec((1, LANES)), wspec((1, GROUP_W))] + [any_spec] * n_alias_m
    args = [main3, tail3, ml_conv, gate_row, ml_norm2] + list(state_bufs or ())
    alias_m_at = 1 + 5
    in_specs += [pl.BlockSpec((None, T, TAIL_W), lambda b, q, l: (b, 0, 0)),
                 pl.BlockSpec((None, T, MAIN_W // 2), lambda b, q, l: (b, 0, 0)),
                 pl.BlockSpec((None, T, D_MODEL), lambda b, q, l: (b, 0, 0)),
                 pl.BlockSpec((None, None, 1, 3 * D_MODEL), lambda b, q, l: (l[0], 0, 0, 0)),
                 wspec((MLA_Q_RANK, 2 * GROUP_W)), wspec((MLA_KV_RANK, 2 * GROUP_W)), wspec((3 * GROUP_W, D_MODEL)),
                 wspec((1, MLA_Q_RANK)), wspec((1, MLA_KV_RANK)), wspec((1, HEAD_W)), wspec((4, DIFF_D))]
    args += [tail3, main3, x3, mod_all, w['wuq'], w['wukv'], w['wout'], w['gq'], w['gkv'], w['gdn'], w['lam']]
    if is_last:
        in_specs.append(pl.BlockSpec((1, D_MODEL), lambda b, q, l: (0, 0)))
        args.append(w['gfin'])
    alias_a_at = 1 + len(args)
    in_specs += [any_spec] * n_alias_a
    args += list(kv_bufs or ())

    state_shapes = [(2 * ML_HEADS, ML_DK, HEAD_W), (2 * ML_HEADS, ML_DK), (2 * ML_HEADS, LANES)]
    kv_shapes = [(T, MLA_KV_RANK), (MLA_ROPE, T)]
    out_shape = [jax.ShapeDtypeStruct((B, DEPTH) + s, F32) for s in state_shapes]
    out_specs = side(n_alias_m == 0, state_shapes)
    out_shape.append(jax.ShapeDtypeStruct((B, T, D_MODEL), F32))
    out_specs.append(pl.BlockSpec((None, T, D_MODEL), lambda b, q, l: (b, 0, 0)))
    if is_last:
        out_shape.append(jax.ShapeDtypeStruct((B, T, D_MODEL), F32))
        out_specs.append(pl.BlockSpec((None, T, D_MODEL), lambda b, q, l: (b, 0, 0)))
    kv_out_at = len(out_shape)
    out_shape += [jax.ShapeDtypeStruct((B, DEPTH) + s, F32) for s in kv_shapes]
    out_specs += side(n_alias_a == 0, kv_shapes)
    aliases = {alias_m_at + i: i for i in range(n_alias_m)}
    aliases.update({alias_a_at + i: kv_out_at + i for i in range(n_alias_a)})

    grid_spec = pltpu.PrefetchScalarGridSpec(
        num_scalar_prefetch=1, grid=(B, 1), in_specs=in_specs, out_specs=out_specs,
        scratch_shapes=[pltpu.VMEM((T, GROUP_W), BF16),
                        pltpu.VMEM((T, GROUP_W), F32),
                        pltpu.VMEM((1, ML_HEADS, VT_ROWS, ML_CHUNK), BF16),
                        pltpu.VMEM((T, GROUP_W), F32),
                        pltpu.VMEM((T, GROUP_W), F32),
                        pltpu.VMEM((2 * ML_HEADS, VT_ROWS, ML_DK), F32),
                        pltpu.VMEM((2, 2 * ML_HEADS, ML_CHUNK), F32),
                        pltpu.VMEM((1, N_ML_GATES, ML_CHUNK), F32),
                        pltpu.VMEM((ML_CHUNK + 16, LANES), F32),
                        pltpu.VMEM((T, 2 * GROUP_W), BF16),
                        pltpu.VMEM((MLA_HEADS, VT_ROWS, T), BF16),
                        pltpu.VMEM((T, GROUP_W), BF16),
                        pltpu.VMEM((DIFF_HEADS, VT_ROWS, T), BF16),
                        pltpu.VMEM((T, 3 * GROUP_W), BF16),
                        pltpu.VMEM((T, GROUP_W), BF16)])
    return pl.pallas_call(
        functools.partial(_ctx_kernel, is_last, n_alias_m, n_alias_a, T),
        grid_spec=grid_spec,
        out_shape=out_shape,
        input_output_aliases=aliases,
        compiler_params=pltpu.CompilerParams(dimension_semantics=("arbitrary", "arbitrary"),
                                             vmem_limit_bytes=VMEM_LIMIT),
        name="ctx_mix",
    )(l_arr, *args)


def _rope_tables(n_tok):
    n_freq = MLA_ROPE // 4
    inv = ROPE_THETA ** (-jnp.arange(n_freq, dtype=F32) / n_freq)
    n_rows = n_tok // GRID_W
    rowp = jnp.repeat(jnp.arange(n_rows, dtype=F32), GRID_W)
    colp = jnp.tile(jnp.arange(GRID_W, dtype=F32), n_rows)
    ang = jnp.concatenate([rowp[:, None] * inv, colp[:, None] * inv], axis=-1)
    cos, sin = jnp.cos(ang), jnp.sin(ang)
    cos64 = jnp.concatenate([cos, cos], axis=-1)
    sin64 = jnp.concatenate([-sin, sin], axis=-1)
    return jnp.concatenate([cos64, cos64], axis=-1), jnp.concatenate([sin64, sin64], axis=-1)


def _prep_weights(W_in, W_uq, W_ukv, W_out):
    w_in_t = jnp.swapaxes(W_in, 1, 2).astype(BF16)
    wq = W_uq.reshape(DEPTH, MLA_Q_RANK, MLA_HEADS, MLA_NOPE + MLA_ROPE)
    wq = jnp.pad(wq, ((0, 0), (0, 0), (0, 0), (0, 2 * HEAD_W - MLA_NOPE - MLA_ROPE)))
    wuq_r = wq.reshape(DEPTH, MLA_Q_RANK, MLA_HEADS * 2 * HEAD_W).astype(BF16)
    return w_in_t, wuq_r, W_ukv.astype(BF16), W_out.astype(BF16)


def kernel(x_prompt, x_sample, cache_mla_ckv, cache_mla_krope, cache_diff_k, cache_diff_v, state_mlstm_C, state_mlstm_n, state_mlstm_m, c, c_ctx, g_norm, W_mod, b_mod, W_in, mla_q_norm, W_uq, mla_kv_norm, W_ukv, diff_lambda, diff_norm, ml_conv, ml_gate_b, ml_norm, W_out, g_final):
    Bc, Tc, _ = x_prompt.shape
    Bs, Ts, _ = x_sample.shape

    w_in_t, wuq_r, wukv_r, wout_r = _prep_weights(W_in, W_uq, W_ukv, W_out)
    w = {'wuq': wuq_r, 'wukv': wukv_r, 'wout': wout_r,
         'gq': mla_q_norm.reshape(DEPTH, 1, MLA_Q_RANK), 'gkv': mla_kv_norm.reshape(DEPTH, 1, MLA_KV_RANK),
         'gdn': diff_norm.reshape(DEPTH, 1, 2 * DIFF_D), 'lam': diff_lambda,
         'gfin': g_final.reshape(1, D_MODEL)}
    g_norm3 = g_norm.reshape(DEPTH, 1, D_MODEL)
    ml_norm2 = ml_norm.reshape(DEPTH, 1, GROUP_W)
    gate_row = jnp.pad(ml_gate_b.reshape(DEPTH, 1, N_ML_GATES),
                       ((0, 0), (0, 0), (GATE_LANE0, LANES - GATE_LANE0 - N_ML_GATES)))

    cc = jnp.concatenate([c_ctx[None, :], c, jnp.zeros((8 - 1 - Bs, D_MODEL), F32)], axis=0)
    mod_all = _mod_call(cc, W_mod, b_mod).reshape(DEPTH, 8, 1, 3 * D_MODEL)

    dkv = cnm = ckr = None
    x2 = x_prompt.reshape(Bc * Tc, D_MODEL)
    y_prompt = None
    for l in range(DEPTH):
        l_arr = jnp.full((1,), l, jnp.int32)
        main, tail, *dkv = _proj_call(l_arr, x2, mod_all, g_norm3, w_in_t, Tc, False, dkv)
        outs = _ctx_call(l_arr, main, tail, x2, mod_all, w, ml_conv, gate_row, ml_norm2, Tc, l == DEPTH - 1, cnm, ckr)
        cnm, outs = outs[:3], outs[3:]
        ckr = outs[-2:]
        dk_o, dv_o = dkv
        c_o, n_o, m_o = cnm
        if l == DEPTH - 1:
            x3, y_prompt, ckvn, kro = outs
        else:
            x3, ckvn, kro = outs
        x2 = x3.reshape(Bc * Tc, D_MODEL)
    side_outs = (ckvn, jnp.swapaxes(kro, 2, 3),
                 dk_o.reshape(Bc, DEPTH, Tc, DIFF_HEADS, 2 * DIFF_D),
                 dv_o.reshape(Bc, DEPTH, Tc, DIFF_HEADS, 2 * DIFF_D),
                 c_o.reshape(Bc, DEPTH, 2, ML_HEADS, ML_DK, HEAD_W),
                 n_o.reshape(Bc, DEPTH, 2, ML_HEADS, ML_DK),
                 m_o[:, :, :, 0].reshape(Bc, DEPTH, 2, ML_HEADS))

    rope = _rope_tables(Ts)
    ctx = {'ckv': cache_mla_ckv,
           'krope': jnp.swapaxes(cache_mla_krope, 2, 3),
           'dk': cache_diff_k.reshape(Bs, DEPTH, PAST_LEN * DIFF_HEADS, HEAD_W),
           'dv': cache_diff_v.reshape(Bs, DEPTH, PAST_LEN * DIFF_HEADS, HEAD_W)}
    ctx_state = (state_mlstm_C.reshape(Bs, DEPTH, 2 * ML_HEADS, ML_DK, HEAD_W),
                 state_mlstm_n.reshape(Bs, DEPTH, 2 * ML_HEADS, ML_DK),
                 jnp.broadcast_to(state_mlstm_m.reshape(Bs, DEPTH, 2 * ML_HEADS, 1),
                                  (Bs, DEPTH, 2 * ML_HEADS, LANES)))
    x2 = x_sample.reshape(Bs * Ts, D_MODEL)
    y_sample = None
    for l in range(DEPTH):
        l_arr = jnp.full((1,), l, jnp.int32)
        main, tail = _proj_call(l_arr, x2, mod_all, g_norm3, w_in_t, Ts, True)
        (yc,) = _mlstm_call(l_arr, main, tail, ml_conv, gate_row, ml_norm2, Ts, True, ctx_state)
        outs = _attn_call(l_arr, main, tail, yc, x2, mod_all, w, Ts, True, l == DEPTH - 1, rope, ctx)
        if l == DEPTH - 1:
            x3, y_sample = outs
        else:
            (x3,) = outs
        x2 = x3.reshape(Bs * Ts, D_MODEL)

    return (y_prompt, y_sample, *side_outs)
```

```python
import functools
import math

import jax
import jax.numpy as jnp
import numpy as np
from jax import lax
from jax.experimental import pallas as pl
from jax.experimental.pallas import tpu as pltpu

F32 = jnp.float32
BF16 = jnp.bfloat16

D_MODEL = 1024
DEPTH = 4
PAST_LEN = 256
GRID_W = 64
GROUP_W = 512
MLA_HEADS = 4
MLA_NOPE = 128
MLA_ROPE = 64
MLA_Q_RANK = 384
MLA_KV_RANK = 256
DIFF_HEADS = 4
DIFF_D = 64
ML_HEADS = 4
ML_DK = 128
N_ML_GATES = 16
ROPE_THETA = 10000.0
NORM_EPS = 1e-6
MLA_SCALE = (MLA_NOPE + MLA_ROPE) ** -0.5
DIFF_SCALE = DIFF_D ** -0.5
LOG2E = math.log2(math.e)

LANES = 128
HEAD_W = 128
ML_CHUNK = 256
VT_ROWS = 144
PROJ_TM = 512
ATTN_TQ = 512
ATTN_PVT_MAX_KEYS = 256
ATTN_AHEAD_CTX = 12
ATTN_AHEAD_LAT = 2
MAIN_W = 10 * GROUP_W
TAIL_W = 768
N_IN = 5840
VMEM_LIMIT = 56 * 1024 * 1024

C_ZA, C_DQ, C_DK, C_DV, C_ZB, C_MQ, C_MK, C_MV, C_MO, C_ZC = range(10)

_IN_SIZES = (MLA_Q_RANK, MLA_KV_RANK, MLA_ROPE, GROUP_W, GROUP_W, GROUP_W, GROUP_W, GROUP_W,
             GROUP_W, GROUP_W, GROUP_W, GROUP_W, GROUP_W, N_ML_GATES)
_IN_OFF = np.concatenate([[0], np.cumsum(_IN_SIZES)])
(_O_CQ, _O_CKV, _O_KR, _O_ZA, _O_DQ, _O_DK, _O_DV, _O_ZB, _O_MQ, _O_MK, _O_MV, _O_MO, _O_ZC,
 _O_MG) = [int(v) for v in _IN_OFF[:-1]]
GATE_LANE0 = MLA_ROPE


def _rms(x, g):
    ms = jnp.mean(x * x, axis=-1, keepdims=True)
    return x * lax.rsqrt(ms + NORM_EPS) * g


def _sigmoid(x):
    return 0.5 + 0.5 * jnp.tanh(0.5 * x)


def _silu(x):
    hx = 0.5 * x
    return hx + hx * jnp.tanh(hx)


def _log_sigmoid(x):
    return jnp.minimum(x, 0.0) - jnp.log1p(jnp.exp(-jnp.abs(x)))


def _swap32(x):
    lane = lax.broadcasted_iota(jnp.int32, x.shape, 1)
    fwd = pltpu.roll(x, LANES - 32, 1)
    bwd = pltpu.roll(x, 32, 1)
    return jnp.where((lane % 64) < 32, fwd, bwd)


def _rope_tile(x, cos, sin):
    return x * cos + _swap32(x) * sin


def _mod_kernel(c_ref, w_ref, b_ref, o_ref):
    a = _silu(c_ref[...]).astype(BF16)
    o_ref[...] = jnp.dot(a, w_ref[...].astype(BF16), preferred_element_type=F32) + b_ref[...]


def _mod_call(cc, W_mod, b_mod):
    tn = 1024
    return pl.pallas_call(
        _mod_kernel,
        grid=(DEPTH, 3 * D_MODEL // tn),
        in_specs=[pl.BlockSpec((8, D_MODEL), lambda l, j: (0, 0)),
                  pl.BlockSpec((None, D_MODEL, tn), lambda l, j: (l, 0, j)),
                  pl.BlockSpec((None, 1, tn), lambda l, j: (l, 0, j))],
        out_specs=pl.BlockSpec((None, 8, tn), lambda l, j: (l, 0, j)),
        out_shape=jax.ShapeDtypeStruct((DEPTH, 8, 3 * D_MODEL), F32),
        compiler_params=pltpu.CompilerParams(dimension_semantics=("arbitrary", "arbitrary"),
                                             vmem_limit_bytes=VMEM_LIMIT),
        name="mod",
    )(cc, W_mod, b_mod.reshape(DEPTH, 1, 3 * D_MODEL))


def _proj_kernel(has_side, n_alias, T, *refs):
    l_ref, x_ref, mod_ref, g_ref, w_ref = refs[:5]
    refs = refs[5 + n_alias:]
    if has_side:
        main_ref, tail_ref, dk_ref, dv_ref, wt_s = refs
    else:
        main_ref, tail_ref, wt_s = refs
    del l_ref
    if has_side and n_alias == 0:
        for r in (dk_ref, dv_ref):
            r[:, 1:] = jnp.zeros((r.shape[0], DEPTH - 1) + r.shape[2:], F32)
        dk_ref, dv_ref = dk_ref.at[:, 0], dv_ref.at[:, 0]
    tm = x_ref.shape[0]
    nt = (((1,), (1,)), ((), ()))

    @pl.when(pl.program_id(0) == 0)
    def _gather_tail_rows():
        wt_s[0:MLA_ROPE, :] = w_ref[_O_KR:_O_KR + MLA_ROPE, :]
        wt_s[MLA_ROPE:MLA_ROPE + N_ML_GATES, :] = w_ref[_O_MG:_O_MG + N_ML_GATES, :]
        wt_s[MLA_ROPE + N_ML_GATES:LANES, :] = jnp.zeros((LANES - MLA_ROPE - N_ML_GATES, D_MODEL), BF16)
        wt_s[LANES:, :] = w_ref[_O_CQ:_O_CQ + MLA_Q_RANK, :]

    mod = mod_ref[...]
    y = _rms(x_ref[...], g_ref[...])
    h = (y * (1.0 + mod[:, D_MODEL:2 * D_MODEL]) + mod[:, :D_MODEL]).astype(BF16)
    for s in range(MAIN_W // GROUP_W):
        cols = slice(s * GROUP_W, (s + 1) * GROUP_W)
        acc = lax.dot_general(h, w_ref[_O_ZA + s * GROUP_W:_O_ZA + (s + 1) * GROUP_W, :], nt,
                              preferred_element_type=F32)
        if s in (C_ZA, C_ZB, C_ZC):
            main_ref[:, cols] = _silu(acc).astype(BF16)
        elif s == C_MO:
            main_ref[:, cols] = _sigmoid(acc).astype(BF16)
        else:
            main_ref[:, cols] = acc.astype(BF16)
        if has_side and s in (C_DK, C_DV):
            side_ref = dk_ref if s == C_DK else dv_ref
            for b in range(tm // T):
                for hd in range(DIFF_HEADS):
                    side_ref[b, pl.ds(hd, T, stride=DIFF_HEADS), :] = (
                        acc[b * T:(b + 1) * T, hd * HEAD_W:(hd + 1) * HEAD_W])
    tail_ref[:, 0:MLA_KV_RANK] = lax.dot_general(h, w_ref[_O_CKV:_O_CKV + MLA_KV_RANK, :], nt,
                                                 preferred_element_type=F32)
    tail_ref[:, MLA_KV_RANK:] = lax.dot_general(h, wt_s[...], nt, preferred_element_type=F32)


def _proj_call(l_arr, x2, mod_all, g_norm3, w_in_t, T, has_ctx, side_bufs=None):
    n_tok = x2.shape[0]
    tm = PROJ_TM
    has_side = not has_ctx
    if has_ctx:
        mod_map = lambda i, l: (l[0], 1 + (i * tm) // T, 0, 0)
    else:
        mod_map = lambda i, l: (l[0], 0, 0, 0)
    in_specs = [pl.BlockSpec((tm, D_MODEL), lambda i, l: (i, 0)),
                pl.BlockSpec((None, None, 1, 3 * D_MODEL), mod_map),
                pl.BlockSpec((None, 1, D_MODEL), lambda i, l: (l[0], 0, 0)),
                pl.BlockSpec((None, N_IN, D_MODEL), lambda i, l: (l[0], 0, 0), pipeline_mode=pl.Buffered(1))]
    args = [x2, mod_all, g_norm3, w_in_t]
    out_shape = [jax.ShapeDtypeStruct((n_tok, MAIN_W), BF16), jax.ShapeDtypeStruct((n_tok, TAIL_W), F32)]
    out_specs = [pl.BlockSpec((tm, MAIN_W), lambda i, l: (i, 0)), pl.BlockSpec((tm, TAIL_W), lambda i, l: (i, 0))]
    aliases = {}
    n_alias = 0
    if has_side:
        bt = tm // T
        out_shape += [jax.ShapeDtypeStruct((n_tok // T, DEPTH, T * DIFF_HEADS, HEAD_W), F32)] * 2
        if side_bufs is not None:
            n_alias = len(side_bufs)
            aliases = {1 + len(args) + i: 2 + i for i in range(n_alias)}
            in_specs += [pl.BlockSpec(memory_space=pl.ANY)] * n_alias
            args += list(side_bufs)
            out_specs += [pl.BlockSpec((bt, None, T * DIFF_HEADS, HEAD_W), lambda i, l: (i, l[0], 0, 0))] * 2
        else:
            out_specs += [pl.BlockSpec((bt, DEPTH, T * DIFF_HEADS, HEAD_W), lambda i, l: (i, 0, 0, 0))] * 2
    grid_spec = pltpu.PrefetchScalarGridSpec(
        num_scalar_prefetch=1, grid=(n_tok // tm,), in_specs=in_specs, out_specs=out_specs,
        scratch_shapes=[pltpu.VMEM((TAIL_W - MLA_KV_RANK, D_MODEL), BF16)])
    return pl.pallas_call(
        functools.partial(_proj_kernel, has_side, n_alias, T),
        grid_spec=grid_spec,
        out_shape=out_shape,
        input_output_aliases=aliases,
        compiler_params=pltpu.CompilerParams(dimension_semantics=("arbitrary",),
                                             vmem_limit_bytes=VMEM_LIMIT),
        name="proj_lat" if has_ctx else "proj_ctx",
    )(l_arr, *args)


def _mlstm_kernel(has_ctx, T, n_alias, *refs):
    nc = T // ML_CHUNK
    L = ML_CHUNK
    use_inter = has_ctx or nc > 1
    it = iter(refs)
    l_ref = next(it)
    ml_ref, aux_ref, convw_ref, gb_ref, mln_ref = [next(it) for _ in range(5)]
    mq_ref, mk_ref, mv_ref, mo_ref, zc_ref = [ml_ref.at[:, i * GROUP_W:(i + 1) * GROUP_W] for i in range(5)]
    if has_ctx:
        c0_ref, n0_ref, m0_ref = [next(it) for _ in range(3)]
    else:
        for _ in range(n_alias):
            next(it)
    yc_ref = next(it)
    if not has_ctx:
        cout_ref, nout_ref, mout_ref = [next(it) for _ in range(3)]
    q_s, k_s, vt_s, hf_s, hb_s, caug_s, m_s, gt_s, ub_s = [next(it) for _ in range(9)]
    del l_ref
    if not has_ctx and n_alias == 0:
        for r in (cout_ref, nout_ref, mout_ref):
            r[1:] = jnp.zeros((DEPTH - 1,) + r.shape[1:], F32)
        cout_ref, nout_ref, mout_ref = cout_ref.at[0], nout_ref.at[0], mout_ref.at[0]

    convw = convw_ref[...]
    halo = 16
    for c in range(nc):
        cs = slice(c * L, (c + 1) * L)
        for j in range(2 * GROUP_W // LANES):
            src = mq_ref if j < GROUP_W // LANES else mk_ref
            ls = slice((j % (GROUP_W // LANES)) * LANES, (j % (GROUP_W // LANES) + 1) * LANES)
            u = src[cs, ls].astype(F32)
            prev = src[c * L - halo:c * L, ls].astype(F32)[halo - 1:halo, :] if c > 0 else 0.0
            nxt = src[(c + 1) * L:(c + 1) * L + halo, ls].astype(F32)[0:1, :] if c < nc - 1 else 0.0
            ub_s[0:8, :] = jnp.broadcast_to(prev, (8, LANES)) if c > 0 else jnp.zeros((8, LANES), F32)
            ub_s[8:8 + L, :] = u
            ub_s[8 + L:16 + L, :] = jnp.broadcast_to(nxt, (8, LANES)) if c < nc - 1 else jnp.zeros((8, LANES), F32)
            up, un = ub_s[7:7 + L, :], ub_s[9:9 + L, :]
            w3 = convw[:, j * LANES:(j + 1) * LANES]
            y = _silu(w3[0:1, :] * up + w3[1:2, :] * u + w3[2:3, :] * un)
            if j < GROUP_W // LANES:
                q_s[cs, ls] = (y * (ML_DK ** -0.5)).astype(BF16)
            else:
                k_s[cs, ls] = y
    row16 = lax.broadcasted_iota(jnp.int32, (VT_ROWS - HEAD_W, L), 0)
    ones_rows = jnp.where(row16 == 0, 1.0, 0.0).astype(BF16)
    g_t = (aux_ref[...] + gb_ref[...]).T
    for c in range(nc):
        cs = slice(c * L, (c + 1) * L)
        gt_s[c] = g_t[GATE_LANE0:GATE_LANE0 + N_ML_GATES, cs]
        for h in range(ML_HEADS):
            vt_s[c, h, 0:HEAD_W, :] = mv_ref[cs, h * HEAD_W:(h + 1) * HEAD_W].astype(F32).T.astype(BF16)
            vt_s[c, h, HEAD_W:, :] = ones_rows

    reps = L // LANES
    if has_ctx:
        n0 = n0_ref[...]
        m0 = m0_ref[...]
        m_s[0] = jnp.concatenate([m0] * reps, axis=1)
        m_s[1] = jnp.concatenate([pltpu.roll(m0, ML_HEADS, 0)] * reps, axis=1)
        row_n = lax.broadcasted_iota(jnp.int32, (VT_ROWS - HEAD_W, ML_DK), 0)
        for r in range(2 * ML_HEADS):
            caug_s[r, 0:HEAD_W, :] = c0_ref[r].T
            caug_s[r, HEAD_W:, :] = jnp.where(row_n == 0, n0[r:r + 1, :], 0.0)
    else:
        caug_s[...] = jnp.zeros_like(caug_s)
        m_s[...] = jnp.zeros_like(m_s)

    ri = lax.broadcasted_iota(jnp.int32, (L, L), 0)
    ci = lax.broadcasted_iota(jnp.int32, (L, L), 1)
    row_r = lax.broadcasted_iota(jnp.int32, (2 * ML_HEADS, L), 0)
    nt = (((1,), (1,)), ((), ()))

    def split3(x):
        hi = x.astype(BF16).astype(F32)
        mid = (x - hi).astype(BF16).astype(F32)
        return hi, mid, (x - hi - mid).astype(BF16).astype(F32)

    masks = ((ri <= ci), (ri >= ci))

    def gate_rows(d, c):
        fwd = d == 0
        tri_t = jnp.where(masks[d], 1.0, 0.0).astype(BF16)
        last = L - 1 if fwd else 0

        g8 = gt_s[c, d * 2 * ML_HEADS:(d + 1) * 2 * ML_HEADS, :]
        hi8, mid8, lo8 = split3(_log_sigmoid(g8))
        stack = jnp.concatenate([hi8, mid8, lo8, jnp.zeros_like(hi8)], axis=0).astype(BF16)
        part8 = jnp.dot(stack, tri_t, preferred_element_type=F32)
        bc8 = part8[0:8] + part8[8:16] + part8[16:24]
        bcs = pltpu.roll(bc8, ML_HEADS, 0)
        a8 = g8 - bcs
        a_n = jnp.concatenate([a8, jnp.zeros((LANES - 2 * ML_HEADS, L), F32)], axis=0).T
        cm8 = jnp.zeros((2 * ML_HEADS, L), F32)
        for h in range(ML_HEADS):
            col_max = jnp.max(jnp.where(masks[d], a_n[:, h:h + 1], -jnp.inf), axis=0, keepdims=True)
            cm8 = jnp.where(row_r == h, col_max, cm8)
        m8 = m_s[d]
        g_row = bcs + m8
        m_t = jnp.maximum(g_row, bcs + cm8)
        w_inter = jnp.exp(g_row - m_t)
        e_inv = jnp.exp(-m_t)
        c_row = bcs - m_t
        b_last = bcs[:, last:last + 1]
        m_new = m_t[:, last:last + 1]
        a_prev = jnp.exp(b_last + m8[:, 0:1] - m_new)
        w_s = jnp.exp(a8 + (b_last - m_new))
        m_s[d] = jnp.broadcast_to(m_new, (2 * ML_HEADS, L))
        return a_n, c_row, w_inter, e_inv, w_s, a_prev

    def do_pair(c_f, c_b):
        stats = (gate_rows(0, c_f), gate_rows(1, c_b))
        combos = [(d, h) for d in range(2) for h in range(ML_HEADS)]
        rows = []
        for c in (c_f, c_b):
            rows.append(slice(c * L, (c + 1) * L) if isinstance(c, int) else pl.ds(pl.multiple_of(c * L, L), L))
        chunk = (c_f, c_b)
        hsl = [slice(h * HEAD_W, (h + 1) * HEAD_W) for h in range(ML_HEADS)]
        qcs = [q_s[rows[d], hsl[h]] for d, h in combos]
        kcs = [k_s[rows[d], hsl[h]].astype(BF16) for d, h in combos]
        vts = [vt_s[chunk[d], h] for d, h in combos]
        n = range(len(combos))
        s_ts = [lax.dot_general(kcs[i], qcs[i], nt, preferred_element_type=F32) for i in n]
        sws = []
        for i, (d, h) in enumerate(combos):
            a_n, c_row = stats[d][0], stats[d][1]
            w_t = jnp.exp(jnp.where(masks[d], a_n[:, h:h + 1] + c_row[h:h + 1, :], -jnp.inf))
            sws.append((s_ts[i] * w_t).astype(BF16))
        nds = [jnp.dot(vts[i], sws[i], preferred_element_type=F32) for i in n]
        if use_inter:
            inters = [lax.dot_general(caug_s[d * ML_HEADS + h].astype(BF16), qcs[i], nt,
                                      preferred_element_type=F32) for i, (d, h) in enumerate(combos)]
            nds = [nds[i] + stats[d][2][h:h + 1, :] * inters[i] for i, (d, h) in enumerate(combos)]
        for i, (d, h) in enumerate(combos):
            inv = 1.0 / jnp.maximum(jnp.abs(nds[i][HEAD_W:HEAD_W + 1, :]), stats[d][3][h:h + 1, :])
            h_s = hf_s if d == 0 else hb_s
            h_s[rows[d], hsl[h]] = (nds[i][0:HEAD_W, :] * inv).T
        upds = [jnp.dot((vts[i].astype(F32) * stats[d][4][h:h + 1, :]).astype(BF16), kcs[i],
                        preferred_element_type=F32) for i, (d, h) in enumerate(combos)]
        for i, (d, h) in enumerate(combos):
            r = d * ML_HEADS + h
            caug_s[r] = upds[i] + stats[d][5][h:h + 1, :] * caug_s[r] if use_inter else upds[i]

    for i in range(nc):
        do_pair(i, nc - 1 - i)

    mln = mln_ref[...]
    for c in range(nc):
        cs = slice(c * L, (c + 1) * L)
        for h in range(ML_HEADS):
            hs = slice(h * HEAD_W, (h + 1) * HEAD_W)
            hc = mo_ref[cs, hs].astype(F32) * (hf_s[cs, hs] + hb_s[cs, hs])
            yc_ref[cs, hs] = (_rms(hc, mln[:, hs]) * zc_ref[cs, hs].astype(F32)).astype(BF16)

    if not has_ctx:
        for r in range(2 * ML_HEADS):
            cout_ref[r] = caug_s[r, 0:HEAD_W, :].T
            nout_ref[r:r + 1, :] = caug_s[r, HEAD_W:HEAD_W + 1, :]
        for d in range(2):
            mout_ref[d * ML_HEADS:(d + 1) * ML_HEADS, :] = m_s[d, 0:ML_HEADS, 0:LANES]


def _mlstm_call(l_arr, main, tail, ml_conv, gate_row, ml_norm2, T, has_ctx, ctx_state=None, side_bufs=None):
    n_tok = main.shape[0]
    B = n_tok // T
    main3 = main.reshape(B, T, MAIN_W)
    tail3 = tail.reshape(B, T, TAIL_W)

    assert (C_MQ, C_ZC) == (5, 9)
    in_specs = [pl.BlockSpec((None, T, MAIN_W // 2), lambda b, l: (b, 0, 1)),
                pl.BlockSpec((None, T, LANES), lambda b, l: (b, 0, 2)),
                pl.BlockSpec((None, 3, 2 * GROUP_W), lambda b, l: (l[0], 0, 0)),
                pl.BlockSpec((None, 1, LANES), lambda b, l: (l[0], 0, 0)),
                pl.BlockSpec((None, 1, GROUP_W), lambda b, l: (l[0], 0, 0))]
    args = [main3, tail3, ml_conv, gate_row, ml_norm2]
    n_alias = 0 if (has_ctx or side_bufs is None) else len(side_bufs)
    out_shape = [jax.ShapeDtypeStruct((B, T, GROUP_W), BF16)]
    out_specs = [pl.BlockSpec((None, T, GROUP_W), lambda b, l: (b, 0, 0))]
    if has_ctx:
        c0, n0, m0 = ctx_state
        in_specs += [pl.BlockSpec((None, None, 2 * ML_HEADS, ML_DK, HEAD_W), lambda b, l: (b, l[0], 0, 0, 0)),
                     pl.BlockSpec((None, None, 2 * ML_HEADS, ML_DK), lambda b, l: (b, l[0], 0, 0)),
                     pl.BlockSpec((None, None, 2 * ML_HEADS, LANES), lambda b, l: (b, l[0], 0, 0))]
        args += [c0, n0, m0]
        aliases = {}
    else:
        aliases = {}
        out_shape += [jax.ShapeDtypeStruct((B, DEPTH, 2 * ML_HEADS, ML_DK, HEAD_W), F32),
                      jax.ShapeDtypeStruct((B, DEPTH, 2 * ML_HEADS, ML_DK), F32),
                      jax.ShapeDtypeStruct((B, DEPTH, 2 * ML_HEADS, LANES), F32)]
        if n_alias:
            aliases = {1 + len(args) + i: 1 + i for i in range(n_alias)}
            in_specs += [pl.BlockSpec(memory_space=pl.ANY)] * n_alias
            args += list(side_bufs)
            out_specs += [pl.BlockSpec((None, None, 2 * ML_HEADS, ML_DK, HEAD_W), lambda b, l: (b, l[0], 0, 0, 0)),
                          pl.BlockSpec((None, None, 2 * ML_HEADS, ML_DK), lambda b, l: (b, l[0], 0, 0)),
                          pl.BlockSpec((None, None, 2 * ML_HEADS, LANES), lambda b, l: (b, l[0], 0, 0))]
        else:
            out_specs += [pl.BlockSpec((None, DEPTH, 2 * ML_HEADS, ML_DK, HEAD_W), lambda b, l: (b, 0, 0, 0, 0)),
                          pl.BlockSpec((None, DEPTH, 2 * ML_HEADS, ML_DK), lambda b, l: (b, 0, 0, 0)),
                          pl.BlockSpec((None, DEPTH, 2 * ML_HEADS, LANES), lambda b, l: (b, 0, 0, 0))]
    grid_spec = pltpu.PrefetchScalarGridSpec(
        num_scalar_prefetch=1, grid=(B,), in_specs=in_specs, out_specs=out_specs,
        scratch_shapes=[pltpu.VMEM((T, GROUP_W), BF16),
                        pltpu.VMEM((T, GROUP_W), F32),
                        pltpu.VMEM((T // ML_CHUNK, ML_HEADS, VT_ROWS, ML_CHUNK), BF16),
                        pltpu.VMEM((T, GROUP_W), F32),
                        pltpu.VMEM((T, GROUP_W), F32),
                        pltpu.VMEM((2 * ML_HEADS, VT_ROWS, ML_DK), F32),
                        pltpu.VMEM((2, 2 * ML_HEADS, ML_CHUNK), F32),
                        pltpu.VMEM((T // ML_CHUNK, N_ML_GATES, ML_CHUNK), F32),
                        pltpu.VMEM((ML_CHUNK + 16, LANES), F32)])
    return pl.pallas_call(
        functools.partial(_mlstm_kernel, has_ctx, T, n_alias),
        grid_spec=grid_spec,
        out_shape=out_shape,
        input_output_aliases=aliases,
        compiler_params=pltpu.CompilerParams(dimension_semantics=("arbitrary",),
                                             vmem_limit_bytes=VMEM_LIMIT),
        name="mlstm_lat" if has_ctx else "mlstm_ctx",
    )(l_arr, *args)


def _attn_kernel(has_ctx, is_last, merged, n_alias, T, *refs):
    Tk = T + (PAST_LEN if has_ctx else 0)
    pv_t = Tk <= ATTN_PVT_MAX_KEYS
    it = iter(refs)
    l_ref = next(it)
    if merged:
        tail_ref, att_ref = next(it), next(it)
        ckv_ref = tail_ref.at[:, 0:MLA_KV_RANK]
        aux_ref = tail_ref.at[:, MLA_KV_RANK:MLA_KV_RANK + LANES]
        cq_ref = tail_ref.at[:, MLA_KV_RANK + LANES:]
        za_ref, dq_ref, dk_ref, dv_ref, zb_ref = [att_ref.at[:, i * GROUP_W:(i + 1) * GROUP_W] for i in range(5)]
    else:
        cq_ref, ckv_ref, aux_ref, dq_ref, dk_ref, dv_ref, zb_ref, za_ref = [next(it) for _ in range(8)]
    (yc_ref, x_ref, mod_ref, wuq_ref, wukv_ref, wout_ref, gq_ref, gkv_ref, gdn_ref,
     lam_ref) = [next(it) for _ in range(10)]
    if is_last:
        gfin_ref = next(it)
    if has_ctx:
        cosq_ref, sinq_ref, cosk_ref, sink_ref, cckv_ref, ckr_ref, cdk_ref, cdv_ref = [next(it) for _ in range(8)]
    else:
        for _ in range(n_alias):
            next(it)
    xo_ref = next(it)
    if is_last:
        yfin_ref = next(it)
    if not has_ctx:
        ckvn_ref, kro_ref = [next(it) for _ in range(2)]
    ka_s, va_s, kb_s, vb_s, ycat_s = [next(it) for _ in range(5)]
    first = not has_ctx and n_alias == 0
    if first:
        ckvn_all, kro_all = ckvn_ref, kro_ref
        ckvn_ref, kro_ref = ckvn_ref.at[0], kro_ref.at[0]

    def _build_keys():
        wukv = wukv_ref[...]
        ckv_n = _rms(ckv_ref[...], gkv_ref[...])
        aux = aux_ref[...]
        if not has_ctx:
            ckvn_ref[...] = ckv_n
            kro_ref[...] = aux.T[0:MLA_ROPE, :]
        if first:
            for r in (ckvn_all, kro_all):
                r[1:] = jnp.zeros((DEPTH - 1,) + r.shape[1:], F32)
        kv = jnp.dot(ckv_n.astype(BF16), wukv, preferred_element_type=F32)
        lane = lax.broadcasted_iota(jnp.int32, aux.shape, 1)
        kr = _rope_tile(aux, cosk_ref[...], sink_ref[...]) if has_ctx else aux
        kr = jnp.where(lane < MLA_ROPE, kr, 0.0).astype(BF16)
        for h in range(MLA_HEADS):
            ka_s[0:T, 2 * h * HEAD_W:(2 * h + 1) * HEAD_W] = kv[:, 2 * h * HEAD_W:(2 * h + 1) * HEAD_W].astype(BF16)
            ka_s[0:T, (2 * h + 1) * HEAD_W:(2 * h + 2) * HEAD_W] = kr
        if pv_t:
            row16 = lax.broadcasted_iota(jnp.int32, (VT_ROWS - HEAD_W, Tk), 0)
            ones_rows = jnp.where(row16 == 0, 1.0, 0.0).astype(BF16)
            for h in range(MLA_HEADS):
                hs = slice(h * HEAD_W, (h + 1) * HEAD_W)
                va_s[h, 0:HEAD_W, 0:T] = kv[:, (2 * h + 1) * HEAD_W:(2 * h + 2) * HEAD_W].T.astype(BF16)
                va_s[h, HEAD_W:, :] = ones_rows
                vb_s[h, 0:HEAD_W, 0:T] = dv_ref[:, hs].astype(F32).T.astype(BF16)
                vb_s[h, HEAD_W:, :] = ones_rows
        else:
            lane_k = lax.broadcasted_iota(jnp.int32, (Tk, LANES), 1)
            ones_cols = jnp.where(lane_k == 0, 1.0, 0.0).astype(BF16)
            for h in range(MLA_HEADS):
                hs = slice(h * HEAD_W, (h + 1) * HEAD_W)
                va_s[0:T, 2 * h * HEAD_W:(2 * h + 1) * HEAD_W] = kv[:, (2 * h + 1) * HEAD_W:(2 * h + 2) * HEAD_W].astype(BF16)
                va_s[:, (2 * h + 1) * HEAD_W:(2 * h + 2) * HEAD_W] = ones_cols
                vb_s[0:T, 2 * h * HEAD_W:(2 * h + 1) * HEAD_W] = dv_ref[:, hs]
                vb_s[:, (2 * h + 1) * HEAD_W:(2 * h + 2) * HEAD_W] = ones_cols
        if has_ctx:
            for h in range(DIFF_HEADS):
                hs = slice(h * HEAD_W, (h + 1) * HEAD_W)
                kb_s[0:T, hs] = _rope_tile(dk_ref[:, hs].astype(F32), cosk_ref[...], sink_ref[...]).astype(BF16)
            kvc = jnp.dot(cckv_ref[...].astype(BF16), wukv, preferred_element_type=F32)
            ckr = jnp.concatenate([ckr_ref[...], jnp.zeros((LANES - MLA_ROPE, PAST_LEN), F32)], axis=0).T.astype(BF16)
            for h in range(MLA_HEADS):
                ka_s[T:Tk, 2 * h * HEAD_W:(2 * h + 1) * HEAD_W] = kvc[:, 2 * h * HEAD_W:(2 * h + 1) * HEAD_W].astype(BF16)
                ka_s[T:Tk, (2 * h + 1) * HEAD_W:(2 * h + 2) * HEAD_W] = ckr
            for h in range(DIFF_HEADS):
                hs = slice(h * HEAD_W, (h + 1) * HEAD_W)
                kb_s[T:Tk, hs] = cdk_ref[pl.ds(h, PAST_LEN, stride=DIFF_HEADS), :].astype(BF16)
                v_c = kvc[:, (2 * h + 1) * HEAD_W:(2 * h + 2) * HEAD_W]
                vd_c = cdv_ref[pl.ds(h, PAST_LEN, stride=DIFF_HEADS), :]
                if pv_t:
                    va_s[h, 0:HEAD_W, T:Tk] = v_c.T.astype(BF16)
                    vb_s[h, 0:HEAD_W, T:Tk] = vd_c.T.astype(BF16)
                else:
                    va_s[T:Tk, 2 * h * HEAD_W:(2 * h + 1) * HEAD_W] = v_c.astype(BF16)
                    vb_s[T:Tk, 2 * h * HEAD_W:(2 * h + 1) * HEAD_W] = vd_c.astype(BF16)
        else:
            kb_s[0:T, :] = dk_ref[...]

    if merged:
        _build_keys()
    else:
        pl.when(pl.program_id(1) == 0)(_build_keys)

    nt = (((1,), (1,)), ((), ()))
    tq = cq_ref.shape[0]

    maps = []
    qa = jnp.dot(_rms(cq_ref[...], gq_ref[...]).astype(BF16), wuq_ref[...], preferred_element_type=F32)
    for h in range(MLA_HEADS):
        q_nope = qa[:, 2 * h * HEAD_W:(2 * h + 1) * HEAD_W]
        q_rope = qa[:, (2 * h + 1) * HEAD_W:(2 * h + 2) * HEAD_W]
        if has_ctx:
            q_rope = _rope_tile(q_rope, cosq_ref[...], sinq_ref[...])
        q_h = (jnp.concatenate([q_nope, q_rope], axis=-1) * (MLA_SCALE * LOG2E)).astype(BF16)
        two = slice(2 * h * HEAD_W, (2 * h + 2) * HEAD_W)
        v_load = functools.partial(lambda hh: va_s[hh], h) if pv_t else functools.partial(lambda sl: va_s[:, sl], two)
        maps.append((q_h, functools.partial(lambda sl: ka_s[:, sl], two), v_load))
    lane_q = lax.broadcasted_iota(jnp.int32, (tq, HEAD_W), 1)
    for h in range(DIFF_HEADS):
        hs = slice(h * HEAD_W, (h + 1) * HEAD_W)
        q_h = dq_ref[:, hs].astype(F32)
        if has_ctx:
            q_h = _rope_tile(q_h, cosq_ref[...], sinq_ref[...])
        q_h = q_h * (DIFF_SCALE * LOG2E)
        two = slice(2 * h * HEAD_W, (2 * h + 2) * HEAD_W)
        v_load = functools.partial(lambda hh: vb_s[hh], h) if pv_t else functools.partial(lambda sl: vb_s[:, sl], two)
        for q_m in (jnp.where(lane_q < DIFF_D, q_h, 0.0), jnp.where(lane_q >= DIFF_D, q_h, 0.0)):
            maps.append((q_m.astype(BF16), functools.partial(lambda sl: kb_s[:, sl], hs), v_load))

    def scores(i):
        if pv_t:
            return lax.dot_general(maps[i][1](), maps[i][0], nt, preferred_element_type=F32)
        return lax.dot_general(maps[i][0], maps[i][1](), nt, preferred_element_type=F32)

    def softmax_pv(i, s):
        if pv_t:
            e = jnp.exp2(s - jnp.max(s, axis=0, keepdims=True))
            pv = jnp.dot(maps[i][2](), e.astype(BF16), preferred_element_type=F32)
            return (pv[0:HEAD_W, :] * (1.0 / pv[HEAD_W:HEAD_W + 1, :])).T
        e = jnp.exp2(s - jnp.max(s, axis=-1, keepdims=True))
        pv = jnp.dot(e.astype(BF16), maps[i][2](), preferred_element_type=F32)
        return pv[:, :HEAD_W] * (1.0 / pv[:, HEAD_W:HEAD_W + 1])

    n_maps = len(maps)
    ahead = min(ATTN_AHEAD_CTX if not has_ctx else ATTN_AHEAD_LAT, n_maps)
    pending = {i: scores(i) for i in range(ahead)}
    outs = []
    for i in range(n_maps):
        if i + ahead < n_maps:
            pending[i + ahead] = scores(i + ahead)
        outs.append(softmax_pv(i, pending.pop(i)))

    for h in range(MLA_HEADS):
        hs = slice(h * HEAD_W, (h + 1) * HEAD_W)
        ycat_s[:, hs] = (outs[h] * za_ref[:, hs].astype(F32)).astype(BF16)

    lp = lam_ref[...]
    lf = l_ref[0].astype(F32)
    lam_init = 0.8 - 0.6 * jnp.exp(jnp.full((1, 1), -0.3, F32) * lf)
    lam = (jnp.exp(jnp.sum(lp[0:1, :] * lp[1:2, :], axis=-1, keepdims=True))
           - jnp.exp(jnp.sum(lp[2:3, :] * lp[3:4, :], axis=-1, keepdims=True)) + lam_init)
    for h in range(DIFF_HEADS):
        hs = slice(h * HEAD_W, (h + 1) * HEAD_W)
        o1, o2 = outs[MLA_HEADS + 2 * h], outs[MLA_HEADS + 2 * h + 1]
        o = _rms(o1 - lam * o2, gdn_ref[...]) * (1.0 - lam_init)
        ycat_s[:, GROUP_W + h * HEAD_W:GROUP_W + (h + 1) * HEAD_W] = (o * zb_ref[:, hs].astype(F32)).astype(BF16)

    ycat_s[:, 2 * GROUP_W:] = yc_ref[...]
    y = jnp.dot(ycat_s[...], wout_ref[...], preferred_element_type=F32)
    x_new = x_ref[...] + mod_ref[:, 2 * D_MODEL:] * y
    xo_ref[...] = x_new
    if is_last:
        yfin_ref[...] = _rms(x_new, gfin_ref[...])


def _attn_call(l_arr, main, tail, yc, x2, mod_all, w, T, has_ctx, is_last, rope=None, ctx=None, side_bufs=None):
    n_tok = main.shape[0]
    B = n_tok // T
    tq = min(ATTN_TQ, T)
    nq = T // tq
    Tk = T + (PAST_LEN if has_ctx else 0)
    main3 = main.reshape(B, T, MAIN_W)
    tail3 = tail.reshape(B, T, TAIL_W)
    x3 = x2.reshape(B, T, D_MODEL)

    def tile(c, width):
        return pl.BlockSpec((None, tq, width), lambda b, q, l, c=c: (b, q, c))

    def full(c, width):
        return pl.BlockSpec((None, T, width), lambda b, q, l, c=c: (b, 0, c))

    def wspec(shape):
        return pl.BlockSpec((None,) + shape, lambda b, q, l: (l[0],) + (0,) * len(shape))

    if has_ctx:
        mod_map = lambda b, q, l: (l[0], 1 + b, 0, 0)
    else:
        mod_map = lambda b, q, l: (l[0], 0, 0, 0)

    merged = nq == 1
    if merged:
        assert (C_ZA, C_ZB) == (0, 4)
        in_specs = [pl.BlockSpec((None, T, TAIL_W), lambda b, q, l: (b, 0, 0)),
                    pl.BlockSpec((None, T, MAIN_W // 2), lambda b, q, l: (b, 0, 0))]
        args = [tail3, main3]
    else:
        in_specs = [tile(1, MLA_Q_RANK),
                    full(0, MLA_KV_RANK),
                    full(2, LANES),
                    tile(C_DQ, GROUP_W), full(C_DK, GROUP_W), full(C_DV, GROUP_W),
                    tile(C_ZB, GROUP_W), tile(C_ZA, GROUP_W)]
        args = [tail3, tail3, tail3, main3, main3, main3, main3, main3]
    in_specs += [pl.BlockSpec((None, tq, GROUP_W), lambda b, q, l: (b, q, 0)),
                 pl.BlockSpec((None, tq, D_MODEL), lambda b, q, l: (b, q, 0)),
                 pl.BlockSpec((None, None, 1, 3 * D_MODEL), mod_map),
                 wspec((MLA_Q_RANK, 2 * GROUP_W)), wspec((MLA_KV_RANK, 2 * GROUP_W)),
                 wspec((3 * GROUP_W, D_MODEL)),
                 wspec((1, MLA_Q_RANK)), wspec((1, MLA_KV_RANK)), wspec((1, HEAD_W)),
                 wspec((4, DIFF_D))]
    args += [yc, x3, mod_all, w['wuq'], w['wukv'], w['wout'], w['gq'], w['gkv'], w['gdn'], w['lam']]
    if is_last:
        in_specs.append(pl.BlockSpec((1, D_MODEL), lambda b, q, l: (0, 0)))
        args.append(w['gfin'])
    if has_ctx:
        cos_t, sin_t = rope
        in_specs += [pl.BlockSpec((tq, LANES), lambda b, q, l: (q, 0)),
                     pl.BlockSpec((tq, LANES), lambda b, q, l: (q, 0)),
                     pl.BlockSpec((T, LANES), lambda b, q, l: (0, 0)),
                     pl.BlockSpec((T, LANES), lambda b, q, l: (0, 0)),
                     pl.BlockSpec((None, None, PAST_LEN, MLA_KV_RANK), lambda b, q, l: (b, l[0], 0, 0)),
                     pl.BlockSpec((None, None, MLA_ROPE, PAST_LEN), lambda b, q, l: (b, l[0], 0, 0)),
                     pl.BlockSpec((None, None, PAST_LEN * DIFF_HEADS, HEAD_W), lambda b, q, l: (b, l[0], 0, 0)),
                     pl.BlockSpec((None, None, PAST_LEN * DIFF_HEADS, HEAD_W), lambda b, q, l: (b, l[0], 0, 0))]
        args += [cos_t, sin_t, cos_t, sin_t, ctx['ckv'], ctx['krope'], ctx['dk'], ctx['dv']]
    out_shape = [jax.ShapeDtypeStruct((B, T, D_MODEL), F32)]
    out_specs = [pl.BlockSpec((None, tq, D_MODEL), lambda b, q, l: (b, q, 0))]
    if is_last:
        out_shape.append(jax.ShapeDtypeStruct((B, T, D_MODEL), F32))
        out_specs.append(pl.BlockSpec((None, tq, D_MODEL), lambda b, q, l: (b, q, 0)))
    aliases = {}
    n_alias = 0
    if not has_ctx:
        if side_bufs is not None:
            n_alias = len(side_bufs)
            n_in = 1 + len(args)
            aliases = {n_in + i: len(out_shape) + i for i in range(n_alias)}
            in_specs += [pl.BlockSpec(memory_space=pl.ANY)] * n_alias
            args += list(side_bufs)
            out_specs += [pl.BlockSpec((None, None, T, MLA_KV_RANK), lambda b, q, l: (b, l[0], 0, 0)),
                          pl.BlockSpec((None, None, MLA_ROPE, T), lambda b, q, l: (b, l[0], 0, 0))]
        else:
            assert nq == 1
            out_specs += [pl.BlockSpec((None, DEPTH, T, MLA_KV_RANK), lambda b, q, l: (b, 0, 0, 0)),
                          pl.BlockSpec((None, DEPTH, MLA_ROPE, T), lambda b, q, l: (b, 0, 0, 0))]
        out_shape += [jax.ShapeDtypeStruct((B, DEPTH, T, MLA_KV_RANK), F32),
                      jax.ShapeDtypeStruct((B, DEPTH, MLA_ROPE, T), F32)]
    if Tk <= ATTN_PVT_MAX_KEYS:
        v_scratch = pltpu.VMEM((MLA_HEADS, VT_ROWS, Tk), BF16)
    else:
        v_scratch = pltpu.VMEM((Tk, 2 * GROUP_W), BF16)
    grid_spec = pltpu.PrefetchScalarGridSpec(
        num_scalar_prefetch=1, grid=(B, nq), in_specs=in_specs, out_specs=out_specs,
        scratch_shapes=[pltpu.VMEM((Tk, 2 * GROUP_W), BF16),
                        v_scratch,
                        pltpu.VMEM((Tk, GROUP_W), BF16),
                        v_scratch,
                        pltpu.VMEM((tq, 3 * GROUP_W), BF16)])
    return pl.pallas_call(
        functools.partial(_attn_kernel, has_ctx, is_last, merged, n_alias, T),
        grid_spec=grid_spec,
        out_shape=out_shape,
        input_output_aliases=aliases,
        compiler_params=pltpu.CompilerParams(dimension_semantics=("arbitrary", "arbitrary"),
                                             vmem_limit_bytes=VMEM_LIMIT),
        name="attn_lat" if has_ctx else "attn_ctx",
    )(l_arr, *args)


def _ctx_kernel(is_last, n_alias_m, n_alias_a, T, *refs):
    l_ref = refs[0]
    n_m_in = 5 + n_alias_m
    n_a_in = 2 + 9 + (1 if is_last else 0) + n_alias_a
    m_in = refs[1:1 + n_m_in]
    a_in = refs[1 + n_m_in:1 + n_m_in + n_a_in]
    rest = refs[1 + n_m_in + n_a_in:]
    m_out, rest = rest[:3], rest[3:]
    n_a_out = 3 + (1 if is_last else 0)
    a_out, rest = rest[:n_a_out], rest[n_a_out:]
    m_scr, a_scr, yc_s = rest[:9], rest[9:14], rest[14]
    _mlstm_kernel(False, T, n_alias_m, l_ref, *m_in, yc_s, *m_out, *m_scr)
    _attn_kernel(False, is_last, True, n_alias_a, T, l_ref, a_in[0], a_in[1], yc_s, *a_in[2:], *a_out, *a_scr)


def _ctx_call(l_arr, main, tail, x2, mod_all, w, ml_conv, gate_row, ml_norm2, T, is_last, state_bufs, kv_bufs):
    n_tok = main.shape[0]
    B = n_tok // T
    main3 = main.reshape(B, T, MAIN_W)
    tail3 = tail.reshape(B, T, TAIL_W)
    x3 = x2.reshape(B, T, D_MODEL)
    assert (C_ZA, C_ZB, C_MQ, C_ZC) == (0, 4, 5, 9) and T == ML_CHUNK and T <= ATTN_TQ

    def wspec(shape):
        return pl.BlockSpec((None,) + shape, lambda b, q, l: (l[0],) + (0,) * len(shape))

    def side(first, shapes):
        if first:
            return [pl.BlockSpec((None, DEPTH) + s, lambda b, q, l, n=len(s): (b, 0) + (0,) * n) for s in shapes]
        return [pl.BlockSpec((None, None) + s, lambda b, q, l, n=len(s): (b, l[0]) + (0,) * n) for s in shapes]

    any_spec = pl.BlockSpec(memory_space=pl.ANY)
    n_alias_m = 0 if state_bufs is None else len(state_bufs)
    n_alias_a = 0 if kv_bufs is None else len(kv_bufs)
    in_specs = [pl.BlockSpec((None, T, MAIN_W // 2), lambda b, q, l: (b, 0, 1)),
                pl.BlockSpec((None, T, LANES), lambda b, q, l: (b, 0, 2)),
                wspec((3, 2 * GROUP_W)), wspec((1, LANES)), wspec((1, GROUP_W))] + [any_spec] * n_alias_m
    args = [main3, tail3, ml_conv, gate_row, ml_norm2] + list(state_bufs or ())
    alias_m_at = 1 + 5
    in_specs += [pl.BlockSpec((None, T, TAIL_W), lambda b, q, l: (b, 0, 0)),
                 pl.BlockSpec((None, T, MAIN_W // 2), lambda b, q, l: (b, 0, 0)),
                 pl.BlockSpec((None, T, D_MODEL), lambda b, q, l: (b, 0, 0)),
                 pl.BlockSpec((None, None, 1, 3 * D_MODEL), lambda b, q, l: (l[0], 0, 0, 0)),
                 wspec((MLA_Q_RANK, 2 * GROUP_W)), wspec((MLA_KV_RANK, 2 * GROUP_W)), wspec((3 * GROUP_W, D_MODEL)),
                 wspec((1, MLA_Q_RANK)), wspec((1, MLA_KV_RANK)), wspec((1, HEAD_W)), wspec((4, DIFF_D))]
    args += [tail3, main3, x3, mod_all, w['wuq'], w['wukv'], w['wout'], w['gq'], w['gkv'], w['gdn'], w['lam']]
    if is_last:
        in_specs.append(pl.BlockSpec((1, D_MODEL), lambda b, q, l: (0, 0)))
        args.append(w['gfin'])
    alias_a_at = 1 + len(args)
    in_specs += [any_spec] * n_alias_a
    args += list(kv_bufs or ())

    state_shapes = [(2 * ML_HEADS, ML_DK, HEAD_W), (2 * ML_HEADS, ML_DK), (2 * ML_HEADS, LANES)]
    kv_shapes = [(T, MLA_KV_RANK), (MLA_ROPE, T)]
    out_shape = [jax.ShapeDtypeStruct((B, DEPTH) + s, F32) for s in state_shapes]
    out_specs = side(n_alias_m == 0, state_shapes)
    out_shape.append(jax.ShapeDtypeStruct((B, T, D_MODEL), F32))
    out_specs.append(pl.BlockSpec((None, T, D_MODEL), lambda b, q, l: (b, 0, 0)))
    if is_last:
        out_shape.append(jax.ShapeDtypeStruct((B, T, D_MODEL), F32))
        out_specs.append(pl.BlockSpec((None, T, D_MODEL), lambda b, q, l: (b, 0, 0)))
    kv_out_at = len(out_shape)
    out_shape += [jax.ShapeDtypeStruct((B, DEPTH) + s, F32) for s in kv_shapes]
    out_specs += side(n_alias_a == 0, kv_shapes)
    aliases = {alias_m_at + i: i for i in range(n_alias_m)}
    aliases.update({alias_a_at + i: kv_out_at + i for i in range(n_alias_a)})

    grid_spec = pltpu.PrefetchScalarGridSpec(
        num_scalar_prefetch=1, grid=(B, 1), in_specs=in_specs, out_specs=out_specs,
        scratch_shapes=[pltpu.VMEM((T, GROUP_W), BF16),
                        pltpu.VMEM((T, GROUP_W), F32),
                        pltpu.VMEM((1, ML_HEADS, VT_ROWS, ML_CHUNK), BF16),
                        pltpu.VMEM((T, GROUP_W), F32),
                        pltpu.VMEM((T, GROUP_W), F32),
                        pltpu.VMEM((2 * ML_HEADS, VT_ROWS, ML_DK), F32),
                        pltpu.VMEM((2, 2 * ML_HEADS, ML_CHUNK), F32),
                        pltpu.VMEM((1, N_ML_GATES, ML_CHUNK), F32),
                        pltpu.VMEM((ML_CHUNK + 16, LANES), F32),
                        pltpu.VMEM((T, 2 * GROUP_W), BF16),
                        pltpu.VMEM((MLA_HEADS, VT_ROWS, T), BF16),
                        pltpu.VMEM((T, GROUP_W), BF16),
                        pltpu.VMEM((DIFF_HEADS, VT_ROWS, T), BF16),
                        pltpu.VMEM((T, 3 * GROUP_W), BF16),
                        pltpu.VMEM((T, GROUP_W), BF16)])
    return pl.pallas_call(
        functools.partial(_ctx_kernel, is_last, n_alias_m, n_alias_a, T),
        grid_spec=grid_spec,
        out_shape=out_shape,
        input_output_aliases=aliases,
        compiler_params=pltpu.CompilerParams(dimension_semantics=("arbitrary", "arbitrary"),
                                             vmem_limit_bytes=VMEM_LIMIT),
        name="ctx_mix",
    )(l_arr, *args)


def _rope_tables(n_tok):
    n_freq = MLA_ROPE // 4
    inv = ROPE_THETA ** (-jnp.arange(n_freq, dtype=F32) / n_freq)
    n_rows = n_tok // GRID_W
    rowp = jnp.repeat(jnp.arange(n_rows, dtype=F32), GRID_W)
    colp = jnp.tile(jnp.arange(GRID_W, dtype=F32), n_rows)
    ang = jnp.concatenate([rowp[:, None] * inv, colp[:, None] * inv], axis=-1)
    cos, sin = jnp.cos(ang), jnp.sin(ang)
    cos64 = jnp.concatenate([cos, cos], axis=-1)
    sin64 = jnp.concatenate([-sin, sin], axis=-1)
    return jnp.concatenate([cos64, cos64], axis=-1), jnp.concatenate([sin64, sin64], axis=-1)


def _prep_weights(W_in, W_uq, W_ukv, W_out):
    w_in_t = jnp.swapaxes(W_in, 1, 2).astype(BF16)
    wq = W_uq.reshape(DEPTH, MLA_Q_RANK, MLA_HEADS, MLA_NOPE + MLA_ROPE)
    wq = jnp.pad(wq, ((0, 0), (0, 0), (0, 0), (0, 2 * HEAD_W - MLA_NOPE - MLA_ROPE)))
    wuq_r = wq.reshape(DEPTH, MLA_Q_RANK, MLA_HEADS * 2 * HEAD_W).astype(BF16)
    return w_in_t, wuq_r, W_ukv.astype(BF16), W_out.astype(BF16)


def kernel(x_prompt, x_sample, cache_mla_ckv, cache_mla_krope, cache_diff_k, cache_diff_v, state_mlstm_C, state_mlstm_n, state_mlstm_m, c, c_ctx, g_norm, W_mod, b_mod, W_in, mla_q_norm, W_uq, mla_kv_norm, W_ukv, diff_lambda, diff_norm, ml_conv, ml_gate_b, ml_norm, W_out, g_final):
    Bc, Tc, _ = x_prompt.shape
    Bs, Ts, _ = x_sample.shape

    w_in_t, wuq_r, wukv_r, wout_r = _prep_weights(W_in, W_uq, W_ukv, W_out)
    w = {'wuq': wuq_r, 'wukv': wukv_r, 'wout': wout_r,
         'gq': mla_q_norm.reshape(DEPTH, 1, MLA_Q_RANK), 'gkv': mla_kv_norm.reshape(DEPTH, 1, MLA_KV_RANK),
         'gdn': diff_norm.reshape(DEPTH, 1, 2 * DIFF_D), 'lam': diff_lambda,
         'gfin': g_final.reshape(1, D_MODEL)}
    g_norm3 = g_norm.reshape(DEPTH, 1, D_MODEL)
    ml_norm2 = ml_norm.reshape(DEPTH, 1, GROUP_W)
    gate_row = jnp.pad(ml_gate_b.reshape(DEPTH, 1, N_ML_GATES),
                       ((0, 0), (0, 0), (GATE_LANE0, LANES - GATE_LANE0 - N_ML_GATES)))

    cc = jnp.concatenate([c_ctx[None, :], c, jnp.zeros((8 - 1 - Bs, D_MODEL), F32)], axis=0)
    mod_all = _mod_call(cc, W_mod, b_mod).reshape(DEPTH, 8, 1, 3 * D_MODEL)

    dkv = cnm = ckr = None
    x2 = x_prompt.reshape(Bc * Tc, D_MODEL)
    y_prompt = None
    for l in range(DEPTH):
        l_arr = jnp.full((1,), l, jnp.int32)
        main, tail, *dkv = _proj_call(l_arr, x2, mod_all, g_norm3, w_in_t, Tc, False, dkv)
        outs = _ctx_call(l_arr, main, tail, x2, mod_all, w, ml_conv, gate_row, ml_norm2, Tc, l == DEPTH - 1, cnm, ckr)
        cnm, outs = outs[:3], outs[3:]
        ckr = outs[-2:]
        dk_o, dv_o = dkv
        c_o, n_o, m_o = cnm
        if l == DEPTH - 1:
            x3, y_prompt, ckvn, kro = outs
        else:
            x3, ckvn, kro = outs
        x2 = x3.reshape(Bc * Tc, D_MODEL)
    side_outs = (ckvn, jnp.swapaxes(kro, 2, 3),
                 dk_o.reshape(Bc, DEPTH, Tc, DIFF_HEADS, 2 * DIFF_D),
                 dv_o.reshape(Bc, DEPTH, Tc, DIFF_HEADS, 2 * DIFF_D),
                 c_o.reshape(Bc, DEPTH, 2, ML_HEADS, ML_DK, HEAD_W),
                 n_o.reshape(Bc, DEPTH, 2, ML_HEADS, ML_DK),
                 m_o[:, :, :, 0].reshape(Bc, DEPTH, 2, ML_HEADS))

    rope = _rope_tables(Ts)
    ctx = {'ckv': cache_mla_ckv,
           'krope': jnp.swapaxes(cache_mla_krope, 2, 3),
           'dk': cache_diff_k.reshape(Bs, DEPTH, PAST_LEN * DIFF_HEADS, HEAD_W),
           'dv': cache_diff_v.reshape(Bs, DEPTH, PAST_LEN * DIFF_HEADS, HEAD_W)}
    ctx_state = (state_mlstm_C.reshape(Bs, DEPTH, 2 * ML_HEADS, ML_DK, HEAD_W),
                 state_mlstm_n.reshape(Bs, DEPTH, 2 * ML_HEADS, ML_DK),
                 jnp.broadcast_to(state_mlstm_m.reshape(Bs, DEPTH, 2 * ML_HEADS, 1),
                                  (Bs, DEPTH, 2 * ML_HEADS, LANES)))
    x2 = x_sample.reshape(Bs * Ts, D_MODEL)
    y_sample = None
    for l in range(DEPTH):
        l_arr = jnp.full((1,), l, jnp.int32)
        main, tail = _proj_call(l_arr, x2, mod_all, g_norm3, w_in_t, Ts, True)
        (yc,) = _mlstm_call(l_arr, main, tail, ml_conv, gate_row, ml_norm2, Ts, True, ctx_state)
        outs = _attn_call(l_arr, main, tail, yc, x2, mod_all, w, Ts, True, l == DEPTH - 1, rope, ctx)
        if l == DEPTH - 1:
            x3, y_sample = outs
        else:
            (x3,) = outs
        x2 = x3.reshape(Bs * Ts, D_MODEL)

    return (y_prompt, y_sample, *side_outs)
```

```python
import functools
import math

import jax
import jax.numpy as jnp
import numpy as np
from jax import lax
from jax.experimental import pallas as pl
from jax.experimental.pallas import tpu as pltpu

F32 = jnp.float32
BF16 = jnp.bfloat16

D_MODEL = 1024
DEPTH = 4
PAST_LEN = 256
GRID_W = 64
GROUP_W = 512
MLA_HEADS = 4
MLA_NOPE = 128
MLA_ROPE = 64
MLA_Q_RANK = 384
MLA_KV_RANK = 256
DIFF_HEADS = 4
DIFF_D = 64
ML_HEADS = 4
ML_DK = 128
N_ML_GATES = 16
ROPE_THETA = 10000.0
NORM_EPS = 1e-6
MLA_SCALE = (MLA_NOPE + MLA_ROPE) ** -0.5
DIFF_SCALE = DIFF_D ** -0.5
LOG2E = math.log2(math.e)

LANES = 128
HEAD_W = 128
ML_CHUNK = 256
VT_ROWS = 144
PROJ_TM = 512
ATTN_TQ = 512
ATTN_PVT_MAX_KEYS = 256
ATTN_AHEAD_CTX = 12
ATTN_AHEAD_LAT = 2
MAIN_W = 10 * GROUP_W
TAIL_W = 768
N_IN = 5840
VMEM_LIMIT = 56 * 1024 * 1024

C_ZA, C_DQ, C_DK, C_DV, C_ZB, C_MQ, C_MK, C_MV, C_MO, C_ZC = range(10)

_IN_SIZES = (MLA_Q_RANK, MLA_KV_RANK, MLA_ROPE, GROUP_W, GROUP_W, GROUP_W, GROUP_W, GROUP_W,
             GROUP_W, GROUP_W, GROUP_W, GROUP_W, GROUP_W, N_ML_GATES)
_IN_OFF = np.concatenate([[0], np.cumsum(_IN_SIZES)])
(_O_CQ, _O_CKV, _O_KR, _O_ZA, _O_DQ, _O_DK, _O_DV, _O_ZB, _O_MQ, _O_MK, _O_MV, _O_MO, _O_ZC,
 _O_MG) = [int(v) for v in _IN_OFF[:-1]]
GATE_LANE0 = MLA_ROPE


def _rms(x, g):
    ms = jnp.mean(x * x, axis=-1, keepdims=True)
    return x * lax.rsqrt(ms + NORM_EPS) * g


def _sigmoid(x):
    return 0.5 + 0.5 * jnp.tanh(0.5 * x)


def _silu(x):
    hx = 0.5 * x
    return hx + hx * jnp.tanh(hx)


def _log_sigmoid(x):
    return jnp.minimum(x, 0.0) - jnp.log1p(jnp.exp(-jnp.abs(x)))


def _swap32(x):
    lane = lax.broadcasted_iota(jnp.int32, x.shape, 1)
    fwd = pltpu.roll(x, LANES - 32, 1)
    bwd = pltpu.roll(x, 32, 1)
    return jnp.where((lane % 64) < 32, fwd, bwd)


def _rope_tile(x, cos, sin):
    return x * cos + _swap32(x) * sin


def _mod_kernel(c_ref, w_ref, b_ref, o_ref):
    a = _silu(c_ref[...]).astype(BF16)
    o_ref[...] = jnp.dot(a, w_ref[...].astype(BF16), preferred_element_type=F32) + b_ref[...]


def _mod_call(cc, W_mod, b_mod):
    tn = 1024
    return pl.pallas_call(
        _mod_kernel,
        grid=(DEPTH, 3 * D_MODEL // tn),
        in_specs=[pl.BlockSpec((8, D_MODEL), lambda l, j: (0, 0)),
                  pl.BlockSpec((None, D_MODEL, tn), lambda l, j: (l, 0, j)),
                  pl.BlockSpec((None, 1, tn), lambda l, j: (l, 0, j))],
        out_specs=pl.BlockSpec((None, 8, tn), lambda l, j: (l, 0, j)),
        out_shape=jax.ShapeDtypeStruct((DEPTH, 8, 3 * D_MODEL), F32),
        compiler_params=pltpu.CompilerParams(dimension_semantics=("arbitrary", "arbitrary"),
                                             vmem_limit_bytes=VMEM_LIMIT),
        name="mod",
    )(cc, W_mod, b_mod.reshape(DEPTH, 1, 3 * D_MODEL))


def _proj_kernel(has_side, n_alias, T, *refs):
    l_ref, x_ref, mod_ref, g_ref, w_ref = refs[:5]
    refs = refs[5 + n_alias:]
    if has_side:
        main_ref, tail_ref, dk_ref, dv_ref, wt_s = refs
    else:
        main_ref, tail_ref, wt_s = refs
    del l_ref
    if has_side and n_alias == 0:
        for r in (dk_ref, dv_ref):
            r[:, 1:] = jnp.zeros((r.shape[0], DEPTH - 1) + r.shape[2:], F32)
        dk_ref, dv_ref = dk_ref.at[:, 0], dv_ref.at[:, 0]
    tm = x_ref.shape[0]
    nt = (((1,), (1,)), ((), ()))

    @pl.when(pl.program_id(0) == 0)
    def _gather_tail_rows():
        wt_s[0:MLA_ROPE, :] = w_ref[_O_KR:_O_KR + MLA_ROPE, :]
        wt_s[MLA_ROPE:MLA_ROPE + N_ML_GATES, :] = w_ref[_O_MG:_O_MG + N_ML_GATES, :]
        wt_s[MLA_ROPE + N_ML_GATES:LANES, :] = jnp.zeros((LANES - MLA_ROPE - N_ML_GATES, D_MODEL), BF16)
        wt_s[LANES:, :] = w_ref[_O_CQ:_O_CQ + MLA_Q_RANK, :]

    mod = mod_ref[...]
    y = _rms(x_ref[...], g_ref[...])
    h = (y * (1.0 + mod[:, D_MODEL:2 * D_MODEL]) + mod[:, :D_MODEL]).astype(BF16)
    for s in range(MAIN_W // GROUP_W):
        cols = slice(s * GROUP_W, (s + 1) * GROUP_W)
        acc = lax.dot_general(h, w_ref[_O_ZA + s * GROUP_W:_O_ZA + (s + 1) * GROUP_W, :], nt,
                              preferred_element_type=F32)
        if s in (C_ZA, C_ZB, C_ZC):
            main_ref[:, cols] = _silu(acc).astype(BF16)
        elif s == C_MO:
            main_ref[:, cols] = _sigmoid(acc).astype(BF16)
        else:
            main_ref[:, cols] = acc.astype(BF16)
        if has_side and s in (C_DK, C_DV):
            side_ref = dk_ref if s == C_DK else dv_ref
            for b in range(tm // T):
                for hd in range(DIFF_HEADS):
                    side_ref[b, pl.ds(hd, T, stride=DIFF_HEADS), :] = (
                        acc[b * T:(b + 1) * T, hd * HEAD_W:(hd + 1) * HEAD_W])
    tail_ref[:, 0:MLA_KV_RANK] = lax.dot_general(h, w_ref[_O_CKV:_O_CKV + MLA_KV_RANK, :], nt,
                                                 preferred_element_type=F32)
    tail_ref[:, MLA_KV_RANK:] = lax.dot_general(h, wt_s[...], nt, preferred_element_type=F32)


def _proj_call(l_arr, x2, mod_all, g_norm3, w_in_t, T, has_ctx, side_bufs=None):
    n_tok = x2.shape[0]
    tm = PROJ_TM
    has_side = not has_ctx
    if has_ctx:
        mod_map = lambda i, l: (l[0], 1 + (i * tm) // T, 0, 0)
    else:
        mod_map = lambda i, l: (l[0], 0, 0, 0)
    in_specs = [pl.BlockSpec((tm, D_MODEL), lambda i, l: (i, 0)),
                pl.BlockSpec((None, None, 1, 3 * D_MODEL), mod_map),
                pl.BlockSpec((None, 1, D_MODEL), lambda i, l: (l[0], 0, 0)),
                pl.BlockSpec((None, N_IN, D_MODEL), lambda i, l: (l[0], 0, 0), pipeline_mode=pl.Buffered(1))]
    args = [x2, mod_all, g_norm3, w_in_t]
    out_shape = [jax.ShapeDtypeStruct((n_tok, MAIN_W), BF16), jax.ShapeDtypeStruct((n_tok, TAIL_W), F32)]
    out_specs = [pl.BlockSpec((tm, MAIN_W), lambda i, l: (i, 0)), pl.BlockSpec((tm, TAIL_W), lambda i, l: (i, 0))]
    aliases = {}
    n_alias = 0
    if has_side:
        bt = tm // T
        out_shape += [jax.ShapeDtypeStruct((n_tok // T, DEPTH, T * DIFF_HEADS, HEAD_W), F32)] * 2
        if side_bufs is not None:
            n_alias = len(side_bufs)
            aliases = {1 + len(args) + i: 2 + i for i in range(n_alias)}
            in_specs += [pl.BlockSpec(memory_space=pl.ANY)] * n_alias
            args += list(side_bufs)
            out_specs += [pl.BlockSpec((bt, None, T * DIFF_HEADS, HEAD_W), lambda i, l: (i, l[0], 0, 0))] * 2
        else:
            out_specs += [pl.BlockSpec((bt, DEPTH, T * DIFF_HEADS, HEAD_W), lambda i, l: (i, 0, 0, 0))] * 2
    grid_spec = pltpu.PrefetchScalarGridSpec(
        num_scalar_prefetch=1, grid=(n_tok // tm,), in_specs=in_specs, out_specs=out_specs,
        scratch_shapes=[pltpu.VMEM((TAIL_W - MLA_KV_RANK, D_MODEL), BF16)])
    return pl.pallas_call(
        functools.partial(_proj_kernel, has_side, n_alias, T),
        grid_spec=grid_spec,
        out_shape=out_shape,
        input_output_aliases=aliases,
        compiler_params=pltpu.CompilerParams(dimension_semantics=("arbitrary",),
                                             vmem_limit_bytes=VMEM_LIMIT),
        name="proj_lat" if has_ctx else "proj_ctx",
    )(l_arr, *args)


def _mlstm_kernel(has_ctx, T, n_alias, *refs):
    nc = T // ML_CHUNK
    L = ML_CHUNK
    use_inter = has_ctx or nc > 1
    it = iter(refs)
    l_ref = next(it)
    ml_ref, aux_ref, convw_ref, gb_ref, mln_ref = [next(it) for _ in range(5)]
    mq_ref, mk_ref, mv_ref, mo_ref, zc_ref = [ml_ref.at[:, i * GROUP_W:(i + 1) * GROUP_W] for i in range(5)]
    if has_ctx:
        c0_ref, n0_ref, m0_ref = [next(it) for _ in range(3)]
    else:
        for _ in range(n_alias):
            next(it)
    yc_ref = next(it)
    if not has_ctx:
        cout_ref, nout_ref, mout_ref = [next(it) for _ in range(3)]
    q_s, k_s, vt_s, hf_s, hb_s, caug_s, m_s, gt_s, ub_s = [next(it) for _ in range(9)]
    del l_ref
    if not has_ctx and n_alias == 0:
        for r in (cout_ref, nout_ref, mout_ref):
            r[1:] = jnp.zeros((DEPTH - 1,) + r.shape[1:], F32)
        cout_ref, nout_ref, mout_ref = cout_ref.at[0], nout_ref.at[0], mout_ref.at[0]

    convw = convw_ref[...]
    halo = 16
    for c in range(nc):
        cs = slice(c * L, (c + 1) * L)
        for j in range(2 * GROUP_W // LANES):
            src = mq_ref if j < GROUP_W // LANES else mk_ref
            ls = slice((j % (GROUP_W // LANES)) * LANES, (j % (GROUP_W // LANES) + 1) * LANES)
            u = src[cs, ls].astype(F32)
            prev = src[c * L - halo:c * L, ls].astype(F32)[halo - 1:halo, :] if c > 0 else 0.0
            nxt = src[(c + 1) * L:(c + 1) * L + halo, ls].astype(F32)[0:1, :] if c < nc - 1 else 0.0
            ub_s[0:8, :] = jnp.broadcast_to(prev, (8, LANES)) if c > 0 else jnp.zeros((8, LANES), F32)
            ub_s[8:8 + L, :] = u
            ub_s[8 + L:16 + L, :] = jnp.broadcast_to(nxt, (8, LANES)) if c < nc - 1 else jnp.zeros((8, LANES), F32)
            up, un = ub_s[7:7 + L, :], ub_s[9:9 + L, :]
            w3 = convw[:, j * LANES:(j + 1) * LANES]
            y = _silu(w3[0:1, :] * up + w3[1:2, :] * u + w3[2:3, :] * un)
            if j < GROUP_W // LANES:
                q_s[cs, ls] = (y * (ML_DK ** -0.5)).astype(BF16)
            else:
                k_s[cs, ls] = y
    row16 = lax.broadcasted_iota(jnp.int32, (VT_ROWS - HEAD_W, L), 0)
    ones_rows = jnp.where(row16 == 0, 1.0, 0.0).astype(BF16)
    g_t = (aux_ref[...] + gb_ref[...]).T
    for c in range(nc):
        cs = slice(c * L, (c + 1) * L)
        gt_s[c] = g_t[GATE_LANE0:GATE_LANE0 + N_ML_GATES, cs]
        for h in range(ML_HEADS):
            vt_s[c, h, 0:HEAD_W, :] = mv_ref[cs, h * HEAD_W:(h + 1) * HEAD_W].astype(F32).T.astype(BF16)
            vt_s[c, h, HEAD_W:, :] = ones_rows

    reps = L // LANES
    if has_ctx:
        n0 = n0_ref[...]
        m0 = m0_ref[...]
        m_s[0] = jnp.concatenate([m0] * reps, axis=1)
        m_s[1] = jnp.concatenate([pltpu.roll(m0, ML_HEADS, 0)] * reps, axis=1)
        row_n = lax.broadcasted_iota(jnp.int32, (VT_ROWS - HEAD_W, ML_DK), 0)
        for r in range(2 * ML_HEADS):
            caug_s[r, 0:HEAD_W, :] = c0_ref[r].T
            caug_s[r, HEAD_W:, :] = jnp.where(row_n == 0, n0[r:r + 1, :], 0.0)
    else:
        caug_s[...] = jnp.zeros_like(caug_s)
        m_s[...] = jnp.zeros_like(m_s)

    ri = lax.broadcasted_iota(jnp.int32, (L, L), 0)
    ci = lax.broadcasted_iota(jnp.int32, (L, L), 1)
    row_r = lax.broadcasted_iota(jnp.int32, (2 * ML_HEADS, L), 0)
    nt = (((1,), (1,)), ((), ()))

    def split3(x):
        hi = x.astype(BF16).astype(F32)
        mid = (x - hi).astype(BF16).astype(F32)
        return hi, mid, (x - hi - mid).astype(BF16).astype(F32)

    masks = ((ri <= ci), (ri >= ci))

    def gate_rows(d, c):
        fwd = d == 0
        tri_t = jnp.where(masks[d], 1.0, 0.0).astype(BF16)
        last = L - 1 if fwd else 0

        g8 = gt_s[c, d * 2 * ML_HEADS:(d + 1) * 2 * ML_HEADS, :]
        hi8, mid8, lo8 = split3(_log_sigmoid(g8))
        stack = jnp.concatenate([hi8, mid8, lo8, jnp.zeros_like(hi8)], axis=0).astype(BF16)
        part8 = jnp.dot(stack, tri_t, preferred_element_type=F32)
        bc8 = part8[0:8] + part8[8:16] + part8[16:24]
        bcs = pltpu.roll(bc8, ML_HEADS, 0)
        a8 = g8 - bcs
        a_n = jnp.concatenate([a8, jnp.zeros((LANES - 2 * ML_HEADS, L), F32)], axis=0).T
        cm8 = jnp.zeros((2 * ML_HEADS, L), F32)
        for h in range(ML_HEADS):
            col_max = jnp.max(jnp.where(masks[d], a_n[:, h:h + 1], -jnp.inf), axis=0, keepdims=True)
            cm8 = jnp.where(row_r == h, col_max, cm8)
        m8 = m_s[d]
        g_row = bcs + m8
        m_t = jnp.maximum(g_row, bcs + cm8)
        w_inter = jnp.exp(g_row - m_t)
        e_inv = jnp.exp(-m_t)
        c_row = bcs - m_t
        b_last = bcs[:, last:last + 1]
        m_new = m_t[:, last:last + 1]
        a_prev = jnp.exp(b_last + m8[:, 0:1] - m_new)
        w_s = jnp.exp(a8 + (b_last - m_new))
        m_s[d] = jnp.broadcast_to(m_new, (2 * ML_HEADS, L))
        return a_n, c_row, w_inter, e_inv, w_s, a_prev

    def do_pair(c_f, c_b):
        stats = (gate_rows(0, c_f), gate_rows(1, c_b))
        combos = [(d, h) for d in range(2) for h in range(ML_HEADS)]
        rows = []
        for c in (c_f, c_b):
            rows.append(slice(c * L, (c + 1) * L) if isinstance(c, int) else pl.ds(pl.multiple_of(c * L, L), L))
        chunk = (c_f, c_b)
        hsl = [slice(h * HEAD_W, (h + 1) * HEAD_W) for h in range(ML_HEADS)]
        qcs = [q_s[rows[d], hsl[h]] for d, h in combos]
        kcs = [k_s[rows[d], hsl[h]].astype(BF16) for d, h in combos]
        vts = [vt_s[chunk[d], h] for d, h in combos]
        n = range(len(combos))
        s_ts = [lax.dot_general(kcs[i], qcs[i], nt, preferred_element_type=F32) for i in n]
        sws = []
        for i, (d, h) in enumerate(combos):
            a_n, c_row = stats[d][0], stats[d][1]
            w_t = jnp.exp(jnp.where(masks[d], a_n[:, h:h + 1] + c_row[h:h + 1, :], -jnp.inf))
            sws.append((s_ts[i] * w_t).astype(BF16))
        nds = [jnp.dot(vts[i], sws[i], preferred_element_type=F32) for i in n]
        if use_inter:
            inters = [lax.dot_general(caug_s[d * ML_HEADS + h].astype(BF16), qcs[i], nt,
                                      preferred_element_type=F32) for i, (d, h) in enumerate(combos)]
            nds = [nds[i] + stats[d][2][h:h + 1, :] * inters[i] for i, (d, h) in enumerate(combos)]
        for i, (d, h) in enumerate(combos):
            inv = 1.0 / jnp.maximum(jnp.abs(nds[i][HEAD_W:HEAD_W + 1, :]), stats[d][3][h:h + 1, :])
            h_s = hf_s if d == 0 else hb_s
            h_s[rows[d], hsl[h]] = (nds[i][0:HEAD_W, :] * inv).T
        upds = [jnp.dot((vts[i].astype(F32) * stats[d][4][h:h + 1, :]).astype(BF16), kcs[i],
                        preferred_element_type=F32) for i, (d, h) in enumerate(combos)]
        for i, (d, h) in enumerate(combos):
            r = d * ML_HEADS + h
            caug_s[r] = upds[i] + stats[d][5][h:h + 1, :] * caug_s[r] if use_inter else upds[i]

    for i in range(nc):
        do_pair(i, nc - 1 - i)

    mln = mln_ref[...]
    for c in range(nc):
        cs = slice(c * L, (c + 1) * L)
        for h in range(ML_HEADS):
            hs = slice(h * HEAD_W, (h + 1) * HEAD_W)
            hc = mo_ref[cs, hs].astype(F32) * (hf_s[cs, hs] + hb_s[cs, hs])
            yc_ref[cs, hs] = (_rms(hc, mln[:, hs]) * zc_ref[cs, hs].astype(F32)).astype(BF16)

    if not has_ctx:
        for r in range(2 * ML_HEADS):
            cout_ref[r] = caug_s[r, 0:HEAD_W, :].T
            nout_ref[r:r + 1, :] = caug_s[r, HEAD_W:HEAD_W + 1, :]
        for d in range(2):
            mout_ref[d * ML_HEADS:(d + 1) * ML_HEADS, :] = m_s[d, 0:ML_HEADS, 0:LANES]


def _mlstm_scratch(T):
    nc = T // ML_CHUNK
    return [pltpu.VMEM((T, GROUP_W), BF16),
            pltpu.VMEM((T, GROUP_W), F32),
            pltpu.VMEM((nc, ML_HEADS, VT_ROWS, ML_CHUNK), BF16),
            pltpu.VMEM((T, GROUP_W), F32),
            pltpu.VMEM((T, GROUP_W), F32),
            pltpu.VMEM((2 * ML_HEADS, VT_ROWS, ML_DK), F32),
            pltpu.VMEM((2, 2 * ML_HEADS, ML_CHUNK), F32),
            pltpu.VMEM((nc, N_ML_GATES, ML_CHUNK), F32),
            pltpu.VMEM((ML_CHUNK + 16, LANES), F32)]


def _mlstm_lat_call(l_arr, main, tail, ml_conv, gate_row, ml_norm2, T, ctx_state):
    n_tok = main.shape[0]
    B = n_tok // T
    main3 = main.reshape(B, T, MAIN_W)
    tail3 = tail.reshape(B, T, TAIL_W)

    assert (C_MQ, C_ZC) == (5, 9)
    in_specs = [pl.BlockSpec((None, T, MAIN_W // 2), lambda b, l: (b, 0, 1)),
                pl.BlockSpec((None, T, LANES), lambda b, l: (b, 0, 2)),
                pl.BlockSpec((None, 3, 2 * GROUP_W), lambda b, l: (l[0], 0, 0)),
                pl.BlockSpec((None, 1, LANES), lambda b, l: (l[0], 0, 0)),
                pl.BlockSpec((None, 1, GROUP_W), lambda b, l: (l[0], 0, 0)),
                pl.BlockSpec((None, None, 2 * ML_HEADS, ML_DK, HEAD_W), lambda b, l: (b, l[0], 0, 0, 0)),
                pl.BlockSpec((None, None, 2 * ML_HEADS, ML_DK), lambda b, l: (b, l[0], 0, 0)),
                pl.BlockSpec((None, None, 2 * ML_HEADS, LANES), lambda b, l: (b, l[0], 0, 0))]
    grid_spec = pltpu.PrefetchScalarGridSpec(
        num_scalar_prefetch=1, grid=(B,), in_specs=in_specs,
        out_specs=[pl.BlockSpec((None, T, GROUP_W), lambda b, l: (b, 0, 0))],
        scratch_shapes=_mlstm_scratch(T))
    return pl.pallas_call(
        functools.partial(_mlstm_kernel, True, T, 0),
        grid_spec=grid_spec,
        out_shape=[jax.ShapeDtypeStruct((B, T, GROUP_W), BF16)],
        compiler_params=pltpu.CompilerParams(dimension_semantics=("arbitrary",),
                                             vmem_limit_bytes=VMEM_LIMIT),
        name="mlstm_lat",
    )(l_arr, main3, tail3, ml_conv, gate_row, ml_norm2, *ctx_state)


def _attn_kernel(has_ctx, is_last, merged, n_alias, T, *refs):
    Tk = T + (PAST_LEN if has_ctx else 0)
    pv_t = Tk <= ATTN_PVT_MAX_KEYS
    it = iter(refs)
    l_ref = next(it)
    if merged:
        tail_ref, att_ref = next(it), next(it)
        ckv_ref = tail_ref.at[:, 0:MLA_KV_RANK]
        aux_ref = tail_ref.at[:, MLA_KV_RANK:MLA_KV_RANK + LANES]
        cq_ref = tail_ref.at[:, MLA_KV_RANK + LANES:]
        za_ref, dq_ref, dk_ref, dv_ref, zb_ref = [att_ref.at[:, i * GROUP_W:(i + 1) * GROUP_W] for i in range(5)]
    else:
        cq_ref, ckv_ref, aux_ref, dq_ref, dk_ref, dv_ref, zb_ref, za_ref = [next(it) for _ in range(8)]
    (yc_ref, x_ref, mod_ref, wuq_ref, wukv_ref, wout_ref, gq_ref, gkv_ref, gdn_ref,
     lam_ref) = [next(it) for _ in range(10)]
    if is_last:
        gfin_ref = next(it)
    if has_ctx:
        cosq_ref, sinq_ref, cosk_ref, sink_ref, cckv_ref, ckr_ref, cdk_ref, cdv_ref = [next(it) for _ in range(8)]
    else:
        for _ in range(n_alias):
            next(it)
    xo_ref = next(it)
    if is_last:
        yfin_ref = next(it)
    if not has_ctx:
        ckvn_ref, kro_ref = [next(it) for _ in range(2)]
    ka_s, va_s, kb_s, vb_s, ycat_s = [next(it) for _ in range(5)]
    first = not has_ctx and n_alias == 0
    if first:
        ckvn_all, kro_all = ckvn_ref, kro_ref
        ckvn_ref, kro_ref = ckvn_ref.at[0], kro_ref.at[0]

    def _build_keys():
        wukv = wukv_ref[...]
        ckv_n = _rms(ckv_ref[...], gkv_ref[...])
        aux = aux_ref[...]
        if not has_ctx:
            ckvn_ref[...] = ckv_n
            kro_ref[...] = aux.T[0:MLA_ROPE, :]
        if first:
            for r in (ckvn_all, kro_all):
                r[1:] = jnp.zeros((DEPTH - 1,) + r.shape[1:], F32)
        kv = jnp.dot(ckv_n.astype(BF16), wukv, preferred_element_type=F32)
        lane = lax.broadcasted_iota(jnp.int32, aux.shape, 1)
        kr = _rope_tile(aux, cosk_ref[...], sink_ref[...]) if has_ctx else aux
        kr = jnp.where(lane < MLA_ROPE, kr, 0.0).astype(BF16)
        for h in range(MLA_HEADS):
            ka_s[0:T, 2 * h * HEAD_W:(2 * h + 1) * HEAD_W] = kv[:, 2 * h * HEAD_W:(2 * h + 1) * HEAD_W].astype(BF16)
            ka_s[0:T, (2 * h + 1) * HEAD_W:(2 * h + 2) * HEAD_W] = kr
        if pv_t:
            row16 = lax.broadcasted_iota(jnp.int32, (VT_ROWS - HEAD_W, Tk), 0)
            ones_rows = jnp.where(row16 == 0, 1.0, 0.0).astype(BF16)
            for h in range(MLA_HEADS):
                hs = slice(h * HEAD_W, (h + 1) * HEAD_W)
                va_s[h, 0:HEAD_W, 0:T] = kv[:, (2 * h + 1) * HEAD_W:(2 * h + 2) * HEAD_W].T.astype(BF16)
                va_s[h, HEAD_W:, :] = ones_rows
                vb_s[h, 0:HEAD_W, 0:T] = dv_ref[:, hs].astype(F32).T.astype(BF16)
                vb_s[h, HEAD_W:, :] = ones_rows
        else:
            lane_k = lax.broadcasted_iota(jnp.int32, (Tk, LANES), 1)
            ones_cols = jnp.where(lane_k == 0, 1.0, 0.0).astype(BF16)
            for h in range(MLA_HEADS):
                hs = slice(h * HEAD_W, (h + 1) * HEAD_W)
                va_s[0:T, 2 * h * HEAD_W:(2 * h + 1) * HEAD_W] = kv[:, (2 * h + 1) * HEAD_W:(2 * h + 2) * HEAD_W].astype(BF16)
                va_s[:, (2 * h + 1) * HEAD_W:(2 * h + 2) * HEAD_W] = ones_cols
                vb_s[0:T, 2 * h * HEAD_W:(2 * h + 1) * HEAD_W] = dv_ref[:, hs]
                vb_s[:, (2 * h + 1) * HEAD_W:(2 * h + 2) * HEAD_W] = ones_cols
        if has_ctx:
            for h in range(DIFF_HEADS):
                hs = slice(h * HEAD_W, (h + 1) * HEAD_W)
                kb_s[0:T, hs] = _rope_tile(dk_ref[:, hs].astype(F32), cosk_ref[...], sink_ref[...]).astype(BF16)
            kvc = jnp.dot(cckv_ref[...].astype(BF16), wukv, preferred_element_type=F32)
            ckr = jnp.concatenate([ckr_ref[...], jnp.zeros((LANES - MLA_ROPE, PAST_LEN), F32)], axis=0).T.astype(BF16)
            for h in range(MLA_HEADS):
                ka_s[T:Tk, 2 * h * HEAD_W:(2 * h + 1) * HEAD_W] = kvc[:, 2 * h * HEAD_W:(2 * h + 1) * HEAD_W].astype(BF16)
                ka_s[T:Tk, (2 * h + 1) * HEAD_W:(2 * h + 2) * HEAD_W] = ckr
            for h in range(DIFF_HEADS):
                hs = slice(h * HEAD_W, (h + 1) * HEAD_W)
                kb_s[T:Tk, hs] = cdk_ref[pl.ds(h, PAST_LEN, stride=DIFF_HEADS), :].astype(BF16)
                v_c = kvc[:, (2 * h + 1) * HEAD_W:(2 * h + 2) * HEAD_W]
                vd_c = cdv_ref[pl.ds(h, PAST_LEN, stride=DIFF_HEADS), :]
                if pv_t:
                    va_s[h, 0:HEAD_W, T:Tk] = v_c.T.astype(BF16)
                    vb_s[h, 0:HEAD_W, T:Tk] = vd_c.T.astype(BF16)
                else:
                    va_s[T:Tk, 2 * h * HEAD_W:(2 * h + 1) * HEAD_W] = v_c.astype(BF16)
                    vb_s[T:Tk, 2 * h * HEAD_W:(2 * h + 1) * HEAD_W] = vd_c.astype(BF16)
        else:
            kb_s[0:T, :] = dk_ref[...]

    if merged:
        _build_keys()
    else:
        pl.when(pl.program_id(1) == 0)(_build_keys)

    nt = (((1,), (1,)), ((), ()))
    tq = cq_ref.shape[0]

    maps = []
    qa = jnp.dot(_rms(cq_ref[...], gq_ref[...]).astype(BF16), wuq_ref[...], preferred_element_type=F32)
    for h in range(MLA_HEADS):
        q_nope = qa[:, 2 * h * HEAD_W:(2 * h + 1) * HEAD_W]
        q_rope = qa[:, (2 * h + 1) * HEAD_W:(2 * h + 2) * HEAD_W]
        if has_ctx:
            q_rope = _rope_tile(q_rope, cosq_ref[...], sinq_ref[...])
        q_h = (jnp.concatenate([q_nope, q_rope], axis=-1) * (MLA_SCALE * LOG2E)).astype(BF16)
        two = slice(2 * h * HEAD_W, (2 * h + 2) * HEAD_W)
        v_load = functools.partial(lambda hh: va_s[hh], h) if pv_t else functools.partial(lambda sl: va_s[:, sl], two)
        maps.append((q_h, functools.partial(lambda sl: ka_s[:, sl], two), v_load))
    lane_q = lax.broadcasted_iota(jnp.int32, (tq, HEAD_W), 1)
    for h in range(DIFF_HEADS):
        hs = slice(h * HEAD_W, (h + 1) * HEAD_W)
        q_h = dq_ref[:, hs].astype(F32)
        if has_ctx:
            q_h = _rope_tile(q_h, cosq_ref[...], sinq_ref[...])
        q_h = q_h * (DIFF_SCALE * LOG2E)
        two = slice(2 * h * HEAD_W, (2 * h + 2) * HEAD_W)
        v_load = functools.partial(lambda hh: vb_s[hh], h) if pv_t else functools.partial(lambda sl: vb_s[:, sl], two)
        for q_m in (jnp.where(lane_q < DIFF_D, q_h, 0.0), jnp.where(lane_q >= DIFF_D, q_h, 0.0)):
            maps.append((q_m.astype(BF16), functools.partial(lambda sl: kb_s[:, sl], hs), v_load))

    def scores(i):
        if pv_t:
            return lax.dot_general(maps[i][1](), maps[i][0], nt, preferred_element_type=F32)
        return lax.dot_general(maps[i][0], maps[i][1](), nt, preferred_element_type=F32)

    def softmax_pv(i, s):
        if pv_t:
            e = jnp.exp2(s - jnp.max(s, axis=0, keepdims=True))
            pv = jnp.dot(maps[i][2](), e.astype(BF16), preferred_element_type=F32)
            return (pv[0:HEAD_W, :] * (1.0 / pv[HEAD_W:HEAD_W + 1, :])).T
        e = jnp.exp2(s - jnp.max(s, axis=-1, keepdims=True))
        pv = jnp.dot(e.astype(BF16), maps[i][2](), preferred_element_type=F32)
        return pv[:, :HEAD_W] * (1.0 / pv[:, HEAD_W:HEAD_W + 1])

    n_maps = len(maps)
    ahead = min(ATTN_AHEAD_CTX if not has_ctx else ATTN_AHEAD_LAT, n_maps)
    pending = {i: scores(i) for i in range(ahead)}
    outs = []
    for i in range(n_maps):
        if i + ahead < n_maps:
            pending[i + ahead] = scores(i + ahead)
        outs.append(softmax_pv(i, pending.pop(i)))

    for h in range(MLA_HEADS):
        hs = slice(h * HEAD_W, (h + 1) * HEAD_W)
        ycat_s[:, hs] = (outs[h] * za_ref[:, hs].astype(F32)).astype(BF16)

    lp = lam_ref[...]
    lf = l_ref[0].astype(F32)
    lam_init = 0.8 - 0.6 * jnp.exp(jnp.full((1, 1), -0.3, F32) * lf)
    lam = (jnp.exp(jnp.sum(lp[0:1, :] * lp[1:2, :], axis=-1, keepdims=True))
           - jnp.exp(jnp.sum(lp[2:3, :] * lp[3:4, :], axis=-1, keepdims=True)) + lam_init)
    for h in range(DIFF_HEADS):
        hs = slice(h * HEAD_W, (h + 1) * HEAD_W)
        o1, o2 = outs[MLA_HEADS + 2 * h], outs[MLA_HEADS + 2 * h + 1]
        o = _rms(o1 - lam * o2, gdn_ref[...]) * (1.0 - lam_init)
        ycat_s[:, GROUP_W + h * HEAD_W:GROUP_W + (h + 1) * HEAD_W] = (o * zb_ref[:, hs].astype(F32)).astype(BF16)

    ycat_s[:, 2 * GROUP_W:] = yc_ref[...]
    y = jnp.dot(ycat_s[...], wout_ref[...], preferred_element_type=F32)
    x_new = x_ref[...] + mod_ref[:, 2 * D_MODEL:] * y
    xo_ref[...] = x_new
    if is_last:
        yfin_ref[...] = _rms(x_new, gfin_ref[...])


def _attn_lat_call(l_arr, main, tail, yc, x2, mod_all, w, T, is_last, rope, ctx):
    n_tok = main.shape[0]
    B = n_tok // T
    tq = min(ATTN_TQ, T)
    nq = T // tq
    Tk = T + PAST_LEN
    main3 = main.reshape(B, T, MAIN_W)
    tail3 = tail.reshape(B, T, TAIL_W)
    x3 = x2.reshape(B, T, D_MODEL)

    def tile(c, width):
        return pl.BlockSpec((None, tq, width), lambda b, q, l, c=c: (b, q, c))

    def full(c, width):
        return pl.BlockSpec((None, T, width), lambda b, q, l, c=c: (b, 0, c))

    def wspec(shape):
        return pl.BlockSpec((None,) + shape, lambda b, q, l: (l[0],) + (0,) * len(shape))

    in_specs = [tile(1, MLA_Q_RANK),
                full(0, MLA_KV_RANK),
                full(2, LANES),
                tile(C_DQ, GROUP_W), full(C_DK, GROUP_W), full(C_DV, GROUP_W),
                tile(C_ZB, GROUP_W), tile(C_ZA, GROUP_W),
                pl.BlockSpec((None, tq, GROUP_W), lambda b, q, l: (b, q, 0)),
                pl.BlockSpec((None, tq, D_MODEL), lambda b, q, l: (b, q, 0)),
                pl.BlockSpec((None, None, 1, 3 * D_MODEL), lambda b, q, l: (l[0], 1 + b, 0, 0)),
                wspec((MLA_Q_RANK, 2 * GROUP_W)), wspec((MLA_KV_RANK, 2 * GROUP_W)),
                wspec((3 * GROUP_W, D_MODEL)),
                wspec((1, MLA_Q_RANK)), wspec((1, MLA_KV_RANK)), wspec((1, HEAD_W)),
                wspec((4, DIFF_D))]
    args = [tail3, tail3, tail3, main3, main3, main3, main3, main3,
            yc, x3, mod_all, w['wuq'], w['wukv'], w['wout'], w['gq'], w['gkv'], w['gdn'], w['lam']]
    if is_last:
        in_specs.append(pl.BlockSpec((1, D_MODEL), lambda b, q, l: (0, 0)))
        args.append(w['gfin'])
    cos_t, sin_t = rope
    in_specs += [pl.BlockSpec((tq, LANES), lambda b, q, l: (q, 0)),
                 pl.BlockSpec((tq, LANES), lambda b, q, l: (q, 0)),
                 pl.BlockSpec((T, LANES), lambda b, q, l: (0, 0)),
                 pl.BlockSpec((T, LANES), lambda b, q, l: (0, 0)),
                 pl.BlockSpec((None, None, PAST_LEN, MLA_KV_RANK), lambda b, q, l: (b, l[0], 0, 0)),
                 pl.BlockSpec((None, None, MLA_ROPE, PAST_LEN), lambda b, q, l: (b, l[0], 0, 0)),
                 pl.BlockSpec((None, None, PAST_LEN * DIFF_HEADS, HEAD_W), lambda b, q, l: (b, l[0], 0, 0)),
                 pl.BlockSpec((None, None, PAST_LEN * DIFF_HEADS, HEAD_W), lambda b, q, l: (b, l[0], 0, 0))]
    args += [cos_t, sin_t, cos_t, sin_t, ctx['ckv'], ctx['krope'], ctx['dk'], ctx['dv']]
    n_out = 2 if is_last else 1
    out_shape = [jax.ShapeDtypeStruct((B, T, D_MODEL), F32)] * n_out
    out_specs = [pl.BlockSpec((None, tq, D_MODEL), lambda b, q, l: (b, q, 0))] * n_out
    if Tk <= ATTN_PVT_MAX_KEYS:
        v_scratch = pltpu.VMEM((MLA_HEADS, VT_ROWS, Tk), BF16)
    else:
        v_scratch = pltpu.VMEM((Tk, 2 * GROUP_W), BF16)
    grid_spec = pltpu.PrefetchScalarGridSpec(
        num_scalar_prefetch=1, grid=(B, nq), in_specs=in_specs, out_specs=out_specs,
        scratch_shapes=[pltpu.VMEM((Tk, 2 * GROUP_W), BF16),
                        v_scratch,
                        pltpu.VMEM((Tk, GROUP_W), BF16),
                        v_scratch,
                        pltpu.VMEM((tq, 3 * GROUP_W), BF16)])
    return pl.pallas_call(
        functools.partial(_attn_kernel, True, is_last, False, 0, T),
        grid_spec=grid_spec,
        out_shape=out_shape,
        compiler_params=pltpu.CompilerParams(dimension_semantics=("arbitrary", "arbitrary"),
                                             vmem_limit_bytes=VMEM_LIMIT),
        name="attn_lat",
    )(l_arr, *args)


def _ctx_kernel(is_last, n_alias_m, n_alias_a, T, *refs):
    l_ref = refs[0]
    n_m_in = 5 + n_alias_m
    n_a_in = 2 + 9 + (1 if is_last else 0) + n_alias_a
    m_in = refs[1:1 + n_m_in]
    a_in = refs[1 + n_m_in:1 + n_m_in + n_a_in]
    rest = refs[1 + n_m_in + n_a_in:]
    m_out, rest = rest[:3], rest[3:]
    n_a_out = 3 + (1 if is_last else 0)
    a_out, rest = rest[:n_a_out], rest[n_a_out:]
    m_scr, a_scr, yc_s = rest[:9], rest[9:14], rest[14]
    _mlstm_kernel(False, T, n_alias_m, l_ref, *m_in, yc_s, *m_out, *m_scr)
    _attn_kernel(False, is_last, True, n_alias_a, T, l_ref, a_in[0], a_in[1], yc_s, *a_in[2:], *a_out, *a_scr)


def _ctx_call(l_arr, main, tail, x2, mod_all, w, ml_conv, gate_row, ml_norm2, T, is_last, state_bufs, kv_bufs):
    n_tok = main.shape[0]
    B = n_tok // T
    main3 = main.reshape(B, T, MAIN_W)
    tail3 = tail.reshape(B, T, TAIL_W)
    x3 = x2.reshape(B, T, D_MODEL)
    assert (C_ZA, C_ZB, C_MQ, C_ZC) == (0, 4, 5, 9) and T == ML_CHUNK and T <= ATTN_TQ

    def wspec(shape):
        return pl.BlockSpec((None,) + shape, lambda b, q, l: (l[0],) + (0,) * len(shape))

    def side(first, shapes):
        if first:
            return [pl.BlockSpec((None, DEPTH) + s, lambda b, q, l, n=len(s): (b, 0) + (0,) * n) for s in shapes]
        return [pl.BlockSpec((None, None) + s, lambda b, q, l, n=len(s): (b, l[0]) + (0,) * n) for s in shapes]

    any_spec = pl.BlockSpec(memory_space=pl.ANY)
    n_alias_m = 0 if state_bufs is None else len(state_bufs)
    n_alias_a = 0 if kv_bufs is None else len(kv_bufs)
    in_specs = [pl.BlockSpec((None, T, MAIN_W // 2), lambda b, q, l: (b, 0, 1)),
                pl.BlockSpec((None, T, LANES), lambda b, q, l: (b, 0, 2)),
                wspec((3, 2 * GROUP_W)), wspec((1, LANES)), wspec((1, GROUP_W))] + [any_spec] * n_alias_m
    args = [main3, tail3, ml_conv, gate_row, ml_norm2] + list(state_bufs or ())
    alias_m_at = 1 + 5
    in_specs += [pl.BlockSpec((None, T, TAIL_W), lambda b, q, l: (b, 0, 0)),
                 pl.BlockSpec((None, T, MAIN_W // 2), lambda b, q, l: (b, 0, 0)),
                 pl.BlockSpec((None, T, D_MODEL), lambda b, q, l: (b, 0, 0)),
                 pl.BlockSpec((None, None, 1, 3 * D_MODEL), lambda b, q, l: (l[0], 0, 0, 0)),
                 wspec((MLA_Q_RANK, 2 * GROUP_W)), wspec((MLA_KV_RANK, 2 * GROUP_W)), wspec((3 * GROUP_W, D_MODEL)),
                 wspec((1, MLA_Q_RANK)), wspec((1, MLA_KV_RANK)), wspec((1, HEAD_W)), wspec((4, DIFF_D))]
    args += [tail3, main3, x3, mod_all, w['wuq'], w['wukv'], w['wout'], w['gq'], w['gkv'], w['gdn'], w['lam']]
    if is_last:
        in_specs.append(pl.BlockSpec((1, D_MODEL), lambda b, q, l: (0, 0)))
        args.append(w['gfin'])
    alias_a_at = 1 + len(args)
    in_specs += [any_spec] * n_alias_a
    args += list(kv_bufs or ())

    state_shapes = [(2 * ML_HEADS, ML_DK, HEAD_W), (2 * ML_HEADS, ML_DK), (2 * ML_HEADS, LANES)]
    kv_shapes = [(T, MLA_KV_RANK), (MLA_ROPE, T)]
    out_shape = [jax.ShapeDtypeStruct((B, DEPTH) + s, F32) for s in state_shapes]
    out_specs = side(n_alias_m == 0, state_shapes)
    out_shape.append(jax.ShapeDtypeStruct((B, T, D_MODEL), F32))
    out_specs.append(pl.BlockSpec((None, T, D_MODEL), lambda b, q, l: (b, 0, 0)))
    if is_last:
        out_shape.append(jax.ShapeDtypeStruct((B, T, D_MODEL), F32))
        out_specs.append(pl.BlockSpec((None, T, D_MODEL), lambda b, q, l: (b, 0, 0)))
    kv_out_at = len(out_shape)
    out_shape += [jax.ShapeDtypeStruct((B, DEPTH) + s, F32) for s in kv_shapes]
    out_specs += side(n_alias_a == 0, kv_shapes)
    aliases = {alias_m_at + i: i for i in range(n_alias_m)}
    aliases.update({alias_a_at + i: kv_out_at + i for i in range(n_alias_a)})

    grid_spec = pltpu.PrefetchScalarGridSpec(
        num_scalar_prefetch=1, grid=(B, 1), in_specs=in_specs, out_specs=out_specs,
        scratch_shapes=_mlstm_scratch(T) + [
                        pltpu.VMEM((T, 2 * GROUP_W), BF16),
                        pltpu.VMEM((MLA_HEADS, VT_ROWS, T), BF16),
                        pltpu.VMEM((T, GROUP_W), BF16),
                        pltpu.VMEM((DIFF_HEADS, VT_ROWS, T), BF16),
                        pltpu.VMEM((T, 3 * GROUP_W), BF16),
                        pltpu.VMEM((T, GROUP_W), BF16)])
    return pl.pallas_call(
        functools.partial(_ctx_kernel, is_last, n_alias_m, n_alias_a, T),
        grid_spec=grid_spec,
        out_shape=out_shape,
        input_output_aliases=aliases,
        compiler_params=pltpu.CompilerParams(dimension_semantics=("arbitrary", "arbitrary"),
                                             vmem_limit_bytes=VMEM_LIMIT),
        name="ctx_mix",
    )(l_arr, *args)


def _rope_tables(n_tok):
    n_freq = MLA_ROPE // 4
    inv = ROPE_THETA ** (-jnp.arange(n_freq, dtype=F32) / n_freq)
    n_rows = n_tok // GRID_W
    rowp = jnp.repeat(jnp.arange(n_rows, dtype=F32), GRID_W)
    colp = jnp.tile(jnp.arange(GRID_W, dtype=F32), n_rows)
    ang = jnp.concatenate([rowp[:, None] * inv, colp[:, None] * inv], axis=-1)
    cos, sin = jnp.cos(ang), jnp.sin(ang)
    cos64 = jnp.concatenate([cos, cos], axis=-1)
    sin64 = jnp.concatenate([-sin, sin], axis=-1)
    return jnp.concatenate([cos64, cos64], axis=-1), jnp.concatenate([sin64, sin64], axis=-1)


def _prep_weights(W_in, W_uq, W_ukv, W_out):
    w_in_t = jnp.swapaxes(W_in, 1, 2).astype(BF16)
    wq = W_uq.reshape(DEPTH, MLA_Q_RANK, MLA_HEADS, MLA_NOPE + MLA_ROPE)
    wq = jnp.pad(wq, ((0, 0), (0, 0), (0, 0), (0, 2 * HEAD_W - MLA_NOPE - MLA_ROPE)))
    wuq_r = wq.reshape(DEPTH, MLA_Q_RANK, MLA_HEADS * 2 * HEAD_W).astype(BF16)
    return w_in_t, wuq_r, W_ukv.astype(BF16), W_out.astype(BF16)


def kernel(x_prompt, x_sample, cache_mla_ckv, cache_mla_krope, cache_diff_k, cache_diff_v, state_mlstm_C, state_mlstm_n, state_mlstm_m, c, c_ctx, g_norm, W_mod, b_mod, W_in, mla_q_norm, W_uq, mla_kv_norm, W_ukv, diff_lambda, diff_norm, ml_conv, ml_gate_b, ml_norm, W_out, g_final):
    Bc, Tc, _ = x_prompt.shape
    Bs, Ts, _ = x_sample.shape

    w_in_t, wuq_r, wukv_r, wout_r = _prep_weights(W_in, W_uq, W_ukv, W_out)
    w = {'wuq': wuq_r, 'wukv': wukv_r, 'wout': wout_r,
         'gq': mla_q_norm.reshape(DEPTH, 1, MLA_Q_RANK), 'gkv': mla_kv_norm.reshape(DEPTH, 1, MLA_KV_RANK),
         'gdn': diff_norm.reshape(DEPTH, 1, 2 * DIFF_D), 'lam': diff_lambda,
         'gfin': g_final.reshape(1, D_MODEL)}
    g_norm3 = g_norm.reshape(DEPTH, 1, D_MODEL)
    ml_norm2 = ml_norm.reshape(DEPTH, 1, GROUP_W)
    gate_row = jnp.pad(ml_gate_b.reshape(DEPTH, 1, N_ML_GATES),
                       ((0, 0), (0, 0), (GATE_LANE0, LANES - GATE_LANE0 - N_ML_GATES)))

    cc = jnp.concatenate([c_ctx[None, :], c, jnp.zeros((8 - 1 - Bs, D_MODEL), F32)], axis=0)
    mod_all = _mod_call(cc, W_mod, b_mod).reshape(DEPTH, 8, 1, 3 * D_MODEL)

    dkv = cnm = ckr = None
    x2 = x_prompt.reshape(Bc * Tc, D_MODEL)
    y_prompt = None
    for l in range(DEPTH):
        l_arr = jnp.full((1,), l, jnp.int32)
        main, tail, *dkv = _proj_call(l_arr, x2, mod_all, g_norm3, w_in_t, Tc, False, dkv)
        outs = _ctx_call(l_arr, main, tail, x2, mod_all, w, ml_conv, gate_row, ml_norm2, Tc, l == DEPTH - 1, cnm, ckr)
        cnm, outs = outs[:3], outs[3:]
        ckr = outs[-2:]
        dk_o, dv_o = dkv
        c_o, n_o, m_o = cnm
        if l == DEPTH - 1:
            x3, y_prompt, ckvn, kro = outs
        else:
            x3, ckvn, kro = outs
        x2 = x3.reshape(Bc * Tc, D_MODEL)
    side_outs = (ckvn, jnp.swapaxes(kro, 2, 3),
                 dk_o.reshape(Bc, DEPTH, Tc, DIFF_HEADS, 2 * DIFF_D),
                 dv_o.reshape(Bc, DEPTH, Tc, DIFF_HEADS, 2 * DIFF_D),
                 c_o.reshape(Bc, DEPTH, 2, ML_HEADS, ML_DK, HEAD_W),
                 n_o.reshape(Bc, DEPTH, 2, ML_HEADS, ML_DK),
                 m_o[:, :, :, 0].reshape(Bc, DEPTH, 2, ML_HEADS))

    rope = _rope_tables(Ts)
    ctx = {'ckv': cache_mla_ckv,
           'krope': jnp.swapaxes(cache_mla_krope, 2, 3),
           'dk': cache_diff_k.reshape(Bs, DEPTH, PAST_LEN * DIFF_HEADS, HEAD_W),
           'dv': cache_diff_v.reshape(Bs, DEPTH, PAST_LEN * DIFF_HEADS, HEAD_W)}
    ctx_state = (state_mlstm_C.reshape(Bs, DEPTH, 2 * ML_HEADS, ML_DK, HEAD_W),
                 state_mlstm_n.reshape(Bs, DEPTH, 2 * ML_HEADS, ML_DK),
                 jnp.broadcast_to(state_mlstm_m.reshape(Bs, DEPTH, 2 * ML_HEADS, 1),
                                  (Bs, DEPTH, 2 * ML_HEADS, LANES)))
    x2 = x_sample.reshape(Bs * Ts, D_MODEL)
    y_sample = None
    for l in range(DEPTH):
        l_arr = jnp.full((1,), l, jnp.int32)
        main, tail = _proj_call(l_arr, x2, mod_all, g_norm3, w_in_t, Ts, True)
        (yc,) = _mlstm_lat_call(l_arr, main, tail, ml_conv, gate_row, ml_norm2, Ts, ctx_state)
        outs = _attn_lat_call(l_arr, main, tail, yc, x2, mod_all, w, Ts, l == DEPTH - 1, rope, ctx)
        if l == DEPTH - 1:
            x3, y_sample = outs
        else:
            (x3,) = outs
        x2 = x3.reshape(Bs * Ts, D_MODEL)

    return (y_prompt, y_sample, *side_outs)
```

```python
import functools
import math

import jax
import jax.numpy as jnp
import numpy as np
from jax import lax
from jax.experimental import pallas as pl
from jax.experimental.pallas import tpu as pltpu

F32 = jnp.float32
BF16 = jnp.bfloat16

D_MODEL = 1024
DEPTH = 4
PAST_LEN = 256
GRID_W = 64
GROUP_W = 512
MLA_HEADS = 4
MLA_NOPE = 128
MLA_ROPE = 64
MLA_Q_RANK = 384
MLA_KV_RANK = 256
DIFF_HEADS = 4
DIFF_D = 64
ML_HEADS = 4
ML_DK = 128
N_ML_GATES = 16
ROPE_THETA = 10000.0
NORM_EPS = 1e-6
MLA_SCALE = (MLA_NOPE + MLA_ROPE) ** -0.5
DIFF_SCALE = DIFF_D ** -0.5
LOG2E = math.log2(math.e)

LANES = 128
HEAD_W = 128
ML_CHUNK = 256
VT_ROWS = 144
PROJ_TM = 512
ATTN_TQ = 512
ATTN_PVT_MAX_KEYS = 256
ATTN_AHEAD_CTX = 12
ATTN_AHEAD_LAT = 2
MAIN_W = 10 * GROUP_W
TAIL_W = 768
N_IN = 5840
VMEM_LIMIT = 56 * 1024 * 1024

C_ZA, C_DQ, C_DK, C_DV, C_ZB, C_MQ, C_MK, C_MV, C_MO, C_ZC = range(10)

_IN_SIZES = (MLA_Q_RANK, MLA_KV_RANK, MLA_ROPE, GROUP_W, GROUP_W, GROUP_W, GROUP_W, GROUP_W,
             GROUP_W, GROUP_W, GROUP_W, GROUP_W, GROUP_W, N_ML_GATES)
_IN_OFF = np.concatenate([[0], np.cumsum(_IN_SIZES)])
(_O_CQ, _O_CKV, _O_KR, _O_ZA, _O_DQ, _O_DK, _O_DV, _O_ZB, _O_MQ, _O_MK, _O_MV, _O_MO, _O_ZC,
 _O_MG) = [int(v) for v in _IN_OFF[:-1]]
GATE_LANE0 = MLA_ROPE


def _rms(x, g):
    ms = jnp.mean(x * x, axis=-1, keepdims=True)
    return x * lax.rsqrt(ms + NORM_EPS) * g


def _sigmoid(x):
    return 0.5 + 0.5 * jnp.tanh(0.5 * x)


def _silu(x):
    hx = 0.5 * x
    return hx + hx * jnp.tanh(hx)


def _log_sigmoid(x):
    return jnp.minimum(x, 0.0) - jnp.log1p(jnp.exp(-jnp.abs(x)))


def _swap32(x):
    lane = lax.broadcasted_iota(jnp.int32, x.shape, 1)
    fwd = pltpu.roll(x, LANES - 32, 1)
    bwd = pltpu.roll(x, 32, 1)
    return jnp.where((lane % 64) < 32, fwd, bwd)


def _rope_tile(x, cos, sin):
    return x * cos + _swap32(x) * sin


def _mod_kernel(c_ref, w_ref, b_ref, o_ref):
    a = _silu(c_ref[...]).astype(BF16)
    o_ref[...] = jnp.dot(a, w_ref[...].astype(BF16), preferred_element_type=F32) + b_ref[...]


def _mod_call(cc, W_mod, b_mod):
    tn = 1024
    return pl.pallas_call(
        _mod_kernel,
        grid=(DEPTH, 3 * D_MODEL // tn),
        in_specs=[pl.BlockSpec((8, D_MODEL), lambda l, j: (0, 0)),
                  pl.BlockSpec((None, D_MODEL, tn), lambda l, j: (l, 0, j)),
                  pl.BlockSpec((None, 1, tn), lambda l, j: (l, 0, j))],
        out_specs=pl.BlockSpec((None, 8, tn), lambda l, j: (l, 0, j)),
        out_shape=jax.ShapeDtypeStruct((DEPTH, 8, 3 * D_MODEL), F32),
        compiler_params=pltpu.CompilerParams(dimension_semantics=("arbitrary", "arbitrary"),
                                             vmem_limit_bytes=VMEM_LIMIT),
        name="mod",
    )(cc, W_mod, b_mod.reshape(DEPTH, 1, 3 * D_MODEL))


def _proj_kernel(n_alias, T, n_lat_steps, *refs):
    l_ref, xl_ref, xc_ref, mod_ref, g_ref, w_ref = refs[:6]
    main_ref, tail_ref, dk_ref, dv_ref, wt_s = refs[6 + n_alias:]
    del l_ref
    if n_alias == 0:
        for r in (dk_ref, dv_ref):
            r[:, 1:] = jnp.zeros((r.shape[0], DEPTH - 1) + r.shape[2:], F32)
        dk_ref, dv_ref = dk_ref.at[:, 0], dv_ref.at[:, 0]
    tm = xc_ref.shape[0]
    nt = (((1,), (1,)), ((), ()))

    @pl.when(pl.program_id(0) == 0)
    def _gather_tail_rows():
        wt_s[0:MLA_ROPE, :] = w_ref[_O_KR:_O_KR + MLA_ROPE, :]
        wt_s[MLA_ROPE:MLA_ROPE + N_ML_GATES, :] = w_ref[_O_MG:_O_MG + N_ML_GATES, :]
        wt_s[MLA_ROPE + N_ML_GATES:LANES, :] = jnp.zeros((LANES - MLA_ROPE - N_ML_GATES, D_MODEL), BF16)
        wt_s[LANES:, :] = w_ref[_O_CQ:_O_CQ + MLA_Q_RANK, :]

    mod = mod_ref[...]
    n_lat_rows = jnp.where(pl.program_id(0) < n_lat_steps, tm, 0)
    from_lat = lax.broadcasted_iota(jnp.int32, (tm, D_MODEL), 0) < n_lat_rows
    y = _rms(jnp.where(from_lat, xl_ref[...], xc_ref[...]), g_ref[...])
    h = (y * (1.0 + mod[:, D_MODEL:2 * D_MODEL]) + mod[:, :D_MODEL]).astype(BF16)
    for s in range(MAIN_W // GROUP_W):
        cols = slice(s * GROUP_W, (s + 1) * GROUP_W)
        acc = lax.dot_general(h, w_ref[_O_ZA + s * GROUP_W:_O_ZA + (s + 1) * GROUP_W, :], nt,
                              preferred_element_type=F32)
        if s in (C_ZA, C_ZB, C_ZC):
            main_ref[:, cols] = _silu(acc).astype(BF16)
        elif s == C_MO:
            main_ref[:, cols] = _sigmoid(acc).astype(BF16)
        else:
            main_ref[:, cols] = acc.astype(BF16)
        if s in (C_DK, C_DV):
            side_ref = dk_ref if s == C_DK else dv_ref
            for b in range(tm // T):
                for hd in range(DIFF_HEADS):
                    side_ref[b, pl.ds(hd, T, stride=DIFF_HEADS), :] = (
                        acc[b * T:(b + 1) * T, hd * HEAD_W:(hd + 1) * HEAD_W])
    tail_ref[:, 0:MLA_KV_RANK] = lax.dot_general(h, w_ref[_O_CKV:_O_CKV + MLA_KV_RANK, :], nt,
                                                 preferred_element_type=F32)
    tail_ref[:, MLA_KV_RANK:] = lax.dot_general(h, wt_s[...], nt, preferred_element_type=F32)


def _proj_call(l_arr, xl2, xc2, mod_all, g_norm3, w_in_t, Tl, Tc, side_bufs=None):
    tm = PROJ_TM
    n_lat, n_ctx = xl2.shape[0], xc2.shape[0]
    assert Tl % tm == 0 and tm % Tc == 0 and n_lat % tm == 0 and n_ctx % tm == 0
    nl = n_lat // tm
    n_tok = n_lat + n_ctx
    ctx_tile = lambda i: jnp.maximum(i - nl, 0)
    in_specs = [pl.BlockSpec((tm, D_MODEL), lambda i, l: (jnp.minimum(i, nl - 1), 0)),
                pl.BlockSpec((tm, D_MODEL), lambda i, l: (ctx_tile(i), 0)),
                pl.BlockSpec((None, None, 1, 3 * D_MODEL),
                             lambda i, l: (l[0], jnp.where(i < nl, 1 + (i * tm) // Tl, 0), 0, 0)),
                pl.BlockSpec((None, 1, D_MODEL), lambda i, l: (l[0], 0, 0)),
                pl.BlockSpec((None, N_IN, D_MODEL), lambda i, l: (l[0], 0, 0), pipeline_mode=pl.Buffered(1))]
    args = [xl2, xc2, mod_all, g_norm3, w_in_t]
    out_shape = [jax.ShapeDtypeStruct((n_tok, MAIN_W), BF16), jax.ShapeDtypeStruct((n_tok, TAIL_W), F32)]
    out_specs = [pl.BlockSpec((tm, MAIN_W), lambda i, l: (i, 0)), pl.BlockSpec((tm, TAIL_W), lambda i, l: (i, 0))]
    aliases = {}
    n_alias = 0
    bt = tm // Tc
    out_shape += [jax.ShapeDtypeStruct((n_ctx // Tc, DEPTH, Tc * DIFF_HEADS, HEAD_W), F32)] * 2
    if side_bufs is not None:
        n_alias = len(side_bufs)
        aliases = {1 + len(args) + i: 2 + i for i in range(n_alias)}
        in_specs += [pl.BlockSpec(memory_space=pl.ANY)] * n_alias
        args += list(side_bufs)
        out_specs += [pl.BlockSpec((bt, None, Tc * DIFF_HEADS, HEAD_W), lambda i, l: (ctx_tile(i), l[0], 0, 0))] * 2
    else:
        out_specs += [pl.BlockSpec((bt, DEPTH, Tc * DIFF_HEADS, HEAD_W), lambda i, l: (ctx_tile(i), 0, 0, 0))] * 2
    grid_spec = pltpu.PrefetchScalarGridSpec(
        num_scalar_prefetch=1, grid=(n_tok // tm,), in_specs=in_specs, out_specs=out_specs,
        scratch_shapes=[pltpu.VMEM((TAIL_W - MLA_KV_RANK, D_MODEL), BF16)])
    return pl.pallas_call(
        functools.partial(_proj_kernel, n_alias, Tc, nl),
        grid_spec=grid_spec,
        out_shape=out_shape,
        input_output_aliases=aliases,
        compiler_params=pltpu.CompilerParams(dimension_semantics=("arbitrary",),
                                             vmem_limit_bytes=VMEM_LIMIT),
        name="proj",
    )(l_arr, *args)


def _mlstm_kernel(has_ctx, T, n_alias, *refs):
    nc = T // ML_CHUNK
    L = ML_CHUNK
    use_inter = has_ctx or nc > 1
    it = iter(refs)
    l_ref = next(it)
    ml_ref, aux_ref, convw_ref, gb_ref, mln_ref = [next(it) for _ in range(5)]
    mq_ref, mk_ref, mv_ref, mo_ref, zc_ref = [ml_ref.at[:, i * GROUP_W:(i + 1) * GROUP_W] for i in range(5)]
    if has_ctx:
        c0_ref, n0_ref, m0_ref = [next(it) for _ in range(3)]
    else:
        for _ in range(n_alias):
            next(it)
    yc_ref = next(it)
    if not has_ctx:
        cout_ref, nout_ref, mout_ref = [next(it) for _ in range(3)]
    q_s, k_s, vt_s, hf_s, hb_s, caug_s, m_s, gt_s, ub_s = [next(it) for _ in range(9)]
    del l_ref
    if not has_ctx and n_alias == 0:
        for r in (cout_ref, nout_ref, mout_ref):
            r[1:] = jnp.zeros((DEPTH - 1,) + r.shape[1:], F32)
        cout_ref, nout_ref, mout_ref = cout_ref.at[0], nout_ref.at[0], mout_ref.at[0]

    convw = convw_ref[...]
    halo = 16
    for c in range(nc):
        cs = slice(c * L, (c + 1) * L)
        for j in range(2 * GROUP_W // LANES):
            src = mq_ref if j < GROUP_W // LANES else mk_ref
            ls = slice((j % (GROUP_W // LANES)) * LANES, (j % (GROUP_W // LANES) + 1) * LANES)
            u = src[cs, ls].astype(F32)
            prev = src[c * L - halo:c * L, ls].astype(F32)[halo - 1:halo, :] if c > 0 else 0.0
            nxt = src[(c + 1) * L:(c + 1) * L + halo, ls].astype(F32)[0:1, :] if c < nc - 1 else 0.0
            ub_s[0:8, :] = jnp.broadcast_to(prev, (8, LANES)) if c > 0 else jnp.zeros((8, LANES), F32)
            ub_s[8:8 + L, :] = u
            ub_s[8 + L:16 + L, :] = jnp.broadcast_to(nxt, (8, LANES)) if c < nc - 1 else jnp.zeros((8, LANES), F32)
            up, un = ub_s[7:7 + L, :], ub_s[9:9 + L, :]
            w3 = convw[:, j * LANES:(j + 1) * LANES]
            y = _silu(w3[0:1, :] * up + w3[1:2, :] * u + w3[2:3, :] * un)
            if j < GROUP_W // LANES:
                q_s[cs, ls] = (y * (ML_DK ** -0.5)).astype(BF16)
            else:
                k_s[cs, ls] = y
    row16 = lax.broadcasted_iota(jnp.int32, (VT_ROWS - HEAD_W, L), 0)
    ones_rows = jnp.where(row16 == 0, 1.0, 0.0).astype(BF16)
    g_t = (aux_ref[...] + gb_ref[...]).T
    for c in range(nc):
        cs = slice(c * L, (c + 1) * L)
        gt_s[c] = g_t[GATE_LANE0:GATE_LANE0 + N_ML_GATES, cs]
        for h in range(ML_HEADS):
            vt_s[c, h, 0:HEAD_W, :] = mv_ref[cs, h * HEAD_W:(h + 1) * HEAD_W].astype(F32).T.astype(BF16)
            vt_s[c, h, HEAD_W:, :] = ones_rows

    reps = L // LANES
    if has_ctx:
        n0 = n0_ref[...]
        m0 = m0_ref[...]
        m_s[0] = jnp.concatenate([m0] * reps, axis=1)
        m_s[1] = jnp.concatenate([pltpu.roll(m0, ML_HEADS, 0)] * reps, axis=1)
        row_n = lax.broadcasted_iota(jnp.int32, (VT_ROWS - HEAD_W, ML_DK), 0)
        for r in range(2 * ML_HEADS):
            caug_s[r, 0:HEAD_W, :] = c0_ref[r].T
            caug_s[r, HEAD_W:, :] = jnp.where(row_n == 0, n0[r:r + 1, :], 0.0)
    else:
        caug_s[...] = jnp.zeros_like(caug_s)
        m_s[...] = jnp.zeros_like(m_s)

    ri = lax.broadcasted_iota(jnp.int32, (L, L), 0)
    ci = lax.broadcasted_iota(jnp.int32, (L, L), 1)
    row_r = lax.broadcasted_iota(jnp.int32, (2 * ML_HEADS, L), 0)
    nt = (((1,), (1,)), ((), ()))

    def split3(x):
        hi = x.astype(BF16).astype(F32)
        mid = (x - hi).astype(BF16).astype(F32)
        return hi, mid, (x - hi - mid).astype(BF16).astype(F32)

    masks = ((ri <= ci), (ri >= ci))

    def gate_rows(d, c):
        fwd = d == 0
        tri_t = jnp.where(masks[d], 1.0, 0.0).astype(BF16)
        last = L - 1 if fwd else 0

        g8 = gt_s[c, d * 2 * ML_HEADS:(d + 1) * 2 * ML_HEADS, :]
        hi8, mid8, lo8 = split3(_log_sigmoid(g8))
        stack = jnp.concatenate([hi8, mid8, lo8, jnp.zeros_like(hi8)], axis=0).astype(BF16)
        part8 = jnp.dot(stack, tri_t, preferred_element_type=F32)
        bc8 = part8[0:8] + part8[8:16] + part8[16:24]
        bcs = pltpu.roll(bc8, ML_HEADS, 0)
        a8 = g8 - bcs
        a_n = jnp.concatenate([a8, jnp.zeros((LANES - 2 * ML_HEADS, L), F32)], axis=0).T
        cm8 = jnp.zeros((2 * ML_HEADS, L), F32)
        for h in range(ML_HEADS):
            col_max = jnp.max(jnp.where(masks[d], a_n[:, h:h + 1], -jnp.inf), axis=0, keepdims=True)
            cm8 = jnp.where(row_r == h, col_max, cm8)
        m8 = m_s[d]
        g_row = bcs + m8
        m_t = jnp.maximum(g_row, bcs + cm8)
        w_inter = jnp.exp(g_row - m_t)
        e_inv = jnp.exp(-m_t)
        c_row = bcs - m_t
        b_last = bcs[:, last:last + 1]
        m_new = m_t[:, last:last + 1]
        a_prev = jnp.exp(b_last + m8[:, 0:1] - m_new)
        w_s = jnp.exp(a8 + (b_last - m_new))
        m_s[d] = jnp.broadcast_to(m_new, (2 * ML_HEADS, L))
        return a_n, c_row, w_inter, e_inv, w_s, a_prev

    def do_pair(c_f, c_b):
        stats = (gate_rows(0, c_f), gate_rows(1, c_b))
        combos = [(d, h) for d in range(2) for h in range(ML_HEADS)]
        rows = []
        for c in (c_f, c_b):
            rows.append(slice(c * L, (c + 1) * L) if isinstance(c, int) else pl.ds(pl.multiple_of(c * L, L), L))
        chunk = (c_f, c_b)
        hsl = [slice(h * HEAD_W, (h + 1) * HEAD_W) for h in range(ML_HEADS)]
        qcs = [q_s[rows[d], hsl[h]] for d, h in combos]
        kcs = [k_s[rows[d], hsl[h]].astype(BF16) for d, h in combos]
        vts = [vt_s[chunk[d], h] for d, h in combos]
        n = range(len(combos))
        s_ts = [lax.dot_general(kcs[i], qcs[i], nt, preferred_element_type=F32) for i in n]
        sws = []
        for i, (d, h) in enumerate(combos):
            a_n, c_row = stats[d][0], stats[d][1]
            w_t = jnp.exp(jnp.where(masks[d], a_n[:, h:h + 1] + c_row[h:h + 1, :], -jnp.inf))
            sws.append((s_ts[i] * w_t).astype(BF16))
        nds = [jnp.dot(vts[i], sws[i], preferred_element_type=F32) for i in n]
        if use_inter:
            inters = [lax.dot_general(caug_s[d * ML_HEADS + h].astype(BF16), qcs[i], nt,
                                      preferred_element_type=F32) for i, (d, h) in enumerate(combos)]
            nds = [nds[i] + stats[d][2][h:h + 1, :] * inters[i] for i, (d, h) in enumerate(combos)]
        for i, (d, h) in enumerate(combos):
            inv = 1.0 / jnp.maximum(jnp.abs(nds[i][HEAD_W:HEAD_W + 1, :]), stats[d][3][h:h + 1, :])
            h_s = hf_s if d == 0 else hb_s
            h_s[rows[d], hsl[h]] = (nds[i][0:HEAD_W, :] * inv).T
        upds = [jnp.dot((vts[i].astype(F32) * stats[d][4][h:h + 1, :]).astype(BF16), kcs[i],
                        preferred_element_type=F32) for i, (d, h) in enumerate(combos)]
        for i, (d, h) in enumerate(combos):
            r = d * ML_HEADS + h
            caug_s[r] = upds[i] + stats[d][5][h:h + 1, :] * caug_s[r] if use_inter else upds[i]

    for i in range(nc):
        do_pair(i, nc - 1 - i)

    mln = mln_ref[...]
    for c in range(nc):
        cs = slice(c * L, (c + 1) * L)
        for h in range(ML_HEADS):
            hs = slice(h * HEAD_W, (h + 1) * HEAD_W)
            hc = mo_ref[cs, hs].astype(F32) * (hf_s[cs, hs] + hb_s[cs, hs])
            yc_ref[cs, hs] = (_rms(hc, mln[:, hs]) * zc_ref[cs, hs].astype(F32)).astype(BF16)

    if not has_ctx:
        for r in range(2 * ML_HEADS):
            cout_ref[r] = caug_s[r, 0:HEAD_W, :].T
            nout_ref[r:r + 1, :] = caug_s[r, HEAD_W:HEAD_W + 1, :]
        for d in range(2):
            mout_ref[d * ML_HEADS:(d + 1) * ML_HEADS, :] = m_s[d, 0:ML_HEADS, 0:LANES]


def _mlstm_scratch(T):
    nc = T // ML_CHUNK
    return [pltpu.VMEM((T, GROUP_W), BF16),
            pltpu.VMEM((T, GROUP_W), F32),
            pltpu.VMEM((nc, ML_HEADS, VT_ROWS, ML_CHUNK), BF16),
            pltpu.VMEM((T, GROUP_W), F32),
            pltpu.VMEM((T, GROUP_W), F32),
            pltpu.VMEM((2 * ML_HEADS, VT_ROWS, ML_DK), F32),
            pltpu.VMEM((2, 2 * ML_HEADS, ML_CHUNK), F32),
            pltpu.VMEM((nc, N_ML_GATES, ML_CHUNK), F32),
            pltpu.VMEM((ML_CHUNK + 16, LANES), F32)]


def _mlstm_lat_call(l_arr, main, tail, ml_conv, gate_row, ml_norm2, T, ctx_state):
    B = ctx_state[0].shape[0]
    main3 = main.reshape(-1, T, MAIN_W)
    tail3 = tail.reshape(-1, T, TAIL_W)

    assert (C_MQ, C_ZC) == (5, 9)
    in_specs = [pl.BlockSpec((None, T, MAIN_W // 2), lambda b, l: (b, 0, 1)),
                pl.BlockSpec((None, T, LANES), lambda b, l: (b, 0, 2)),
                pl.BlockSpec((None, 3, 2 * GROUP_W), lambda b, l: (l[0], 0, 0)),
                pl.BlockSpec((None, 1, LANES), lambda b, l: (l[0], 0, 0)),
                pl.BlockSpec((None, 1, GROUP_W), lambda b, l: (l[0], 0, 0)),
                pl.BlockSpec((None, None, 2 * ML_HEADS, ML_DK, HEAD_W), lambda b, l: (b, l[0], 0, 0, 0)),
                pl.BlockSpec((None, None, 2 * ML_HEADS, ML_DK), lambda b, l: (b, l[0], 0, 0)),
                pl.BlockSpec((None, None, 2 * ML_HEADS, LANES), lambda b, l: (b, l[0], 0, 0))]
    grid_spec = pltpu.PrefetchScalarGridSpec(
        num_scalar_prefetch=1, grid=(B,), in_specs=in_specs,
        out_specs=[pl.BlockSpec((None, T, GROUP_W), lambda b, l: (b, 0, 0))],
        scratch_shapes=_mlstm_scratch(T))
    return pl.pallas_call(
        functools.partial(_mlstm_kernel, True, T, 0),
        grid_spec=grid_spec,
        out_shape=[jax.ShapeDtypeStruct((B, T, GROUP_W), BF16)],
        compiler_params=pltpu.CompilerParams(dimension_semantics=("arbitrary",),
                                             vmem_limit_bytes=VMEM_LIMIT),
        name="mlstm_lat",
    )(l_arr, main3, tail3, ml_conv, gate_row, ml_norm2, *ctx_state)


def _attn_kernel(has_ctx, is_last, merged, n_alias, T, *refs):
    Tk = T + (PAST_LEN if has_ctx else 0)
    pv_t = Tk <= ATTN_PVT_MAX_KEYS
    it = iter(refs)
    l_ref = next(it)
    if merged:
        tail_ref, att_ref = next(it), next(it)
        ckv_ref = tail_ref.at[:, 0:MLA_KV_RANK]
        aux_ref = tail_ref.at[:, MLA_KV_RANK:MLA_KV_RANK + LANES]
        cq_ref = tail_ref.at[:, MLA_KV_RANK + LANES:]
        za_ref, dq_ref, dk_ref, dv_ref, zb_ref = [att_ref.at[:, i * GROUP_W:(i + 1) * GROUP_W] for i in range(5)]
    else:
        cq_ref, ckv_ref, aux_ref, dq_ref, dk_ref, dv_ref, zb_ref, za_ref = [next(it) for _ in range(8)]
    (yc_ref, x_ref, mod_ref, wuq_ref, wukv_ref, wout_ref, gq_ref, gkv_ref, gdn_ref,
     lam_ref) = [next(it) for _ in range(10)]
    if is_last:
        gfin_ref = next(it)
    if has_ctx:
        cosq_ref, sinq_ref, cosk_ref, sink_ref, cckv_ref, ckr_ref, cdk_ref, cdv_ref = [next(it) for _ in range(8)]
    else:
        for _ in range(n_alias):
            next(it)
    xo_ref = next(it)
    if is_last:
        yfin_ref = next(it)
    if not has_ctx:
        ckvn_ref, kro_ref = [next(it) for _ in range(2)]
    ka_s, va_s, kb_s, vb_s, ycat_s = [next(it) for _ in range(5)]
    first = not has_ctx and n_alias == 0
    if first:
        ckvn_all, kro_all = ckvn_ref, kro_ref
        ckvn_ref, kro_ref = ckvn_ref.at[0], kro_ref.at[0]

    def _build_keys():
        wukv = wukv_ref[...]
        ckv_n = _rms(ckv_ref[...], gkv_ref[...])
        aux = aux_ref[...]
        if not has_ctx:
            ckvn_ref[...] = ckv_n
            kro_ref[...] = aux.T[0:MLA_ROPE, :]
        if first:
            for r in (ckvn_all, kro_all):
                r[1:] = jnp.zeros((DEPTH - 1,) + r.shape[1:], F32)
        kv = jnp.dot(ckv_n.astype(BF16), wukv, preferred_element_type=F32)
        lane = lax.broadcasted_iota(jnp.int32, aux.shape, 1)
        kr = _rope_tile(aux, cosk_ref[...], sink_ref[...]) if has_ctx else aux
        kr = jnp.where(lane < MLA_ROPE, kr, 0.0).astype(BF16)
        for h in range(MLA_HEADS):
            ka_s[0:T, 2 * h * HEAD_W:(2 * h + 1) * HEAD_W] = kv[:, 2 * h * HEAD_W:(2 * h + 1) * HEAD_W].astype(BF16)
            ka_s[0:T, (2 * h + 1) * HEAD_W:(2 * h + 2) * HEAD_W] = kr
        if pv_t:
            row16 = lax.broadcasted_iota(jnp.int32, (VT_ROWS - HEAD_W, Tk), 0)
            ones_rows = jnp.where(row16 == 0, 1.0, 0.0).astype(BF16)
            for h in range(MLA_HEADS):
                hs = slice(h * HEAD_W, (h + 1) * HEAD_W)
                va_s[h, 0:HEAD_W, 0:T] = kv[:, (2 * h + 1) * HEAD_W:(2 * h + 2) * HEAD_W].T.astype(BF16)
                va_s[h, HEAD_W:, :] = ones_rows
                vb_s[h, 0:HEAD_W, 0:T] = dv_ref[:, hs].astype(F32).T.astype(BF16)
                vb_s[h, HEAD_W:, :] = ones_rows
        else:
            lane_k = lax.broadcasted_iota(jnp.int32, (Tk, LANES), 1)
            ones_cols = jnp.where(lane_k == 0, 1.0, 0.0).astype(BF16)
            for h in range(MLA_HEADS):
                hs = slice(h * HEAD_W, (h + 1) * HEAD_W)
                va_s[0:T, 2 * h * HEAD_W:(2 * h + 1) * HEAD_W] = kv[:, (2 * h + 1) * HEAD_W:(2 * h + 2) * HEAD_W].astype(BF16)
                va_s[:, (2 * h + 1) * HEAD_W:(2 * h + 2) * HEAD_W] = ones_cols
                vb_s[0:T, 2 * h * HEAD_W:(2 * h + 1) * HEAD_W] = dv_ref[:, hs]
                vb_s[:, (2 * h + 1) * HEAD_W:(2 * h + 2) * HEAD_W] = ones_cols
        if has_ctx:
            for h in range(DIFF_HEADS):
                hs = slice(h * HEAD_W, (h + 1) * HEAD_W)
                kb_s[0:T, hs] = _rope_tile(dk_ref[:, hs].astype(F32), cosk_ref[...], sink_ref[...]).astype(BF16)
            kvc = jnp.dot(cckv_ref[...].astype(BF16), wukv, preferred_element_type=F32)
            ckr = jnp.concatenate([ckr_ref[...], jnp.zeros((LANES - MLA_ROPE, PAST_LEN), F32)], axis=0).T.astype(BF16)
            for h in range(MLA_HEADS):
                ka_s[T:Tk, 2 * h * HEAD_W:(2 * h + 1) * HEAD_W] = kvc[:, 2 * h * HEAD_W:(2 * h + 1) * HEAD_W].astype(BF16)
                ka_s[T:Tk, (2 * h + 1) * HEAD_W:(2 * h + 2) * HEAD_W] = ckr
            for h in range(DIFF_HEADS):
                hs = slice(h * HEAD_W, (h + 1) * HEAD_W)
                kb_s[T:Tk, hs] = cdk_ref[pl.ds(h, PAST_LEN, stride=DIFF_HEADS), :].astype(BF16)
                v_c = kvc[:, (2 * h + 1) * HEAD_W:(2 * h + 2) * HEAD_W]
                vd_c = cdv_ref[pl.ds(h, PAST_LEN, stride=DIFF_HEADS), :]
                if pv_t:
                    va_s[h, 0:HEAD_W, T:Tk] = v_c.T.astype(BF16)
                    vb_s[h, 0:HEAD_W, T:Tk] = vd_c.T.astype(BF16)
                else:
                    va_s[T:Tk, 2 * h * HEAD_W:(2 * h + 1) * HEAD_W] = v_c.astype(BF16)
                    vb_s[T:Tk, 2 * h * HEAD_W:(2 * h + 1) * HEAD_W] = vd_c.astype(BF16)
        else:
            kb_s[0:T, :] = dk_ref[...]

    if merged:
        _build_keys()
    else:
        pl.when(pl.program_id(1) == 0)(_build_keys)

    nt = (((1,), (1,)), ((), ()))
    tq = cq_ref.shape[0]

    maps = []
    qa = jnp.dot(_rms(cq_ref[...], gq_ref[...]).astype(BF16), wuq_ref[...], preferred_element_type=F32)
    for h in range(MLA_HEADS):
        q_nope = qa[:, 2 * h * HEAD_W:(2 * h + 1) * HEAD_W]
        q_rope = qa[:, (2 * h + 1) * HEAD_W:(2 * h + 2) * HEAD_W]
        if has_ctx:
            q_rope = _rope_tile(q_rope, cosq_ref[...], sinq_ref[...])
        q_h = (jnp.concatenate([q_nope, q_rope], axis=-1) * (MLA_SCALE * LOG2E)).astype(BF16)
        two = slice(2 * h * HEAD_W, (2 * h + 2) * HEAD_W)
        v_load = functools.partial(lambda hh: va_s[hh], h) if pv_t else functools.partial(lambda sl: va_s[:, sl], two)
        maps.append((q_h, functools.partial(lambda sl: ka_s[:, sl], two), v_load))
    lane_q = lax.broadcasted_iota(jnp.int32, (tq, HEAD_W), 1)
    for h in range(DIFF_HEADS):
        hs = slice(h * HEAD_W, (h + 1) * HEAD_W)
        q_h = dq_ref[:, hs].astype(F32)
        if has_ctx:
            q_h = _rope_tile(q_h, cosq_ref[...], sinq_ref[...])
        q_h = q_h * (DIFF_SCALE * LOG2E)
        two = slice(2 * h * HEAD_W, (2 * h + 2) * HEAD_W)
        v_load = functools.partial(lambda hh: vb_s[hh], h) if pv_t else functools.partial(lambda sl: vb_s[:, sl], two)
        for q_m in (jnp.where(lane_q < DIFF_D, q_h, 0.0), jnp.where(lane_q >= DIFF_D, q_h, 0.0)):
            maps.append((q_m.astype(BF16), functools.partial(lambda sl: kb_s[:, sl], hs), v_load))

    def scores(i):
        if pv_t:
            return lax.dot_general(maps[i][1](), maps[i][0], nt, preferred_element_type=F32)
        return lax.dot_general(maps[i][0], maps[i][1](), nt, preferred_element_type=F32)

    def softmax_pv(i, s):
        if pv_t:
            e = jnp.exp2(s - jnp.max(s, axis=0, keepdims=True))
            pv = jnp.dot(maps[i][2](), e.astype(BF16), preferred_element_type=F32)
            return (pv[0:HEAD_W, :] * (1.0 / pv[HEAD_W:HEAD_W + 1, :])).T
        e = jnp.exp2(s - jnp.max(s, axis=-1, keepdims=True))
        pv = jnp.dot(e.astype(BF16), maps[i][2](), preferred_element_type=F32)
        return pv[:, :HEAD_W] * (1.0 / pv[:, HEAD_W:HEAD_W + 1])

    n_maps = len(maps)
    ahead = min(ATTN_AHEAD_CTX if not has_ctx else ATTN_AHEAD_LAT, n_maps)
    pending = {i: scores(i) for i in range(ahead)}
    outs = []
    for i in range(n_maps):
        if i + ahead < n_maps:
            pending[i + ahead] = scores(i + ahead)
        outs.append(softmax_pv(i, pending.pop(i)))

    for h in range(MLA_HEADS):
        hs = slice(h * HEAD_W, (h + 1) * HEAD_W)
        ycat_s[:, hs] = (outs[h] * za_ref[:, hs].astype(F32)).astype(BF16)

    lp = lam_ref[...]
    lf = l_ref[0].astype(F32)
    lam_init = 0.8 - 0.6 * jnp.exp(jnp.full((1, 1), -0.3, F32) * lf)
    lam = (jnp.exp(jnp.sum(lp[0:1, :] * lp[1:2, :], axis=-1, keepdims=True))
           - jnp.exp(jnp.sum(lp[2:3, :] * lp[3:4, :], axis=-1, keepdims=True)) + lam_init)
    for h in range(DIFF_HEADS):
        hs = slice(h * HEAD_W, (h + 1) * HEAD_W)
        o1, o2 = outs[MLA_HEADS + 2 * h], outs[MLA_HEADS + 2 * h + 1]
        o = _rms(o1 - lam * o2, gdn_ref[...]) * (1.0 - lam_init)
        ycat_s[:, GROUP_W + h * HEAD_W:GROUP_W + (h + 1) * HEAD_W] = (o * zb_ref[:, hs].astype(F32)).astype(BF16)

    ycat_s[:, 2 * GROUP_W:] = yc_ref[...]
    y = jnp.dot(ycat_s[...], wout_ref[...], preferred_element_type=F32)
    x_new = x_ref[...] + mod_ref[:, 2 * D_MODEL:] * y
    xo_ref[...] = x_new
    if is_last:
        yfin_ref[...] = _rms(x_new, gfin_ref[...])


def _attn_lat_call(l_arr, main, tail, yc, x2, mod_all, w, T, is_last, rope, ctx):
    B = x2.shape[0] // T
    tq = min(ATTN_TQ, T)
    nq = T // tq
    Tk = T + PAST_LEN
    main3 = main.reshape(-1, T, MAIN_W)
    tail3 = tail.reshape(-1, T, TAIL_W)
    x3 = x2.reshape(B, T, D_MODEL)

    def tile(c, width):
        return pl.BlockSpec((None, tq, width), lambda b, q, l, c=c: (b, q, c))

    def full(c, width):
        return pl.BlockSpec((None, T, width), lambda b, q, l, c=c: (b, 0, c))

    def wspec(shape):
        return pl.BlockSpec((None,) + shape, lambda b, q, l: (l[0],) + (0,) * len(shape))

    in_specs = [tile(1, MLA_Q_RANK),
                full(0, MLA_KV_RANK),
                full(2, LANES),
                tile(C_DQ, GROUP_W), full(C_DK, GROUP_W), full(C_DV, GROUP_W),
                tile(C_ZB, GROUP_W), tile(C_ZA, GROUP_W),
                pl.BlockSpec((None, tq, GROUP_W), lambda b, q, l: (b, q, 0)),
                pl.BlockSpec((None, tq, D_MODEL), lambda b, q, l: (b, q, 0)),
                pl.BlockSpec((None, None, 1, 3 * D_MODEL), lambda b, q, l: (l[0], 1 + b, 0, 0)),
                wspec((MLA_Q_RANK, 2 * GROUP_W)), wspec((MLA_KV_RANK, 2 * GROUP_W)),
                wspec((3 * GROUP_W, D_MODEL)),
                wspec((1, MLA_Q_RANK)), wspec((1, MLA_KV_RANK)), wspec((1, HEAD_W)),
                wspec((4, DIFF_D))]
    args = [tail3, tail3, tail3, main3, main3, main3, main3, main3,
            yc, x3, mod_all, w['wuq'], w['wukv'], w['wout'], w['gq'], w['gkv'], w['gdn'], w['lam']]
    if is_last:
        in_specs.append(pl.BlockSpec((1, D_MODEL), lambda b, q, l: (0, 0)))
        args.append(w['gfin'])
    cos_t, sin_t = rope
    in_specs += [pl.BlockSpec((tq, LANES), lambda b, q, l: (q, 0)),
                 pl.BlockSpec((tq, LANES), lambda b, q, l: (q, 0)),
                 pl.BlockSpec((T, LANES), lambda b, q, l: (0, 0)),
                 pl.BlockSpec((T, LANES), lambda b, q, l: (0, 0)),
                 pl.BlockSpec((None, None, PAST_LEN, MLA_KV_RANK), lambda b, q, l: (b, l[0], 0, 0)),
                 pl.BlockSpec((None, None, MLA_ROPE, PAST_LEN), lambda b, q, l: (b, l[0], 0, 0)),
                 pl.BlockSpec((None, None, PAST_LEN * DIFF_HEADS, HEAD_W), lambda b, q, l: (b, l[0], 0, 0)),
                 pl.BlockSpec((None, None, PAST_LEN * DIFF_HEADS, HEAD_W), lambda b, q, l: (b, l[0], 0, 0))]
    args += [cos_t, sin_t, cos_t, sin_t, ctx['ckv'], ctx['krope'], ctx['dk'], ctx['dv']]
    n_out = 2 if is_last else 1
    out_shape = [jax.ShapeDtypeStruct((B, T, D_MODEL), F32)] * n_out
    out_specs = [pl.BlockSpec((None, tq, D_MODEL), lambda b, q, l: (b, q, 0))] * n_out
    if Tk <= ATTN_PVT_MAX_KEYS:
        v_scratch = pltpu.VMEM((MLA_HEADS, VT_ROWS, Tk), BF16)
    else:
        v_scratch = pltpu.VMEM((Tk, 2 * GROUP_W), BF16)
    grid_spec = pltpu.PrefetchScalarGridSpec(
        num_scalar_prefetch=1, grid=(B, nq), in_specs=in_specs, out_specs=out_specs,
        scratch_shapes=[pltpu.VMEM((Tk, 2 * GROUP_W), BF16),
                        v_scratch,
                        pltpu.VMEM((Tk, GROUP_W), BF16),
                        v_scratch,
                        pltpu.VMEM((tq, 3 * GROUP_W), BF16)])
    return pl.pallas_call(
        functools.partial(_attn_kernel, True, is_last, False, 0, T),
        grid_spec=grid_spec,
        out_shape=out_shape,
        compiler_params=pltpu.CompilerParams(dimension_semantics=("arbitrary", "arbitrary"),
                                             vmem_limit_bytes=VMEM_LIMIT),
        name="attn_lat",
    )(l_arr, *args)


def _ctx_kernel(is_last, n_alias_m, n_alias_a, T, *refs):
    l_ref = refs[0]
    n_m_in = 5 + n_alias_m
    n_a_in = 2 + 9 + (1 if is_last else 0) + n_alias_a
    m_in = refs[1:1 + n_m_in]
    a_in = refs[1 + n_m_in:1 + n_m_in + n_a_in]
    rest = refs[1 + n_m_in + n_a_in:]
    m_out, rest = rest[:3], rest[3:]
    n_a_out = 3 + (1 if is_last else 0)
    a_out, rest = rest[:n_a_out], rest[n_a_out:]
    m_scr, a_scr, yc_s = rest[:9], rest[9:14], rest[14]
    _mlstm_kernel(False, T, n_alias_m, l_ref, *m_in, yc_s, *m_out, *m_scr)
    _attn_kernel(False, is_last, True, n_alias_a, T, l_ref, a_in[0], a_in[1], yc_s, *a_in[2:], *a_out, *a_scr)


def _ctx_call(l_arr, main, tail, x2, mod_all, w, ml_conv, gate_row, ml_norm2, T, is_last, state_bufs, kv_bufs):
    B = x2.shape[0] // T
    main3 = main.reshape(-1, T, MAIN_W)
    tail3 = tail.reshape(-1, T, TAIL_W)
    b0 = main3.shape[0] - B
    x3 = x2.reshape(B, T, D_MODEL)
    assert (C_ZA, C_ZB, C_MQ, C_ZC) == (0, 4, 5, 9) and T == ML_CHUNK and T <= ATTN_TQ

    def wspec(shape):
        return pl.BlockSpec((None,) + shape, lambda b, q, l: (l[0],) + (0,) * len(shape))

    def side(first, shapes):
        if first:
            return [pl.BlockSpec((None, DEPTH) + s, lambda b, q, l, n=len(s): (b, 0) + (0,) * n) for s in shapes]
        return [pl.BlockSpec((None, None) + s, lambda b, q, l, n=len(s): (b, l[0]) + (0,) * n) for s in shapes]

    any_spec = pl.BlockSpec(memory_space=pl.ANY)
    n_alias_m = 0 if state_bufs is None else len(state_bufs)
    n_alias_a = 0 if kv_bufs is None else len(kv_bufs)
    in_specs = [pl.BlockSpec((None, T, MAIN_W // 2), lambda b, q, l: (b0 + b, 0, 1)),
                pl.BlockSpec((None, T, LANES), lambda b, q, l: (b0 + b, 0, 2)),
                wspec((3, 2 * GROUP_W)), wspec((1, LANES)), wspec((1, GROUP_W))] + [any_spec] * n_alias_m
    args = [main3, tail3, ml_conv, gate_row, ml_norm2] + list(state_bufs or ())
    alias_m_at = 1 + 5
    in_specs += [pl.BlockSpec((None, T, TAIL_W), lambda b, q, l: (b0 + b, 0, 0)),
                 pl.BlockSpec((None, T, MAIN_W // 2), lambda b, q, l: (b0 + b, 0, 0)),
                 pl.BlockSpec((None, T, D_MODEL), lambda b, q, l: (b, 0, 0)),
                 pl.BlockSpec((None, None, 1, 3 * D_MODEL), lambda b, q, l: (l[0], 0, 0, 0)),
                 wspec((MLA_Q_RANK, 2 * GROUP_W)), wspec((MLA_KV_RANK, 2 * GROUP_W)), wspec((3 * GROUP_W, D_MODEL)),
                 wspec((1, MLA_Q_RANK)), wspec((1, MLA_KV_RANK)), wspec((1, HEAD_W)), wspec((4, DIFF_D))]
    args += [tail3, main3, x3, mod_all, w['wuq'], w['wukv'], w['wout'], w['gq'], w['gkv'], w['gdn'], w['lam']]
    if is_last:
        in_specs.append(pl.BlockSpec((1, D_MODEL), lambda b, q, l: (0, 0)))
        args.append(w['gfin'])
    alias_a_at = 1 + len(args)
    in_specs += [any_spec] * n_alias_a
    args += list(kv_bufs or ())

    state_shapes = [(2 * ML_HEADS, ML_DK, HEAD_W), (2 * ML_HEADS, ML_DK), (2 * ML_HEADS, LANES)]
    kv_shapes = [(T, MLA_KV_RANK), (MLA_ROPE, T)]
    out_shape = [jax.ShapeDtypeStruct((B, DEPTH) + s, F32) for s in state_shapes]
    out_specs = side(n_alias_m == 0, state_shapes)
    out_shape.append(jax.ShapeDtypeStruct((B, T, D_MODEL), F32))
    out_specs.append(pl.BlockSpec((None, T, D_MODEL), lambda b, q, l: (b, 0, 0)))
    if is_last:
        out_shape.append(jax.ShapeDtypeStruct((B, T, D_MODEL), F32))
        out_specs.append(pl.BlockSpec((None, T, D_MODEL), lambda b, q, l: (b, 0, 0)))
    kv_out_at = len(out_shape)
    out_shape += [jax.ShapeDtypeStruct((B, DEPTH) + s, F32) for s in kv_shapes]
    out_specs += side(n_alias_a == 0, kv_shapes)
    aliases = {alias_m_at + i: i for i in range(n_alias_m)}
    aliases.update({alias_a_at + i: kv_out_at + i for i in range(n_alias_a)})

    grid_spec = pltpu.PrefetchScalarGridSpec(
        num_scalar_prefetch=1, grid=(B, 1), in_specs=in_specs, out_specs=out_specs,
        scratch_shapes=_mlstm_scratch(T) + [
                        pltpu.VMEM((T, 2 * GROUP_W), BF16),
                        pltpu.VMEM((MLA_HEADS, VT_ROWS, T), BF16),
                        pltpu.VMEM((T, GROUP_W), BF16),
                        pltpu.VMEM((DIFF_HEADS, VT_ROWS, T), BF16),
                        pltpu.VMEM((T, 3 * GROUP_W), BF16),
                        pltpu.VMEM((T, GROUP_W), BF16)])
    return pl.pallas_call(
        functools.partial(_ctx_kernel, is_last, n_alias_m, n_alias_a, T),
        grid_spec=grid_spec,
        out_shape=out_shape,
        input_output_aliases=aliases,
        compiler_params=pltpu.CompilerParams(dimension_semantics=("arbitrary", "arbitrary"),
                                             vmem_limit_bytes=VMEM_LIMIT),
        name="ctx_mix",
    )(l_arr, *args)


def _rope_tables(n_tok):
    n_freq = MLA_ROPE // 4
    inv = ROPE_THETA ** (-jnp.arange(n_freq, dtype=F32) / n_freq)
    n_rows = n_tok // GRID_W
    rowp = jnp.repeat(jnp.arange(n_rows, dtype=F32), GRID_W)
    colp = jnp.tile(jnp.arange(GRID_W, dtype=F32), n_rows)
    ang = jnp.concatenate([rowp[:, None] * inv, colp[:, None] * inv], axis=-1)
    cos, sin = jnp.cos(ang), jnp.sin(ang)
    cos64 = jnp.concatenate([cos, cos], axis=-1)
    sin64 = jnp.concatenate([-sin, sin], axis=-1)
    return jnp.concatenate([cos64, cos64], axis=-1), jnp.concatenate([sin64, sin64], axis=-1)


def _prep_weights(W_in, W_uq, W_ukv, W_out):
    w_in_t = jnp.swapaxes(W_in, 1, 2).astype(BF16)
    wq = W_uq.reshape(DEPTH, MLA_Q_RANK, MLA_HEADS, MLA_NOPE + MLA_ROPE)
    wq = jnp.pad(wq, ((0, 0), (0, 0), (0, 0), (0, 2 * HEAD_W - MLA_NOPE - MLA_ROPE)))
    wuq_r = wq.reshape(DEPTH, MLA_Q_RANK, MLA_HEADS * 2 * HEAD_W).astype(BF16)
    return w_in_t, wuq_r, W_ukv.astype(BF16), W_out.astype(BF16)


def kernel(x_prompt, x_sample, cache_mla_ckv, cache_mla_krope, cache_diff_k, cache_diff_v, state_mlstm_C, state_mlstm_n, state_mlstm_m, c, c_ctx, g_norm, W_mod, b_mod, W_in, mla_q_norm, W_uq, mla_kv_norm, W_ukv, diff_lambda, diff_norm, ml_conv, ml_gate_b, ml_norm, W_out, g_final):
    Bc, Tc, _ = x_prompt.shape
    Bs, Ts, _ = x_sample.shape

    w_in_t, wuq_r, wukv_r, wout_r = _prep_weights(W_in, W_uq, W_ukv, W_out)
    w = {'wuq': wuq_r, 'wukv': wukv_r, 'wout': wout_r,
         'gq': mla_q_norm.reshape(DEPTH, 1, MLA_Q_RANK), 'gkv': mla_kv_norm.reshape(DEPTH, 1, MLA_KV_RANK),
         'gdn': diff_norm.reshape(DEPTH, 1, 2 * DIFF_D), 'lam': diff_lambda,
         'gfin': g_final.reshape(1, D_MODEL)}
    g_norm3 = g_norm.reshape(DEPTH, 1, D_MODEL)
    ml_norm2 = ml_norm.reshape(DEPTH, 1, GROUP_W)
    gate_row = jnp.pad(ml_gate_b.reshape(DEPTH, 1, N_ML_GATES),
                       ((0, 0), (0, 0), (GATE_LANE0, LANES - GATE_LANE0 - N_ML_GATES)))

    cc = jnp.concatenate([c_ctx[None, :], c, jnp.zeros((8 - 1 - Bs, D_MODEL), F32)], axis=0)
    mod_all = _mod_call(cc, W_mod, b_mod).reshape(DEPTH, 8, 1, 3 * D_MODEL)

    rope = _rope_tables(Ts)
    ctx = {'ckv': cache_mla_ckv,
           'krope': jnp.swapaxes(cache_mla_krope, 2, 3),
           'dk': cache_diff_k.reshape(Bs, DEPTH, PAST_LEN * DIFF_HEADS, HEAD_W),
           'dv': cache_diff_v.reshape(Bs, DEPTH, PAST_LEN * DIFF_HEADS, HEAD_W)}
    ctx_state = (state_mlstm_C.reshape(Bs, DEPTH, 2 * ML_HEADS, ML_DK, HEAD_W),
                 state_mlstm_n.reshape(Bs, DEPTH, 2 * ML_HEADS, ML_DK),
                 jnp.broadcast_to(state_mlstm_m.reshape(Bs, DEPTH, 2 * ML_HEADS, 1),
                                  (Bs, DEPTH, 2 * ML_HEADS, LANES)))

    dkv = cnm = ckr = None
    xc2 = x_prompt.reshape(Bc * Tc, D_MODEL)
    xl2 = x_sample.reshape(Bs * Ts, D_MODEL)
    y_prompt = y_sample = None
    for l in range(DEPTH):
        l_arr = jnp.full((1,), l, jnp.int32)
        is_last = l == DEPTH - 1
        main, tail, *dkv = _proj_call(l_arr, xl2, xc2, mod_all, g_norm3, w_in_t, Ts, Tc, dkv)
        outs = _ctx_call(l_arr, main, tail, xc2, mod_all, w, ml_conv, gate_row, ml_norm2, Tc, is_last, cnm, ckr)
        cnm, outs = outs[:3], outs[3:]
        ckr = outs[-2:]
        if is_last:
            x3, y_prompt, ckvn, kro = outs
        else:
            x3, ckvn, kro = outs
        xc2 = x3.reshape(Bc * Tc, D_MODEL)
        (yc,) = _mlstm_lat_call(l_arr, main, tail, ml_conv, gate_row, ml_norm2, Ts, ctx_state)
        outs = _attn_lat_call(l_arr, main, tail, yc, xl2, mod_all, w, Ts, is_last, rope, ctx)
        if is_last:
            x3, y_sample = outs
        else:
            (x3,) = outs
        xl2 = x3.reshape(Bs * Ts, D_MODEL)
    dk_o, dv_o = dkv
    c_o, n_o, m_o = cnm
    side_outs = (ckvn, jnp.swapaxes(kro, 2, 3),
                 dk_o.reshape(Bc, DEPTH, Tc, DIFF_HEADS, 2 * DIFF_D),
                 dv_o.reshape(Bc, DEPTH, Tc, DIFF_HEADS, 2 * DIFF_D),
                 c_o.reshape(Bc, DEPTH, 2, ML_HEADS, ML_DK, HEAD_W),
                 n_o.reshape(Bc, DEPTH, 2, ML_HEADS, ML_DK),
                 m_o[:, :, :, 0].reshape(Bc, DEPTH, 2, ML_HEADS))
    return (y_prompt, y_sample, *side_outs)
```

```python
import functools
import math

import jax
import jax.numpy as jnp
import numpy as np
from jax import lax
from jax.experimental import pallas as pl
from jax.experimental.pallas import tpu as pltpu

F32 = jnp.float32
BF16 = jnp.bfloat16

D_MODEL = 1024
DEPTH = 4
PAST_LEN = 256
GRID_W = 64
GROUP_W = 512
MLA_HEADS = 4
MLA_NOPE = 128
MLA_ROPE = 64
MLA_Q_RANK = 384
MLA_KV_RANK = 256
DIFF_HEADS = 4
DIFF_D = 64
ML_HEADS = 4
ML_DK = 128
N_ML_GATES = 16
ROPE_THETA = 10000.0
NORM_EPS = 1e-6
MLA_SCALE = (MLA_NOPE + MLA_ROPE) ** -0.5
DIFF_SCALE = DIFF_D ** -0.5
LOG2E = math.log2(math.e)

LANES = 128
HEAD_W = 128
ML_CHUNK = 256
VT_ROWS = 144
PROJ_TM = 512
ATTN_TQ = 512
ATTN_PVT_MAX_KEYS = 256
ATTN_AHEAD_CTX = 12
ATTN_AHEAD_LAT = 2
MAIN_W = 10 * GROUP_W
TAIL_W = 768
N_IN = 5840
W_STAGE_ROWS = 256
W_STAGE_BUFS = 3
VMEM_LIMIT = 56 * 1024 * 1024

C_ZA, C_DQ, C_DK, C_DV, C_ZB, C_MQ, C_MK, C_MV, C_MO, C_ZC = range(10)

_IN_SIZES = (MLA_Q_RANK, MLA_KV_RANK, MLA_ROPE, GROUP_W, GROUP_W, GROUP_W, GROUP_W, GROUP_W,
             GROUP_W, GROUP_W, GROUP_W, GROUP_W, GROUP_W, N_ML_GATES)
_IN_OFF = np.concatenate([[0], np.cumsum(_IN_SIZES)])
(_O_CQ, _O_CKV, _O_KR, _O_ZA, _O_DQ, _O_DK, _O_DV, _O_ZB, _O_MQ, _O_MK, _O_MV, _O_MO, _O_ZC,
 _O_MG) = [int(v) for v in _IN_OFF[:-1]]
GATE_LANE0 = MLA_ROPE


def _rms(x, g):
    ms = jnp.mean(x * x, axis=-1, keepdims=True)
    return x * lax.rsqrt(ms + NORM_EPS) * g


def _sigmoid(x):
    return 0.5 + 0.5 * jnp.tanh(0.5 * x)


def _silu(x):
    hx = 0.5 * x
    return hx + hx * jnp.tanh(hx)


def _log_sigmoid(x):
    return jnp.minimum(x, 0.0) - jnp.log1p(jnp.exp(-jnp.abs(x)))


def _swap32(x):
    lane = lax.broadcasted_iota(jnp.int32, x.shape, 1)
    fwd = pltpu.roll(x, LANES - 32, 1)
    bwd = pltpu.roll(x, 32, 1)
    return jnp.where((lane % 64) < 32, fwd, bwd)


def _rope_tile(x, cos, sin):
    return x * cos + _swap32(x) * sin


def _mod_kernel(c_ref, w_ref, b_ref, o_ref):
    a = _silu(c_ref[...]).astype(BF16)
    o_ref[...] = jnp.dot(a, w_ref[...].astype(BF16), preferred_element_type=F32) + b_ref[...]


def _mod_call(cc, W_mod, b_mod):
    tn = 1024
    return pl.pallas_call(
        _mod_kernel,
        grid=(DEPTH, 3 * D_MODEL // tn),
        in_specs=[pl.BlockSpec((8, D_MODEL), lambda l, j: (0, 0)),
                  pl.BlockSpec((None, D_MODEL, tn), lambda l, j: (l, 0, j)),
                  pl.BlockSpec((None, 1, tn), lambda l, j: (l, 0, j))],
        out_specs=pl.BlockSpec((None, 8, tn), lambda l, j: (l, 0, j)),
        out_shape=jax.ShapeDtypeStruct((DEPTH, 8, 3 * D_MODEL), F32),
        compiler_params=pltpu.CompilerParams(dimension_semantics=("arbitrary", "arbitrary"),
                                             vmem_limit_bytes=VMEM_LIMIT),
        name="mod",
    )(cc, W_mod, b_mod.reshape(DEPTH, 1, 3 * D_MODEL))


def _w_stage_copy(w_hbm, layer, stage, sem, j):
    r = j * W_STAGE_ROWS
    n = min(W_STAGE_ROWS, N_IN - r)
    slot = j % W_STAGE_BUFS
    return pltpu.make_async_copy(w_hbm.at[layer, pl.ds(r, n), :], stage.at[slot, pl.ds(0, n), :], sem.at[slot])


def _proj_kernel(n_alias, T, n_lat_steps, *refs):
    l_ref, xl_ref, xc_ref, mod_ref, g_ref, w_hbm = refs[:6]
    main_ref, tail_ref, dk_ref, dv_ref, wt_s, w_ref, stage, sem = refs[6 + n_alias:]
    if n_alias == 0:
        for r in (dk_ref, dv_ref):
            r[:, 1:] = jnp.zeros((r.shape[0], DEPTH - 1) + r.shape[2:], F32)
        dk_ref, dv_ref = dk_ref.at[:, 0], dv_ref.at[:, 0]
    tm = xc_ref.shape[0]
    nt = (((1,), (1,)), ((), ()))

    @pl.when(pl.program_id(0) == 0)
    def _load_weights():
        n_copies = -(-N_IN // W_STAGE_ROWS)
        layer = l_ref[0]
        for j in range(min(W_STAGE_BUFS, n_copies)):
            _w_stage_copy(w_hbm, layer, stage, sem, j).start()
        for j in range(n_copies):
            r = j * W_STAGE_ROWS
            n = min(W_STAGE_ROWS, N_IN - r)
            _w_stage_copy(w_hbm, layer, stage, sem, j).wait()
            w_ref[r:r + n, :] = stage[j % W_STAGE_BUFS, 0:n, :].astype(BF16)
            if j + W_STAGE_BUFS < n_copies:
                _w_stage_copy(w_hbm, layer, stage, sem, j + W_STAGE_BUFS).start()
        wt_s[0:MLA_ROPE, :] = w_ref[_O_KR:_O_KR + MLA_ROPE, :]
        wt_s[MLA_ROPE:MLA_ROPE + N_ML_GATES, :] = w_ref[_O_MG:_O_MG + N_ML_GATES, :]
        wt_s[MLA_ROPE + N_ML_GATES:LANES, :] = jnp.zeros((LANES - MLA_ROPE - N_ML_GATES, D_MODEL), BF16)
        wt_s[LANES:, :] = w_ref[_O_CQ:_O_CQ + MLA_Q_RANK, :]

    mod = mod_ref[...]
    n_lat_rows = jnp.where(pl.program_id(0) < n_lat_steps, tm, 0)
    from_lat = lax.broadcasted_iota(jnp.int32, (tm, D_MODEL), 0) < n_lat_rows
    y = _rms(jnp.where(from_lat, xl_ref[...], xc_ref[...]), g_ref[...])
    h = (y * (1.0 + mod[:, D_MODEL:2 * D_MODEL]) + mod[:, :D_MODEL]).astype(BF16)
    for s in range(MAIN_W // GROUP_W):
        cols = slice(s * GROUP_W, (s + 1) * GROUP_W)
        acc = lax.dot_general(h, w_ref[_O_ZA + s * GROUP_W:_O_ZA + (s + 1) * GROUP_W, :], nt,
                              preferred_element_type=F32)
        if s in (C_ZA, C_ZB, C_ZC):
            main_ref[:, cols] = _silu(acc).astype(BF16)
        elif s == C_MO:
            main_ref[:, cols] = _sigmoid(acc).astype(BF16)
        else:
            main_ref[:, cols] = acc.astype(BF16)
        if s in (C_DK, C_DV):
            side_ref = dk_ref if s == C_DK else dv_ref
            for b in range(tm // T):
                for hd in range(DIFF_HEADS):
                    side_ref[b, pl.ds(hd, T, stride=DIFF_HEADS), :] = (
                        acc[b * T:(b + 1) * T, hd * HEAD_W:(hd + 1) * HEAD_W])
    tail_ref[:, 0:MLA_KV_RANK] = lax.dot_general(h, w_ref[_O_CKV:_O_CKV + MLA_KV_RANK, :], nt,
                                                 preferred_element_type=F32)
    tail_ref[:, MLA_KV_RANK:] = lax.dot_general(h, wt_s[...], nt, preferred_element_type=F32)


def _proj_call(l_arr, xl2, xc2, mod_all, g_norm3, w_in_t, Tl, Tc, side_bufs=None):
    tm = PROJ_TM
    n_lat, n_ctx = xl2.shape[0], xc2.shape[0]
    assert Tl % tm == 0 and tm % Tc == 0 and n_lat % tm == 0 and n_ctx % tm == 0
    nl = n_lat // tm
    n_tok = n_lat + n_ctx
    ctx_tile = lambda i: jnp.maximum(i - nl, 0)
    in_specs = [pl.BlockSpec((tm, D_MODEL), lambda i, l: (jnp.minimum(i, nl - 1), 0)),
                pl.BlockSpec((tm, D_MODEL), lambda i, l: (ctx_tile(i), 0)),
                pl.BlockSpec((None, None, 1, 3 * D_MODEL),
                             lambda i, l: (l[0], jnp.where(i < nl, 1 + (i * tm) // Tl, 0), 0, 0)),
                pl.BlockSpec((None, 1, D_MODEL), lambda i, l: (l[0], 0, 0)),
                pl.BlockSpec(memory_space=pl.ANY)]
    args = [xl2, xc2, mod_all, g_norm3, w_in_t]
    out_shape = [jax.ShapeDtypeStruct((n_tok, MAIN_W), BF16), jax.ShapeDtypeStruct((n_tok, TAIL_W), F32)]
    out_specs = [pl.BlockSpec((tm, MAIN_W), lambda i, l: (i, 0)), pl.BlockSpec((tm, TAIL_W), lambda i, l: (i, 0))]
    aliases = {}
    n_alias = 0
    bt = tm // Tc
    out_shape += [jax.ShapeDtypeStruct((n_ctx // Tc, DEPTH, Tc * DIFF_HEADS, HEAD_W), F32)] * 2
    if side_bufs is not None:
        n_alias = len(side_bufs)
        aliases = {1 + len(args) + i: 2 + i for i in range(n_alias)}
        in_specs += [pl.BlockSpec(memory_space=pl.ANY)] * n_alias
        args += list(side_bufs)
        out_specs += [pl.BlockSpec((bt, None, Tc * DIFF_HEADS, HEAD_W), lambda i, l: (ctx_tile(i), l[0], 0, 0))] * 2
    else:
        out_specs += [pl.BlockSpec((bt, DEPTH, Tc * DIFF_HEADS, HEAD_W), lambda i, l: (ctx_tile(i), 0, 0, 0))] * 2
    grid_spec = pltpu.PrefetchScalarGridSpec(
        num_scalar_prefetch=1, grid=(n_tok // tm,), in_specs=in_specs, out_specs=out_specs,
        scratch_shapes=[pltpu.VMEM((TAIL_W - MLA_KV_RANK, D_MODEL), BF16),
                        pltpu.VMEM((N_IN, D_MODEL), BF16),
                        pltpu.VMEM((W_STAGE_BUFS, W_STAGE_ROWS, D_MODEL), F32),
                        pltpu.SemaphoreType.DMA((W_STAGE_BUFS,))])
    return pl.pallas_call(
        functools.partial(_proj_kernel, n_alias, Tc, nl),
        grid_spec=grid_spec,
        out_shape=out_shape,
        input_output_aliases=aliases,
        compiler_params=pltpu.CompilerParams(dimension_semantics=("arbitrary",),
                                             vmem_limit_bytes=VMEM_LIMIT),
        name="proj",
    )(l_arr, *args)


def _mlstm_kernel(has_ctx, T, n_alias, *refs):
    nc = T // ML_CHUNK
    L = ML_CHUNK
    use_inter = has_ctx or nc > 1
    it = iter(refs)
    l_ref = next(it)
    ml_ref, aux_ref, convw_ref, gb_ref, mln_ref = [next(it) for _ in range(5)]
    mq_ref, mk_ref, mv_ref, mo_ref, zc_ref = [ml_ref.at[:, i * GROUP_W:(i + 1) * GROUP_W] for i in range(5)]
    if has_ctx:
        c0_ref, n0_ref, m0_ref = [next(it) for _ in range(3)]
    else:
        for _ in range(n_alias):
            next(it)
    yc_ref = next(it)
    if not has_ctx:
        cout_ref, nout_ref, mout_ref = [next(it) for _ in range(3)]
    q_s, k_s, vt_s, hf_s, hb_s, caug_s, m_s, gt_s, ub_s = [next(it) for _ in range(9)]
    del l_ref
    if not has_ctx and n_alias == 0:
        for r in (cout_ref, nout_ref, mout_ref):
            r[1:] = jnp.zeros((DEPTH - 1,) + r.shape[1:], F32)
        cout_ref, nout_ref, mout_ref = cout_ref.at[0], nout_ref.at[0], mout_ref.at[0]

    convw = convw_ref[...]
    halo = 16
    for c in range(nc):
        cs = slice(c * L, (c + 1) * L)
        for j in range(2 * GROUP_W // LANES):
            src = mq_ref if j < GROUP_W // LANES else mk_ref
            ls = slice((j % (GROUP_W // LANES)) * LANES, (j % (GROUP_W // LANES) + 1) * LANES)
            u = src[cs, ls].astype(F32)
            prev = src[c * L - halo:c * L, ls].astype(F32)[halo - 1:halo, :] if c > 0 else 0.0
            nxt = src[(c + 1) * L:(c + 1) * L + halo, ls].astype(F32)[0:1, :] if c < nc - 1 else 0.0
            ub_s[0:8, :] = jnp.broadcast_to(prev, (8, LANES)) if c > 0 else jnp.zeros((8, LANES), F32)
            ub_s[8:8 + L, :] = u
            ub_s[8 + L:16 + L, :] = jnp.broadcast_to(nxt, (8, LANES)) if c < nc - 1 else jnp.zeros((8, LANES), F32)
            up, un = ub_s[7:7 + L, :], ub_s[9:9 + L, :]
            w3 = convw[:, j * LANES:(j + 1) * LANES]
            y = _silu(w3[0:1, :] * up + w3[1:2, :] * u + w3[2:3, :] * un)
            if j < GROUP_W // LANES:
                q_s[cs, ls] = (y * (ML_DK ** -0.5)).astype(BF16)
            else:
                k_s[cs, ls] = y
    row16 = lax.broadcasted_iota(jnp.int32, (VT_ROWS - HEAD_W, L), 0)
    ones_rows = jnp.where(row16 == 0, 1.0, 0.0).astype(BF16)
    g_t = (aux_ref[...] + gb_ref[...]).T
    for c in range(nc):
        cs = slice(c * L, (c + 1) * L)
        gt_s[c] = g_t[GATE_LANE0:GATE_LANE0 + N_ML_GATES, cs]
        for h in range(ML_HEADS):
            vt_s[c, h, 0:HEAD_W, :] = mv_ref[cs, h * HEAD_W:(h + 1) * HEAD_W].astype(F32).T.astype(BF16)
            vt_s[c, h, HEAD_W:, :] = ones_rows

    reps = L // LANES
    if has_ctx:
        n0 = n0_ref[...]
        m0 = m0_ref[...]
        m_s[0] = jnp.concatenate([m0] * reps, axis=1)
        m_s[1] = jnp.concatenate([pltpu.roll(m0, ML_HEADS, 0)] * reps, axis=1)
        row_n = lax.broadcasted_iota(jnp.int32, (VT_ROWS - HEAD_W, ML_DK), 0)
        for r in range(2 * ML_HEADS):
            caug_s[r, 0:HEAD_W, :] = c0_ref[r].T
            caug_s[r, HEAD_W:, :] = jnp.where(row_n == 0, n0[r:r + 1, :], 0.0)
    else:
        caug_s[...] = jnp.zeros_like(caug_s)
        m_s[...] = jnp.zeros_like(m_s)

    ri = lax.broadcasted_iota(jnp.int32, (L, L), 0)
    ci = lax.broadcasted_iota(jnp.int32, (L, L), 1)
    row_r = lax.broadcasted_iota(jnp.int32, (2 * ML_HEADS, L), 0)
    nt = (((1,), (1,)), ((), ()))

    def split3(x):
        hi = x.astype(BF16).astype(F32)
        mid = (x - hi).astype(BF16).astype(F32)
        return hi, mid, (x - hi - mid).astype(BF16).astype(F32)

    masks = ((ri <= ci), (ri >= ci))

    def gate_rows(d, c):
        fwd = d == 0
        tri_t = jnp.where(masks[d], 1.0, 0.0).astype(BF16)
        last = L - 1 if fwd else 0

        g8 = gt_s[c, d * 2 * ML_HEADS:(d + 1) * 2 * ML_HEADS, :]
        hi8, mid8, lo8 = split3(_log_sigmoid(g8))
        stack = jnp.concatenate([hi8, mid8, lo8, jnp.zeros_like(hi8)], axis=0).astype(BF16)
        part8 = jnp.dot(stack, tri_t, preferred_element_type=F32)
        bc8 = part8[0:8] + part8[8:16] + part8[16:24]
        bcs = pltpu.roll(bc8, ML_HEADS, 0)
        a8 = g8 - bcs
        a_n = jnp.concatenate([a8, jnp.zeros((LANES - 2 * ML_HEADS, L), F32)], axis=0).T
        cm8 = jnp.zeros((2 * ML_HEADS, L), F32)
        for h in range(ML_HEADS):
            col_max = jnp.max(jnp.where(masks[d], a_n[:, h:h + 1], -jnp.inf), axis=0, keepdims=True)
            cm8 = jnp.where(row_r == h, col_max, cm8)
        m8 = m_s[d]
        g_row = bcs + m8
        m_t = jnp.maximum(g_row, bcs + cm8)
        w_inter = jnp.exp(g_row - m_t)
        e_inv = jnp.exp(-m_t)
        c_row = bcs - m_t
        b_last = bcs[:, last:last + 1]
        m_new = m_t[:, last:last + 1]
        a_prev = jnp.exp(b_last + m8[:, 0:1] - m_new)
        w_s = jnp.exp(a8 + (b_last - m_new))
        m_s[d] = jnp.broadcast_to(m_new, (2 * ML_HEADS, L))
        return a_n, c_row, w_inter, e_inv, w_s, a_prev

    def do_pair(c_f, c_b):
        stats = (gate_rows(0, c_f), gate_rows(1, c_b))
        combos = [(d, h) for d in range(2) for h in range(ML_HEADS)]
        rows = []
        for c in (c_f, c_b):
            rows.append(slice(c * L, (c + 1) * L) if isinstance(c, int) else pl.ds(pl.multiple_of(c * L, L), L))
        chunk = (c_f, c_b)
        hsl = [slice(h * HEAD_W, (h + 1) * HEAD_W) for h in range(ML_HEADS)]
        qcs = [q_s[rows[d], hsl[h]] for d, h in combos]
        kcs = [k_s[rows[d], hsl[h]].astype(BF16) for d, h in combos]
        vts = [vt_s[chunk[d], h] for d, h in combos]
        n = range(len(combos))
        s_ts = [lax.dot_general(kcs[i], qcs[i], nt, preferred_element_type=F32) for i in n]
        sws = []
        for i, (d, h) in enumerate(combos):
            a_n, c_row = stats[d][0], stats[d][1]
            w_t = jnp.exp(jnp.where(masks[d], a_n[:, h:h + 1] + c_row[h:h + 1, :], -jnp.inf))
            sws.append((s_ts[i] * w_t).astype(BF16))
        nds = [jnp.dot(vts[i], sws[i], preferred_element_type=F32) for i in n]
        if use_inter:
            inters = [lax.dot_general(caug_s[d * ML_HEADS + h].astype(BF16), qcs[i], nt,
                                      preferred_element_type=F32) for i, (d, h) in enumerate(combos)]
            nds = [nds[i] + stats[d][2][h:h + 1, :] * inters[i] for i, (d, h) in enumerate(combos)]
        for i, (d, h) in enumerate(combos):
            inv = 1.0 / jnp.maximum(jnp.abs(nds[i][HEAD_W:HEAD_W + 1, :]), stats[d][3][h:h + 1, :])
            h_s = hf_s if d == 0 else hb_s
            h_s[rows[d], hsl[h]] = (nds[i][0:HEAD_W, :] * inv).T
        upds = [jnp.dot((vts[i].astype(F32) * stats[d][4][h:h + 1, :]).astype(BF16), kcs[i],
                        preferred_element_type=F32) for i, (d, h) in enumerate(combos)]
        for i, (d, h) in enumerate(combos):
            r = d * ML_HEADS + h
            caug_s[r] = upds[i] + stats[d][5][h:h + 1, :] * caug_s[r] if use_inter else upds[i]

    for i in range(nc):
        do_pair(i, nc - 1 - i)

    mln = mln_ref[...]
    for c in range(nc):
        cs = slice(c * L, (c + 1) * L)
        for h in range(ML_HEADS):
            hs = slice(h * HEAD_W, (h + 1) * HEAD_W)
            hc = mo_ref[cs, hs].astype(F32) * (hf_s[cs, hs] + hb_s[cs, hs])
            yc_ref[cs, hs] = (_rms(hc, mln[:, hs]) * zc_ref[cs, hs].astype(F32)).astype(BF16)

    if not has_ctx:
        for r in range(2 * ML_HEADS):
            cout_ref[r] = caug_s[r, 0:HEAD_W, :].T
            nout_ref[r:r + 1, :] = caug_s[r, HEAD_W:HEAD_W + 1, :]
        for d in range(2):
            mout_ref[d * ML_HEADS:(d + 1) * ML_HEADS, :] = m_s[d, 0:ML_HEADS, 0:LANES]


def _mlstm_scratch(T):
    nc = T // ML_CHUNK
    return [pltpu.VMEM((T, GROUP_W), BF16),
            pltpu.VMEM((T, GROUP_W), F32),
            pltpu.VMEM((nc, ML_HEADS, VT_ROWS, ML_CHUNK), BF16),
            pltpu.VMEM((T, GROUP_W), F32),
            pltpu.VMEM((T, GROUP_W), F32),
            pltpu.VMEM((2 * ML_HEADS, VT_ROWS, ML_DK), F32),
            pltpu.VMEM((2, 2 * ML_HEADS, ML_CHUNK), F32),
            pltpu.VMEM((nc, N_ML_GATES, ML_CHUNK), F32),
            pltpu.VMEM((ML_CHUNK + 16, LANES), F32)]


def _mlstm_lat_call(l_arr, main, tail, ml_conv, gate_row, ml_norm2, T, ctx_state):
    B = ctx_state[0].shape[0]
    main3 = main.reshape(-1, T, MAIN_W)
    tail3 = tail.reshape(-1, T, TAIL_W)

    assert (C_MQ, C_ZC) == (5, 9)
    in_specs = [pl.BlockSpec((None, T, MAIN_W // 2), lambda b, l: (b, 0, 1)),
                pl.BlockSpec((None, T, LANES), lambda b, l: (b, 0, 2)),
                pl.BlockSpec((None, 3, 2 * GROUP_W), lambda b, l: (l[0], 0, 0)),
                pl.BlockSpec((None, 1, LANES), lambda b, l: (l[0], 0, 0)),
                pl.BlockSpec((None, 1, GROUP_W), lambda b, l: (l[0], 0, 0)),
                pl.BlockSpec((None, None, 2 * ML_HEADS, ML_DK, HEAD_W), lambda b, l: (b, l[0], 0, 0, 0)),
                pl.BlockSpec((None, None, 2 * ML_HEADS, ML_DK), lambda b, l: (b, l[0], 0, 0)),
                pl.BlockSpec((None, None, 2 * ML_HEADS, LANES), lambda b, l: (b, l[0], 0, 0))]
    grid_spec = pltpu.PrefetchScalarGridSpec(
        num_scalar_prefetch=1, grid=(B,), in_specs=in_specs,
        out_specs=[pl.BlockSpec((None, T, GROUP_W), lambda b, l: (b, 0, 0))],
        scratch_shapes=_mlstm_scratch(T))
    return pl.pallas_call(
        functools.partial(_mlstm_kernel, True, T, 0),
        grid_spec=grid_spec,
        out_shape=[jax.ShapeDtypeStruct((B, T, GROUP_W), BF16)],
        compiler_params=pltpu.CompilerParams(dimension_semantics=("arbitrary",),
                                             vmem_limit_bytes=VMEM_LIMIT),
        name="mlstm_lat",
    )(l_arr, main3, tail3, ml_conv, gate_row, ml_norm2, *ctx_state)


def _attn_kernel(has_ctx, is_last, merged, n_alias, T, *refs):
    Tk = T + (PAST_LEN if has_ctx else 0)
    pv_t = Tk <= ATTN_PVT_MAX_KEYS
    it = iter(refs)
    l_ref = next(it)
    if merged:
        tail_ref, att_ref = next(it), next(it)
        ckv_ref = tail_ref.at[:, 0:MLA_KV_RANK]
        aux_ref = tail_ref.at[:, MLA_KV_RANK:MLA_KV_RANK + LANES]
        cq_ref = tail_ref.at[:, MLA_KV_RANK + LANES:]
        za_ref, dq_ref, dk_ref, dv_ref, zb_ref = [att_ref.at[:, i * GROUP_W:(i + 1) * GROUP_W] for i in range(5)]
    else:
        cq_ref, ckv_ref, aux_ref, dq_ref, dk_ref, dv_ref, zb_ref, za_ref = [next(it) for _ in range(8)]
    (yc_ref, x_ref, mod_ref, wuq_ref, wukv_ref, wout_ref, gq_ref, gkv_ref, gdn_ref,
     lam_ref) = [next(it) for _ in range(10)]
    if is_last:
        gfin_ref = next(it)
    if has_ctx:
        cosq_ref, sinq_ref, cosk_ref, sink_ref, cckv_ref, ckr_ref, cdk_ref, cdv_ref = [next(it) for _ in range(8)]
    else:
        for _ in range(n_alias):
            next(it)
    xo_ref = next(it)
    if is_last:
        yfin_ref = next(it)
    if not has_ctx:
        ckvn_ref, kro_ref = [next(it) for _ in range(2)]
    ka_s, va_s, kb_s, vb_s, ycat_s = [next(it) for _ in range(5)]
    first = not has_ctx and n_alias == 0
    if first:
        ckvn_all, kro_all = ckvn_ref, kro_ref
        ckvn_ref, kro_ref = ckvn_ref.at[0], kro_ref.at[0]

    def _build_keys():
        wukv = wukv_ref[...]
        ckv_n = _rms(ckv_ref[...], gkv_ref[...])
        aux = aux_ref[...]
        if not has_ctx:
            ckvn_ref[...] = ckv_n
            kro_ref[...] = aux.T[0:MLA_ROPE, :]
        if first:
            for r in (ckvn_all, kro_all):
                r[1:] = jnp.zeros((DEPTH - 1,) + r.shape[1:], F32)
        kv = jnp.dot(ckv_n.astype(BF16), wukv, preferred_element_type=F32)
        lane = lax.broadcasted_iota(jnp.int32, aux.shape, 1)
        kr = _rope_tile(aux, cosk_ref[...], sink_ref[...]) if has_ctx else aux
        kr = jnp.where(lane < MLA_ROPE, kr, 0.0).astype(BF16)
        for h in range(MLA_HEADS):
            ka_s[0:T, 2 * h * HEAD_W:(2 * h + 1) * HEAD_W] = kv[:, 2 * h * HEAD_W:(2 * h + 1) * HEAD_W].astype(BF16)
            ka_s[0:T, (2 * h + 1) * HEAD_W:(2 * h + 2) * HEAD_W] = kr
        if pv_t:
            row16 = lax.broadcasted_iota(jnp.int32, (VT_ROWS - HEAD_W, Tk), 0)
            ones_rows = jnp.where(row16 == 0, 1.0, 0.0).astype(BF16)
            for h in range(MLA_HEADS):
                hs = slice(h * HEAD_W, (h + 1) * HEAD_W)
                va_s[h, 0:HEAD_W, 0:T] = kv[:, (2 * h + 1) * HEAD_W:(2 * h + 2) * HEAD_W].T.astype(BF16)
                va_s[h, HEAD_W:, :] = ones_rows
                vb_s[h, 0:HEAD_W, 0:T] = dv_ref[:, hs].astype(F32).T.astype(BF16)
                vb_s[h, HEAD_W:, :] = ones_rows
        else:
            lane_k = lax.broadcasted_iota(jnp.int32, (Tk, LANES), 1)
            ones_cols = jnp.where(lane_k == 0, 1.0, 0.0).astype(BF16)
            for h in range(MLA_HEADS):
                hs = slice(h * HEAD_W, (h + 1) * HEAD_W)
                va_s[0:T, 2 * h * HEAD_W:(2 * h + 1) * HEAD_W] = kv[:, (2 * h + 1) * HEAD_W:(2 * h + 2) * HEAD_W].astype(BF16)
                va_s[:, (2 * h + 1) * HEAD_W:(2 * h + 2) * HEAD_W] = ones_cols
                vb_s[0:T, 2 * h * HEAD_W:(2 * h + 1) * HEAD_W] = dv_ref[:, hs]
                vb_s[:, (2 * h + 1) * HEAD_W:(2 * h + 2) * HEAD_W] = ones_cols
        if has_ctx:
            for h in range(DIFF_HEADS):
                hs = slice(h * HEAD_W, (h + 1) * HEAD_W)
                kb_s[0:T, hs] = _rope_tile(dk_ref[:, hs].astype(F32), cosk_ref[...], sink_ref[...]).astype(BF16)
            kvc = jnp.dot(cckv_ref[...].astype(BF16), wukv, preferred_element_type=F32)
            ckr = jnp.concatenate([ckr_ref[...], jnp.zeros((LANES - MLA_ROPE, PAST_LEN), F32)], axis=0).T.astype(BF16)
            for h in range(MLA_HEADS):
                ka_s[T:Tk, 2 * h * HEAD_W:(2 * h + 1) * HEAD_W] = kvc[:, 2 * h * HEAD_W:(2 * h + 1) * HEAD_W].astype(BF16)
                ka_s[T:Tk, (2 * h + 1) * HEAD_W:(2 * h + 2) * HEAD_W] = ckr
            for h in range(DIFF_HEADS):
                hs = slice(h * HEAD_W, (h + 1) * HEAD_W)
                kb_s[T:Tk, hs] = cdk_ref[pl.ds(h, PAST_LEN, stride=DIFF_HEADS), :].astype(BF16)
                v_c = kvc[:, (2 * h + 1) * HEAD_W:(2 * h + 2) * HEAD_W]
                vd_c = cdv_ref[pl.ds(h, PAST_LEN, stride=DIFF_HEADS), :]
                if pv_t:
                    va_s[h, 0:HEAD_W, T:Tk] = v_c.T.astype(BF16)
                    vb_s[h, 0:HEAD_W, T:Tk] = vd_c.T.astype(BF16)
                else:
                    va_s[T:Tk, 2 * h * HEAD_W:(2 * h + 1) * HEAD_W] = v_c.astype(BF16)
                    vb_s[T:Tk, 2 * h * HEAD_W:(2 * h + 1) * HEAD_W] = vd_c.astype(BF16)
        else:
            kb_s[0:T, :] = dk_ref[...]

    if merged:
        _build_keys()
    else:
        pl.when(pl.program_id(1) == 0)(_build_keys)

    nt = (((1,), (1,)), ((), ()))
    tq = cq_ref.shape[0]

    maps = []
    qa = jnp.dot(_rms(cq_ref[...], gq_ref[...]).astype(BF16), wuq_ref[...], preferred_element_type=F32)
    for h in range(MLA_HEADS):
        q_nope = qa[:, 2 * h * HEAD_W:(2 * h + 1) * HEAD_W]
        q_rope = qa[:, (2 * h + 1) * HEAD_W:(2 * h + 2) * HEAD_W]
        if has_ctx:
            q_rope = _rope_tile(q_rope, cosq_ref[...], sinq_ref[...])
        q_h = (jnp.concatenate([q_nope, q_rope], axis=-1) * (MLA_SCALE * LOG2E)).astype(BF16)
        two = slice(2 * h * HEAD_W, (2 * h + 2) * HEAD_W)
        v_load = functools.partial(lambda hh: va_s[hh], h) if pv_t else functools.partial(lambda sl: va_s[:, sl], two)
        maps.append((q_h, functools.partial(lambda sl: ka_s[:, sl], two), v_load))
    lane_q = lax.broadcasted_iota(jnp.int32, (tq, HEAD_W), 1)
    for h in range(DIFF_HEADS):
        hs = slice(h * HEAD_W, (h + 1) * HEAD_W)
        q_h = dq_ref[:, hs].astype(F32)
        if has_ctx:
            q_h = _rope_tile(q_h, cosq_ref[...], sinq_ref[...])
        q_h = q_h * (DIFF_SCALE * LOG2E)
        two = slice(2 * h * HEAD_W, (2 * h + 2) * HEAD_W)
        v_load = functools.partial(lambda hh: vb_s[hh], h) if pv_t else functools.partial(lambda sl: vb_s[:, sl], two)
        for q_m in (jnp.where(lane_q < DIFF_D, q_h, 0.0), jnp.where(lane_q >= DIFF_D, q_h, 0.0)):
            maps.append((q_m.astype(BF16), functools.partial(lambda sl: kb_s[:, sl], hs), v_load))

    def scores(i):
        if pv_t:
            return lax.dot_general(maps[i][1](), maps[i][0], nt, preferred_element_type=F32)
        return lax.dot_general(maps[i][0], maps[i][1](), nt, preferred_element_type=F32)

    def softmax_pv(i, s):
        if pv_t:
            e = jnp.exp2(s - jnp.max(s, axis=0, keepdims=True))
            pv = jnp.dot(maps[i][2](), e.astype(BF16), preferred_element_type=F32)
            return (pv[0:HEAD_W, :] * (1.0 / pv[HEAD_W:HEAD_W + 1, :])).T
        e = jnp.exp2(s - jnp.max(s, axis=-1, keepdims=True))
        pv = jnp.dot(e.astype(BF16), maps[i][2](), preferred_element_type=F32)
        return pv[:, :HEAD_W] * (1.0 / pv[:, HEAD_W:HEAD_W + 1])

    n_maps = len(maps)
    ahead = min(ATTN_AHEAD_CTX if not has_ctx else ATTN_AHEAD_LAT, n_maps)
    pending = {i: scores(i) for i in range(ahead)}
    outs = []
    for i in range(n_maps):
        if i + ahead < n_maps:
            pending[i + ahead] = scores(i + ahead)
        outs.append(softmax_pv(i, pending.pop(i)))

    for h in range(MLA_HEADS):
        hs = slice(h * HEAD_W, (h + 1) * HEAD_W)
        ycat_s[:, hs] = (outs[h] * za_ref[:, hs].astype(F32)).astype(BF16)

    lp = lam_ref[...]
    lf = l_ref[0].astype(F32)
    lam_init = 0.8 - 0.6 * jnp.exp(jnp.full((1, 1), -0.3, F32) * lf)
    lam = (jnp.exp(jnp.sum(lp[0:1, :] * lp[1:2, :], axis=-1, keepdims=True))
           - jnp.exp(jnp.sum(lp[2:3, :] * lp[3:4, :], axis=-1, keepdims=True)) + lam_init)
    for h in range(DIFF_HEADS):
        hs = slice(h * HEAD_W, (h + 1) * HEAD_W)
        o1, o2 = outs[MLA_HEADS + 2 * h], outs[MLA_HEADS + 2 * h + 1]
        o = _rms(o1 - lam * o2, gdn_ref[...]) * (1.0 - lam_init)
        ycat_s[:, GROUP_W + h * HEAD_W:GROUP_W + (h + 1) * HEAD_W] = (o * zb_ref[:, hs].astype(F32)).astype(BF16)

    ycat_s[:, 2 * GROUP_W:] = yc_ref[...]
    y = jnp.dot(ycat_s[...], wout_ref[...], preferred_element_type=F32)
    x_new = x_ref[...] + mod_ref[:, 2 * D_MODEL:] * y
    xo_ref[...] = x_new
    if is_last:
        yfin_ref[...] = _rms(x_new, gfin_ref[...])


def _attn_lat_call(l_arr, main, tail, yc, x2, mod_all, w, T, is_last, rope, ctx):
    B = x2.shape[0] // T
    tq = min(ATTN_TQ, T)
    nq = T // tq
    Tk = T + PAST_LEN
    main3 = main.reshape(-1, T, MAIN_W)
    tail3 = tail.reshape(-1, T, TAIL_W)
    x3 = x2.reshape(B, T, D_MODEL)

    def tile(c, width):
        return pl.BlockSpec((None, tq, width), lambda b, q, l, c=c: (b, q, c))

    def full(c, width):
        return pl.BlockSpec((None, T, width), lambda b, q, l, c=c: (b, 0, c))

    def wspec(shape):
        return pl.BlockSpec((None,) + shape, lambda b, q, l: (l[0],) + (0,) * len(shape))

    in_specs = [tile(1, MLA_Q_RANK),
                full(0, MLA_KV_RANK),
                full(2, LANES),
                tile(C_DQ, GROUP_W), full(C_DK, GROUP_W), full(C_DV, GROUP_W),
                tile(C_ZB, GROUP_W), tile(C_ZA, GROUP_W),
                pl.BlockSpec((None, tq, GROUP_W), lambda b, q, l: (b, q, 0)),
                pl.BlockSpec((None, tq, D_MODEL), lambda b, q, l: (b, q, 0)),
                pl.BlockSpec((None, None, 1, 3 * D_MODEL), lambda b, q, l: (l[0], 1 + b, 0, 0)),
                wspec((MLA_Q_RANK, 2 * GROUP_W)), wspec((MLA_KV_RANK, 2 * GROUP_W)),
                wspec((3 * GROUP_W, D_MODEL)),
                wspec((1, MLA_Q_RANK)), wspec((1, MLA_KV_RANK)), wspec((1, HEAD_W)),
                wspec((4, DIFF_D))]
    args = [tail3, tail3, tail3, main3, main3, main3, main3, main3,
            yc, x3, mod_all, w['wuq'], w['wukv'], w['wout'], w['gq'], w['gkv'], w['gdn'], w['lam']]
    if is_last:
        in_specs.append(pl.BlockSpec((1, D_MODEL), lambda b, q, l: (0, 0)))
        args.append(w['gfin'])
    cos_t, sin_t = rope
    in_specs += [pl.BlockSpec((tq, LANES), lambda b, q, l: (q, 0)),
                 pl.BlockSpec((tq, LANES), lambda b, q, l: (q, 0)),
                 pl.BlockSpec((T, LANES), lambda b, q, l: (0, 0)),
                 pl.BlockSpec((T, LANES), lambda b, q, l: (0, 0)),
                 pl.BlockSpec((None, None, PAST_LEN, MLA_KV_RANK), lambda b, q, l: (b, l[0], 0, 0)),
                 pl.BlockSpec((None, None, MLA_ROPE, PAST_LEN), lambda b, q, l: (b, l[0], 0, 0)),
                 pl.BlockSpec((None, None, PAST_LEN * DIFF_HEADS, HEAD_W), lambda b, q, l: (b, l[0], 0, 0)),
                 pl.BlockSpec((None, None, PAST_LEN * DIFF_HEADS, HEAD_W), lambda b, q, l: (b, l[0], 0, 0))]
    args += [cos_t, sin_t, cos_t, sin_t, ctx['ckv'], ctx['krope'], ctx['dk'], ctx['dv']]
    n_out = 2 if is_last else 1
    out_shape = [jax.ShapeDtypeStruct((B, T, D_MODEL), F32)] * n_out
    out_specs = [pl.BlockSpec((None, tq, D_MODEL), lambda b, q, l: (b, q, 0))] * n_out
    if Tk <= ATTN_PVT_MAX_KEYS:
        v_scratch = pltpu.VMEM((MLA_HEADS, VT_ROWS, Tk), BF16)
    else:
        v_scratch = pltpu.VMEM((Tk, 2 * GROUP_W), BF16)
    grid_spec = pltpu.PrefetchScalarGridSpec(
        num_scalar_prefetch=1, grid=(B, nq), in_specs=in_specs, out_specs=out_specs,
        scratch_shapes=[pltpu.VMEM((Tk, 2 * GROUP_W), BF16),
                        v_scratch,
                        pltpu.VMEM((Tk, GROUP_W), BF16),
                        v_scratch,
                        pltpu.VMEM((tq, 3 * GROUP_W), BF16)])
    return pl.pallas_call(
        functools.partial(_attn_kernel, True, is_last, False, 0, T),
        grid_spec=grid_spec,
        out_shape=out_shape,
        compiler_params=pltpu.CompilerParams(dimension_semantics=("arbitrary", "arbitrary"),
                                             vmem_limit_bytes=VMEM_LIMIT),
        name="attn_lat",
    )(l_arr, *args)


def _ctx_kernel(is_last, n_alias_m, n_alias_a, T, *refs):
    l_ref = refs[0]
    n_m_in = 5 + n_alias_m
    n_a_in = 2 + 9 + (1 if is_last else 0) + n_alias_a
    m_in = refs[1:1 + n_m_in]
    a_in = refs[1 + n_m_in:1 + n_m_in + n_a_in]
    rest = refs[1 + n_m_in + n_a_in:]
    m_out, rest = rest[:3], rest[3:]
    n_a_out = 3 + (1 if is_last else 0)
    a_out, rest = rest[:n_a_out], rest[n_a_out:]
    m_scr, a_scr, yc_s = rest[:9], rest[9:14], rest[14]
    _mlstm_kernel(False, T, n_alias_m, l_ref, *m_in, yc_s, *m_out, *m_scr)
    _attn_kernel(False, is_last, True, n_alias_a, T, l_ref, a_in[0], a_in[1], yc_s, *a_in[2:], *a_out, *a_scr)


def _ctx_call(l_arr, main, tail, x2, mod_all, w, ml_conv, gate_row, ml_norm2, T, is_last, state_bufs, kv_bufs):
    B = x2.shape[0] // T
    main3 = main.reshape(-1, T, MAIN_W)
    tail3 = tail.reshape(-1, T, TAIL_W)
    b0 = main3.shape[0] - B
    x3 = x2.reshape(B, T, D_MODEL)
    assert (C_ZA, C_ZB, C_MQ, C_ZC) == (0, 4, 5, 9) and T == ML_CHUNK and T <= ATTN_TQ

    def wspec(shape):
        return pl.BlockSpec((None,) + shape, lambda b, q, l: (l[0],) + (0,) * len(shape))

    def side(first, shapes):
        if first:
            return [pl.BlockSpec((None, DEPTH) + s, lambda b, q, l, n=len(s): (b, 0) + (0,) * n) for s in shapes]
        return [pl.BlockSpec((None, None) + s, lambda b, q, l, n=len(s): (b, l[0]) + (0,) * n) for s in shapes]

    any_spec = pl.BlockSpec(memory_space=pl.ANY)
    n_alias_m = 0 if state_bufs is None else len(state_bufs)
    n_alias_a = 0 if kv_bufs is None else len(kv_bufs)
    in_specs = [pl.BlockSpec((None, T, MAIN_W // 2), lambda b, q, l: (b0 + b, 0, 1)),
                pl.BlockSpec((None, T, LANES), lambda b, q, l: (b0 + b, 0, 2)),
                wspec((3, 2 * GROUP_W)), wspec((1, LANES)), wspec((1, GROUP_W))] + [any_spec] * n_alias_m
    args = [main3, tail3, ml_conv, gate_row, ml_norm2] + list(state_bufs or ())
    alias_m_at = 1 + 5
    in_specs += [pl.BlockSpec((None, T, TAIL_W), lambda b, q, l: (b0 + b, 0, 0)),
                 pl.BlockSpec((None, T, MAIN_W // 2), lambda b, q, l: (b0 + b, 0, 0)),
                 pl.BlockSpec((None, T, D_MODEL), lambda b, q, l: (b, 0, 0)),
                 pl.BlockSpec((None, None, 1, 3 * D_MODEL), lambda b, q, l: (l[0], 0, 0, 0)),
                 wspec((MLA_Q_RANK, 2 * GROUP_W)), wspec((MLA_KV_RANK, 2 * GROUP_W)), wspec((3 * GROUP_W, D_MODEL)),
                 wspec((1, MLA_Q_RANK)), wspec((1, MLA_KV_RANK)), wspec((1, HEAD_W)), wspec((4, DIFF_D))]
    args += [tail3, main3, x3, mod_all, w['wuq'], w['wukv'], w['wout'], w['gq'], w['gkv'], w['gdn'], w['lam']]
    if is_last:
        in_specs.append(pl.BlockSpec((1, D_MODEL), lambda b, q, l: (0, 0)))
        args.append(w['gfin'])
    alias_a_at = 1 + len(args)
    in_specs += [any_spec] * n_alias_a
    args += list(kv_bufs or ())

    state_shapes = [(2 * ML_HEADS, ML_DK, HEAD_W), (2 * ML_HEADS, ML_DK), (2 * ML_HEADS, LANES)]
    kv_shapes = [(T, MLA_KV_RANK), (MLA_ROPE, T)]
    out_shape = [jax.ShapeDtypeStruct((B, DEPTH) + s, F32) for s in state_shapes]
    out_specs = side(n_alias_m == 0, state_shapes)
    out_shape.append(jax.ShapeDtypeStruct((B, T, D_MODEL), F32))
    out_specs.append(pl.BlockSpec((None, T, D_MODEL), lambda b, q, l: (b, 0, 0)))
    if is_last:
        out_shape.append(jax.ShapeDtypeStruct((B, T, D_MODEL), F32))
        out_specs.append(pl.BlockSpec((None, T, D_MODEL), lambda b, q, l: (b, 0, 0)))
    kv_out_at = len(out_shape)
    out_shape += [jax.ShapeDtypeStruct((B, DEPTH) + s, F32) for s in kv_shapes]
    out_specs += side(n_alias_a == 0, kv_shapes)
    aliases = {alias_m_at + i: i for i in range(n_alias_m)}
    aliases.update({alias_a_at + i: kv_out_at + i for i in range(n_alias_a)})

    grid_spec = pltpu.PrefetchScalarGridSpec(
        num_scalar_prefetch=1, grid=(B, 1), in_specs=in_specs, out_specs=out_specs,
        scratch_shapes=_mlstm_scratch(T) + [
                        pltpu.VMEM((T, 2 * GROUP_W), BF16),
                        pltpu.VMEM((MLA_HEADS, VT_ROWS, T), BF16),
                        pltpu.VMEM((T, GROUP_W), BF16),
                        pltpu.VMEM((DIFF_HEADS, VT_ROWS, T), BF16),
                        pltpu.VMEM((T, 3 * GROUP_W), BF16),
                        pltpu.VMEM((T, GROUP_W), BF16)])
    return pl.pallas_call(
        functools.partial(_ctx_kernel, is_last, n_alias_m, n_alias_a, T),
        grid_spec=grid_spec,
        out_shape=out_shape,
        input_output_aliases=aliases,
        compiler_params=pltpu.CompilerParams(dimension_semantics=("arbitrary", "arbitrary"),
                                             vmem_limit_bytes=VMEM_LIMIT),
        name="ctx_mix",
    )(l_arr, *args)


def _rope_tables(n_tok):
    n_freq = MLA_ROPE // 4
    inv = ROPE_THETA ** (-jnp.arange(n_freq, dtype=F32) / n_freq)
    n_rows = n_tok // GRID_W
    rowp = jnp.repeat(jnp.arange(n_rows, dtype=F32), GRID_W)
    colp = jnp.tile(jnp.arange(GRID_W, dtype=F32), n_rows)
    ang = jnp.concatenate([rowp[:, None] * inv, colp[:, None] * inv], axis=-1)
    cos, sin = jnp.cos(ang), jnp.sin(ang)
    cos64 = jnp.concatenate([cos, cos], axis=-1)
    sin64 = jnp.concatenate([-sin, sin], axis=-1)
    return jnp.concatenate([cos64, cos64], axis=-1), jnp.concatenate([sin64, sin64], axis=-1)


def _prep_weights(W_in, W_uq, W_ukv, W_out):
    w_in_t = jnp.swapaxes(W_in, 1, 2)
    wq = W_uq.reshape(DEPTH, MLA_Q_RANK, MLA_HEADS, MLA_NOPE + MLA_ROPE)
    wq = jnp.pad(wq, ((0, 0), (0, 0), (0, 0), (0, 2 * HEAD_W - MLA_NOPE - MLA_ROPE)))
    wuq_r = wq.reshape(DEPTH, MLA_Q_RANK, MLA_HEADS * 2 * HEAD_W).astype(BF16)
    return w_in_t, wuq_r, W_ukv.astype(BF16), W_out.astype(BF16)


def kernel(x_prompt, x_sample, cache_mla_ckv, cache_mla_krope, cache_diff_k, cache_diff_v, state_mlstm_C, state_mlstm_n, state_mlstm_m, c, c_ctx, g_norm, W_mod, b_mod, W_in, mla_q_norm, W_uq, mla_kv_norm, W_ukv, diff_lambda, diff_norm, ml_conv, ml_gate_b, ml_norm, W_out, g_final):
    Bc, Tc, _ = x_prompt.shape
    Bs, Ts, _ = x_sample.shape

    w_in_t, wuq_r, wukv_r, wout_r = _prep_weights(W_in, W_uq, W_ukv, W_out)
    w = {'wuq': wuq_r, 'wukv': wukv_r, 'wout': wout_r,
         'gq': mla_q_norm.reshape(DEPTH, 1, MLA_Q_RANK), 'gkv': mla_kv_norm.reshape(DEPTH, 1, MLA_KV_RANK),
         'gdn': diff_norm.reshape(DEPTH, 1, 2 * DIFF_D), 'lam': diff_lambda,
         'gfin': g_final.reshape(1, D_MODEL)}
    g_norm3 = g_norm.reshape(DEPTH, 1, D_MODEL)
    ml_norm2 = ml_norm.reshape(DEPTH, 1, GROUP_W)
    gate_row = jnp.pad(ml_gate_b.reshape(DEPTH, 1, N_ML_GATES),
                       ((0, 0), (0, 0), (GATE_LANE0, LANES - GATE_LANE0 - N_ML_GATES)))

    cc = jnp.concatenate([c_ctx[None, :], c, jnp.zeros((8 - 1 - Bs, D_MODEL), F32)], axis=0)
    mod_all = _mod_call(cc, W_mod, b_mod).reshape(DEPTH, 8, 1, 3 * D_MODEL)

    rope = _rope_tables(Ts)
    ctx = {'ckv': cache_mla_ckv,
           'krope': jnp.swapaxes(cache_mla_krope, 2, 3),
           'dk': cache_diff_k.reshape(Bs, DEPTH, PAST_LEN * DIFF_HEADS, HEAD_W),
           'dv': cache_diff_v.reshape(Bs, DEPTH, PAST_LEN * DIFF_HEADS, HEAD_W)}
    ctx_state = (state_mlstm_C.reshape(Bs, DEPTH, 2 * ML_HEADS, ML_DK, HEAD_W),
                 state_mlstm_n.reshape(Bs, DEPTH, 2 * ML_HEADS, ML_DK),
                 jnp.broadcast_to(state_mlstm_m.reshape(Bs, DEPTH, 2 * ML_HEADS, 1),
                                  (Bs, DEPTH, 2 * ML_HEADS, LANES)))

    dkv = cnm = ckr = None
    xc2 = x_prompt.reshape(Bc * Tc, D_MODEL)
    xl2 = x_sample.reshape(Bs * Ts, D_MODEL)
    y_prompt = y_sample = None
    for l in range(DEPTH):
        l_arr = jnp.full((1,), l, jnp.int32)
        is_last = l == DEPTH - 1
        main, tail, *dkv = _proj_call(l_arr, xl2, xc2, mod_all, g_norm3, w_in_t, Ts, Tc, dkv)
        outs = _ctx_call(l_arr, main, tail, xc2, mod_all, w, ml_conv, gate_row, ml_norm2, Tc, is_last, cnm, ckr)
        cnm, outs = outs[:3], outs[3:]
        ckr = outs[-2:]
        if is_last:
            x3, y_prompt, ckvn, kro = outs
        else:
            x3, ckvn, kro = outs
        xc2 = x3.reshape(Bc * Tc, D_MODEL)
        (yc,) = _mlstm_lat_call(l_arr, main, tail, ml_conv, gate_row, ml_norm2, Ts, ctx_state)
        outs = _attn_lat_call(l_arr, main, tail, yc, xl2, mod_all, w, Ts, is_last, rope, ctx)
        if is_last:
            x3, y_sample = outs
        else:
            (x3,) = outs
        xl2 = x3.reshape(Bs * Ts, D_MODEL)
    dk_o, dv_o = dkv
    c_o, n_o, m_o = cnm
    side_outs = (ckvn, jnp.swapaxes(kro, 2, 3),
                 dk_o.reshape(Bc, DEPTH, Tc, DIFF_HEADS, 2 * DIFF_D),
                 dv_o.reshape(Bc, DEPTH, Tc, DIFF_HEADS, 2 * DIFF_D),
                 c_o.reshape(Bc, DEPTH, 2, ML_HEADS, ML_DK, HEAD_W),
                 n_o.reshape(Bc, DEPTH, 2, ML_HEADS, ML_DK),
                 m_o[:, :, :, 0].reshape(Bc, DEPTH, 2, ML_HEADS))
    return (y_prompt, y_sample, *side_outs)
```
